```python
import jax, jax.numpy as jnp
from jax import lax
import numpy as np

D_MODEL = 1024
BATCH = 2
SEQ = 8192
DEPTH = 2

MIX_WIDTH = D_MODEL
HGRN_WIDTH = D_MODEL // 2
HGRN_EXPAND = 128
HGRN_HEADS = HGRN_WIDTH // HGRN_EXPAND
HGRN_DV = HGRN_WIDTH // HGRN_HEADS
HGRN_CHUNK = 64
FOX_WIDTH = MIX_WIDTH - HGRN_WIDTH
FOX_HEAD_DIM = 64
FOX_HEADS = FOX_WIDTH // FOX_HEAD_DIM
Q_BLOCK = 128
D_FF = 4 * D_MODEL
NORM_EPS = 1e-6
SPLIT_SIZES = (HGRN_WIDTH, HGRN_WIDTH, HGRN_WIDTH, HGRN_WIDTH,
               FOX_WIDTH, FOX_WIDTH, FOX_WIDTH, FOX_HEADS)
IN_COLS = sum(SPLIT_SIZES)
SPLIT_POINTS = tuple(int(v) for v in np.cumsum(SPLIT_SIZES)[:-1])

kernel_name = "hymba_hgrn2_fox_hybrid"


def rms_norm(x, w):
    xf = x.astype(jnp.float32)
    y = xf * lax.rsqrt(jnp.mean(xf * xf, axis=-1, keepdims=True) + NORM_EPS)
    return (y * w.astype(jnp.float32)).astype(x.dtype)


def layer_lower_bounds(lower_bounds):
    p = jax.nn.softmax(lower_bounds.astype(jnp.float32), axis=0)
    c = jnp.cumsum(p, axis=0)
    return c - c[0:1]


def hgrn2_mixer(q, f_pre, i_in, g, lb, norm_w):
    B, S, _ = q.shape
    nc = S // HGRN_CHUNK
    f = lb + (1.0 - lb) * jax.nn.sigmoid(f_pre.astype(jnp.float32))
    log_f = jnp.log(f)
    k = 1.0 - f
    qf = q.astype(jnp.float32) * (HGRN_EXPAND ** -0.5)
    vf = i_in.astype(jnp.float32)

    def to_chunks(t, d):
        t = t.reshape(B, nc, HGRN_CHUNK, HGRN_HEADS, d)
        return jnp.transpose(t, (1, 0, 3, 2, 4))

    qc = to_chunks(qf, HGRN_EXPAND)
    kc = to_chunks(k, HGRN_EXPAND)
    vc = to_chunks(vf, HGRN_DV)
    lfc = to_chunks(log_f, HGRN_EXPAND)
    causal = jnp.tril(jnp.ones((HGRN_CHUNK, HGRN_CHUNK), dtype=bool))

    def step(state, inp):
        qb, kb, vb, lfb = inp
        b = jnp.cumsum(lfb, axis=2)
        inter = jnp.einsum('bhtk,bhkv->bhtv', qb * jnp.exp(b), state)
        diff = b[:, :, :, None, :] - b[:, :, None, :, :]
        decay = jnp.exp(jnp.where(causal[None, None, :, :, None], diff, -jnp.inf))
        a = jnp.einsum('bhtk,bhsk,bhtsk->bhts', qb, kb, decay)
        intra = jnp.einsum('bhts,bhsv->bhtv', a, vb)
        b_last = b[:, :, -1:, :]
        new_state = (jnp.exp(b_last[:, :, 0, :])[..., None] * state
                     + jnp.einsum('bhsk,bhsv->bhkv', kb * jnp.exp(b_last - b), vb))
        return new_state, inter + intra

    s0 = jnp.zeros((B, HGRN_HEADS, HGRN_EXPAND, HGRN_DV), jnp.float32)
    _, o = lax.scan(step, s0, (qc, kc, vc, lfc))
    o = jnp.transpose(o, (1, 0, 3, 2, 4)).reshape(B, S, HGRN_HEADS, HGRN_DV)
    o = rms_norm(o, norm_w)
    gate = jax.nn.silu(g.astype(jnp.float32)).reshape(B, S, HGRN_HEADS, HGRN_DV)
    return (o * gate).reshape(B, S, HGRN_WIDTH).astype(q.dtype)


def fox_mixer(q, k, v, f_logit, f_bias, q_norm_w, k_norm_w):
    B, S, _ = q.shape

    def heads(t):
        return jnp.transpose(t.reshape(B, S, FOX_HEADS, FOX_HEAD_DIM), (0, 2, 1, 3)).astype(jnp.float32)

    qh = rms_norm(heads(q), q_norm_w) * (FOX_HEAD_DIM ** -0.5)
    kh = rms_norm(heads(k), k_norm_w)
    vh = heads(v)
    log_f = jax.nn.log_sigmoid(f_logit.astype(jnp.float32) + f_bias.astype(jnp.float32))
    c = jnp.cumsum(jnp.transpose(log_f, (0, 2, 1)), axis=-1)
    kpos = jnp.arange(S)

    def block(bi):
        start = bi * Q_BLOCK
        qb = lax.dynamic_slice_in_dim(qh, start, Q_BLOCK, axis=2)
        cb = lax.dynamic_slice_in_dim(c, start, Q_BLOCK, axis=2)
        s = jnp.einsum('bhqd,bhkd->bhqk', qb, kh) + (cb[..., :, None] - c[..., None, :])
        qpos = start + jnp.arange(Q_BLOCK)
        mask = kpos[None, :] <= qpos[:, None]
        p = jax.nn.softmax(jnp.where(mask[None, None], s, -jnp.inf), axis=-1)
        return jnp.einsum('bhqk,bhkd->bhqd', p, vh)

    o = lax.map(block, jnp.arange(S // Q_BLOCK))
    o = jnp.transpose(o, (1, 0, 3, 2, 4)).reshape(B, S, FOX_WIDTH)
    return o.astype(q.dtype)


def setup_inputs(seed: int = 0) -> dict:
    key = jax.random.key(seed)
    ks = jax.random.split(key, 12)
    f32 = jnp.float32
    x = jax.random.normal(ks[0], (BATCH, SEQ, D_MODEL), f32)
    lower_bounds = 0.1 * jax.random.normal(ks[1], (DEPTH, HGRN_WIDTH), f32)
    norm1_w = 1.0 + 0.02 * jax.random.normal(ks[2], (DEPTH, D_MODEL), f32)
    w_in = jax.random.normal(ks[3], (DEPTH, D_MODEL, IN_COLS), f32) * D_MODEL ** -0.5
    fox_f_bias = jax.random.uniform(ks[4], (DEPTH, FOX_HEADS), f32, minval=0.0, maxval=3.0)
    q_norm_w = 1.0 + 0.02 * jax.random.normal(ks[5], (DEPTH, FOX_HEAD_DIM), f32)
    k_norm_w = 1.0 + 0.02 * jax.random.normal(ks[6], (DEPTH, FOX_HEAD_DIM), f32)
    hgrn_norm_w = 1.0 + 0.02 * jax.random.normal(ks[7], (DEPTH, HGRN_DV), f32)
    w_out = jax.random.normal(ks[8], (DEPTH, MIX_WIDTH, D_MODEL), f32) * MIX_WIDTH ** -0.5
    norm2_w = 1.0 + 0.02 * jax.random.normal(ks[9], (DEPTH, D_MODEL), f32)
    w_up = jax.random.normal(ks[10], (DEPTH, D_MODEL, D_FF), f32) * D_MODEL ** -0.5
    w_down = jax.random.normal(ks[11], (DEPTH, D_FF, D_MODEL), f32) * D_FF ** -0.5
    return {"x": x, "lower_bounds": lower_bounds, "norm1_w": norm1_w, "w_in": w_in,
            "fox_f_bias": fox_f_bias, "q_norm_w": q_norm_w, "k_norm_w": k_norm_w,
            "hgrn_norm_w": hgrn_norm_w, "w_out": w_out, "norm2_w": norm2_w,
            "w_up": w_up, "w_down": w_down}


def reference(x, lower_bounds, norm1_w, w_in, fox_f_bias, q_norm_w, k_norm_w,
              hgrn_norm_w, w_out, norm2_w, w_up, w_down):
    lbs = layer_lower_bounds(lower_bounds)
    for l in range(DEPTH):
        h = rms_norm(x, norm1_w[l])
        proj = jnp.einsum('bsd,dc->bsc', h, w_in[l])
        hq, hf, hi, hg, fq, fk, fv, ff = jnp.split(proj, SPLIT_POINTS, axis=-1)
        o_a = hgrn2_mixer(hq, hf, hi, hg, lbs[l], hgrn_norm_w[l])
        o_b = fox_mixer(fq, fk, fv, ff, fox_f_bias[l], q_norm_w[l], k_norm_w[l])
        mix = jnp.concatenate([o_a, o_b], axis=-1)
        x = x + jnp.einsum('bsc,cd->bsd', mix, w_out[l])
        h2 = rms_norm(x, norm2_w[l])
        u = jax.nn.relu(jnp.einsum('bsd,df->bsf', h2, w_up[l]))
        x = x + jnp.einsum('bsf,fd->bsd', u * u, w_down[l])
    return x
```

```python
import functools

import numpy as np
import jax
import jax.numpy as jnp
from jax import lax
from jax.experimental import pallas as pl
from jax.experimental.pallas import tpu as pltpu

F32 = jnp.float32
BF16 = jnp.bfloat16

D_MODEL = 1024
D_FF = 4 * D_MODEL
NORM_EPS = 1e-6

HG_WIDTH = 512
HG_HEADS = 4
HG_DK = 128
HG_CHUNK = 64
HG_LEVELS = 6

FX_WIDTH = 512
FX_HEADS = 8
FX_DH = 64

PROJ_MAIN = 4 * HG_WIDTH + 3 * FX_WIDTH
LANE = 128
PROJ_PAD = PROJ_MAIN + LANE

VMEM_LIMIT = 56 * 1024 * 1024

NEG_BIG = -1e30

NT_DIMS = (((1,), (1,)), ((), ()))
TN_DIMS = (((0,), (0,)), ((), ()))


def _dot(a, b):
    return jnp.dot(a, b, preferred_element_type=F32)


def _dot_nt(a, b):
    return lax.dot_general(a, b, NT_DIMS, preferred_element_type=F32)


def _dot_tn(a, b):
    return lax.dot_general(a, b, TN_DIMS, preferred_element_type=F32)


def _split2(x):
    hi = x.astype(BF16)
    lo = (x - hi.astype(F32)).astype(BF16)
    return hi, lo


def _split3(x):
    hi = x.astype(BF16)
    r = x - hi.astype(F32)
    mid = r.astype(BF16)
    lo = (r - mid.astype(F32)).astype(BF16)
    return hi, mid, lo


IN_TM = 512
IN_TN = 512


def _inproj_kernel(x_ref, nw_ref, w_ref, proj_ref, ff_ref):
    x = x_ref[...]
    ms = jnp.mean(x * x, axis=-1, keepdims=True)
    h = (x * lax.rsqrt(ms + NORM_EPS) * nw_ref[...]).astype(BF16)
    for c in range(PROJ_MAIN // IN_TN):
        cols = slice(c * IN_TN, (c + 1) * IN_TN)
        proj_ref[:, cols] = _dot(h, w_ref[:, cols]).astype(BF16)
    ff_ref[...] = _dot(h, w_ref[:, PROJ_MAIN:PROJ_PAD])


def _inproj(x2d, norm_w, w_pad):
    t = x2d.shape[0]
    return pl.pallas_call(
        _inproj_kernel,
        out_shape=(jax.ShapeDtypeStruct((t, PROJ_MAIN), BF16),
                   jax.ShapeDtypeStruct((t, LANE), F32)),
        grid=(t // IN_TM,),
        in_specs=[
            pl.BlockSpec((IN_TM, D_MODEL), lambda i: (i, 0)),
            pl.BlockSpec((1, D_MODEL), lambda i: (0, 0)),
            pl.BlockSpec((D_MODEL, PROJ_PAD), lambda i: (0, 0)),
        ],
        out_specs=(pl.BlockSpec((IN_TM, PROJ_MAIN), lambda i: (i, 0)),
                   pl.BlockSpec((IN_TM, LANE), lambda i: (i, 0))),
        compiler_params=pltpu.CompilerParams(
            dimension_semantics=("arbitrary",), vmem_limit_bytes=VMEM_LIMIT),
        name="inproj",
    )(x2d, norm_w, w_pad)


HG_ROWS = 512
HG_NCHUNK = HG_ROWS // HG_CHUNK
HG_E_ROWS = (2 + HG_LEVELS) * HG_CHUNK


def _hgrn_constants():
    c = HG_CHUNK
    tri = np.tril(np.ones((c, c), np.float32))
    rows = [tri, 1.0 - tri]
    idx = np.arange(c)
    for lvl in range(HG_LEVELS):
        m = c >> (lvl + 1)
        ref = (idx // (2 * m)) * (2 * m) + m - 1
        rows.append(tri - tri[ref])
    mat = np.concatenate(rows, axis=0)
    mcat = np.concatenate([mat, mat], axis=1)
    x = idx[:, None] ^ idx[None, :]
    top = np.floor(np.log2(np.maximum(x, 1))).astype(np.int32)
    level = (HG_LEVELS - 1) - top
    level = np.where(idx[:, None] == idx[None, :], HG_LEVELS, level)
    level = np.where(idx[:, None] < idx[None, :], -1, level)
    return mcat.astype(np.float32), level.astype(np.int32)


def _hgrn_kernel(q_ref, f_ref, i_ref, g_ref, lb_ref, nw_ref, mcat_ref, lvl_ref,
                 o_ref, st_ref):
    @pl.when(pl.program_id(1) == 0)
    def _():
        st_ref[...] = jnp.zeros_like(st_ref)

    lb = lb_ref[...]
    nw = nw_ref[...]
    mcat = mcat_ref[...]
    level = lvl_ref[...]
    scale = HG_DK ** -0.5

    def chunk(c, carry):
        r0 = pl.multiple_of(c * HG_CHUNK, HG_CHUNK)
        rows = pl.ds(r0, HG_CHUNK)
        fp = f_ref[rows, :].astype(F32)
        f = lb + (1.0 - lb) * jax.nn.sigmoid(fp)
        lf = jnp.log(f)
        kk = 1.0 - f
        hi, lo = _split2(lf)
        e_all = _dot(mcat, jnp.concatenate([hi, lo], axis=0))
        b = e_all[0:HG_CHUNK]
        qs = q_ref[rows, :].astype(F32) * scale
        q_in = (qs * jnp.exp(b)).astype(BF16)
        k_out = (kk * jnp.exp(e_all[HG_CHUNK:2 * HG_CHUNK])).astype(BF16)
        dec = jnp.exp(b[HG_CHUNK - 1:HG_CHUNK, :])
        qb = qs.astype(BF16)
        kb = kk.astype(BF16)
        v = i_ref[rows, :]
        g = g_ref[rows, :].astype(F32)
        for h in range(HG_HEADS):
            ls = slice(h * HG_DK, (h + 1) * HG_DK)
            a = jnp.zeros((HG_CHUNK, HG_CHUNK), F32)
            for lvl in range(HG_LEVELS):
                e = e_all[(2 + lvl) * HG_CHUNK:(3 + lvl) * HG_CHUNK, ls]
                x = jnp.exp(-jnp.abs(e))
                d = _dot_nt((qs[:, ls] * x).astype(BF16), (kk[:, ls] * x).astype(BF16))
                a = jnp.where(level == lvl, d, a)
            a = jnp.where(level == HG_LEVELS, _dot_nt(qb[:, ls], kb[:, ls]), a)
            st = st_ref[h]
            o = _dot(a.astype(BF16), v[:, ls]) + _dot_nt(q_in[:, ls], st.astype(BF16))
            st_ref[h] = st * dec[:, ls] + _dot_tn(v[:, ls], k_out[:, ls])
            var = jnp.mean(o * o, axis=-1, keepdims=True)
            gh = g[:, ls]
            y = o * lax.rsqrt(var + NORM_EPS) * nw * (gh * jax.nn.sigmoid(gh))
            o_ref[rows, ls] = y.astype(o_ref.dtype)
        return carry

    lax.fori_loop(0, HG_NCHUNK, chunk, 0)


def _hgrn(proj3d, lb, norm_w):
    bsz, s, _ = proj3d.shape
    mcat, level = _hgrn_constants()

    def col(j):
        return pl.BlockSpec((None, HG_ROWS, HG_WIDTH), lambda b, i, j=j: (b, i, j))

    return pl.pallas_call(
        _hgrn_kernel,
        out_shape=jax.ShapeDtypeStruct((bsz, s, HG_WIDTH), BF16),
        grid=(bsz, s // HG_ROWS),
        in_specs=[
            col(0), col(1), col(2), col(3),
            pl.BlockSpec((1, HG_WIDTH), lambda b, i: (0, 0)),
            pl.BlockSpec((1, HG_DK), lambda b, i: (0, 0)),
            pl.BlockSpec((HG_E_ROWS, 2 * HG_CHUNK), lambda b, i: (0, 0)),
            pl.BlockSpec((HG_CHUNK, HG_CHUNK), lambda b, i: (0, 0)),
        ],
        out_specs=pl.BlockSpec((None, HG_ROWS, HG_WIDTH), lambda b, i: (b, i, 0)),
        scratch_shapes=[pltpu.VMEM((HG_HEADS, HG_DK, HG_DK), F32)],
        compiler_params=pltpu.CompilerParams(
            dimension_semantics=("arbitrary", "arbitrary"), vmem_limit_bytes=VMEM_LIMIT),
        name="hgrn2",
    )(proj3d, proj3d, proj3d, proj3d, lb, norm_w,
      jnp.asarray(mcat, BF16), jnp.asarray(level))


AT_BLK = 256
PREP_ROWS = 512
PREP_SUB = PREP_ROWS // AT_BLK
AUG_PIECES = 3


def _aug_base(h):
    return h * LANE + (FX_DH if h % 2 == 0 else 0)


def _prep_constants():
    grp = np.zeros((FX_WIDTH, FX_WIDTH), np.float32)
    for h in range(FX_HEADS):
        grp[h * FX_DH:(h + 1) * FX_DH, h * FX_DH:(h + 1) * FX_DH] = 1.0 / FX_DH
    tri = np.tril(np.ones((AT_BLK, AT_BLK), np.float32))
    width = FX_HEADS * LANE
    pq = np.zeros((AUG_PIECES * LANE, width), np.float32)
    pk = np.zeros((AUG_PIECES * LANE, width), np.float32)
    cq = np.zeros((1, width), np.float32)
    ck = np.zeros((1, width), np.float32)
    for h in range(FX_HEADS):
        base = _aug_base(h)
        for p in range(AUG_PIECES):
            pq[p * LANE + h, base + p] = 1.0
            ck[0, base + p] = 1.0
            pk[p * LANE + h, base + AUG_PIECES + p] = -1.0
            cq[0, base + AUG_PIECES + p] = 1.0
    return grp, tri, pq, pk, cq, ck


def _prep_kernel(q_ref, k_ref, v_ref, ff_ref, fb_ref, qnw_ref, knw_ref,
                 grp_ref, tri_ref, pq_ref, pk_ref, cq_ref, ck_ref,
                 qa_ref, ka_ref, vt_ref, r_ref, carry_ref):
    @pl.when(pl.program_id(1) == 0)
    def _():
        carry_ref[...] = jnp.zeros_like(carry_ref)

    z = ff_ref[...] + fb_ref[...]
    lf = jnp.minimum(z, 0.0) - jnp.log(1.0 + jnp.exp(-jnp.abs(z)))
    grp = grp_ref[...]
    tri = tri_ref[...]
    lane = lax.broadcasted_iota(jnp.int32, (AT_BLK, LANE), 1)
    low_half = lane < FX_DH

    def qk_norm(t, w):
        ms = _dot((t * t).astype(BF16), grp)
        return t * lax.rsqrt(ms + NORM_EPS) * w

    for j in range(PREP_SUB):
        rows = slice(j * AT_BLK, (j + 1) * AT_BLK)
        hi, mid, lo = _split3(lf[rows])
        c_rel = _dot(tri, hi) + _dot(tri, mid) + _dot(tri, lo)
        r_ref[j] = carry_ref[...]
        carry_ref[...] = carry_ref[...] + c_rel[AT_BLK - 1:AT_BLK, :]
        pieces = jnp.concatenate(_split3(c_rel), axis=1)
        aug_q = _dot(pieces, pq_ref[...]) + cq_ref[...]
        aug_k = _dot(pieces, pk_ref[...]) + ck_ref[...]
        qn = qk_norm(q_ref[rows, :].astype(F32), qnw_ref[...]) * (FX_DH ** -0.5)
        kn = qk_norm(k_ref[rows, :].astype(F32), knw_ref[...])
        for h in range(FX_HEADS):
            pair = slice((h // 2) * LANE, (h // 2 + 1) * LANE)
            head = slice(h * LANE, (h + 1) * LANE)
            data = low_half if h % 2 == 0 else jnp.logical_not(low_half)
            qa_ref[rows, head] = jnp.where(data, qn[:, pair], aug_q[:, head]).astype(BF16)
            ka_ref[rows, head] = jnp.where(data, kn[:, pair], aug_k[:, head]).astype(BF16)
        vt_ref[j] = v_ref[rows, :].astype(F32).T.astype(BF16)


def _attn_prep(proj3d, ff3d, f_bias, qnw, knw):
    bsz, s, _ = proj3d.shape
    nblk = s // AT_BLK
    consts = _prep_constants()
    grp, tri, pq, pk = (jnp.asarray(c, BF16) for c in consts[:4])
    cq, ck = (jnp.asarray(c, F32) for c in consts[4:])
    width = FX_HEADS * LANE

    def col(j):
        return pl.BlockSpec((None, PREP_ROWS, FX_WIDTH), lambda b, i, j=j: (b, i, j))

    def full(shape):
        return pl.BlockSpec(shape, lambda b, i: (0,) * len(shape))

    return pl.pallas_call(
        _prep_kernel,
        out_shape=(jax.ShapeDtypeStruct((bsz, s, width), BF16),
                   jax.ShapeDtypeStruct((bsz, s, width), BF16),
                   jax.ShapeDtypeStruct((bsz, nblk, FX_WIDTH, AT_BLK), BF16),
                   jax.ShapeDtypeStruct((bsz, nblk, 8, LANE), F32)),
        grid=(bsz, s // PREP_ROWS),
        in_specs=[
            col(4), col(5), col(6),
            pl.BlockSpec((None, PREP_ROWS, LANE), lambda b, i: (b, i, 0)),
            full((1, LANE)), full((1, FX_WIDTH)), full((1, FX_WIDTH)),
            full((FX_WIDTH, FX_WIDTH)), full((AT_BLK, AT_BLK)),
            full((AUG_PIECES * LANE, width)), full((AUG_PIECES * LANE, width)),
            full((1, width)), full((1, width)),
        ],
        out_specs=(pl.BlockSpec((None, PREP_ROWS, width), lambda b, i: (b, i, 0)),
                   pl.BlockSpec((None, PREP_ROWS, width), lambda b, i: (b, i, 0)),
                   pl.BlockSpec((None, PREP_SUB, FX_WIDTH, AT_BLK), lambda b, i: (b, i, 0, 0)),
                   pl.BlockSpec((None, PREP_SUB, 8, LANE), lambda b, i: (b, i, 0, 0))),
        scratch_shapes=[pltpu.VMEM((8, LANE), F32)],
        compiler_params=pltpu.CompilerParams(
            dimension_semantics=("arbitrary", "arbitrary"), vmem_limit_bytes=VMEM_LIMIT),
        name="attn_prep",
    )(proj3d, proj3d, proj3d, ff3d, f_bias, qnw, knw, grp, tri, pq, pk, cq, ck)


def _attn_kernel(r_ref, q_ref, k_ref, vt_ref, o_ref, *, nblk):
    b = pl.program_id(0)
    h = pl.program_id(1)
    qi = pl.program_id(2)
    base = (b * FX_HEADS + h) * nblk
    q = q_ref[...]
    r_q = r_ref[base + qi]

    def block(ki, carry, masked):
        m, l, acc = carry
        k = k_ref[pl.ds(pl.multiple_of(ki * AT_BLK, AT_BLK), AT_BLK), :]
        s = _dot_nt(k, q)
        if masked:
            key = lax.broadcasted_iota(jnp.int32, s.shape, 0)
            qry = lax.broadcasted_iota(jnp.int32, s.shape, 1)
            s = jnp.where(key <= qry, s, NEG_BIG)
        delta = r_q - r_ref[base + ki]
        m_new = jnp.maximum(m, jnp.max(s, axis=0, keepdims=True) + delta)
        p = jnp.exp(s - (m_new - delta))
        alpha = jnp.exp(m - m_new)
        l = alpha * l + jnp.sum(p, axis=0, keepdims=True)
        acc = alpha * acc + _dot(vt_ref[ki], p.astype(BF16))
        return m_new, l, acc

    init = (jnp.full((1, AT_BLK), NEG_BIG, F32),
            jnp.zeros((1, AT_BLK), F32),
            jnp.zeros((FX_DH, AT_BLK), F32))
    carry = lax.fori_loop(0, qi, functools.partial(block, masked=False), init)
    _, l, acc = block(qi, carry, masked=True)
    o_ref[...] = (acc / l).astype(o_ref.dtype)


def _attention(r_flat, q_aug, k_aug, v_t):
    bsz, s, _ = q_aug.shape
    nblk = s // AT_BLK
    return pl.pallas_call(
        functools.partial(_attn_kernel, nblk=nblk),
        out_shape=jax.ShapeDtypeStruct((bsz, FX_WIDTH, s), BF16),
        grid=(bsz, FX_HEADS, nblk),
        in_specs=[
            pl.BlockSpec(memory_space=pltpu.SMEM),
            pl.BlockSpec((None, AT_BLK, LANE), lambda b, h, i: (b, i, h)),
            pl.BlockSpec((None, s, LANE), lambda b, h, i: (b, 0, h)),
            pl.BlockSpec((None, nblk, FX_DH, AT_BLK), lambda b, h, i: (b, 0, h, 0)),
        ],
        out_specs=pl.BlockSpec((None, FX_DH, AT_BLK), lambda b, h, i: (b, h, i)),
        compiler_params=pltpu.CompilerParams(
            dimension_semantics=("arbitrary", "arbitrary", "arbitrary"),
            vmem_limit_bytes=VMEM_LIMIT),
        name="fox_attention",
    )(r_flat, q_aug, k_aug, v_t)


FF_TM = 1024
FF_TF = 1024


def _ffn_kernel(x_ref, oa_ref, obt_ref, woa_ref, wob_ref, nw_ref, wu_ref, wd_ref,
                o_ref, h_ref):
    @pl.when(pl.program_id(2) == 0)
    def _():
        x1 = (x_ref[...] + _dot(oa_ref[...], woa_ref[...])
              + _dot_tn(obt_ref[...], wob_ref[...]))
        ms = jnp.mean(x1 * x1, axis=-1, keepdims=True)
        h_ref[...] = (x1 * lax.rsqrt(ms + NORM_EPS) * nw_ref[...]).astype(BF16)
        o_ref[...] = x1

    u = jnp.maximum(_dot(h_ref[...], wu_ref[...]), 0.0)
    o_ref[...] += _dot((u * u).astype(BF16), wd_ref[...])


def _outproj_ffn(x3d, o_a, o_bt, w_out, norm_w, w_up, w_down):
    bsz, s, _ = x3d.shape
    return pl.pallas_call(
        _ffn_kernel,
        out_shape=jax.ShapeDtypeStruct(x3d.shape, F32),
        grid=(bsz, s // FF_TM, D_FF // FF_TF),
        in_specs=[
            pl.BlockSpec((None, FF_TM, D_MODEL), lambda b, i, j: (b, i, 0)),
            pl.BlockSpec((None, FF_TM, HG_WIDTH), lambda b, i, j: (b, i, 0)),
            pl.BlockSpec((None, FX_WIDTH, FF_TM), lambda b, i, j: (b, 0, i)),
            pl.BlockSpec((HG_WIDTH, D_MODEL), lambda b, i, j: (0, 0)),
            pl.BlockSpec((FX_WIDTH, D_MODEL), lambda b, i, j: (1, 0)),
            pl.BlockSpec((1, D_MODEL), lambda b, i, j: (0, 0)),
            pl.BlockSpec((D_MODEL, FF_TF), lambda b, i, j: (0, j)),
            pl.BlockSpec((FF_TF, D_MODEL), lambda b, i, j: (j, 0)),
        ],
        out_specs=pl.BlockSpec((None, FF_TM, D_MODEL), lambda b, i, j: (b, i, 0)),
        scratch_shapes=[pltpu.VMEM((FF_TM, D_MODEL), BF16)],
        compiler_params=pltpu.CompilerParams(
            dimension_semantics=("arbitrary", "arbitrary", "arbitrary"),
            vmem_limit_bytes=VMEM_LIMIT),
        name="outproj_ffn",
    )(x3d, o_a, o_bt, w_out, w_out, norm_w, w_up, w_down)


def _layer_lower_bounds(lower_bounds):
    p = jax.nn.softmax(lower_bounds.astype(F32), axis=0)
    c = jnp.cumsum(p, axis=0)
    return c - c[0:1]


def kernel(x, lower_bounds, norm1_w, w_in, fox_f_bias, q_norm_w, k_norm_w,
           hgrn_norm_w, w_out, norm2_w, w_up, w_down):
    bsz, s, d = x.shape
    depth = w_in.shape[0]
    nblk = s // AT_BLK
    lbs = _layer_lower_bounds(lower_bounds)
    w_in_pad = jnp.pad(w_in, ((0, 0), (0, 0), (0, PROJ_PAD - w_in.shape[-1]))).astype(BF16)
    w_out_b = w_out.astype(BF16)
    w_up_b = w_up.astype(BF16)
    w_down_b = w_down.astype(BF16)
    f_bias = jnp.pad(fox_f_bias, ((0, 0), (0, LANE - FX_HEADS)))

    for l in range(depth):
        proj, ff = _inproj(x.reshape(bsz * s, d), norm1_w[l][None, :], w_in_pad[l])
        proj3d = proj.reshape(bsz, s, PROJ_MAIN)
        ff3d = ff.reshape(bsz, s, LANE)
        o_a = _hgrn(proj3d, lbs[l][None, :], hgrn_norm_w[l][None, :])
        q_aug, k_aug, v_t, r_blk = _attn_prep(
            proj3d, ff3d, f_bias[l][None, :],
            jnp.tile(q_norm_w[l], FX_HEADS)[None, :],
            jnp.tile(k_norm_w[l], FX_HEADS)[None, :])
        r_flat = jnp.transpose(r_blk[:, :, 0, :FX_HEADS], (0, 2, 1)).reshape(-1)
        o_bt = _attention(r_flat, q_aug, k_aug, v_t)
        x = _outproj_ffn(x, o_a, o_bt, w_out_b[l], norm2_w[l][None, :],
                         w_up_b[l], w_down_b[l])
    return x
```

```python
import functools

import numpy as np
import jax
import jax.numpy as jnp
from jax import lax
from jax.experimental import pallas as pl
from jax.experimental.pallas import tpu as pltpu

F32 = jnp.float32
BF16 = jnp.bfloat16

D_MODEL = 1024
D_FF = 4 * D_MODEL
NORM_EPS = 1e-6

HG_WIDTH = 512
HG_HEADS = 4
HG_DK = 128
HG_CHUNK = 64
HG_LEVELS = 6

FX_WIDTH = 512
FX_HEADS = 8
FX_DH = 64

PROJ_MAIN = 4 * HG_WIDTH + 3 * FX_WIDTH
LANE = 128
PROJ_PAD = PROJ_MAIN + LANE

VMEM_LIMIT = 56 * 1024 * 1024

NEG_BIG = -1e30
LOG2E = 1.4426950408889634

NT_DIMS = (((1,), (1,)), ((), ()))
TN_DIMS = (((0,), (0,)), ((), ()))


def _dot(a, b):
    return jnp.dot(a, b, preferred_element_type=F32)


def _dot_nt(a, b):
    return lax.dot_general(a, b, NT_DIMS, preferred_element_type=F32)


def _dot_tn(a, b):
    return lax.dot_general(a, b, TN_DIMS, preferred_element_type=F32)


def _split2(x):
    hi = x.astype(BF16)
    lo = (x - hi.astype(F32)).astype(BF16)
    return hi, lo


def _split3(x):
    hi = x.astype(BF16)
    r = x - hi.astype(F32)
    mid = r.astype(BF16)
    lo = (r - mid.astype(F32)).astype(BF16)
    return hi, mid, lo


IN_TM = 512
IN_TN = 512


def _inproj_kernel(x_ref, nw_ref, w_ref, proj_ref, ff_ref):
    x = x_ref[...]
    ms = jnp.mean(x * x, axis=-1, keepdims=True)
    h = (x * lax.rsqrt(ms + NORM_EPS) * nw_ref[...]).astype(BF16)
    for c in range(PROJ_MAIN // IN_TN):
        cols = slice(c * IN_TN, (c + 1) * IN_TN)
        proj_ref[:, cols] = _dot(h, w_ref[:, cols]).astype(BF16)
    ff_ref[...] = _dot(h, w_ref[:, PROJ_MAIN:PROJ_PAD])


def _inproj(x2d, norm_w, w_pad):
    t = x2d.shape[0]
    return pl.pallas_call(
        _inproj_kernel,
        out_shape=(jax.ShapeDtypeStruct((t, PROJ_MAIN), BF16),
                   jax.ShapeDtypeStruct((t, LANE), F32)),
        grid=(t // IN_TM,),
        in_specs=[
            pl.BlockSpec((IN_TM, D_MODEL), lambda i: (i, 0)),
            pl.BlockSpec((1, D_MODEL), lambda i: (0, 0)),
            pl.BlockSpec((D_MODEL, PROJ_PAD), lambda i: (0, 0)),
        ],
        out_specs=(pl.BlockSpec((IN_TM, PROJ_MAIN), lambda i: (i, 0)),
                   pl.BlockSpec((IN_TM, LANE), lambda i: (i, 0))),
        compiler_params=pltpu.CompilerParams(
            dimension_semantics=("arbitrary",), vmem_limit_bytes=VMEM_LIMIT),
        name="inproj",
    )(x2d, norm_w, w_pad)


HG_ROWS = 512
HG_NCHUNK = HG_ROWS // HG_CHUNK
HG_E_ROWS = (2 + HG_LEVELS) * HG_CHUNK


def _hgrn_constants():
    c = HG_CHUNK
    tri = np.tril(np.ones((c, c), np.float32))
    rows = [tri, 1.0 - tri]
    idx = np.arange(c)
    for lvl in range(HG_LEVELS):
        m = c >> (lvl + 1)
        ref = (idx // (2 * m)) * (2 * m) + m - 1
        rows.append(tri - tri[ref])
    mat = np.concatenate(rows, axis=0)
    mcat = np.concatenate([mat, mat], axis=1)
    x = idx[:, None] ^ idx[None, :]
    top = np.floor(np.log2(np.maximum(x, 1))).astype(np.int32)
    level = (HG_LEVELS - 1) - top
    level = np.where(idx[:, None] == idx[None, :], HG_LEVELS, level)
    level = np.where(idx[:, None] < idx[None, :], -1, level)
    return mcat.astype(np.float32), level.astype(np.int32)


def _hgrn_kernel(q_ref, f_ref, i_ref, g_ref, lb_ref, nw_ref, mcat_ref, lvl_ref,
                 o_ref, st_ref):
    @pl.when(pl.program_id(1) == 0)
    def _():
        st_ref[...] = jnp.zeros_like(st_ref)

    lb = lb_ref[...]
    nw = nw_ref[...]
    mcat = mcat_ref[...]
    level = lvl_ref[...]
    scale = HG_DK ** -0.5

    def chunk(c, carry):
        r0 = pl.multiple_of(c * HG_CHUNK, HG_CHUNK)
        rows = pl.ds(r0, HG_CHUNK)
        fp = f_ref[rows, :].astype(F32)
        f = lb + (1.0 - lb) * jax.nn.sigmoid(fp)
        lf = jnp.log(f)
        kk = 1.0 - f
        hi, lo = _split2(lf)
        e_all = _dot(mcat, jnp.concatenate([hi, lo], axis=0))
        b = e_all[0:HG_CHUNK]
        qs = q_ref[rows, :].astype(F32) * scale
        q_in = (qs * jnp.exp(b)).astype(BF16)
        k_out = (kk * jnp.exp(e_all[HG_CHUNK:2 * HG_CHUNK])).astype(BF16)
        dec = jnp.exp(b[HG_CHUNK - 1:HG_CHUNK, :])
        qb = qs.astype(BF16)
        kb = kk.astype(BF16)
        v = i_ref[rows, :]
        g = g_ref[rows, :].astype(F32)
        for h in range(HG_HEADS):
            ls = slice(h * HG_DK, (h + 1) * HG_DK)
            a = jnp.zeros((HG_CHUNK, HG_CHUNK), F32)
            for lvl in range(HG_LEVELS):
                e = e_all[(2 + lvl) * HG_CHUNK:(3 + lvl) * HG_CHUNK, ls]
                x = jnp.exp(-jnp.abs(e))
                d = _dot_nt((qs[:, ls] * x).astype(BF16), (kk[:, ls] * x).astype(BF16))
                a = jnp.where(level == lvl, d, a)
            a = jnp.where(level == HG_LEVELS, _dot_nt(qb[:, ls], kb[:, ls]), a)
            st = st_ref[h]
            o = _dot(a.astype(BF16), v[:, ls]) + _dot_nt(q_in[:, ls], st.astype(BF16))
            st_ref[h] = st * dec[:, ls] + _dot_tn(v[:, ls], k_out[:, ls])
            var = jnp.mean(o * o, axis=-1, keepdims=True)
            gh = g[:, ls]
            y = o * lax.rsqrt(var + NORM_EPS) * nw * (gh * jax.nn.sigmoid(gh))
            o_ref[rows, ls] = y.astype(o_ref.dtype)
        return carry

    lax.fori_loop(0, HG_NCHUNK, chunk, 0)


def _hgrn(proj3d, lb, norm_w):
    bsz, s, _ = proj3d.shape
    mcat, level = _hgrn_constants()

    def col(j):
        return pl.BlockSpec((None, HG_ROWS, HG_WIDTH), lambda b, i, j=j: (b, i, j))

    return pl.pallas_call(
        _hgrn_kernel,
        out_shape=jax.ShapeDtypeStruct((bsz, s, HG_WIDTH), BF16),
        grid=(bsz, s // HG_ROWS),
        in_specs=[
            col(0), col(1), col(2), col(3),
            pl.BlockSpec((1, HG_WIDTH), lambda b, i: (0, 0)),
            pl.BlockSpec((1, HG_DK), lambda b, i: (0, 0)),
            pl.BlockSpec((HG_E_ROWS, 2 * HG_CHUNK), lambda b, i: (0, 0)),
            pl.BlockSpec((HG_CHUNK, HG_CHUNK), lambda b, i: (0, 0)),
        ],
        out_specs=pl.BlockSpec((None, HG_ROWS, HG_WIDTH), lambda b, i: (b, i, 0)),
        scratch_shapes=[pltpu.VMEM((HG_HEADS, HG_DK, HG_DK), F32)],
        compiler_params=pltpu.CompilerParams(
            dimension_semantics=("arbitrary", "arbitrary"), vmem_limit_bytes=VMEM_LIMIT),
        name="hgrn2",
    )(proj3d, proj3d, proj3d, proj3d, lb, norm_w,
      jnp.asarray(mcat, BF16), jnp.asarray(level))


AT_BLK = 256
PREP_ROWS = 512
PREP_SUB = PREP_ROWS // AT_BLK
AUG_PIECES = 3


def _aug_base(h):
    return h * LANE + (FX_DH if h % 2 == 0 else 0)


def _prep_constants():
    grp = np.zeros((FX_WIDTH, FX_WIDTH), np.float32)
    for h in range(FX_HEADS):
        grp[h * FX_DH:(h + 1) * FX_DH, h * FX_DH:(h + 1) * FX_DH] = 1.0 / FX_DH
    tri = np.tril(np.ones((AT_BLK, AT_BLK), np.float32))
    width = FX_HEADS * LANE
    pq = np.zeros((AUG_PIECES * LANE, width), np.float32)
    pk = np.zeros((AUG_PIECES * LANE, width), np.float32)
    cq = np.zeros((1, width), np.float32)
    ck = np.zeros((1, width), np.float32)
    for h in range(FX_HEADS):
        base = _aug_base(h)
        for p in range(AUG_PIECES):
            pq[p * LANE + h, base + p] = 1.0
            ck[0, base + p] = 1.0
            pk[p * LANE + h, base + AUG_PIECES + p] = -1.0
            cq[0, base + AUG_PIECES + p] = 1.0
    return grp, tri, pq, pk, cq, ck


def _prep_kernel(q_ref, k_ref, v_ref, ff_ref, fb_ref, qnw_ref, knw_ref,
                 grp_ref, tri_ref, pq_ref, pk_ref, cq_ref, ck_ref,
                 qa_ref, ka_ref, vt_ref, r_ref, carry_ref):
    @pl.when(pl.program_id(1) == 0)
    def _():
        carry_ref[...] = jnp.zeros_like(carry_ref)

    z = ff_ref[...] + fb_ref[...]
    lf = (jnp.minimum(z, 0.0) - jnp.log(1.0 + jnp.exp(-jnp.abs(z)))) * LOG2E
    grp = grp_ref[...]
    tri = tri_ref[...]
    lane = lax.broadcasted_iota(jnp.int32, (AT_BLK, LANE), 1)
    low_half = lane < FX_DH

    def qk_norm(t, w):
        ms = _dot((t * t).astype(BF16), grp)
        return t * lax.rsqrt(ms + NORM_EPS) * w

    for j in range(PREP_SUB):
        rows = slice(j * AT_BLK, (j + 1) * AT_BLK)
        hi, mid, lo = _split3(lf[rows])
        c_rel = _dot(tri, hi) + _dot(tri, mid) + _dot(tri, lo)
        r_ref[j] = carry_ref[...]
        carry_ref[...] = carry_ref[...] + c_rel[AT_BLK - 1:AT_BLK, :]
        pieces = jnp.concatenate(_split3(c_rel), axis=1)
        aug_q = _dot(pieces, pq_ref[...]) + cq_ref[...]
        aug_k = _dot(pieces, pk_ref[...]) + ck_ref[...]
        qn = qk_norm(q_ref[rows, :].astype(F32), qnw_ref[...]) * (FX_DH ** -0.5 * LOG2E)
        kn = qk_norm(k_ref[rows, :].astype(F32), knw_ref[...])
        for h in range(FX_HEADS):
            pair = slice((h // 2) * LANE, (h // 2 + 1) * LANE)
            head = slice(h * LANE, (h + 1) * LANE)
            data = low_half if h % 2 == 0 else jnp.logical_not(low_half)
            qa_ref[rows, head] = jnp.where(data, qn[:, pair], aug_q[:, head]).astype(BF16)
            ka_ref[rows, head] = jnp.where(data, kn[:, pair], aug_k[:, head]).astype(BF16)
        vt_ref[j] = v_ref[rows, :].astype(F32).T.astype(BF16)


def _attn_prep(proj3d, ff3d, f_bias, qnw, knw):
    bsz, s, _ = proj3d.shape
    nblk = s // AT_BLK
    consts = _prep_constants()
    grp, tri, pq, pk = (jnp.asarray(c, BF16) for c in consts[:4])
    cq, ck = (jnp.asarray(c, F32) for c in consts[4:])
    width = FX_HEADS * LANE

    def col(j):
        return pl.BlockSpec((None, PREP_ROWS, FX_WIDTH), lambda b, i, j=j: (b, i, j))

    def full(shape):
        return pl.BlockSpec(shape, lambda b, i: (0,) * len(shape))

    return pl.pallas_call(
        _prep_kernel,
        out_shape=(jax.ShapeDtypeStruct((bsz, s, width), BF16),
                   jax.ShapeDtypeStruct((bsz, s, width), BF16),
                   jax.ShapeDtypeStruct((bsz, nblk, FX_WIDTH, AT_BLK), BF16),
                   jax.ShapeDtypeStruct((bsz, nblk, 8, LANE), F32)),
        grid=(bsz, s // PREP_ROWS),
        in_specs=[
            col(4), col(5), col(6),
            pl.BlockSpec((None, PREP_ROWS, LANE), lambda b, i: (b, i, 0)),
            full((1, LANE)), full((1, FX_WIDTH)), full((1, FX_WIDTH)),
            full((FX_WIDTH, FX_WIDTH)), full((AT_BLK, AT_BLK)),
            full((AUG_PIECES * LANE, width)), full((AUG_PIECES * LANE, width)),
            full((1, width)), full((1, width)),
        ],
        out_specs=(pl.BlockSpec((None, PREP_ROWS, width), lambda b, i: (b, i, 0)),
                   pl.BlockSpec((None, PREP_ROWS, width), lambda b, i: (b, i, 0)),
                   pl.BlockSpec((None, PREP_SUB, FX_WIDTH, AT_BLK), lambda b, i: (b, i, 0, 0)),
                   pl.BlockSpec((None, PREP_SUB, 8, LANE), lambda b, i: (b, i, 0, 0))),
        scratch_shapes=[pltpu.VMEM((8, LANE), F32)],
        compiler_params=pltpu.CompilerParams(
            dimension_semantics=("arbitrary", "arbitrary"), vmem_limit_bytes=VMEM_LIMIT),
        name="attn_prep",
    )(proj3d, proj3d, proj3d, ff3d, f_bias, qnw, knw, grp, tri, pq, pk, cq, ck)


AT_HG = 2


def _attn_kernel(r_ref, q_ref, k_ref, vt_ref, o_ref,
                 s_ref, p_ref, acc_ref, m_ref, l_ref, a_ref, *, nblk):
    b = pl.program_id(0)
    hg = pl.program_id(1)
    qi = pl.program_id(2)


    def scores(g, ki):
        lanes = slice(g * LANE, (g + 1) * LANE)
        k = k_ref[pl.ds(pl.multiple_of(ki * AT_BLK, AT_BLK), AT_BLK), lanes]
        return _dot_nt(k, q_ref[:, lanes])

    def softmax_step(g, ki, s):
        base = (b * FX_HEADS + hg * AT_HG + g) * nblk
        delta = r_ref[base + qi] - r_ref[base + ki]
        m = m_ref[g]
        m_new = jnp.maximum(m, jnp.max(s, axis=0, keepdims=True) + delta)
        p = jnp.exp2(s - (m_new - delta))
        alpha = jnp.exp2(m - m_new)
        l_ref[g] = alpha * l_ref[g] + jnp.sum(p, axis=0, keepdims=True)
        m_ref[g] = m_new
        return p.astype(BF16), alpha

    def pv(g, ki, p, alpha):
        v_t = vt_ref[ki, g * FX_DH:(g + 1) * FX_DH, :]
        acc_ref[g] = alpha * acc_ref[g] + _dot(v_t, p)

    def step(slot, ki):
        other = 1 - slot
        for g in range(AT_HG):
            s_ref[other, g] = scores(g, ki + 1)
            pv(g, jnp.maximum(ki - 1, 0), p_ref[other, g], a_ref[g])
            p, alpha = softmax_step(g, ki, s_ref[slot, g])
            p_ref[slot, g] = p
            a_ref[g] = alpha

    def last_step(slot):
        other = 1 - slot
        key = lax.broadcasted_iota(jnp.int32, (AT_BLK, AT_BLK), 0)
        qry = lax.broadcasted_iota(jnp.int32, (AT_BLK, AT_BLK), 1)
        for g in range(AT_HG):
            pv(g, jnp.maximum(qi - 1, 0), p_ref[other, g], a_ref[g])
            s = jnp.where(key <= qry, s_ref[slot, g], NEG_BIG)
            p, alpha = softmax_step(g, qi, s)
            pv(g, qi, p, alpha)
            o_ref[g * FX_DH:(g + 1) * FX_DH, :] = (acc_ref[g] / l_ref[g]).astype(o_ref.dtype)

    start = 0
    for g in range(AT_HG):
        s_ref[0, g] = scores(g, start)
        p_ref[1, g] = jnp.zeros((AT_BLK, AT_BLK), BF16)
        a_ref[g] = jnp.ones((1, AT_BLK), F32)
        m_ref[g] = jnp.full((1, AT_BLK), NEG_BIG, F32)
        l_ref[g] = jnp.zeros((1, AT_BLK), F32)
        acc_ref[g] = jnp.zeros((FX_DH, AT_BLK), F32)

    n_full = qi - start
    n_pairs = n_full // 2

    def pair(j, carry):
        ki = start + 2 * j
        step(0, ki)
        step(1, ki + 1)
        return carry

    lax.fori_loop(0, n_pairs, pair, 0)

    @pl.when(n_full % 2 == 0)
    def _():
        last_step(0)

    @pl.when(n_full % 2 == 1)
    def _():
        step(0, qi - 1)
        last_step(1)


def _attention(r_flat, q_aug, k_aug, v_t):
    bsz, s, _ = q_aug.shape
    nblk = s // AT_BLK
    return pl.pallas_call(
        functools.partial(_attn_kernel, nblk=nblk),
        out_shape=jax.ShapeDtypeStruct((bsz, FX_WIDTH, s), BF16),
        grid=(bsz, FX_HEADS // AT_HG, nblk),
        in_specs=[
            pl.BlockSpec(memory_space=pltpu.SMEM),
            pl.BlockSpec((None, AT_BLK, AT_HG * LANE), lambda b, h, i: (b, i, h)),
            pl.BlockSpec((None, s, AT_HG * LANE), lambda b, h, i: (b, 0, h)),
            pl.BlockSpec((None, nblk, AT_HG * FX_DH, AT_BLK), lambda b, h, i: (b, 0, h, 0)),
        ],
        out_specs=pl.BlockSpec((None, AT_HG * FX_DH, AT_BLK), lambda b, h, i: (b, h, i)),
        scratch_shapes=[
            pltpu.VMEM((2, AT_HG, AT_BLK, AT_BLK), F32),
            pltpu.VMEM((2, AT_HG, AT_BLK, AT_BLK), BF16),
            pltpu.VMEM((AT_HG, FX_DH, AT_BLK), F32),
            pltpu.VMEM((AT_HG, 1, AT_BLK), F32),
            pltpu.VMEM((AT_HG, 1, AT_BLK), F32),
            pltpu.VMEM((AT_HG, 1, AT_BLK), F32),
        ],
        compiler_params=pltpu.CompilerParams(
            dimension_semantics=("arbitrary", "arbitrary", "arbitrary"),
            vmem_limit_bytes=VMEM_LIMIT),
        name="fox_attention",
    )(r_flat, q_aug, k_aug, v_t)


FF_TM = 1024
FF_TF = 1024


def _ffn_kernel(x_ref, oa_ref, obt_ref, woa_ref, wob_ref, nw_ref, wu_ref, wd_ref,
                o_ref, h_ref):
    @pl.when(pl.program_id(2) == 0)
    def _():
        x1 = (x_ref[...] + _dot(oa_ref[...], woa_ref[...])
              + _dot_tn(obt_ref[...], wob_ref[...]))
        ms = jnp.mean(x1 * x1, axis=-1, keepdims=True)
        h_ref[...] = (x1 * lax.rsqrt(ms + NORM_EPS) * nw_ref[...]).astype(BF16)
        o_ref[...] = x1

    u = jnp.maximum(_dot(h_ref[...], wu_ref[...]), 0.0)
    o_ref[...] += _dot((u * u).astype(BF16), wd_ref[...])


def _outproj_ffn(x3d, o_a, o_bt, w_out, norm_w, w_up, w_down):
    bsz, s, _ = x3d.shape
    return pl.pallas_call(
        _ffn_kernel,
        out_shape=jax.ShapeDtypeStruct(x3d.shape, F32),
        grid=(bsz, s // FF_TM, D_FF // FF_TF),
        in_specs=[
            pl.BlockSpec((None, FF_TM, D_MODEL), lambda b, i, j: (b, i, 0)),
            pl.BlockSpec((None, FF_TM, HG_WIDTH), lambda b, i, j: (b, i, 0)),
            pl.BlockSpec((None, FX_WIDTH, FF_TM), lambda b, i, j: (b, 0, i)),
            pl.BlockSpec((HG_WIDTH, D_MODEL), lambda b, i, j: (0, 0)),
            pl.BlockSpec((FX_WIDTH, D_MODEL), lambda b, i, j: (1, 0)),
            pl.BlockSpec((1, D_MODEL), lambda b, i, j: (0, 0)),
            pl.BlockSpec((D_MODEL, FF_TF), lambda b, i, j: (0, j)),
            pl.BlockSpec((FF_TF, D_MODEL), lambda b, i, j: (j, 0)),
        ],
        out_specs=pl.BlockSpec((None, FF_TM, D_MODEL), lambda b, i, j: (b, i, 0)),
        scratch_shapes=[pltpu.VMEM((FF_TM, D_MODEL), BF16)],
        compiler_params=pltpu.CompilerParams(
            dimension_semantics=("arbitrary", "arbitrary", "arbitrary"),
            vmem_limit_bytes=VMEM_LIMIT),
        name="outproj_ffn",
    )(x3d, o_a, o_bt, w_out, w_out, norm_w, w_up, w_down)


def _layer_lower_bounds(lower_bounds):
    p = jax.nn.softmax(lower_bounds.astype(F32), axis=0)
    c = jnp.cumsum(p, axis=0)
    return c - c[0:1]


def kernel(x, lower_bounds, norm1_w, w_in, fox_f_bias, q_norm_w, k_norm_w,
           hgrn_norm_w, w_out, norm2_w, w_up, w_down):
    bsz, s, d = x.shape
    depth = w_in.shape[0]
    nblk = s // AT_BLK
    lbs = _layer_lower_bounds(lower_bounds)
    w_in_pad = jnp.pad(w_in, ((0, 0), (0, 0), (0, PROJ_PAD - w_in.shape[-1]))).astype(BF16)
    w_out_b = w_out.astype(BF16)
    w_up_b = w_up.astype(BF16)
    w_down_b = w_down.astype(BF16)
    f_bias = jnp.pad(fox_f_bias, ((0, 0), (0, LANE - FX_HEADS)))

    for l in range(depth):
        proj, ff = _inproj(x.reshape(bsz * s, d), norm1_w[l][None, :], w_in_pad[l])
        proj3d = proj.reshape(bsz, s, PROJ_MAIN)
        ff3d = ff.reshape(bsz, s, LANE)
        o_a = _hgrn(proj3d, lbs[l][None, :], hgrn_norm_w[l][None, :])
        q_aug, k_aug, v_t, r_blk = _attn_prep(
            proj3d, ff3d, f_bias[l][None, :],
            jnp.tile(q_norm_w[l], FX_HEADS)[None, :],
            jnp.tile(k_norm_w[l], FX_HEADS)[None, :])
        r_flat = jnp.transpose(r_blk[:, :, 0, :FX_HEADS], (0, 2, 1)).reshape(-1)
        o_bt = _attention(r_flat, q_aug, k_aug, v_t)
        x = _outproj_ffn(x, o_a, o_bt, w_out_b[l], norm2_w[l][None, :],
                         w_up_b[l], w_down_b[l])
    return x
```

```python
import functools

import numpy as np
import jax
import jax.numpy as jnp
from jax import lax
from jax.experimental import pallas as pl
from jax.experimental.pallas import tpu as pltpu

F32 = jnp.float32
BF16 = jnp.bfloat16

D_MODEL = 1024
D_FF = 4 * D_MODEL
NORM_EPS = 1e-6

HG_WIDTH = 512
HG_HEADS = 4
HG_DK = 128
HG_CHUNK = 64
HG_LEVELS = 6

FX_WIDTH = 512
FX_HEADS = 8
FX_DH = 64

PROJ_MAIN = 4 * HG_WIDTH + 3 * FX_WIDTH
LANE = 128
PROJ_PAD = PROJ_MAIN + LANE

VMEM_LIMIT = 56 * 1024 * 1024

NEG_BIG = -1e30
LOG2E = 1.4426950408889634

NT_DIMS = (((1,), (1,)), ((), ()))
TN_DIMS = (((0,), (0,)), ((), ()))


def _dot(a, b):
    return jnp.dot(a, b, preferred_element_type=F32)


def _dot_nt(a, b):
    return lax.dot_general(a, b, NT_DIMS, preferred_element_type=F32)


def _dot_tn(a, b):
    return lax.dot_general(a, b, TN_DIMS, preferred_element_type=F32)


def _split2(x):
    hi = x.astype(BF16)
    lo = (x - hi.astype(F32)).astype(BF16)
    return hi, lo


def _split3(x):
    hi = x.astype(BF16)
    r = x - hi.astype(F32)
    mid = r.astype(BF16)
    lo = (r - mid.astype(F32)).astype(BF16)
    return hi, mid, lo


IN_TM = 512
IN_TN = 512


def _inproj_kernel(x_ref, nw_ref, w_ref, proj_ref, ff_ref):
    x = x_ref[...]
    ms = jnp.mean(x * x, axis=-1, keepdims=True)
    h = (x * lax.rsqrt(ms + NORM_EPS) * nw_ref[...]).astype(BF16)
    for c in range(PROJ_MAIN // IN_TN):
        cols = slice(c * IN_TN, (c + 1) * IN_TN)
        proj_ref[:, cols] = _dot(h, w_ref[:, cols]).astype(BF16)
    ff_ref[...] = _dot(h, w_ref[:, PROJ_MAIN:PROJ_PAD])


def _inproj(x2d, norm_w, w_pad):
    t = x2d.shape[0]
    return pl.pallas_call(
        _inproj_kernel,
        out_shape=(jax.ShapeDtypeStruct((t, PROJ_MAIN), BF16),
                   jax.ShapeDtypeStruct((t, LANE), F32)),
        grid=(t // IN_TM,),
        in_specs=[
            pl.BlockSpec((IN_TM, D_MODEL), lambda i: (i, 0)),
            pl.BlockSpec((1, D_MODEL), lambda i: (0, 0)),
            pl.BlockSpec((D_MODEL, PROJ_PAD), lambda i: (0, 0)),
        ],
        out_specs=(pl.BlockSpec((IN_TM, PROJ_MAIN), lambda i: (i, 0)),
                   pl.BlockSpec((IN_TM, LANE), lambda i: (i, 0))),
        compiler_params=pltpu.CompilerParams(
            dimension_semantics=("arbitrary",), vmem_limit_bytes=VMEM_LIMIT),
        name="inproj",
    )(x2d, norm_w, w_pad)


HG_ROWS = 512
HG_NCHUNK = HG_ROWS // HG_CHUNK
HG_E_ROWS = (2 + HG_LEVELS) * HG_CHUNK


def _hgrn_constants():
    c = HG_CHUNK
    tri = np.tril(np.ones((c, c), np.float32))
    rows = [tri, 1.0 - tri]
    idx = np.arange(c)
    for lvl in range(HG_LEVELS):
        m = c >> (lvl + 1)
        ref = (idx // (2 * m)) * (2 * m) + m - 1
        rows.append(tri - tri[ref])
    mat = np.concatenate(rows, axis=0)
    mcat = np.concatenate([mat, mat], axis=1)
    x = idx[:, None] ^ idx[None, :]
    top = np.floor(np.log2(np.maximum(x, 1))).astype(np.int32)
    level = (HG_LEVELS - 1) - top
    level = np.where(idx[:, None] == idx[None, :], HG_LEVELS, level)
    level = np.where(idx[:, None] < idx[None, :], -1, level)
    return mcat.astype(np.float32), level.astype(np.int32)


def _hgrn_kernel(q_ref, f_ref, i_ref, g_ref, lb_ref, nw_ref, mcat_ref, lvl_ref,
                 o_ref, st_ref):
    @pl.when(pl.program_id(1) == 0)
    def _():
        st_ref[...] = jnp.zeros_like(st_ref)

    lb = lb_ref[...]
    nw = nw_ref[...]
    mcat = mcat_ref[...]
    level = lvl_ref[...]
    scale = HG_DK ** -0.5

    def chunk(c, carry):
        r0 = pl.multiple_of(c * HG_CHUNK, HG_CHUNK)
        rows = pl.ds(r0, HG_CHUNK)
        fp = f_ref[rows, :].astype(F32)
        f = lb + (1.0 - lb) * jax.nn.sigmoid(fp)
        lf = jnp.log(f)
        kk = 1.0 - f
        hi, lo = _split2(lf)
        e_all = _dot(mcat, jnp.concatenate([hi, lo], axis=0))
        b = e_all[0:HG_CHUNK]
        qs = q_ref[rows, :].astype(F32) * scale
        q_in = (qs * jnp.exp(b)).astype(BF16)
        k_out = (kk * jnp.exp(e_all[HG_CHUNK:2 * HG_CHUNK])).astype(BF16)
        dec = jnp.exp(b[HG_CHUNK - 1:HG_CHUNK, :])
        qb = qs.astype(BF16)
        kb = kk.astype(BF16)
        v = i_ref[rows, :]
        g = g_ref[rows, :].astype(F32)
        for h in range(HG_HEADS):
            ls = slice(h * HG_DK, (h + 1) * HG_DK)
            a = jnp.zeros((HG_CHUNK, HG_CHUNK), F32)
            for lvl in range(HG_LEVELS):
                e = e_all[(2 + lvl) * HG_CHUNK:(3 + lvl) * HG_CHUNK, ls]
                x = jnp.exp(-jnp.abs(e))
                d = _dot_nt((qs[:, ls] * x).astype(BF16), (kk[:, ls] * x).astype(BF16))
                a = jnp.where(level == lvl, d, a)
            a = jnp.where(level == HG_LEVELS, _dot_nt(qb[:, ls], kb[:, ls]), a)
            st = st_ref[h]
            o = _dot(a.astype(BF16), v[:, ls]) + _dot_nt(q_in[:, ls], st.astype(BF16))
            st_ref[h] = st * dec[:, ls] + _dot_tn(v[:, ls], k_out[:, ls])
            var = jnp.mean(o * o, axis=-1, keepdims=True)
            gh = g[:, ls]
            y = o * lax.rsqrt(var + NORM_EPS) * nw * (gh * jax.nn.sigmoid(gh))
            o_ref[rows, ls] = y.astype(o_ref.dtype)
        return carry

    lax.fori_loop(0, HG_NCHUNK, chunk, 0)


def _hgrn(proj3d, lb, norm_w):
    bsz, s, _ = proj3d.shape
    mcat, level = _hgrn_constants()

    def col(j):
        return pl.BlockSpec((None, HG_ROWS, HG_WIDTH), lambda b, i, j=j: (b, i, j))

    return pl.pallas_call(
        _hgrn_kernel,
        out_shape=jax.ShapeDtypeStruct((bsz, s, HG_WIDTH), BF16),
        grid=(bsz, s // HG_ROWS),
        in_specs=[
            col(0), col(1), col(2), col(3),
            pl.BlockSpec((1, HG_WIDTH), lambda b, i: (0, 0)),
            pl.BlockSpec((1, HG_DK), lambda b, i: (0, 0)),
            pl.BlockSpec((HG_E_ROWS, 2 * HG_CHUNK), lambda b, i: (0, 0)),
            pl.BlockSpec((HG_CHUNK, HG_CHUNK), lambda b, i: (0, 0)),
        ],
        out_specs=pl.BlockSpec((None, HG_ROWS, HG_WIDTH), lambda b, i: (b, i, 0)),
        scratch_shapes=[pltpu.VMEM((HG_HEADS, HG_DK, HG_DK), F32)],
        compiler_params=pltpu.CompilerParams(
            dimension_semantics=("arbitrary", "arbitrary"), vmem_limit_bytes=VMEM_LIMIT),
        name="hgrn2",
    )(proj3d, proj3d, proj3d, proj3d, lb, norm_w,
      jnp.asarray(mcat, BF16), jnp.asarray(level))


AT_BLK = 256
PREP_ROWS = 512
PREP_SUB = PREP_ROWS // AT_BLK
AUG_PIECES = 3


def _aug_base(h):
    return h * LANE + (FX_DH if h % 2 == 0 else 0)


def _prep_constants():
    grp = np.zeros((FX_WIDTH, FX_WIDTH), np.float32)
    for h in range(FX_HEADS):
        grp[h * FX_DH:(h + 1) * FX_DH, h * FX_DH:(h + 1) * FX_DH] = 1.0 / FX_DH
    tri = np.tril(np.ones((AT_BLK, AT_BLK), np.float32))
    width = FX_HEADS * LANE
    pq = np.zeros((AUG_PIECES * LANE, width), np.float32)
    pk = np.zeros((AUG_PIECES * LANE, width), np.float32)
    cq = np.zeros((1, width), np.float32)
    ck = np.zeros((1, width), np.float32)
    for h in range(FX_HEADS):
        base = _aug_base(h)
        for p in range(AUG_PIECES):
            pq[p * LANE + h, base + p] = 1.0
            ck[0, base + p] = 1.0
            pk[p * LANE + h, base + AUG_PIECES + p] = -1.0
            cq[0, base + AUG_PIECES + p] = 1.0
    return grp, tri, pq, pk, cq, ck


def _prep_kernel(q_ref, k_ref, v_ref, ff_ref, fb_ref, qnw_ref, knw_ref,
                 grp_ref, tri_ref, pq_ref, pk_ref, cq_ref, ck_ref,
                 qa_ref, ka_ref, vt_ref, r_ref, carry_ref):
    @pl.when(pl.program_id(1) == 0)
    def _():
        carry_ref[...] = jnp.zeros_like(carry_ref)

    z = ff_ref[...] + fb_ref[...]
    lf = (jnp.minimum(z, 0.0) - jnp.log(1.0 + jnp.exp(-jnp.abs(z)))) * LOG2E
    grp = grp_ref[...]
    tri = tri_ref[...]
    lane = lax.broadcasted_iota(jnp.int32, (AT_BLK, LANE), 1)
    low_half = lane < FX_DH

    def qk_norm(t, w):
        ms = _dot((t * t).astype(BF16), grp)
        return t * lax.rsqrt(ms + NORM_EPS) * w

    for j in range(PREP_SUB):
        rows = slice(j * AT_BLK, (j + 1) * AT_BLK)
        hi, mid, lo = _split3(lf[rows])
        c_rel = _dot(tri, hi) + _dot(tri, mid) + _dot(tri, lo)
        r_ref[j] = carry_ref[...]
        carry_ref[...] = carry_ref[...] + c_rel[AT_BLK - 1:AT_BLK, :]
        pieces = jnp.concatenate(_split3(c_rel), axis=1)
        aug_q = _dot(pieces, pq_ref[...]) + cq_ref[...]
        aug_k = _dot(pieces, pk_ref[...]) + ck_ref[...]
        qn = qk_norm(q_ref[rows, :].astype(F32), qnw_ref[...]) * (FX_DH ** -0.5 * LOG2E)
        kn = qk_norm(k_ref[rows, :].astype(F32), knw_ref[...])
        for h in range(FX_HEADS):
            pair = slice((h // 2) * LANE, (h // 2 + 1) * LANE)
            head = slice(h * LANE, (h + 1) * LANE)
            data = low_half if h % 2 == 0 else jnp.logical_not(low_half)
            qa_ref[rows, head] = jnp.where(data, qn[:, pair], aug_q[:, head]).astype(BF16)
            ka_ref[rows, head] = jnp.where(data, kn[:, pair], aug_k[:, head]).astype(BF16)
        vt_ref[j] = v_ref[rows, :].astype(F32).T.astype(BF16)


def _attn_prep(proj3d, ff3d, f_bias, qnw, knw):
    bsz, s, _ = proj3d.shape
    nblk = s // AT_BLK
    consts = _prep_constants()
    grp, tri, pq, pk = (jnp.asarray(c, BF16) for c in consts[:4])
    cq, ck = (jnp.asarray(c, F32) for c in consts[4:])
    width = FX_HEADS * LANE

    def col(j):
        return pl.BlockSpec((None, PREP_ROWS, FX_WIDTH), lambda b, i, j=j: (b, i, j))

    def full(shape):
        return pl.BlockSpec(shape, lambda b, i: (0,) * len(shape))

    return pl.pallas_call(
        _prep_kernel,
        out_shape=(jax.ShapeDtypeStruct((bsz, s, width), BF16),
                   jax.ShapeDtypeStruct((bsz, s, width), BF16),
                   jax.ShapeDtypeStruct((bsz, nblk, FX_WIDTH, AT_BLK), BF16),
                   jax.ShapeDtypeStruct((bsz, nblk, 8, LANE), F32)),
        grid=(bsz, s // PREP_ROWS),
        in_specs=[
            col(4), col(5), col(6),
            pl.BlockSpec((None, PREP_ROWS, LANE), lambda b, i: (b, i, 0)),
            full((1, LANE)), full((1, FX_WIDTH)), full((1, FX_WIDTH)),
            full((FX_WIDTH, FX_WIDTH)), full((AT_BLK, AT_BLK)),
            full((AUG_PIECES * LANE, width)), full((AUG_PIECES * LANE, width)),
            full((1, width)), full((1, width)),
        ],
        out_specs=(pl.BlockSpec((None, PREP_ROWS, width), lambda b, i: (b, i, 0)),
                   pl.BlockSpec((None, PREP_ROWS, width), lambda b, i: (b, i, 0)),
                   pl.BlockSpec((None, PREP_SUB, FX_WIDTH, AT_BLK), lambda b, i: (b, i, 0, 0)),
                   pl.BlockSpec((None, PREP_SUB, 8, LANE), lambda b, i: (b, i, 0, 0))),
        scratch_shapes=[pltpu.VMEM((8, LANE), F32)],
        compiler_params=pltpu.CompilerParams(
            dimension_semantics=("arbitrary", "arbitrary"), vmem_limit_bytes=VMEM_LIMIT),
        name="attn_prep",
    )(proj3d, proj3d, proj3d, ff3d, f_bias, qnw, knw, grp, tri, pq, pk, cq, ck)


AT_HG = 2
F32_EXP2_ZERO = 150.0
SUBLANES = 8


def _tree_reduce(op, final, x):
    while x.shape[0] > SUBLANES:
        half = x.shape[0] // 2
        x = op(x[:half], x[half:])
    return final(x, axis=0, keepdims=True)


def _attn_kernel(r_ref, q_ref, k_ref, vt_ref, o_ref,
                 s_ref, p_ref, acc_ref, m_ref, l_ref, a_ref, *, nblk, n_r):
    b = pl.program_id(0)
    hg = pl.program_id(1)
    qi = pl.program_id(2)


    def scores(g, ki):
        lanes = slice(g * LANE, (g + 1) * LANE)
        k = k_ref[pl.ds(pl.multiple_of(ki * AT_BLK, AT_BLK), AT_BLK), lanes]
        return _dot_nt(k, q_ref[:, lanes])

    def softmax_step(g, ki, s):
        base = (b * FX_HEADS + hg * AT_HG + g) * nblk
        delta = r_ref[base + qi] - r_ref[base + ki]
        m = m_ref[g]
        m_new = jnp.maximum(m, _tree_reduce(jnp.maximum, jnp.max, s) + delta)
        p = jnp.exp2(s - (m_new - delta))
        alpha = jnp.exp2(m - m_new)
        l_ref[g] = alpha * l_ref[g] + _tree_reduce(jnp.add, jnp.sum, p)
        m_ref[g] = m_new
        return p.astype(BF16), alpha

    def pv(g, ki, p, alpha):
        v_t = vt_ref[ki, g * FX_DH:(g + 1) * FX_DH, :]
        acc_ref[g] = alpha * acc_ref[g] + _dot(v_t, p)

    def step(slot, ki):
        other = 1 - slot
        for g in range(AT_HG):
            s_ref[other, g] = scores(g, ki + 1)
            pv(g, jnp.maximum(ki - 1, 0), p_ref[other, g], a_ref[g])
            p, alpha = softmax_step(g, ki, s_ref[slot, g])
            p_ref[slot, g] = p
            a_ref[g] = alpha

    def last_step(slot):
        other = 1 - slot
        key = lax.broadcasted_iota(jnp.int32, (AT_BLK, AT_BLK), 0)
        qry = lax.broadcasted_iota(jnp.int32, (AT_BLK, AT_BLK), 1)
        for g in range(AT_HG):
            pv(g, jnp.maximum(qi - 1, 0), p_ref[other, g], a_ref[g])
            s = jnp.where(key <= qry, s_ref[slot, g], NEG_BIG)
            p, alpha = softmax_step(g, qi, s)
            pv(g, qi, p, alpha)
            o_ref[g * FX_DH:(g + 1) * FX_DH, :] = (acc_ref[g] / l_ref[g]).astype(o_ref.dtype)

    thr = r_ref[n_r]

    def needed(first):
        hit = None
        for g in range(AT_HG):
            base = (b * FX_HEADS + hg * AT_HG + g) * nblk
            c = r_ref[base + qi] - r_ref[base + first] >= thr
            hit = c if hit is None else jnp.logical_or(hit, c)
        return hit

    start = lax.while_loop(lambda f: jnp.logical_and(f > 0, needed(f)),
                           lambda f: f - 1, qi)

    for g in range(AT_HG):
        s_ref[0, g] = scores(g, start)
        p_ref[1, g] = jnp.zeros((AT_BLK, AT_BLK), BF16)
        a_ref[g] = jnp.ones((1, AT_BLK), F32)
        m_ref[g] = jnp.full((1, AT_BLK), NEG_BIG, F32)
        l_ref[g] = jnp.zeros((1, AT_BLK), F32)
        acc_ref[g] = jnp.zeros((FX_DH, AT_BLK), F32)

    n_full = qi - start
    n_pairs = n_full // 2

    def pair(j, carry):
        ki = start + 2 * j
        step(0, ki)
        step(1, ki + 1)
        return carry

    lax.fori_loop(0, n_pairs, pair, 0)

    @pl.when(n_full % 2 == 0)
    def _():
        last_step(0)

    @pl.when(n_full % 2 == 1)
    def _():
        step(0, qi - 1)
        last_step(1)


def _attention(r_flat, q_aug, k_aug, v_t):
    bsz, s, _ = q_aug.shape
    nblk = s // AT_BLK
    return pl.pallas_call(
        functools.partial(_attn_kernel, nblk=nblk, n_r=r_flat.shape[0] - 1),
        out_shape=jax.ShapeDtypeStruct((bsz, FX_WIDTH, s), BF16),
        grid=(bsz, FX_HEADS // AT_HG, nblk),
        in_specs=[
            pl.BlockSpec(memory_space=pltpu.SMEM),
            pl.BlockSpec((None, AT_BLK, AT_HG * LANE), lambda b, h, i: (b, i, h)),
            pl.BlockSpec((None, s, AT_HG * LANE), lambda b, h, i: (b, 0, h)),
            pl.BlockSpec((None, nblk, AT_HG * FX_DH, AT_BLK), lambda b, h, i: (b, 0, h, 0)),
        ],
        out_specs=pl.BlockSpec((None, AT_HG * FX_DH, AT_BLK), lambda b, h, i: (b, h, i)),
        scratch_shapes=[
            pltpu.VMEM((2, AT_HG, AT_BLK, AT_BLK), F32),
            pltpu.VMEM((2, AT_HG, AT_BLK, AT_BLK), BF16),
            pltpu.VMEM((AT_HG, FX_DH, AT_BLK), F32),
            pltpu.VMEM((AT_HG, 1, AT_BLK), F32),
            pltpu.VMEM((AT_HG, 1, AT_BLK), F32),
            pltpu.VMEM((AT_HG, 1, AT_BLK), F32),
        ],
        compiler_params=pltpu.CompilerParams(
            dimension_semantics=("arbitrary", "arbitrary", "arbitrary"),
            vmem_limit_bytes=VMEM_LIMIT),
        name="fox_attention",
    )(r_flat, q_aug, k_aug, v_t)


FF_TM = 1024
FF_TF = 1024


def _ffn_kernel(x_ref, oa_ref, obt_ref, woa_ref, wob_ref, nw_ref, wu_ref, wd_ref,
                o_ref, h_ref):
    @pl.when(pl.program_id(2) == 0)
    def _():
        x1 = (x_ref[...] + _dot(oa_ref[...], woa_ref[...])
              + _dot_tn(obt_ref[...], wob_ref[...]))
        ms = jnp.mean(x1 * x1, axis=-1, keepdims=True)
        h_ref[...] = (x1 * lax.rsqrt(ms + NORM_EPS) * nw_ref[...]).astype(BF16)
        o_ref[...] = x1

    u = jnp.maximum(_dot(h_ref[...], wu_ref[...]), 0.0)
    o_ref[...] += _dot((u * u).astype(BF16), wd_ref[...])


def _outproj_ffn(x3d, o_a, o_bt, w_out, norm_w, w_up, w_down):
    bsz, s, _ = x3d.shape
    return pl.pallas_call(
        _ffn_kernel,
        out_shape=jax.ShapeDtypeStruct(x3d.shape, F32),
        grid=(bsz, s // FF_TM, D_FF // FF_TF),
        in_specs=[
            pl.BlockSpec((None, FF_TM, D_MODEL), lambda b, i, j: (b, i, 0)),
            pl.BlockSpec((None, FF_TM, HG_WIDTH), lambda b, i, j: (b, i, 0)),
            pl.BlockSpec((None, FX_WIDTH, FF_TM), lambda b, i, j: (b, 0, i)),
            pl.BlockSpec((HG_WIDTH, D_MODEL), lambda b, i, j: (0, 0)),
            pl.BlockSpec((FX_WIDTH, D_MODEL), lambda b, i, j: (1, 0)),
            pl.BlockSpec((1, D_MODEL), lambda b, i, j: (0, 0)),
            pl.BlockSpec((D_MODEL, FF_TF), lambda b, i, j: (0, j)),
            pl.BlockSpec((FF_TF, D_MODEL), lambda b, i, j: (j, 0)),
        ],
        out_specs=pl.BlockSpec((None, FF_TM, D_MODEL), lambda b, i, j: (b, i, 0)),
        scratch_shapes=[pltpu.VMEM((FF_TM, D_MODEL), BF16)],
        compiler_params=pltpu.CompilerParams(
            dimension_semantics=("arbitrary", "arbitrary", "arbitrary"),
            vmem_limit_bytes=VMEM_LIMIT),
        name="outproj_ffn",
    )(x3d, o_a, o_bt, w_out, w_out, norm_w, w_up, w_down)


def _layer_lower_bounds(lower_bounds):
    p = jax.nn.softmax(lower_bounds.astype(F32), axis=0)
    c = jnp.cumsum(p, axis=0)
    return c - c[0:1]


def kernel(x, lower_bounds, norm1_w, w_in, fox_f_bias, q_norm_w, k_norm_w,
           hgrn_norm_w, w_out, norm2_w, w_up, w_down):
    bsz, s, d = x.shape
    depth = w_in.shape[0]
    nblk = s // AT_BLK
    lbs = _layer_lower_bounds(lower_bounds)
    w_in_pad = jnp.pad(w_in, ((0, 0), (0, 0), (0, PROJ_PAD - w_in.shape[-1]))).astype(BF16)
    w_out_b = w_out.astype(BF16)
    w_up_b = w_up.astype(BF16)
    w_down_b = w_down.astype(BF16)
    f_bias = jnp.pad(fox_f_bias, ((0, 0), (0, LANE - FX_HEADS)))

    for l in range(depth):
        proj, ff = _inproj(x.reshape(bsz * s, d), norm1_w[l][None, :], w_in_pad[l])
        proj3d = proj.reshape(bsz, s, PROJ_MAIN)
        ff3d = ff.reshape(bsz, s, LANE)
        o_a = _hgrn(proj3d, lbs[l][None, :], hgrn_norm_w[l][None, :])
        q_aug, k_aug, v_t, r_blk = _attn_prep(
            proj3d, ff3d, f_bias[l][None, :],
            jnp.tile(q_norm_w[l], FX_HEADS)[None, :],
            jnp.tile(k_norm_w[l], FX_HEADS)[None, :])
        qk_bound = (1.02 * FX_DH ** 0.5 * LOG2E * jnp.max(jnp.abs(q_norm_w[l]))
                    * jnp.max(jnp.abs(k_norm_w[l])))
        skip_thr = -(F32_EXP2_ZERO + 2.0 * qk_bound)
        r_flat = jnp.concatenate([
            jnp.transpose(r_blk[:, :, 0, :FX_HEADS], (0, 2, 1)).reshape(-1),
            skip_thr.reshape(1).astype(F32)])
        o_bt = _attention(r_flat, q_aug, k_aug, v_t)
        x = _outproj_ffn(x, o_a, o_bt, w_out_b[l], norm2_w[l][None, :],
                         w_up_b[l], w_down_b[l])
    return x
```

```python
import functools

import numpy as np
import jax
import jax.numpy as jnp
from jax import lax
from jax.experimental import pallas as pl
from jax.experimental.pallas import tpu as pltpu

F32 = jnp.float32
BF16 = jnp.bfloat16

D_MODEL = 1024
D_FF = 4 * D_MODEL
NORM_EPS = 1e-6

HG_WIDTH = 512
HG_HEADS = 4
HG_DK = 128
HG_CHUNK = 64
HG_LEVELS = 6

FX_WIDTH = 512
FX_HEADS = 8
FX_DH = 64

PROJ_MAIN = 4 * HG_WIDTH + 3 * FX_WIDTH
LANE = 128
PROJ_PAD = PROJ_MAIN + LANE

VMEM_LIMIT = 56 * 1024 * 1024

NEG_BIG = -1e30
LOG2E = 1.4426950408889634

NT_DIMS = (((1,), (1,)), ((), ()))
TN_DIMS = (((0,), (0,)), ((), ()))


def _dot(a, b):
    return jnp.dot(a, b, preferred_element_type=F32)


def _dot_nt(a, b):
    return lax.dot_general(a, b, NT_DIMS, preferred_element_type=F32)


def _dot_tn(a, b):
    return lax.dot_general(a, b, TN_DIMS, preferred_element_type=F32)


def _split2(x):
    hi = x.astype(BF16)
    lo = (x - hi.astype(F32)).astype(BF16)
    return hi, lo


def _split3(x):
    hi = x.astype(BF16)
    r = x - hi.astype(F32)
    mid = r.astype(BF16)
    lo = (r - mid.astype(F32)).astype(BF16)
    return hi, mid, lo


IN_TM = 512
IN_TN = 512


def _inproj_kernel(x_ref, nw_ref, w_ref, proj_ref, ff_ref):
    x = x_ref[...]
    ms = jnp.mean(x * x, axis=-1, keepdims=True)
    h = (x * lax.rsqrt(ms + NORM_EPS) * nw_ref[...]).astype(BF16)
    for c in range(PROJ_MAIN // IN_TN):
        cols = slice(c * IN_TN, (c + 1) * IN_TN)
        proj_ref[:, cols] = _dot(h, w_ref[:, cols]).astype(BF16)
    ff_ref[...] = _dot(h, w_ref[:, PROJ_MAIN:PROJ_PAD])


def _inproj(x2d, norm_w, w_pad):
    t = x2d.shape[0]
    return pl.pallas_call(
        _inproj_kernel,
        out_shape=(jax.ShapeDtypeStruct((t, PROJ_MAIN), BF16),
                   jax.ShapeDtypeStruct((t, LANE), F32)),
        grid=(t // IN_TM,),
        in_specs=[
            pl.BlockSpec((IN_TM, D_MODEL), lambda i: (i, 0)),
            pl.BlockSpec((1, D_MODEL), lambda i: (0, 0)),
            pl.BlockSpec((D_MODEL, PROJ_PAD), lambda i: (0, 0)),
        ],
        out_specs=(pl.BlockSpec((IN_TM, PROJ_MAIN), lambda i: (i, 0)),
                   pl.BlockSpec((IN_TM, LANE), lambda i: (i, 0))),
        compiler_params=pltpu.CompilerParams(
            dimension_semantics=("arbitrary",), vmem_limit_bytes=VMEM_LIMIT),
        name="inproj",
    )(x2d, norm_w, w_pad)


HG_ROWS = 512
HG_NCHUNK = HG_ROWS // HG_CHUNK
HG_E_ROWS = (2 + HG_LEVELS) * HG_CHUNK


def _hgrn_constants():
    c = HG_CHUNK
    tri = np.tril(np.ones((c, c), np.float32))
    rows = [tri, 1.0 - tri]
    idx = np.arange(c)
    for lvl in range(HG_LEVELS):
        m = c >> (lvl + 1)
        ref = (idx // (2 * m)) * (2 * m) + m - 1
        rows.append(tri - tri[ref])
    mat = np.concatenate(rows, axis=0)
    mcat = np.concatenate([mat, mat], axis=1)
    x = idx[:, None] ^ idx[None, :]
    top = np.floor(np.log2(np.maximum(x, 1))).astype(np.int32)
    level = (HG_LEVELS - 1) - top
    level = np.where(idx[:, None] == idx[None, :], HG_LEVELS, level)
    level = np.where(idx[:, None] < idx[None, :], -1, level)
    return mcat.astype(np.float32), level.astype(np.int32)


def _hgrn_kernel(q_ref, f_ref, i_ref, g_ref, lb_ref, nw_ref, mcat_ref, lvl_ref,
                 o_ref, st_ref):
    @pl.when(pl.program_id(1) == 0)
    def _():
        st_ref[...] = jnp.zeros_like(st_ref)

    lb = lb_ref[...]
    nw = nw_ref[...]
    mcat = mcat_ref[...]
    level = lvl_ref[...]
    scale = HG_DK ** -0.5

    def chunk(c, carry):
        r0 = pl.multiple_of(c * HG_CHUNK, HG_CHUNK)
        rows = pl.ds(r0, HG_CHUNK)
        fp = f_ref[rows, :].astype(F32)
        f = lb + (1.0 - lb) * jax.nn.sigmoid(fp)
        lf = jnp.log(f)
        kk = 1.0 - f
        hi, lo = _split2(lf)
        e_all = _dot(mcat, jnp.concatenate([hi, lo], axis=0))
        b = e_all[0:HG_CHUNK]
        qs = q_ref[rows, :].astype(F32) * scale
        q_in = (qs * jnp.exp(b)).astype(BF16)
        k_out = (kk * jnp.exp(e_all[HG_CHUNK:2 * HG_CHUNK])).astype(BF16)
        dec = jnp.exp(b[HG_CHUNK - 1:HG_CHUNK, :])
        qb = qs.astype(BF16)
        kb = kk.astype(BF16)
        v = i_ref[rows, :]
        g = g_ref[rows, :].astype(F32)
        for h in range(HG_HEADS):
            ls = slice(h * HG_DK, (h + 1) * HG_DK)
            a = jnp.zeros((HG_CHUNK, HG_CHUNK), F32)
            for lvl in range(HG_LEVELS):
                e = e_all[(2 + lvl) * HG_CHUNK:(3 + lvl) * HG_CHUNK, ls]
                x = jnp.exp(-jnp.abs(e))
                d = _dot_nt((qs[:, ls] * x).astype(BF16), (kk[:, ls] * x).astype(BF16))
                a = jnp.where(level == lvl, d, a)
            a = jnp.where(level == HG_LEVELS, _dot_nt(qb[:, ls], kb[:, ls]), a)
            st = st_ref[h]
            o = _dot(a.astype(BF16), v[:, ls]) + _dot_nt(q_in[:, ls], st.astype(BF16))
            st_ref[h] = st * dec[:, ls] + _dot_tn(v[:, ls], k_out[:, ls])
            var = jnp.mean(o * o, axis=-1, keepdims=True)
            gh = g[:, ls]
            y = o * lax.rsqrt(var + NORM_EPS) * nw * (gh * jax.nn.sigmoid(gh))
            o_ref[rows, ls] = y.astype(o_ref.dtype)
        return carry

    lax.fori_loop(0, HG_NCHUNK, chunk, 0, unroll=2)


def _hgrn(proj3d, lb, norm_w):
    bsz, s, _ = proj3d.shape
    mcat, level = _hgrn_constants()

    def col(j):
        return pl.BlockSpec((None, HG_ROWS, HG_WIDTH), lambda b, i, j=j: (b, i, j))

    return pl.pallas_call(
        _hgrn_kernel,
        out_shape=jax.ShapeDtypeStruct((bsz, s, HG_WIDTH), BF16),
        grid=(bsz, s // HG_ROWS),
        in_specs=[
            col(0), col(1), col(2), col(3),
            pl.BlockSpec((1, HG_WIDTH), lambda b, i: (0, 0)),
            pl.BlockSpec((1, HG_DK), lambda b, i: (0, 0)),
            pl.BlockSpec((HG_E_ROWS, 2 * HG_CHUNK), lambda b, i: (0, 0)),
            pl.BlockSpec((HG_CHUNK, HG_CHUNK), lambda b, i: (0, 0)),
        ],
        out_specs=pl.BlockSpec((None, HG_ROWS, HG_WIDTH), lambda b, i: (b, i, 0)),
        scratch_shapes=[pltpu.VMEM((HG_HEADS, HG_DK, HG_DK), F32)],
        compiler_params=pltpu.CompilerParams(
            dimension_semantics=("arbitrary", "arbitrary"), vmem_limit_bytes=VMEM_LIMIT),
        name="hgrn2",
    )(proj3d, proj3d, proj3d, proj3d, lb, norm_w,
      jnp.asarray(mcat, BF16), jnp.asarray(level))


AT_BLK = 256
PREP_ROWS = 512
PREP_SUB = PREP_ROWS // AT_BLK
AUG_PIECES = 3
VT_ROWS = 80


def _aug_base(h):
    return h * LANE + (FX_DH if h % 2 == 0 else 0)


def _prep_constants():
    grp = np.zeros((FX_WIDTH, FX_WIDTH), np.float32)
    for h in range(FX_HEADS):
        grp[h * FX_DH:(h + 1) * FX_DH, h * FX_DH:(h + 1) * FX_DH] = 1.0 / FX_DH
    tri = np.tril(np.ones((AT_BLK, AT_BLK), np.float32))
    width = FX_HEADS * LANE
    pq = np.zeros((AUG_PIECES * LANE, width), np.float32)
    pk = np.zeros((AUG_PIECES * LANE, width), np.float32)
    cq = np.zeros((1, width), np.float32)
    ck = np.zeros((1, width), np.float32)
    for h in range(FX_HEADS):
        base = _aug_base(h)
        for p in range(AUG_PIECES):
            pq[p * LANE + h, base + p] = 1.0
            ck[0, base + p] = 1.0
            pk[p * LANE + h, base + AUG_PIECES + p] = -1.0
            cq[0, base + AUG_PIECES + p] = 1.0
    return grp, tri, pq, pk, cq, ck


def _prep_kernel(q_ref, k_ref, v_ref, ff_ref, fb_ref, qnw_ref, knw_ref,
                 grp_ref, tri_ref, pq_ref, pk_ref, cq_ref, ck_ref,
                 qa_ref, ka_ref, vt_ref, r_ref, carry_ref):
    @pl.when(pl.program_id(1) == 0)
    def _():
        carry_ref[...] = jnp.zeros_like(carry_ref)

    z = ff_ref[...] + fb_ref[...]
    lf = (jnp.minimum(z, 0.0) - jnp.log(1.0 + jnp.exp(-jnp.abs(z)))) * LOG2E
    grp = grp_ref[...]
    tri = tri_ref[...]
    lane = lax.broadcasted_iota(jnp.int32, (AT_BLK, LANE), 1)
    low_half = lane < FX_DH

    def qk_norm(t, w):
        ms = _dot((t * t).astype(BF16), grp)
        return t * lax.rsqrt(ms + NORM_EPS) * w

    for j in range(PREP_SUB):
        rows = slice(j * AT_BLK, (j + 1) * AT_BLK)
        hi, mid, lo = _split3(lf[rows])
        c_rel = _dot(tri, hi) + _dot(tri, mid) + _dot(tri, lo)
        r_ref[j] = carry_ref[...]
        carry_ref[...] = carry_ref[...] + c_rel[AT_BLK - 1:AT_BLK, :]
        pieces = jnp.concatenate(_split3(c_rel), axis=1)
        aug_q = _dot(pieces, pq_ref[...]) + cq_ref[...]
        aug_k = _dot(pieces, pk_ref[...]) + ck_ref[...]
        qn = qk_norm(q_ref[rows, :].astype(F32), qnw_ref[...]) * (FX_DH ** -0.5 * LOG2E)
        kn = qk_norm(k_ref[rows, :].astype(F32), knw_ref[...])
        for h in range(FX_HEADS):
            pair = slice((h // 2) * LANE, (h // 2 + 1) * LANE)
            head = slice(h * LANE, (h + 1) * LANE)
            data = low_half if h % 2 == 0 else jnp.logical_not(low_half)
            qa_ref[rows, head] = jnp.where(data, qn[:, pair], aug_q[:, head]).astype(BF16)
            ka_ref[rows, head] = jnp.where(data, kn[:, pair], aug_k[:, head]).astype(BF16)
        v_t = v_ref[rows, :].astype(F32).T.astype(BF16)
        for h in range(FX_HEADS):
            vt_ref[j, h * VT_ROWS:h * VT_ROWS + FX_DH, :] = v_t[h * FX_DH:(h + 1) * FX_DH]
            vt_ref[j, h * VT_ROWS + FX_DH:(h + 1) * VT_ROWS, :] = jnp.ones(
                (VT_ROWS - FX_DH, AT_BLK), BF16)


def _attn_prep(proj3d, ff3d, f_bias, qnw, knw):
    bsz, s, _ = proj3d.shape
    nblk = s // AT_BLK
    consts = _prep_constants()
    grp, tri, pq, pk = (jnp.asarray(c, BF16) for c in consts[:4])
    cq, ck = (jnp.asarray(c, F32) for c in consts[4:])
    width = FX_HEADS * LANE

    def col(j):
        return pl.BlockSpec((None, PREP_ROWS, FX_WIDTH), lambda b, i, j=j: (b, i, j))

    def full(shape):
        return pl.BlockSpec(shape, lambda b, i: (0,) * len(shape))

    return pl.pallas_call(
        _prep_kernel,
        out_shape=(jax.ShapeDtypeStruct((bsz, s, width), BF16),
                   jax.ShapeDtypeStruct((bsz, s, width), BF16),
                   jax.ShapeDtypeStruct((bsz, nblk, FX_HEADS * VT_ROWS, AT_BLK), BF16),
                   jax.ShapeDtypeStruct((bsz, nblk, 8, LANE), F32)),
        grid=(bsz, s // PREP_ROWS),
        in_specs=[
            col(4), col(5), col(6),
            pl.BlockSpec((None, PREP_ROWS, LANE), lambda b, i: (b, i, 0)),
            full((1, LANE)), full((1, FX_WIDTH)), full((1, FX_WIDTH)),
            full((FX_WIDTH, FX_WIDTH)), full((AT_BLK, AT_BLK)),
            full((AUG_PIECES * LANE, width)), full((AUG_PIECES * LANE, width)),
            full((1, width)), full((1, width)),
        ],
        out_specs=(pl.BlockSpec((None, PREP_ROWS, width), lambda b, i: (b, i, 0)),
                   pl.BlockSpec((None, PREP_ROWS, width), lambda b, i: (b, i, 0)),
                   pl.BlockSpec((None, PREP_SUB, FX_HEADS * VT_ROWS, AT_BLK),
                                lambda b, i: (b, i, 0, 0)),
                   pl.BlockSpec((None, PREP_SUB, 8, LANE), lambda b, i: (b, i, 0, 0))),
        scratch_shapes=[pltpu.VMEM((8, LANE), F32)],
        compiler_params=pltpu.CompilerParams(
            dimension_semantics=("arbitrary", "arbitrary"), vmem_limit_bytes=VMEM_LIMIT),
        name="attn_prep",
    )(proj3d, proj3d, proj3d, ff3d, f_bias, qnw, knw, grp, tri, pq, pk, cq, ck)


AT_HG = 2
F32_EXP2_ZERO = 150.0
SUBLANES = 8


def _tree_reduce(op, final, x):
    while x.shape[0] > SUBLANES:
        half = x.shape[0] // 2
        x = op(x[:half], x[half:])
    return final(x, axis=0, keepdims=True)


def _attn_kernel(r_ref, mask_ref, q_ref, k_ref, vt_ref, o_ref,
                 s_ref, p_ref, acc_ref, m_ref, a_ref, start_ref, *, nblk, n_r):
    b = pl.program_id(0)
    hg = pl.program_id(1)
    bases = [(b * FX_HEADS + hg * AT_HG + g) * nblk for g in range(AT_HG)]
    last = nblk - 1

    thr = r_ref[n_r]

    def find_start(qi, n_items):
        def needed(first):
            hit = None
            for base in bases:
                c = r_ref[base + qi] - r_ref[base + first] >= thr
                hit = c if hit is None else jnp.logical_or(hit, c)
            return hit

        first = lax.while_loop(lambda f: jnp.logical_and(f > 0, needed(f)),
                               lambda f: f - 1, qi)
        start_ref[qi] = first
        return n_items + (qi - first + 1)

    n_items = lax.fori_loop(0, nblk, find_start, 0)

    def next_item(item):
        qi, ki, valid = item
        at_diag = ki == qi
        more = jnp.logical_and(valid != 0,
                               jnp.logical_not(jnp.logical_and(at_diag, qi == last)))
        qn = jnp.where(at_diag, jnp.minimum(qi + 1, last), qi)
        kn = jnp.where(at_diag, start_ref[qn], ki + 1)
        return (jnp.where(more, qn, last), jnp.where(more, kn, last), more.astype(jnp.int32))

    def stage_a(slot, item):
        qi, ki, _ = item
        rows_q = pl.ds(pl.multiple_of(qi * AT_BLK, AT_BLK), AT_BLK)
        rows_k = pl.ds(pl.multiple_of(ki * AT_BLK, AT_BLK), AT_BLK)
        mask = mask_ref[(ki == qi).astype(jnp.int32)]
        for g in range(AT_HG):
            lanes = slice(g * LANE, (g + 1) * LANE)
            s_ref[slot, g] = _dot_nt(k_ref[rows_k, lanes], q_ref[rows_q, lanes]) + mask

    def stage_b(slot, item):
        qi, ki, valid = item
        live = valid != 0
        first = jnp.logical_and(live, ki == start_ref[qi])
        for g in range(AT_HG):
            delta = jnp.where(live, r_ref[bases[g] + qi] - r_ref[bases[g] + ki], NEG_BIG)
            s = s_ref[slot, g]
            m = jnp.where(first, NEG_BIG, m_ref[g])
            m_new = jnp.maximum(m, _tree_reduce(jnp.maximum, jnp.max, s) + delta)
            p_ref[slot, g] = jnp.exp2(s - (m_new - delta)).astype(BF16)
            a_ref[g] = jnp.exp2(m - m_new)
            m_ref[g] = m_new

    def stage_c(slot, item):
        qi, ki, _ = item
        for g in range(AT_HG):
            v_t = vt_ref[ki, g * VT_ROWS:(g + 1) * VT_ROWS, :]
            acc = a_ref[g] * acc_ref[g] + _dot(v_t, p_ref[slot, g])
            acc_ref[g] = acc
            o_ref[qi, g * FX_DH:(g + 1) * FX_DH, :] = (
                acc[:FX_DH] / acc[FX_DH:FX_DH + 1]).astype(o_ref.dtype)

    def tick(slot, items):
        item_c, item_b, item_a = items
        stage_c(1 - slot, item_c)
        stage_b(slot, item_b)
        stage_a(1 - slot, item_a)
        return (item_b, item_a, next_item(item_a))

    for g in range(AT_HG):
        p_ref[1, g] = jnp.zeros((AT_BLK, AT_BLK), BF16)
        a_ref[g] = jnp.ones((1, AT_BLK), F32)
        m_ref[g] = jnp.zeros((1, AT_BLK), F32)
        acc_ref[g] = jnp.ones((VT_ROWS, AT_BLK), F32)

    zero = jnp.int32(0)
    item0 = (zero, zero, jnp.int32(1))
    idle = (zero, zero, zero)
    stage_a(0, item0)

    def trip(_, items):
        return tick(1, tick(0, items))

    lax.fori_loop(0, (n_items + 2) // 2, trip, (idle, item0, next_item(item0)))


def _attention(r_flat, q_aug, k_aug, v_t):
    bsz, s, _ = q_aug.shape
    nblk = s // AT_BLK
    causal = np.where(np.arange(AT_BLK)[:, None] <= np.arange(AT_BLK)[None, :], 0.0, NEG_BIG)
    mask = jnp.asarray(np.stack([np.zeros_like(causal), causal]), F32)
    return pl.pallas_call(
        functools.partial(_attn_kernel, nblk=nblk, n_r=r_flat.shape[0] - 1),
        out_shape=jax.ShapeDtypeStruct((bsz, nblk, FX_WIDTH, AT_BLK), BF16),
        grid=(bsz, FX_HEADS // AT_HG),
        in_specs=[
            pl.BlockSpec(memory_space=pltpu.SMEM),
            pl.BlockSpec((2, AT_BLK, AT_BLK), lambda b, h: (0, 0, 0)),
            pl.BlockSpec((None, s, AT_HG * LANE), lambda b, h: (b, 0, h)),
            pl.BlockSpec((None, s, AT_HG * LANE), lambda b, h: (b, 0, h)),
            pl.BlockSpec((None, nblk, AT_HG * VT_ROWS, AT_BLK), lambda b, h: (b, 0, h, 0)),
        ],
        out_specs=pl.BlockSpec((None, nblk, AT_HG * FX_DH, AT_BLK), lambda b, h: (b, 0, h, 0)),
        scratch_shapes=[
            pltpu.VMEM((2, AT_HG, AT_BLK, AT_BLK), F32),
            pltpu.VMEM((2, AT_HG, AT_BLK, AT_BLK), BF16),
            pltpu.VMEM((AT_HG, VT_ROWS, AT_BLK), F32),
            pltpu.VMEM((AT_HG, 1, AT_BLK), F32),
            pltpu.VMEM((AT_HG, 1, AT_BLK), F32),
            pltpu.SMEM((nblk,), jnp.int32),
        ],
        compiler_params=pltpu.CompilerParams(
            dimension_semantics=("arbitrary", "arbitrary"),
            vmem_limit_bytes=VMEM_LIMIT),
        name="fox_attention",
    )(r_flat, mask, q_aug, k_aug, v_t)


FF_TM = 1024
FF_TF = 1024


def _ffn_kernel(x_ref, oa_ref, obt_ref, woa_ref, wob_ref, nw_ref, wu_ref, wd_ref,
                o_ref, h_ref):
    @pl.when(pl.program_id(2) == 0)
    def _():
        o_b = jnp.concatenate([_dot_tn(obt_ref[j], wob_ref[...])
                               for j in range(FF_TM // AT_BLK)], axis=0)
        x1 = x_ref[...] + _dot(oa_ref[...], woa_ref[...]) + o_b
        ms = jnp.mean(x1 * x1, axis=-1, keepdims=True)
        h_ref[...] = (x1 * lax.rsqrt(ms + NORM_EPS) * nw_ref[...]).astype(BF16)
        o_ref[...] = x1

    u = jnp.maximum(_dot(h_ref[...], wu_ref[...]), 0.0)
    o_ref[...] += _dot((u * u).astype(BF16), wd_ref[...])


def _outproj_ffn(x3d, o_a, o_bt, w_out, norm_w, w_up, w_down):
    bsz, s, _ = x3d.shape
    return pl.pallas_call(
        _ffn_kernel,
        out_shape=jax.ShapeDtypeStruct(x3d.shape, F32),
        grid=(bsz, s // FF_TM, D_FF // FF_TF),
        in_specs=[
            pl.BlockSpec((None, FF_TM, D_MODEL), lambda b, i, j: (b, i, 0)),
            pl.BlockSpec((None, FF_TM, HG_WIDTH), lambda b, i, j: (b, i, 0)),
            pl.BlockSpec((None, FF_TM // AT_BLK, FX_WIDTH, AT_BLK), lambda b, i, j: (b, i, 0, 0)),
            pl.BlockSpec((HG_WIDTH, D_MODEL), lambda b, i, j: (0, 0)),
            pl.BlockSpec((FX_WIDTH, D_MODEL), lambda b, i, j: (1, 0)),
            pl.BlockSpec((1, D_MODEL), lambda b, i, j: (0, 0)),
            pl.BlockSpec((D_MODEL, FF_TF), lambda b, i, j: (0, j)),
            pl.BlockSpec((FF_TF, D_MODEL), lambda b, i, j: (j, 0)),
        ],
        out_specs=pl.BlockSpec((None, FF_TM, D_MODEL), lambda b, i, j: (b, i, 0)),
        scratch_shapes=[pltpu.VMEM((FF_TM, D_MODEL), BF16)],
        compiler_params=pltpu.CompilerParams(
            dimension_semantics=("arbitrary", "arbitrary", "arbitrary"),
            vmem_limit_bytes=VMEM_LIMIT),
        name="outproj_ffn",
    )(x3d, o_a, o_bt, w_out, w_out, norm_w, w_up, w_down)


def _layer_lower_bounds(lower_bounds):
    p = jax.nn.softmax(lower_bounds.astype(F32), axis=0)
    c = jnp.cumsum(p, axis=0)
    return c - c[0:1]


def kernel(x, lower_bounds, norm1_w, w_in, fox_f_bias, q_norm_w, k_norm_w,
           hgrn_norm_w, w_out, norm2_w, w_up, w_down):
    bsz, s, d = x.shape
    depth = w_in.shape[0]
    nblk = s // AT_BLK
    lbs = _layer_lower_bounds(lower_bounds)
    w_in_pad = jnp.pad(w_in, ((0, 0), (0, 0), (0, PROJ_PAD - w_in.shape[-1]))).astype(BF16)
    w_out_b = w_out.astype(BF16)
    w_up_b = w_up.astype(BF16)
    w_down_b = w_down.astype(BF16)
    f_bias = jnp.pad(fox_f_bias, ((0, 0), (0, LANE - FX_HEADS)))

    for l in range(depth):
        proj, ff = _inproj(x.reshape(bsz * s, d), norm1_w[l][None, :], w_in_pad[l])
        proj3d = proj.reshape(bsz, s, PROJ_MAIN)
        ff3d = ff.reshape(bsz, s, LANE)
        o_a = _hgrn(proj3d, lbs[l][None, :], hgrn_norm_w[l][None, :])
        q_aug, k_aug, v_t, r_blk = _attn_prep(
            proj3d, ff3d, f_bias[l][None, :],
            jnp.tile(q_norm_w[l], FX_HEADS)[None, :],
            jnp.tile(k_norm_w[l], FX_HEADS)[None, :])
        qk_bound = (1.02 * FX_DH ** 0.5 * LOG2E * jnp.max(jnp.abs(q_norm_w[l]))
                    * jnp.max(jnp.abs(k_norm_w[l])))
        skip_thr = -(F32_EXP2_ZERO + 2.0 * qk_bound)
        r_flat = jnp.concatenate([
            jnp.transpose(r_blk[:, :, 0, :FX_HEADS], (0, 2, 1)).reshape(-1),
            skip_thr.reshape(1).astype(F32)])
        o_bt = _attention(r_flat, q_aug, k_aug, v_t)
        x = _outproj_ffn(x, o_a, o_bt, w_out_b[l], norm2_w[l][None, :],
                         w_up_b[l], w_down_b[l])
    return x
```

```python
import functools

import numpy as np
import jax
import jax.numpy as jnp
from jax import lax
from jax.experimental import pallas as pl
from jax.experimental.pallas import tpu as pltpu

F32 = jnp.float32
BF16 = jnp.bfloat16

D_MODEL = 1024
D_FF = 4 * D_MODEL
NORM_EPS = 1e-6

HG_WIDTH = 512
HG_HEADS = 4
HG_DK = 128
HG_CHUNK = 64
HG_LEVELS = 6

FX_WIDTH = 512
FX_HEADS = 8
FX_DH = 64

PROJ_MAIN = 4 * HG_WIDTH + 3 * FX_WIDTH
LANE = 128
PROJ_PAD = PROJ_MAIN + LANE

VMEM_LIMIT = 56 * 1024 * 1024

NEG_BIG = -1e30
LOG2E = 1.4426950408889634

NT_DIMS = (((1,), (1,)), ((), ()))
TN_DIMS = (((0,), (0,)), ((), ()))


def _dot(a, b):
    return jnp.dot(a, b, preferred_element_type=F32)


def _dot_nt(a, b):
    return lax.dot_general(a, b, NT_DIMS, preferred_element_type=F32)


def _dot_tn(a, b):
    return lax.dot_general(a, b, TN_DIMS, preferred_element_type=F32)


def _split2(x):
    hi = x.astype(BF16)
    lo = (x - hi.astype(F32)).astype(BF16)
    return hi, lo


def _split3(x):
    hi = x.astype(BF16)
    r = x - hi.astype(F32)
    mid = r.astype(BF16)
    lo = (r - mid.astype(F32)).astype(BF16)
    return hi, mid, lo


IN_TM = 512
IN_TN = 512


def _inproj_kernel(x_ref, nw_ref, w_ref, proj_ref, ff_ref):
    x = x_ref[...]
    ms = jnp.mean(x * x, axis=-1, keepdims=True)
    h = (x * lax.rsqrt(ms + NORM_EPS) * nw_ref[...]).astype(BF16)
    for c in range(PROJ_MAIN // IN_TN):
        cols = slice(c * IN_TN, (c + 1) * IN_TN)
        proj_ref[:, cols] = _dot(h, w_ref[:, cols]).astype(BF16)
    ff_ref[...] = _dot(h, w_ref[:, PROJ_MAIN:PROJ_PAD])


def _inproj(x2d, norm_w, w_pad):
    t = x2d.shape[0]
    return pl.pallas_call(
        _inproj_kernel,
        out_shape=(jax.ShapeDtypeStruct((t, PROJ_MAIN), BF16),
                   jax.ShapeDtypeStruct((t, LANE), F32)),
        grid=(t // IN_TM,),
        in_specs=[
            pl.BlockSpec((IN_TM, D_MODEL), lambda i: (i, 0)),
            pl.BlockSpec((1, D_MODEL), lambda i: (0, 0)),
            pl.BlockSpec((D_MODEL, PROJ_PAD), lambda i: (0, 0)),
        ],
        out_specs=(pl.BlockSpec((IN_TM, PROJ_MAIN), lambda i: (i, 0)),
                   pl.BlockSpec((IN_TM, LANE), lambda i: (i, 0))),
        compiler_params=pltpu.CompilerParams(
            dimension_semantics=("arbitrary",), vmem_limit_bytes=VMEM_LIMIT),
        name="inproj",
    )(x2d, norm_w, w_pad)


HG_ROWS = 512
HG_NCHUNK = HG_ROWS // HG_CHUNK
HG_E_ROWS = (2 + HG_LEVELS) * HG_CHUNK


def _hgrn_constants():
    c = HG_CHUNK
    tri = np.tril(np.ones((c, c), np.float32))
    rows = [tri, 1.0 - tri]
    idx = np.arange(c)
    for lvl in range(HG_LEVELS):
        m = c >> (lvl + 1)
        ref = (idx // (2 * m)) * (2 * m) + m - 1
        upper = (idx % (2 * m) >= m)[:, None]
        diff = tri - tri[ref]
        rows.append(np.where(upper, diff, -diff))
    mat = np.concatenate(rows, axis=0)
    mcat = np.concatenate([mat, mat], axis=1)
    x = idx[:, None] ^ idx[None, :]
    top = np.floor(np.log2(np.maximum(x, 1))).astype(np.int32)
    level = (HG_LEVELS - 1) - top
    level = np.where(idx[:, None] == idx[None, :], HG_LEVELS, level)
    level = np.where(idx[:, None] < idx[None, :], -1, level)
    level = np.concatenate([level, level], axis=1)
    return mcat.astype(np.float32), level.astype(np.int32)


def _stack_heads(t):
    return jnp.concatenate([t[:, h * HG_DK:(h + 1) * HG_DK] for h in range(HG_HEADS)], axis=0)


def _hgrn_kernel(q_ref, f_ref, i_ref, g_ref, lb_ref, nw_ref, mcat_ref, lvl_ref,
                 o_ref, st_ref):
    @pl.when(pl.program_id(1) == 0)
    def _():
        st_ref[...] = jnp.zeros_like(st_ref)

    lb = lb_ref[...]
    nw = nw_ref[...]
    mcat = mcat_ref[...]
    level = lvl_ref[...]
    scale = HG_DK ** -0.5

    def chunk(c, carry):
        r0 = pl.multiple_of(c * HG_CHUNK, HG_CHUNK)
        rows = pl.ds(r0, HG_CHUNK)
        fp = f_ref[rows, :].astype(F32)
        f = lb + (1.0 - lb) * jax.nn.sigmoid(fp)
        lf = jnp.log(f) * LOG2E
        kk = 1.0 - f
        hi, lo = _split2(lf)
        e_all = _dot(mcat, jnp.concatenate([hi, lo], axis=0))
        b = e_all[0:HG_CHUNK]
        qs = q_ref[rows, :].astype(F32) * scale
        q_in = (qs * jnp.exp2(b)).astype(BF16)
        k_out = (kk * jnp.exp2(e_all[HG_CHUNK:2 * HG_CHUNK])).astype(BF16)
        dec = jnp.exp2(b[HG_CHUNK - 1:HG_CHUNK, :])
        qb = qs.astype(BF16)
        kb = kk.astype(BF16)
        v = i_ref[rows, :]
        g = g_ref[rows, :].astype(F32)

        a = [jnp.zeros((HG_CHUNK, LANE), F32) for _ in range(HG_HEADS)]
        for lvl in range(HG_LEVELS + 1):
            if lvl < HG_LEVELS:
                x = jnp.exp2(e_all[(2 + lvl) * HG_CHUNK:(3 + lvl) * HG_CHUNK]).astype(BF16)
                ql, kl = qb * x, kb * x
            else:
                ql, kl = qb, kb
            d = _dot_nt(_stack_heads(ql), _stack_heads(kl))
            for h in range(HG_HEADS):
                blk = d[h * HG_CHUNK:(h + 1) * HG_CHUNK, (h // 2) * LANE:(h // 2 + 1) * LANE]
                a[h] = jnp.where(level == lvl, blk, a[h])

        zeros = jnp.zeros((HG_CHUNK, HG_DK), BF16)
        for h in range(HG_HEADS):
            ls = slice(h * HG_DK, (h + 1) * HG_DK)
            v_pad = jnp.concatenate([v[:, ls], zeros] if h % 2 == 0 else [zeros, v[:, ls]], axis=0)
            st = st_ref[h]
            o = _dot(a[h].astype(BF16), v_pad) + _dot_nt(q_in[:, ls], st.astype(BF16))
            st_ref[h] = st * dec[:, ls] + _dot_tn(v[:, ls], k_out[:, ls])
            var = jnp.mean(o * o, axis=-1, keepdims=True)
            gh = g[:, ls]
            y = o * lax.rsqrt(var + NORM_EPS) * nw * (gh * jax.nn.sigmoid(gh))
            o_ref[rows, ls] = y.astype(o_ref.dtype)
        return carry

    lax.fori_loop(0, HG_NCHUNK, chunk, 0, unroll=4)


def _hgrn(proj3d, lb, norm_w):
    bsz, s, _ = proj3d.shape
    mcat, level = _hgrn_constants()

    def col(j):
        return pl.BlockSpec((None, HG_ROWS, HG_WIDTH), lambda b, i, j=j: (b, i, j))

    return pl.pallas_call(
        _hgrn_kernel,
        out_shape=jax.ShapeDtypeStruct((bsz, s, HG_WIDTH), BF16),
        grid=(bsz, s // HG_ROWS),
        in_specs=[
            col(0), col(1), col(2), col(3),
            pl.BlockSpec((1, HG_WIDTH), lambda b, i: (0, 0)),
            pl.BlockSpec((1, HG_DK), lambda b, i: (0, 0)),
            pl.BlockSpec((HG_E_ROWS, 2 * HG_CHUNK), lambda b, i: (0, 0)),
            pl.BlockSpec((HG_CHUNK, 2 * HG_CHUNK), lambda b, i: (0, 0)),
        ],
        out_specs=pl.BlockSpec((None, HG_ROWS, HG_WIDTH), lambda b, i: (b, i, 0)),
        scratch_shapes=[pltpu.VMEM((HG_HEADS, HG_DK, HG_DK), F32)],
        compiler_params=pltpu.CompilerParams(
            dimension_semantics=("arbitrary", "arbitrary"), vmem_limit_bytes=VMEM_LIMIT),
        name="hgrn2",
    )(proj3d, proj3d, proj3d, proj3d, lb, norm_w,
      jnp.asarray(mcat, BF16), jnp.asarray(level))


AT_BLK = 256
PREP_ROWS = 512
PREP_SUB = PREP_ROWS // AT_BLK
AUG_PIECES = 3
VT_ROWS = 80


def _aug_base(h):
    return h * LANE + (FX_DH if h % 2 == 0 else 0)


def _prep_constants():
    grp = np.zeros((FX_WIDTH, FX_WIDTH), np.float32)
    for h in range(FX_HEADS):
        grp[h * FX_DH:(h + 1) * FX_DH, h * FX_DH:(h + 1) * FX_DH] = 1.0 / FX_DH
    tri = np.tril(np.ones((AT_BLK, AT_BLK), np.float32))
    width = FX_HEADS * LANE
    pq = np.zeros((AUG_PIECES * LANE, width), np.float32)
    pk = np.zeros((AUG_PIECES * LANE, width), np.float32)
    cq = np.zeros((1, width), np.float32)
    ck = np.zeros((1, width), np.float32)
    for h in range(FX_HEADS):
        base = _aug_base(h)
        for p in range(AUG_PIECES):
            pq[p * LANE + h, base + p] = 1.0
            ck[0, base + p] = 1.0
            pk[p * LANE + h, base + AUG_PIECES + p] = -1.0
            cq[0, base + AUG_PIECES + p] = 1.0
    return grp, tri, pq, pk, cq, ck


def _prep_kernel(q_ref, k_ref, v_ref, ff_ref, fb_ref, qnw_ref, knw_ref,
                 grp_ref, tri_ref, pq_ref, pk_ref, cq_ref, ck_ref,
                 qa_ref, ka_ref, vt_ref, r_ref, carry_ref):
    @pl.when(pl.program_id(1) == 0)
    def _():
        carry_ref[...] = jnp.zeros_like(carry_ref)

    z = ff_ref[...] + fb_ref[...]
    lf = (jnp.minimum(z, 0.0) - jnp.log(1.0 + jnp.exp(-jnp.abs(z)))) * LOG2E
    grp = grp_ref[...]
    tri = tri_ref[...]
    lane = lax.broadcasted_iota(jnp.int32, (AT_BLK, LANE), 1)
    low_half = lane < FX_DH

    def qk_norm(t, w):
        ms = _dot((t * t).astype(BF16), grp)
        return t * lax.rsqrt(ms + NORM_EPS) * w

    for j in range(PREP_SUB):
        rows = slice(j * AT_BLK, (j + 1) * AT_BLK)
        hi, mid, lo = _split3(lf[rows])
        c_rel = _dot(tri, hi) + _dot(tri, mid) + _dot(tri, lo)
        r_ref[j] = carry_ref[...]
        carry_ref[...] = carry_ref[...] + c_rel[AT_BLK - 1:AT_BLK, :]
        pieces = jnp.concatenate(_split3(c_rel), axis=1)
        aug_q = _dot(pieces, pq_ref[...]) + cq_ref[...]
        aug_k = _dot(pieces, pk_ref[...]) + ck_ref[...]
        qn = qk_norm(q_ref[rows, :].astype(F32), qnw_ref[...]) * (FX_DH ** -0.5 * LOG2E)
        kn = qk_norm(k_ref[rows, :].astype(F32), knw_ref[...])
        for h in range(FX_HEADS):
            pair = slice((h // 2) * LANE, (h // 2 + 1) * LANE)
            head = slice(h * LANE, (h + 1) * LANE)
            data = low_half if h % 2 == 0 else jnp.logical_not(low_half)
            qa_ref[rows, head] = jnp.where(data, qn[:, pair], aug_q[:, head]).astype(BF16)
            ka_ref[rows, head] = jnp.where(data, kn[:, pair], aug_k[:, head]).astype(BF16)
        v_t = v_ref[rows, :].astype(F32).T.astype(BF16)
        for h in range(FX_HEADS):
            vt_ref[j, h * VT_ROWS:h * VT_ROWS + FX_DH, :] = v_t[h * FX_DH:(h + 1) * FX_DH]
            vt_ref[j, h * VT_ROWS + FX_DH:(h + 1) * VT_ROWS, :] = jnp.ones(
                (VT_ROWS - FX_DH, AT_BLK), BF16)


def _attn_prep(proj3d, ff3d, f_bias, qnw, knw):
    bsz, s, _ = proj3d.shape
    nblk = s // AT_BLK
    consts = _prep_constants()
    grp, tri, pq, pk = (jnp.asarray(c, BF16) for c in consts[:4])
    cq, ck = (jnp.asarray(c, F32) for c in consts[4:])
    width = FX_HEADS * LANE

    def col(j):
        return pl.BlockSpec((None, PREP_ROWS, FX_WIDTH), lambda b, i, j=j: (b, i, j))

    def full(shape):
        return pl.BlockSpec(shape, lambda b, i: (0,) * len(shape))

    return pl.pallas_call(
        _prep_kernel,
        out_shape=(jax.ShapeDtypeStruct((bsz, s, width), BF16),
                   jax.ShapeDtypeStruct((bsz, s, width), BF16),
                   jax.ShapeDtypeStruct((bsz, nblk, FX_HEADS * VT_ROWS, AT_BLK), BF16),
                   jax.ShapeDtypeStruct((bsz, nblk, 8, LANE), F32)),
        grid=(bsz, s // PREP_ROWS),
        in_specs=[
            col(4), col(5), col(6),
            pl.BlockSpec((None, PREP_ROWS, LANE), lambda b, i: (b, i, 0)),
            full((1, LANE)), full((1, FX_WIDTH)), full((1, FX_WIDTH)),
            full((FX_WIDTH, FX_WIDTH)), full((AT_BLK, AT_BLK)),
            full((AUG_PIECES * LANE, width)), full((AUG_PIECES * LANE, width)),
            full((1, width)), full((1, width)),
        ],
        out_specs=(pl.BlockSpec((None, PREP_ROWS, width), lambda b, i: (b, i, 0)),
                   pl.BlockSpec((None, PREP_ROWS, width), lambda b, i: (b, i, 0)),
                   pl.BlockSpec((None, PREP_SUB, FX_HEADS * VT_ROWS, AT_BLK),
                                lambda b, i: (b, i, 0, 0)),
                   pl.BlockSpec((None, PREP_SUB, 8, LANE), lambda b, i: (b, i, 0, 0))),
        scratch_shapes=[pltpu.VMEM((8, LANE), F32)],
        compiler_params=pltpu.CompilerParams(
            dimension_semantics=("arbitrary", "arbitrary"), vmem_limit_bytes=VMEM_LIMIT),
        name="attn_prep",
    )(proj3d, proj3d, proj3d, ff3d, f_bias, qnw, knw, grp, tri, pq, pk, cq, ck)


AT_HG = 2
F32_EXP2_ZERO = 150.0
SUBLANES = 8


def _tree_reduce(op, final, x):
    while x.shape[0] > SUBLANES:
        half = x.shape[0] // 2
        x = op(x[:half], x[half:])
    return final(x, axis=0, keepdims=True)


def _attn_kernel(r_ref, mask_ref, q_ref, k_ref, vt_ref, o_ref,
                 s_ref, p_ref, acc_ref, m_ref, a_ref, start_ref, *, nblk, n_r):
    b = pl.program_id(0)
    hg = pl.program_id(1)
    bases = [(b * FX_HEADS + hg * AT_HG + g) * nblk for g in range(AT_HG)]
    last = nblk - 1

    thr = r_ref[n_r]

    def find_start(qi, n_items):
        def needed(first):
            hit = None
            for base in bases:
                c = r_ref[base + qi] - r_ref[base + first] >= thr
                hit = c if hit is None else jnp.logical_or(hit, c)
            return hit

        first = lax.while_loop(lambda f: jnp.logical_and(f > 0, needed(f)),
                               lambda f: f - 1, qi)
        start_ref[qi] = first
        return n_items + (qi - first + 1)

    n_items = lax.fori_loop(0, nblk, find_start, 0)

    def next_item(item):
        qi, ki, valid = item
        at_diag = ki == qi
        more = jnp.logical_and(valid != 0,
                               jnp.logical_not(jnp.logical_and(at_diag, qi == last)))
        qn = jnp.where(at_diag, jnp.minimum(qi + 1, last), qi)
        kn = jnp.where(at_diag, start_ref[qn], ki + 1)
        return (jnp.where(more, qn, last), jnp.where(more, kn, last), more.astype(jnp.int32))

    def stage_a(slot, item):
        qi, ki, _ = item
        rows_q = pl.ds(pl.multiple_of(qi * AT_BLK, AT_BLK), AT_BLK)
        rows_k = pl.ds(pl.multiple_of(ki * AT_BLK, AT_BLK), AT_BLK)
        mask = mask_ref[(ki == qi).astype(jnp.int32)]
        for g in range(AT_HG):
            lanes = slice(g * LANE, (g + 1) * LANE)
            s_ref[slot, g] = _dot_nt(k_ref[rows_k, lanes], q_ref[rows_q, lanes]) + mask

    def stage_b(slot, item):
        qi, ki, valid = item
        live = valid != 0
        first = jnp.logical_and(live, ki == start_ref[qi])
        for g in range(AT_HG):
            delta = jnp.where(live, r_ref[bases[g] + qi] - r_ref[bases[g] + ki], NEG_BIG)
            s = s_ref[slot, g]
            m = jnp.where(first, NEG_BIG, m_ref[g])
            m_new = jnp.maximum(m, _tree_reduce(jnp.maximum, jnp.max, s) + delta)
            p_ref[slot, g] = jnp.exp2(s - (m_new - delta)).astype(BF16)
            a_ref[g] = jnp.exp2(m - m_new)
            m_ref[g] = m_new

    def stage_c(slot, item):
        qi, ki, _ = item
        for g in range(AT_HG):
            v_t = vt_ref[ki, g * VT_ROWS:(g + 1) * VT_ROWS, :]
            acc = a_ref[g] * acc_ref[g] + _dot(v_t, p_ref[slot, g])
            acc_ref[g] = acc
            o_ref[qi, g * FX_DH:(g + 1) * FX_DH, :] = (
                acc[:FX_DH] / acc[FX_DH:FX_DH + 1]).astype(o_ref.dtype)

    def tick(slot, items):
        item_c, item_b, item_a = items
        stage_c(1 - slot, item_c)
        stage_b(slot, item_b)
        stage_a(1 - slot, item_a)
        return (item_b, item_a, next_item(item_a))

    for g in range(AT_HG):
        p_ref[1, g] = jnp.zeros((AT_BLK, AT_BLK), BF16)
        a_ref[g] = jnp.ones((1, AT_BLK), F32)
        m_ref[g] = jnp.zeros((1, AT_BLK), F32)
        acc_ref[g] = jnp.ones((VT_ROWS, AT_BLK), F32)

    zero = jnp.int32(0)
    item0 = (zero, zero, jnp.int32(1))
    idle = (zero, zero, zero)
    stage_a(0, item0)

    def trip(_, items):
        return tick(1, tick(0, items))

    lax.fori_loop(0, (n_items + 2) // 2, trip, (idle, item0, next_item(item0)))


def _attention(r_flat, q_aug, k_aug, v_t):
    bsz, s, _ = q_aug.shape
    nblk = s // AT_BLK
    causal = np.where(np.arange(AT_BLK)[:, None] <= np.arange(AT_BLK)[None, :], 0.0, NEG_BIG)
    mask = jnp.asarray(np.stack([np.zeros_like(causal), causal]), F32)
    return pl.pallas_call(
        functools.partial(_attn_kernel, nblk=nblk, n_r=r_flat.shape[0] - 1),
        out_shape=jax.ShapeDtypeStruct((bsz, nblk, FX_WIDTH, AT_BLK), BF16),
        grid=(bsz, FX_HEADS // AT_HG),
        in_specs=[
            pl.BlockSpec(memory_space=pltpu.SMEM),
            pl.BlockSpec((2, AT_BLK, AT_BLK), lambda b, h: (0, 0, 0)),
            pl.BlockSpec((None, s, AT_HG * LANE), lambda b, h: (b, 0, h)),
            pl.BlockSpec((None, s, AT_HG * LANE), lambda b, h: (b, 0, h)),
            pl.BlockSpec((None, nblk, AT_HG * VT_ROWS, AT_BLK), lambda b, h: (b, 0, h, 0)),
        ],
        out_specs=pl.BlockSpec((None, nblk, AT_HG * FX_DH, AT_BLK), lambda b, h: (b, 0, h, 0)),
        scratch_shapes=[
            pltpu.VMEM((2, AT_HG, AT_BLK, AT_BLK), F32),
            pltpu.VMEM((2, AT_HG, AT_BLK, AT_BLK), BF16),
            pltpu.VMEM((AT_HG, VT_ROWS, AT_BLK), F32),
            pltpu.VMEM((AT_HG, 1, AT_BLK), F32),
            pltpu.VMEM((AT_HG, 1, AT_BLK), F32),
            pltpu.SMEM((nblk,), jnp.int32),
        ],
        compiler_params=pltpu.CompilerParams(
            dimension_semantics=("arbitrary", "arbitrary"),
            vmem_limit_bytes=VMEM_LIMIT),
        name="fox_attention",
    )(r_flat, mask, q_aug, k_aug, v_t)


FF_TM = 1024
FF_TF = 1024


def _ffn_kernel(x_ref, oa_ref, obt_ref, woa_ref, wob_ref, nw_ref, wu_ref, wd_ref,
                o_ref, h_ref):
    @pl.when(pl.program_id(2) == 0)
    def _():
        o_b = jnp.concatenate([_dot_tn(obt_ref[j], wob_ref[...])
                               for j in range(FF_TM // AT_BLK)], axis=0)
        x1 = x_ref[...] + _dot(oa_ref[...], woa_ref[...]) + o_b
        ms = jnp.mean(x1 * x1, axis=-1, keepdims=True)
        h_ref[...] = (x1 * lax.rsqrt(ms + NORM_EPS) * nw_ref[...]).astype(BF16)
        o_ref[...] = x1

    u = jnp.maximum(_dot(h_ref[...], wu_ref[...]), 0.0)
    o_ref[...] += _dot((u * u).astype(BF16), wd_ref[...])


def _outproj_ffn(x3d, o_a, o_bt, w_out, norm_w, w_up, w_down):
    bsz, s, _ = x3d.shape
    return pl.pallas_call(
        _ffn_kernel,
        out_shape=jax.ShapeDtypeStruct(x3d.shape, F32),
        grid=(bsz, s // FF_TM, D_FF // FF_TF),
        in_specs=[
            pl.BlockSpec((None, FF_TM, D_MODEL), lambda b, i, j: (b, i, 0)),
            pl.BlockSpec((None, FF_TM, HG_WIDTH), lambda b, i, j: (b, i, 0)),
            pl.BlockSpec((None, FF_TM // AT_BLK, FX_WIDTH, AT_BLK), lambda b, i, j: (b, i, 0, 0)),
            pl.BlockSpec((HG_WIDTH, D_MODEL), lambda b, i, j: (0, 0)),
            pl.BlockSpec((FX_WIDTH, D_MODEL), lambda b, i, j: (1, 0)),
            pl.BlockSpec((1, D_MODEL), lambda b, i, j: (0, 0)),
            pl.BlockSpec((D_MODEL, FF_TF), lambda b, i, j: (0, j)),
            pl.BlockSpec((FF_TF, D_MODEL), lambda b, i, j: (j, 0)),
        ],
        out_specs=pl.BlockSpec((None, FF_TM, D_MODEL), lambda b, i, j: (b, i, 0)),
        scratch_shapes=[pltpu.VMEM((FF_TM, D_MODEL), BF16)],
        compiler_params=pltpu.CompilerParams(
            dimension_semantics=("arbitrary", "arbitrary", "arbitrary"),
            vmem_limit_bytes=VMEM_LIMIT),
        name="outproj_ffn",
    )(x3d, o_a, o_bt, w_out, w_out, norm_w, w_up, w_down)


def _layer_lower_bounds(lower_bounds):
    p = jax.nn.softmax(lower_bounds.astype(F32), axis=0)
    c = jnp.cumsum(p, axis=0)
    return c - c[0:1]


def kernel(x, lower_bounds, norm1_w, w_in, fox_f_bias, q_norm_w, k_norm_w,
           hgrn_norm_w, w_out, norm2_w, w_up, w_down):
    bsz, s, d = x.shape
    depth = w_in.shape[0]
    nblk = s // AT_BLK
    lbs = _layer_lower_bounds(lower_bounds)
    w_in_pad = jnp.pad(w_in, ((0, 0), (0, 0), (0, PROJ_PAD - w_in.shape[-1]))).astype(BF16)
    w_out_b = w_out.astype(BF16)
    w_up_b = w_up.astype(BF16)
    w_down_b = w_down.astype(BF16)
    f_bias = jnp.pad(fox_f_bias, ((0, 0), (0, LANE - FX_HEADS)))

    for l in range(depth):
        proj, ff = _inproj(x.reshape(bsz * s, d), norm1_w[l][None, :], w_in_pad[l])
        proj3d = proj.reshape(bsz, s, PROJ_MAIN)
        ff3d = ff.reshape(bsz, s, LANE)
        o_a = _hgrn(proj3d, lbs[l][None, :], hgrn_norm_w[l][None, :])
        q_aug, k_aug, v_t, r_blk = _attn_prep(
            proj3d, ff3d, f_bias[l][None, :],
            jnp.tile(q_norm_w[l], FX_HEADS)[None, :],
            jnp.tile(k_norm_w[l], FX_HEADS)[None, :])
        qk_bound = (1.02 * FX_DH ** 0.5 * LOG2E * jnp.max(jnp.abs(q_norm_w[l]))
                    * jnp.max(jnp.abs(k_norm_w[l])))
        skip_thr = -(F32_EXP2_ZERO + 2.0 * qk_bound)
        r_flat = jnp.concatenate([
            jnp.transpose(r_blk[:, :, 0, :FX_HEADS], (0, 2, 1)).reshape(-1),
            skip_thr.reshape(1).astype(F32)])
        o_bt = _attention(r_flat, q_aug, k_aug, v_t)
        x = _outproj_ffn(x, o_a, o_bt, w_out_b[l], norm2_w[l][None, :],
                         w_up_b[l], w_down_b[l])
    return x
```

```python
import functools

import numpy as np
import jax
import jax.numpy as jnp
from jax import lax
from jax.experimental import pallas as pl
from jax.experimental.pallas import tpu as pltpu

F32 = jnp.float32
BF16 = jnp.bfloat16

D_MODEL = 1024
D_FF = 4 * D_MODEL
NORM_EPS = 1e-6

HG_WIDTH = 512
HG_HEADS = 4
HG_DK = 128
HG_CHUNK = 64
HG_LEVELS = 6

FX_WIDTH = 512
FX_HEADS = 8
FX_DH = 64

PROJ_MAIN = 4 * HG_WIDTH + 3 * FX_WIDTH
LANE = 128
PROJ_PAD = PROJ_MAIN + LANE

VMEM_LIMIT = 56 * 1024 * 1024

NEG_BIG = -1e30
LOG2E = 1.4426950408889634

NT_DIMS = (((1,), (1,)), ((), ()))
TN_DIMS = (((0,), (0,)), ((), ()))


def _dot(a, b):
    return jnp.dot(a, b, preferred_element_type=F32)


def _dot_nt(a, b):
    return lax.dot_general(a, b, NT_DIMS, preferred_element_type=F32)


def _dot_tn(a, b):
    return lax.dot_general(a, b, TN_DIMS, preferred_element_type=F32)


def _split2(x):
    hi = x.astype(BF16)
    lo = (x - hi.astype(F32)).astype(BF16)
    return hi, lo


def _split3(x):
    hi = x.astype(BF16)
    r = x - hi.astype(F32)
    mid = r.astype(BF16)
    lo = (r - mid.astype(F32)).astype(BF16)
    return hi, mid, lo


IN_TM = 512
IN_TN = 512


def _inproj_kernel(x_ref, nw_ref, w_ref, proj_ref, ff_ref):
    x = x_ref[...]
    ms = jnp.mean(x * x, axis=-1, keepdims=True)
    h = (x * lax.rsqrt(ms + NORM_EPS) * nw_ref[...]).astype(BF16)
    for c in range(PROJ_MAIN // IN_TN):
        cols = slice(c * IN_TN, (c + 1) * IN_TN)
        proj_ref[:, cols] = _dot(h, w_ref[:, cols]).astype(BF16)
    ff_ref[...] = _dot(h, w_ref[:, PROJ_MAIN:PROJ_PAD])


def _inproj(x2d, norm_w, w_pad):
    t = x2d.shape[0]
    return pl.pallas_call(
        _inproj_kernel,
        out_shape=(jax.ShapeDtypeStruct((t, PROJ_MAIN), BF16),
                   jax.ShapeDtypeStruct((t, LANE), F32)),
        grid=(t // IN_TM,),
        in_specs=[
            pl.BlockSpec((IN_TM, D_MODEL), lambda i: (i, 0)),
            pl.BlockSpec((1, D_MODEL), lambda i: (0, 0)),
            pl.BlockSpec((D_MODEL, PROJ_PAD), lambda i: (0, 0)),
        ],
        out_specs=(pl.BlockSpec((IN_TM, PROJ_MAIN), lambda i: (i, 0)),
                   pl.BlockSpec((IN_TM, LANE), lambda i: (i, 0))),
        compiler_params=pltpu.CompilerParams(
            dimension_semantics=("arbitrary",), vmem_limit_bytes=VMEM_LIMIT),
        name="inproj",
    )(x2d, norm_w, w_pad)


HG_ROWS = 512
HG_NCHUNK = HG_ROWS // HG_CHUNK
HG_E_ROWS = (2 + HG_LEVELS) * HG_CHUNK


def _hgrn_constants():
    c = HG_CHUNK
    tri = np.tril(np.ones((c, c), np.float32))
    rows = [tri, 1.0 - tri]
    idx = np.arange(c)
    for lvl in range(HG_LEVELS):
        m = c >> (lvl + 1)
        ref = (idx // (2 * m)) * (2 * m) + m - 1
        upper = (idx % (2 * m) >= m)[:, None]
        diff = tri - tri[ref]
        rows.append(np.where(upper, diff, -diff))
    mat = np.concatenate(rows, axis=0)
    mcat = np.concatenate([mat, mat], axis=1)
    x = idx[:, None] ^ idx[None, :]
    top = np.floor(np.log2(np.maximum(x, 1))).astype(np.int32)
    level = (HG_LEVELS - 1) - top
    level = np.where(idx[:, None] == idx[None, :], HG_LEVELS, level)
    level = np.where(idx[:, None] < idx[None, :], -1, level)
    level = np.concatenate([level, level], axis=1)
    return mcat.astype(np.float32), level.astype(np.int32)


def _stack_heads(t):
    return jnp.concatenate([t[:, h * HG_DK:(h + 1) * HG_DK] for h in range(HG_HEADS)], axis=0)


def _hgrn_kernel(q_ref, f_ref, i_ref, g_ref, lb_ref, nw_ref, mcat_ref, lvl_ref,
                 o_ref, st_ref):
    @pl.when(pl.program_id(1) == 0)
    def _():
        st_ref[...] = jnp.zeros_like(st_ref)

    lb = lb_ref[...]
    nw = nw_ref[...]
    mcat = mcat_ref[...]
    level = lvl_ref[...]
    scale = HG_DK ** -0.5

    def chunk(c, carry):
        r0 = pl.multiple_of(c * HG_CHUNK, HG_CHUNK)
        rows = pl.ds(r0, HG_CHUNK)
        fp = f_ref[rows, :].astype(F32)
        f = lb + (1.0 - lb) * jax.nn.sigmoid(fp)
        lf = jnp.log(f) * LOG2E
        kk = 1.0 - f
        hi, lo = _split2(lf)
        e_all = _dot(mcat, jnp.concatenate([hi, lo], axis=0))
        b = e_all[0:HG_CHUNK]
        qs = q_ref[rows, :].astype(F32) * scale
        q_in = (qs * jnp.exp2(b)).astype(BF16)
        k_out = (kk * jnp.exp2(e_all[HG_CHUNK:2 * HG_CHUNK])).astype(BF16)
        dec = jnp.exp2(b[HG_CHUNK - 1:HG_CHUNK, :])
        qb = qs.astype(BF16)
        kb = kk.astype(BF16)
        v = i_ref[rows, :]
        g = g_ref[rows, :].astype(F32)

        a = [jnp.zeros((HG_CHUNK, LANE), F32) for _ in range(HG_HEADS)]
        for lvl in range(HG_LEVELS + 1):
            if lvl < HG_LEVELS:
                x = jnp.exp2(e_all[(2 + lvl) * HG_CHUNK:(3 + lvl) * HG_CHUNK]).astype(BF16)
                ql, kl = qb * x, kb * x
            else:
                ql, kl = qb, kb
            d = _dot_nt(_stack_heads(ql), _stack_heads(kl))
            for h in range(HG_HEADS):
                blk = d[h * HG_CHUNK:(h + 1) * HG_CHUNK, (h // 2) * LANE:(h // 2 + 1) * LANE]
                a[h] = jnp.where(level == lvl, blk, a[h])

        zeros = jnp.zeros((HG_CHUNK, HG_DK), BF16)
        for h in range(HG_HEADS):
            ls = slice(h * HG_DK, (h + 1) * HG_DK)
            v_pad = jnp.concatenate([v[:, ls], zeros] if h % 2 == 0 else [zeros, v[:, ls]], axis=0)
            st = st_ref[h]
            o = _dot(a[h].astype(BF16), v_pad) + _dot_nt(q_in[:, ls], st.astype(BF16))
            st_ref[h] = st * dec[:, ls] + _dot_tn(v[:, ls], k_out[:, ls])
            var = jnp.mean(o * o, axis=-1, keepdims=True)
            gh = g[:, ls]
            y = o * lax.rsqrt(var + NORM_EPS) * nw * (gh * jax.nn.sigmoid(gh))
            o_ref[rows, ls] = y.astype(o_ref.dtype)
        return carry

    lax.fori_loop(0, HG_NCHUNK, chunk, 0, unroll=4)


def _hgrn(proj3d, lb, norm_w):
    bsz, s, _ = proj3d.shape
    mcat, level = _hgrn_constants()

    def col(j):
        return pl.BlockSpec((None, HG_ROWS, HG_WIDTH), lambda b, i, j=j: (b, i, j))

    return pl.pallas_call(
        _hgrn_kernel,
        out_shape=jax.ShapeDtypeStruct((bsz, s, HG_WIDTH), BF16),
        grid=(bsz, s // HG_ROWS),
        in_specs=[
            col(0), col(1), col(2), col(3),
            pl.BlockSpec((1, HG_WIDTH), lambda b, i: (0, 0)),
            pl.BlockSpec((1, HG_DK), lambda b, i: (0, 0)),
            pl.BlockSpec((HG_E_ROWS, 2 * HG_CHUNK), lambda b, i: (0, 0)),
            pl.BlockSpec((HG_CHUNK, 2 * HG_CHUNK), lambda b, i: (0, 0)),
        ],
        out_specs=pl.BlockSpec((None, HG_ROWS, HG_WIDTH), lambda b, i: (b, i, 0)),
        scratch_shapes=[pltpu.VMEM((HG_HEADS, HG_DK, HG_DK), F32)],
        compiler_params=pltpu.CompilerParams(
            dimension_semantics=("arbitrary", "arbitrary"), vmem_limit_bytes=VMEM_LIMIT),
        name="hgrn2",
    )(proj3d, proj3d, proj3d, proj3d, lb, norm_w,
      jnp.asarray(mcat, BF16), jnp.asarray(level))


AT_BLK = 256
PREP_ROWS = 512
PREP_SUB = PREP_ROWS // AT_BLK
AUG_PIECES = 3
VT_ROWS = 80


def _aug_base(h):
    return h * LANE + (FX_DH if h % 2 == 0 else 0)


def _prep_constants():
    grp = np.zeros((FX_WIDTH, FX_WIDTH), np.float32)
    for h in range(FX_HEADS):
        grp[h * FX_DH:(h + 1) * FX_DH, h * FX_DH:(h + 1) * FX_DH] = 1.0 / FX_DH
    tri = np.tril(np.ones((AT_BLK, AT_BLK), np.float32))
    width = FX_HEADS * LANE
    pq = np.zeros((AUG_PIECES * LANE, width), np.float32)
    pk = np.zeros((AUG_PIECES * LANE, width), np.float32)
    cq = np.zeros((1, width), np.float32)
    ck = np.zeros((1, width), np.float32)
    for h in range(FX_HEADS):
        base = _aug_base(h)
        for p in range(AUG_PIECES):
            pq[p * LANE + h, base + p] = 1.0
            ck[0, base + p] = 1.0
            pk[p * LANE + h, base + AUG_PIECES + p] = -1.0
            cq[0, base + AUG_PIECES + p] = 1.0
    return grp, tri, pq, pk, cq, ck


def _prep_kernel(q_ref, k_ref, v_ref, ff_ref, fb_ref, qnw_ref, knw_ref,
                 grp_ref, tri_ref, pq_ref, pk_ref, cq_ref, ck_ref,
                 qa_ref, ka_ref, vt_ref, r_ref, carry_ref):
    @pl.when(pl.program_id(1) == 0)
    def _():
        carry_ref[...] = jnp.zeros_like(carry_ref)

    z = ff_ref[...] + fb_ref[...]
    lf = (jnp.minimum(z, 0.0) - jnp.log(1.0 + jnp.exp(-jnp.abs(z)))) * LOG2E
    grp = grp_ref[...]
    tri = tri_ref[...]
    lane = lax.broadcasted_iota(jnp.int32, (AT_BLK, LANE), 1)
    low_half = lane < FX_DH

    def qk_norm(t, w):
        ms = _dot((t * t).astype(BF16), grp)
        return t * lax.rsqrt(ms + NORM_EPS) * w

    for j in range(PREP_SUB):
        rows = slice(j * AT_BLK, (j + 1) * AT_BLK)
        hi, mid, lo = _split3(lf[rows])
        c_rel = _dot(tri, hi) + _dot(tri, mid) + _dot(tri, lo)
        r_ref[j] = carry_ref[...]
        carry_ref[...] = carry_ref[...] + c_rel[AT_BLK - 1:AT_BLK, :]
        pieces = jnp.concatenate(_split3(c_rel), axis=1)
        aug_q = _dot(pieces, pq_ref[...]) + cq_ref[...]
        aug_k = _dot(pieces, pk_ref[...]) + ck_ref[...]
        qn = qk_norm(q_ref[rows, :].astype(F32), qnw_ref[...]) * (FX_DH ** -0.5 * LOG2E)
        kn = qk_norm(k_ref[rows, :].astype(F32), knw_ref[...])
        for h in range(FX_HEADS):
            pair = slice((h // 2) * LANE, (h // 2 + 1) * LANE)
            head = slice(h * LANE, (h + 1) * LANE)
            data = low_half if h % 2 == 0 else jnp.logical_not(low_half)
            qa_ref[rows, head] = jnp.where(data, qn[:, pair], aug_q[:, head]).astype(BF16)
            ka_ref[rows, head] = jnp.where(data, kn[:, pair], aug_k[:, head]).astype(BF16)
        v_t = v_ref[rows, :].astype(F32).T.astype(BF16)
        for h in range(FX_HEADS):
            vt_ref[j, h * VT_ROWS:h * VT_ROWS + FX_DH, :] = v_t[h * FX_DH:(h + 1) * FX_DH]
            vt_ref[j, h * VT_ROWS + FX_DH:(h + 1) * VT_ROWS, :] = jnp.ones(
                (VT_ROWS - FX_DH, AT_BLK), BF16)


def _attn_prep(proj3d, ff3d, f_bias, qnw, knw):
    bsz, s, _ = proj3d.shape
    nblk = s // AT_BLK
    consts = _prep_constants()
    grp, tri, pq, pk = (jnp.asarray(c, BF16) for c in consts[:4])
    cq, ck = (jnp.asarray(c, F32) for c in consts[4:])
    width = FX_HEADS * LANE

    def col(j):
        return pl.BlockSpec((None, PREP_ROWS, FX_WIDTH), lambda b, i, j=j: (b, i, j))

    def full(shape):
        return pl.BlockSpec(shape, lambda b, i: (0,) * len(shape))

    return pl.pallas_call(
        _prep_kernel,
        out_shape=(jax.ShapeDtypeStruct((bsz, s, width), BF16),
                   jax.ShapeDtypeStruct((bsz, s, width), BF16),
                   jax.ShapeDtypeStruct((bsz, nblk, FX_HEADS * VT_ROWS, AT_BLK), BF16),
                   jax.ShapeDtypeStruct((bsz, nblk, 8, LANE), F32)),
        grid=(bsz, s // PREP_ROWS),
        in_specs=[
            col(4), col(5), col(6),
            pl.BlockSpec((None, PREP_ROWS, LANE), lambda b, i: (b, i, 0)),
            full((1, LANE)), full((1, FX_WIDTH)), full((1, FX_WIDTH)),
            full((FX_WIDTH, FX_WIDTH)), full((AT_BLK, AT_BLK)),
            full((AUG_PIECES * LANE, width)), full((AUG_PIECES * LANE, width)),
            full((1, width)), full((1, width)),
        ],
        out_specs=(pl.BlockSpec((None, PREP_ROWS, width), lambda b, i: (b, i, 0)),
                   pl.BlockSpec((None, PREP_ROWS, width), lambda b, i: (b, i, 0)),
                   pl.BlockSpec((None, PREP_SUB, FX_HEADS * VT_ROWS, AT_BLK),
                                lambda b, i: (b, i, 0, 0)),
                   pl.BlockSpec((None, PREP_SUB, 8, LANE), lambda b, i: (b, i, 0, 0))),
        scratch_shapes=[pltpu.VMEM((8, LANE), F32)],
        compiler_params=pltpu.CompilerParams(
            dimension_semantics=("arbitrary", "arbitrary"), vmem_limit_bytes=VMEM_LIMIT),
        name="attn_prep",
    )(proj3d, proj3d, proj3d, ff3d, f_bias, qnw, knw, grp, tri, pq, pk, cq, ck)


AT_HG = 2
AT_SLOTS = 3
AT_TRIP_TICKS = 2 * AT_SLOTS
F32_EXP2_ZERO = 150.0
SUBLANES = 8


def _tree_reduce(op, final, x):
    while x.shape[0] > SUBLANES:
        half = x.shape[0] // 2
        x = op(x[:half], x[half:])
    return final(x, axis=0, keepdims=True)


def _attn_kernel(r_ref, mask_ref, q_ref, k_ref, vt_ref, o_ref,
                 s_ref, p_ref, acc_ref, m_ref, a_ref, start_ref, *, nblk, n_r):
    b = pl.program_id(0)
    hg = pl.program_id(1)
    bases = [(b * FX_HEADS + hg * AT_HG + g) * nblk for g in range(AT_HG)]
    last = nblk - 1

    thr = r_ref[n_r]

    def find_start(qi, n_items):
        def needed(first):
            hit = None
            for base in bases:
                c = r_ref[base + qi] - r_ref[base + first] >= thr
                hit = c if hit is None else jnp.logical_or(hit, c)
            return hit

        first = lax.while_loop(lambda f: jnp.logical_and(f > 0, needed(f)),
                               lambda f: f - 1, qi)
        start_ref[qi] = first
        return n_items + (qi - first + 1)

    n_items = lax.fori_loop(0, nblk, find_start, 0)

    def next_item(item):
        qi, ki, valid = item
        at_diag = ki == qi
        more = jnp.logical_and(valid != 0,
                               jnp.logical_not(jnp.logical_and(at_diag, qi == last)))
        qn = jnp.where(at_diag, jnp.minimum(qi + 1, last), qi)
        kn = jnp.where(at_diag, start_ref[qn], ki + 1)
        return (jnp.where(more, qn, last), jnp.where(more, kn, last), more.astype(jnp.int32))

    def stage_a(slot, item):
        qi, ki, _ = item
        rows_q = pl.ds(pl.multiple_of(qi * AT_BLK, AT_BLK), AT_BLK)
        rows_k = pl.ds(pl.multiple_of(ki * AT_BLK, AT_BLK), AT_BLK)
        mask = mask_ref[(ki == qi).astype(jnp.int32)]
        for g in range(AT_HG):
            lanes = slice(g * LANE, (g + 1) * LANE)
            s_ref[slot, g] = _dot_nt(k_ref[rows_k, lanes], q_ref[rows_q, lanes]) + mask

    def stage_b(slot, item):
        qi, ki, valid = item
        live = valid != 0
        first = jnp.logical_and(live, ki == start_ref[qi])
        for g in range(AT_HG):
            delta = jnp.where(live, r_ref[bases[g] + qi] - r_ref[bases[g] + ki], NEG_BIG)
            s = s_ref[slot, g]
            m = jnp.where(first, NEG_BIG, m_ref[g])
            m_new = jnp.maximum(m, _tree_reduce(jnp.maximum, jnp.max, s) + delta)
            p_ref[slot, g] = jnp.exp2(s - (m_new - delta)).astype(BF16)
            a_ref[g] = jnp.exp2(m - m_new)
            m_ref[g] = m_new

    def stage_c(slot, item):
        qi, ki, _ = item
        for g in range(AT_HG):
            v_t = vt_ref[ki, g * VT_ROWS:(g + 1) * VT_ROWS, :]
            acc = a_ref[g] * acc_ref[g] + _dot(v_t, p_ref[slot, g])
            acc_ref[g] = acc
            o_ref[qi, g * FX_DH:(g + 1) * FX_DH, :] = (
                acc[:FX_DH] / acc[FX_DH:FX_DH + 1]).astype(o_ref.dtype)

    def tick(slot, items):
        item_c, item_b, item_s, item_a = items
        ahead = (slot + 2) % AT_SLOTS
        stage_c(ahead, item_c)
        stage_b(slot, item_b)
        stage_a(ahead, item_a)
        return (item_b, item_s, item_a, next_item(item_a))

    for g in range(AT_HG):
        p_ref[AT_SLOTS - 1, g] = jnp.zeros((AT_BLK, AT_BLK), BF16)
        a_ref[g] = jnp.ones((1, AT_BLK), F32)
        m_ref[g] = jnp.zeros((1, AT_BLK), F32)
        acc_ref[g] = jnp.ones((VT_ROWS, AT_BLK), F32)

    zero = jnp.int32(0)
    item0 = (zero, zero, jnp.int32(1))
    item1 = next_item(item0)
    idle = (zero, zero, zero)
    stage_a(0, item0)
    stage_a(1, item1)

    def trip(_, items):
        for t in range(AT_TRIP_TICKS):
            items = tick(t % AT_SLOTS, items)
        return items

    lax.fori_loop(0, (n_items + AT_TRIP_TICKS) // AT_TRIP_TICKS, trip,
                  (idle, item0, item1, next_item(item1)))


def _attention(r_flat, q_aug, k_aug, v_t):
    bsz, s, _ = q_aug.shape
    nblk = s // AT_BLK
    causal = np.where(np.arange(AT_BLK)[:, None] <= np.arange(AT_BLK)[None, :], 0.0, NEG_BIG)
    mask = jnp.asarray(np.stack([np.zeros_like(causal), causal]), F32)
    return pl.pallas_call(
        functools.partial(_attn_kernel, nblk=nblk, n_r=r_flat.shape[0] - 1),
        out_shape=jax.ShapeDtypeStruct((bsz, nblk, FX_WIDTH, AT_BLK), BF16),
        grid=(bsz, FX_HEADS // AT_HG),
        in_specs=[
            pl.BlockSpec(memory_space=pltpu.SMEM),
            pl.BlockSpec((2, AT_BLK, AT_BLK), lambda b, h: (0, 0, 0)),
            pl.BlockSpec((None, s, AT_HG * LANE), lambda b, h: (b, 0, h)),
            pl.BlockSpec((None, s, AT_HG * LANE), lambda b, h: (b, 0, h)),
            pl.BlockSpec((None, nblk, AT_HG * VT_ROWS, AT_BLK), lambda b, h: (b, 0, h, 0)),
        ],
        out_specs=pl.BlockSpec((None, nblk, AT_HG * FX_DH, AT_BLK), lambda b, h: (b, 0, h, 0)),
        scratch_shapes=[
            pltpu.VMEM((AT_SLOTS, AT_HG, AT_BLK, AT_BLK), F32),
            pltpu.VMEM((AT_SLOTS, AT_HG, AT_BLK, AT_BLK), BF16),
            pltpu.VMEM((AT_HG, VT_ROWS, AT_BLK), F32),
            pltpu.VMEM((AT_HG, 1, AT_BLK), F32),
            pltpu.VMEM((AT_HG, 1, AT_BLK), F32),
            pltpu.SMEM((nblk,), jnp.int32),
        ],
        compiler_params=pltpu.CompilerParams(
            dimension_semantics=("arbitrary", "arbitrary"),
            vmem_limit_bytes=VMEM_LIMIT),
        name="fox_attention",
    )(r_flat, mask, q_aug, k_aug, v_t)


FF_TM = 1024
FF_TF = 1024


def _ffn_kernel(x_ref, oa_ref, obt_ref, woa_ref, wob_ref, nw_ref, wu_ref, wd_ref,
                o_ref, h_ref):
    @pl.when(pl.program_id(2) == 0)
    def _():
        o_b = jnp.concatenate([_dot_tn(obt_ref[j], wob_ref[...])
                               for j in range(FF_TM // AT_BLK)], axis=0)
        x1 = x_ref[...] + _dot(oa_ref[...], woa_ref[...]) + o_b
        ms = jnp.mean(x1 * x1, axis=-1, keepdims=True)
        h_ref[...] = (x1 * lax.rsqrt(ms + NORM_EPS) * nw_ref[...]).astype(BF16)
        o_ref[...] = x1

    u = jnp.maximum(_dot(h_ref[...], wu_ref[...]), 0.0)
    o_ref[...] += _dot((u * u).astype(BF16), wd_ref[...])


def _outproj_ffn(x3d, o_a, o_bt, w_out, norm_w, w_up, w_down):
    bsz, s, _ = x3d.shape
    return pl.pallas_call(
        _ffn_kernel,
        out_shape=jax.ShapeDtypeStruct(x3d.shape, F32),
        grid=(bsz, s // FF_TM, D_FF // FF_TF),
        in_specs=[
            pl.BlockSpec((None, FF_TM, D_MODEL), lambda b, i, j: (b, i, 0)),
            pl.BlockSpec((None, FF_TM, HG_WIDTH), lambda b, i, j: (b, i, 0)),
            pl.BlockSpec((None, FF_TM // AT_BLK, FX_WIDTH, AT_BLK), lambda b, i, j: (b, i, 0, 0)),
            pl.BlockSpec((HG_WIDTH, D_MODEL), lambda b, i, j: (0, 0)),
            pl.BlockSpec((FX_WIDTH, D_MODEL), lambda b, i, j: (1, 0)),
            pl.BlockSpec((1, D_MODEL), lambda b, i, j: (0, 0)),
            pl.BlockSpec((D_MODEL, FF_TF), lambda b, i, j: (0, j)),
            pl.BlockSpec((FF_TF, D_MODEL), lambda b, i, j: (j, 0)),
        ],
        out_specs=pl.BlockSpec((None, FF_TM, D_MODEL), lambda b, i, j: (b, i, 0)),
        scratch_shapes=[pltpu.VMEM((FF_TM, D_MODEL), BF16)],
        compiler_params=pltpu.CompilerParams(
            dimension_semantics=("arbitrary", "arbitrary", "arbitrary"),
            vmem_limit_bytes=VMEM_LIMIT),
        name="outproj_ffn",
    )(x3d, o_a, o_bt, w_out, w_out, norm_w, w_up, w_down)


def _layer_lower_bounds(lower_bounds):
    p = jax.nn.softmax(lower_bounds.astype(F32), axis=0)
    c = jnp.cumsum(p, axis=0)
    return c - c[0:1]


def kernel(x, lower_bounds, norm1_w, w_in, fox_f_bias, q_norm_w, k_norm_w,
           hgrn_norm_w, w_out, norm2_w, w_up, w_down):
    bsz, s, d = x.shape
    depth = w_in.shape[0]
    nblk = s // AT_BLK
    lbs = _layer_lower_bounds(lower_bounds)
    w_in_pad = jnp.pad(w_in, ((0, 0), (0, 0), (0, PROJ_PAD - w_in.shape[-1]))).astype(BF16)
    w_out_b = w_out.astype(BF16)
    w_up_b = w_up.astype(BF16)
    w_down_b = w_down.astype(BF16)
    f_bias = jnp.pad(fox_f_bias, ((0, 0), (0, LANE - FX_HEADS)))

    for l in range(depth):
        proj, ff = _inproj(x.reshape(bsz * s, d), norm1_w[l][None, :], w_in_pad[l])
        proj3d = proj.reshape(bsz, s, PROJ_MAIN)
        ff3d = ff.reshape(bsz, s, LANE)
        o_a = _hgrn(proj3d, lbs[l][None, :], hgrn_norm_w[l][None, :])
        q_aug, k_aug, v_t, r_blk = _attn_prep(
            proj3d, ff3d, f_bias[l][None, :],
            jnp.tile(q_norm_w[l], FX_HEADS)[None, :],
            jnp.tile(k_norm_w[l], FX_HEADS)[None, :])
        qk_bound = (1.02 * FX_DH ** 0.5 * LOG2E * jnp.max(jnp.abs(q_norm_w[l]))
                    * jnp.max(jnp.abs(k_norm_w[l])))
        skip_thr = -(F32_EXP2_ZERO + 2.0 * qk_bound)
        r_flat = jnp.concatenate([
            jnp.transpose(r_blk[:, :, 0, :FX_HEADS], (0, 2, 1)).reshape(-1),
            skip_thr.reshape(1).astype(F32)])
        o_bt = _attention(r_flat, q_aug, k_aug, v_t)
        x = _outproj_ffn(x, o_a, o_bt, w_out_b[l], norm2_w[l][None, :],
                         w_up_b[l], w_down_b[l])
    return x
```

```python
import functools

import numpy as np
import jax
import jax.numpy as jnp
from jax import lax
from jax.experimental import pallas as pl
from jax.experimental.pallas import tpu as pltpu

F32 = jnp.float32
BF16 = jnp.bfloat16

D_MODEL = 1024
D_FF = 4 * D_MODEL
NORM_EPS = 1e-6

HG_WIDTH = 512
HG_HEADS = 4
HG_DK = 128
HG_CHUNK = 64
HG_LEVELS = 6

FX_WIDTH = 512
FX_HEADS = 8
FX_DH = 64

PROJ_MAIN = 4 * HG_WIDTH + 3 * FX_WIDTH
LANE = 128
PROJ_PAD = PROJ_MAIN + LANE

VMEM_LIMIT = 56 * 1024 * 1024

NEG_BIG = -1e30
LOG2E = 1.4426950408889634

NT_DIMS = (((1,), (1,)), ((), ()))
TN_DIMS = (((0,), (0,)), ((), ()))


def _dot(a, b):
    return jnp.dot(a, b, preferred_element_type=F32)


def _dot_nt(a, b):
    return lax.dot_general(a, b, NT_DIMS, preferred_element_type=F32)


def _dot_tn(a, b):
    return lax.dot_general(a, b, TN_DIMS, preferred_element_type=F32)


def _split2(x):
    hi = x.astype(BF16)
    lo = (x - hi.astype(F32)).astype(BF16)
    return hi, lo


def _split3(x):
    hi = x.astype(BF16)
    r = x - hi.astype(F32)
    mid = r.astype(BF16)
    lo = (r - mid.astype(F32)).astype(BF16)
    return hi, mid, lo


IN_TM = 512
IN_TN = 512


def _inproj_kernel(x_ref, nw_ref, w_ref, proj_ref, ff_ref):
    x = x_ref[...]
    ms = jnp.mean(x * x, axis=-1, keepdims=True)
    h = (x * lax.rsqrt(ms + NORM_EPS) * nw_ref[...]).astype(BF16)
    for c in range(PROJ_MAIN // IN_TN):
        cols = slice(c * IN_TN, (c + 1) * IN_TN)
        proj_ref[:, cols] = _dot(h, w_ref[:, cols]).astype(BF16)
    ff_ref[...] = _dot(h, w_ref[:, PROJ_MAIN:PROJ_PAD])


def _inproj(x2d, norm_w, w_pad, layer):
    t = x2d.shape[0]
    return pl.pallas_call(
        _inproj_kernel,
        out_shape=(jax.ShapeDtypeStruct((t, PROJ_MAIN), BF16),
                   jax.ShapeDtypeStruct((t, LANE), F32)),
        grid=(t // IN_TM,),
        in_specs=[
            pl.BlockSpec((IN_TM, D_MODEL), lambda i: (i, 0)),
            pl.BlockSpec((1, D_MODEL), lambda i: (0, 0)),
            pl.BlockSpec((None, D_MODEL, PROJ_PAD), lambda i: (layer, 0, 0)),
        ],
        out_specs=(pl.BlockSpec((IN_TM, PROJ_MAIN), lambda i: (i, 0)),
                   pl.BlockSpec((IN_TM, LANE), lambda i: (i, 0))),
        compiler_params=pltpu.CompilerParams(
            dimension_semantics=("arbitrary",), vmem_limit_bytes=VMEM_LIMIT),
        name="inproj",
    )(x2d, norm_w, w_pad)


HG_ROWS = 512
HG_NCHUNK = HG_ROWS // HG_CHUNK
HG_E_ROWS = (2 + HG_LEVELS) * HG_CHUNK


def _hgrn_constants():
    c = HG_CHUNK
    tri = np.tril(np.ones((c, c), np.float32))
    rows = [tri, 1.0 - tri]
    idx = np.arange(c)
    for lvl in range(HG_LEVELS):
        m = c >> (lvl + 1)
        ref = (idx // (2 * m)) * (2 * m) + m - 1
        upper = (idx % (2 * m) >= m)[:, None]
        diff = tri - tri[ref]
        rows.append(np.where(upper, diff, -diff))
    mat = np.concatenate(rows, axis=0)
    mcat = np.concatenate([mat, mat], axis=1)
    x = idx[:, None] ^ idx[None, :]
    top = np.floor(np.log2(np.maximum(x, 1))).astype(np.int32)
    level = (HG_LEVELS - 1) - top
    level = np.where(idx[:, None] == idx[None, :], HG_LEVELS, level)
    level = np.where(idx[:, None] < idx[None, :], -1, level)
    level = np.concatenate([level, level], axis=1)
    return mcat.astype(np.float32), level.astype(np.int32)


def _stack_heads(t):
    return jnp.concatenate([t[:, h * HG_DK:(h + 1) * HG_DK] for h in range(HG_HEADS)], axis=0)


def _hgrn_kernel(q_ref, f_ref, i_ref, g_ref, lb_ref, nw_ref, mcat_ref, lvl_ref,
                 o_ref, st_ref):
    @pl.when(pl.program_id(1) == 0)
    def _():
        st_ref[...] = jnp.zeros_like(st_ref)

    lb = lb_ref[...]
    nw = nw_ref[...]
    mcat = mcat_ref[...]
    level = lvl_ref[...]
    scale = HG_DK ** -0.5

    def chunk(c, carry):
        r0 = pl.multiple_of(c * HG_CHUNK, HG_CHUNK)
        rows = pl.ds(r0, HG_CHUNK)
        fp = f_ref[rows, :].astype(F32)
        f = lb + (1.0 - lb) * jax.nn.sigmoid(fp)
        lf = jnp.log(f) * LOG2E
        kk = 1.0 - f
        hi, lo = _split2(lf)
        e_all = _dot(mcat, jnp.concatenate([hi, lo], axis=0))
        b = e_all[0:HG_CHUNK]
        qs = q_ref[rows, :].astype(F32) * scale
        q_in = (qs * jnp.exp2(b)).astype(BF16)
        k_out = (kk * jnp.exp2(e_all[HG_CHUNK:2 * HG_CHUNK])).astype(BF16)
        dec = jnp.exp2(b[HG_CHUNK - 1:HG_CHUNK, :])
        qb = qs.astype(BF16)
        kb = kk.astype(BF16)
        v = i_ref[rows, :]
        g = g_ref[rows, :].astype(F32)

        a = [jnp.zeros((HG_CHUNK, LANE), F32) for _ in range(HG_HEADS)]
        for lvl in range(HG_LEVELS + 1):
            if lvl < HG_LEVELS:
                x = jnp.exp2(e_all[(2 + lvl) * HG_CHUNK:(3 + lvl) * HG_CHUNK]).astype(BF16)
                ql, kl = qb * x, kb * x
            else:
                ql, kl = qb, kb
            d = _dot_nt(_stack_heads(ql), _stack_heads(kl))
            for h in range(HG_HEADS):
                blk = d[h * HG_CHUNK:(h + 1) * HG_CHUNK, (h // 2) * LANE:(h // 2 + 1) * LANE]
                a[h] = jnp.where(level == lvl, blk, a[h])

        zeros = jnp.zeros((HG_CHUNK, HG_DK), BF16)
        for h in range(HG_HEADS):
            ls = slice(h * HG_DK, (h + 1) * HG_DK)
            v_pad = jnp.concatenate([v[:, ls], zeros] if h % 2 == 0 else [zeros, v[:, ls]], axis=0)
            st = st_ref[h]
            o = _dot(a[h].astype(BF16), v_pad) + _dot_nt(q_in[:, ls], st.astype(BF16))
            st_ref[h] = st * dec[:, ls] + _dot_tn(v[:, ls], k_out[:, ls])
            var = jnp.mean(o * o, axis=-1, keepdims=True)
            gh = g[:, ls]
            y = o * lax.rsqrt(var + NORM_EPS) * nw * (gh * jax.nn.sigmoid(gh))
            o_ref[rows, ls] = y.astype(o_ref.dtype)
        return carry

    lax.fori_loop(0, HG_NCHUNK, chunk, 0, unroll=4)


def _hgrn(proj3d, lb, norm_w):
    bsz, s, _ = proj3d.shape
    mcat, level = _hgrn_constants()

    def col(j):
        return pl.BlockSpec((None, HG_ROWS, HG_WIDTH), lambda b, i, j=j: (b, i, j))

    return pl.pallas_call(
        _hgrn_kernel,
        out_shape=jax.ShapeDtypeStruct((bsz, s, HG_WIDTH), BF16),
        grid=(bsz, s // HG_ROWS),
        in_specs=[
            col(0), col(1), col(2), col(3),
            pl.BlockSpec((1, HG_WIDTH), lambda b, i: (0, 0)),
            pl.BlockSpec((1, HG_DK), lambda b, i: (0, 0)),
            pl.BlockSpec((HG_E_ROWS, 2 * HG_CHUNK), lambda b, i: (0, 0)),
            pl.BlockSpec((HG_CHUNK, 2 * HG_CHUNK), lambda b, i: (0, 0)),
        ],
        out_specs=pl.BlockSpec((None, HG_ROWS, HG_WIDTH), lambda b, i: (b, i, 0)),
        scratch_shapes=[pltpu.VMEM((HG_HEADS, HG_DK, HG_DK), F32)],
        compiler_params=pltpu.CompilerParams(
            dimension_semantics=("arbitrary", "arbitrary"), vmem_limit_bytes=VMEM_LIMIT),
        name="hgrn2",
    )(proj3d, proj3d, proj3d, proj3d, lb, norm_w,
      jnp.asarray(mcat, BF16), jnp.asarray(level))


AT_BLK = 256
PREP_ROWS = 512
PREP_SUB = PREP_ROWS // AT_BLK
AUG_PIECES = 3
VT_ROWS = 80


def _aug_base(h):
    return h * LANE + (FX_DH if h % 2 == 0 else 0)


def _prep_constants():
    grp = np.zeros((FX_WIDTH, FX_WIDTH), np.float32)
    for h in range(FX_HEADS):
        grp[h * FX_DH:(h + 1) * FX_DH, h * FX_DH:(h + 1) * FX_DH] = 1.0 / FX_DH
    tri = np.tril(np.ones((AT_BLK, AT_BLK), np.float32))
    width = FX_HEADS * LANE
    ones_at = FX_HEADS * AUG_PIECES
    pk = np.zeros((LANE, width), np.float32)
    cq = np.zeros((1, LANE), np.float32)
    ck = np.zeros((1, width), np.float32)
    for p in range(AUG_PIECES):
        cq[0, ones_at + p] = 1.0
        for h in range(FX_HEADS):
            base = _aug_base(h)
            ck[0, base + FX_HEADS * p + h] = 1.0
            pk[FX_HEADS * p + h, base + ones_at + p] = -1.0
    return grp, tri, pk, cq, ck


def _prep_kernel(q_ref, k_ref, v_ref, ff_ref, fb_ref, qnw_ref, knw_ref,
                 grp_ref, tri_ref, pk_ref, cq_ref, ck_ref,
                 qa_ref, ka_ref, vt_ref, r_ref, carry_ref):
    @pl.when(pl.program_id(1) == 0)
    def _():
        carry_ref[...] = jnp.zeros_like(carry_ref)

    z = ff_ref[...] + fb_ref[...]
    lf = (jnp.minimum(z, 0.0) - jnp.log(1.0 + jnp.exp(-jnp.abs(z)))) * LOG2E
    grp = grp_ref[...]
    tri = tri_ref[...]
    lane = lax.broadcasted_iota(jnp.int32, (AT_BLK, LANE), 1)
    low_half = lane < FX_DH
    head_lane = lane < FX_HEADS

    def qk_norm(t, w):
        ms = _dot((t * t).astype(BF16), grp)
        return t * lax.rsqrt(ms + NORM_EPS) * w

    for j in range(PREP_SUB):
        rows = slice(j * AT_BLK, (j + 1) * AT_BLK)
        hi, mid, lo = _split3(lf[rows])
        c_rel = _dot(tri, hi) + _dot(tri, mid) + _dot(tri, lo)
        r_ref[j] = carry_ref[...]
        carry_ref[...] = carry_ref[...] + c_rel[AT_BLK - 1:AT_BLK, :]
        packed = None
        for p, piece in enumerate(_split3(c_rel)):
            piece = jnp.where(head_lane, piece.astype(F32), 0.0)
            piece = piece if p == 0 else pltpu.roll(piece, FX_HEADS * p, axis=1)
            packed = piece if packed is None else packed + piece
        aug_q_odd = packed + cq_ref[...]
        aug_q_even = pltpu.roll(aug_q_odd, FX_DH, axis=1)
        aug_k = _dot(packed.astype(BF16), pk_ref[...]) + ck_ref[...]
        qn = qk_norm(q_ref[rows, :].astype(F32), qnw_ref[...]) * (FX_DH ** -0.5 * LOG2E)
        kn = qk_norm(k_ref[rows, :].astype(F32), knw_ref[...])
        for h in range(FX_HEADS):
            pair = slice((h // 2) * LANE, (h // 2 + 1) * LANE)
            head = slice(h * LANE, (h + 1) * LANE)
            data = low_half if h % 2 == 0 else jnp.logical_not(low_half)
            aug_q = aug_q_even if h % 2 == 0 else aug_q_odd
            qa_ref[rows, head] = jnp.where(data, qn[:, pair], aug_q).astype(BF16)
            ka_ref[rows, head] = jnp.where(data, kn[:, pair], aug_k[:, head]).astype(BF16)
        v_t = v_ref[rows, :].astype(F32).T.astype(BF16)
        for h in range(FX_HEADS):
            vt_ref[j, h * VT_ROWS:h * VT_ROWS + FX_DH, :] = v_t[h * FX_DH:(h + 1) * FX_DH]
            vt_ref[j, h * VT_ROWS + FX_DH:(h + 1) * VT_ROWS, :] = jnp.ones(
                (VT_ROWS - FX_DH, AT_BLK), BF16)


def _attn_prep(proj3d, ff3d, f_bias, qnw, knw):
    bsz, s, _ = proj3d.shape
    nblk = s // AT_BLK
    consts = _prep_constants()
    grp, tri, pk = (jnp.asarray(c, BF16) for c in consts[:3])
    cq, ck = (jnp.asarray(c, F32) for c in consts[3:])
    width = FX_HEADS * LANE

    def col(j):
        return pl.BlockSpec((None, PREP_ROWS, FX_WIDTH), lambda b, i, j=j: (b, i, j))

    def full(shape):
        return pl.BlockSpec(shape, lambda b, i: (0,) * len(shape))

    return pl.pallas_call(
        _prep_kernel,
        out_shape=(jax.ShapeDtypeStruct((bsz, s, width), BF16),
                   jax.ShapeDtypeStruct((bsz, s, width), BF16),
                   jax.ShapeDtypeStruct((bsz, nblk, FX_HEADS * VT_ROWS, AT_BLK), BF16),
                   jax.ShapeDtypeStruct((bsz, nblk, 8, LANE), F32)),
        grid=(bsz, s // PREP_ROWS),
        in_specs=[
            col(4), col(5), col(6),
            pl.BlockSpec((None, PREP_ROWS, LANE), lambda b, i: (b, i, 0)),
            full((1, LANE)), full((1, FX_WIDTH)), full((1, FX_WIDTH)),
            full((FX_WIDTH, FX_WIDTH)), full((AT_BLK, AT_BLK)),
            full((LANE, width)), full((1, LANE)), full((1, width)),
        ],
        out_specs=(pl.BlockSpec((None, PREP_ROWS, width), lambda b, i: (b, i, 0)),
                   pl.BlockSpec((None, PREP_ROWS, width), lambda b, i: (b, i, 0)),
                   pl.BlockSpec((None, PREP_SUB, FX_HEADS * VT_ROWS, AT_BLK),
                                lambda b, i: (b, i, 0, 0)),
                   pl.BlockSpec((None, PREP_SUB, 8, LANE), lambda b, i: (b, i, 0, 0))),
        scratch_shapes=[pltpu.VMEM((8, LANE), F32)],
        compiler_params=pltpu.CompilerParams(
            dimension_semantics=("arbitrary", "arbitrary"), vmem_limit_bytes=VMEM_LIMIT),
        name="attn_prep",
    )(proj3d, proj3d, proj3d, ff3d, f_bias, qnw, knw, grp, tri, pk, cq, ck)


AT_HG = 2
AT_SLOTS = 3
AT_TRIP_TICKS = 2 * AT_SLOTS
F32_EXP2_ZERO = 150.0
SUBLANES = 8


def _tree_reduce(op, final, x):
    while x.shape[0] > SUBLANES:
        half = x.shape[0] // 2
        x = op(x[:half], x[half:])
    return final(x, axis=0, keepdims=True)


def _attn_kernel(r_ref, mask_ref, q_ref, k_ref, vt_ref, o_ref,
                 s_ref, p_ref, acc_ref, m_ref, a_ref, start_ref, *, nblk, n_r):
    b = pl.program_id(0)
    hg = pl.program_id(1)
    bases = [(b * FX_HEADS + hg * AT_HG + g) * nblk for g in range(AT_HG)]
    last = nblk - 1

    thr = r_ref[n_r]

    def find_start(qi, n_items):
        def needed(first):
            hit = None
            for base in bases:
                c = r_ref[base + qi] - r_ref[base + first] >= thr
                hit = c if hit is None else jnp.logical_or(hit, c)
            return hit

        first = lax.while_loop(lambda f: jnp.logical_and(f > 0, needed(f)),
                               lambda f: f - 1, qi)
        start_ref[qi] = first
        return n_items + (qi - first + 1)

    n_items = lax.fori_loop(0, nblk, find_start, 0)

    def next_item(item):
        qi, ki, valid = item
        at_diag = ki == qi
        more = jnp.logical_and(valid != 0,
                               jnp.logical_not(jnp.logical_and(at_diag, qi == last)))
        qn = jnp.where(at_diag, jnp.minimum(qi + 1, last), qi)
        kn = jnp.where(at_diag, start_ref[qn], ki + 1)
        return (jnp.where(more, qn, last), jnp.where(more, kn, last), more.astype(jnp.int32))

    def stage_a(slot, item):
        qi, ki, _ = item
        rows_q = pl.ds(pl.multiple_of(qi * AT_BLK, AT_BLK), AT_BLK)
        rows_k = pl.ds(pl.multiple_of(ki * AT_BLK, AT_BLK), AT_BLK)
        mask = mask_ref[(ki == qi).astype(jnp.int32)]
        for g in range(AT_HG):
            lanes = slice(g * LANE, (g + 1) * LANE)
            s_ref[slot, g] = _dot_nt(k_ref[rows_k, lanes], q_ref[rows_q, lanes]) + mask

    def stage_b(slot, item):
        qi, ki, valid = item
        live = valid != 0
        first = jnp.logical_and(live, ki == start_ref[qi])
        for g in range(AT_HG):
            delta = jnp.where(live, r_ref[bases[g] + qi] - r_ref[bases[g] + ki], NEG_BIG)
            s = s_ref[slot, g]
            m = jnp.where(first, NEG_BIG, m_ref[g])
            m_new = jnp.maximum(m, _tree_reduce(jnp.maximum, jnp.max, s) + delta)
            p_ref[slot, g] = jnp.exp2(s - (m_new - delta)).astype(BF16)
            a_ref[g] = jnp.exp2(m - m_new)
            m_ref[g] = m_new

    def stage_c(slot, item):
        qi, ki, _ = item
        for g in range(AT_HG):
            v_t = vt_ref[ki, g * VT_ROWS:(g + 1) * VT_ROWS, :]
            acc = a_ref[g] * acc_ref[g] + _dot(v_t, p_ref[slot, g])
            acc_ref[g] = acc
            o_ref[qi, g * FX_DH:(g + 1) * FX_DH, :] = (
                acc[:FX_DH] / acc[FX_DH:FX_DH + 1]).astype(o_ref.dtype)

    def tick(slot, items):
        item_c, item_b, item_s, item_a = items
        ahead = (slot + 2) % AT_SLOTS
        stage_c(ahead, item_c)
        stage_b(slot, item_b)
        stage_a(ahead, item_a)
        return (item_b, item_s, item_a, next_item(item_a))

    for g in range(AT_HG):
        p_ref[AT_SLOTS - 1, g] = jnp.zeros((AT_BLK, AT_BLK), BF16)
        a_ref[g] = jnp.ones((1, AT_BLK), F32)
        m_ref[g] = jnp.zeros((1, AT_BLK), F32)
        acc_ref[g] = jnp.ones((VT_ROWS, AT_BLK), F32)

    zero = jnp.int32(0)
    item0 = (zero, zero, jnp.int32(1))
    item1 = next_item(item0)
    idle = (zero, zero, zero)
    stage_a(0, item0)
    stage_a(1, item1)

    def trip(_, items):
        for t in range(AT_TRIP_TICKS):
            items = tick(t % AT_SLOTS, items)
        return items

    lax.fori_loop(0, (n_items + AT_TRIP_TICKS) // AT_TRIP_TICKS, trip,
                  (idle, item0, item1, next_item(item1)))


def _attention(r_flat, q_aug, k_aug, v_t):
    bsz, s, _ = q_aug.shape
    nblk = s // AT_BLK
    causal = np.where(np.arange(AT_BLK)[:, None] <= np.arange(AT_BLK)[None, :], 0.0, NEG_BIG)
    mask = jnp.asarray(np.stack([np.zeros_like(causal), causal]), F32)
    return pl.pallas_call(
        functools.partial(_attn_kernel, nblk=nblk, n_r=r_flat.shape[0] - 1),
        out_shape=jax.ShapeDtypeStruct((bsz, nblk, FX_WIDTH, AT_BLK), BF16),
        grid=(bsz, FX_HEADS // AT_HG),
        in_specs=[
            pl.BlockSpec(memory_space=pltpu.SMEM),
            pl.BlockSpec((2, AT_BLK, AT_BLK), lambda b, h: (0, 0, 0)),
            pl.BlockSpec((None, s, AT_HG * LANE), lambda b, h: (b, 0, h)),
            pl.BlockSpec((None, s, AT_HG * LANE), lambda b, h: (b, 0, h)),
            pl.BlockSpec((None, nblk, AT_HG * VT_ROWS, AT_BLK), lambda b, h: (b, 0, h, 0)),
        ],
        out_specs=pl.BlockSpec((None, nblk, AT_HG * FX_DH, AT_BLK), lambda b, h: (b, 0, h, 0)),
        scratch_shapes=[
            pltpu.VMEM((AT_SLOTS, AT_HG, AT_BLK, AT_BLK), F32),
            pltpu.VMEM((AT_SLOTS, AT_HG, AT_BLK, AT_BLK), BF16),
            pltpu.VMEM((AT_HG, VT_ROWS, AT_BLK), F32),
            pltpu.VMEM((AT_HG, 1, AT_BLK), F32),
            pltpu.VMEM((AT_HG, 1, AT_BLK), F32),
            pltpu.SMEM((nblk,), jnp.int32),
        ],
        compiler_params=pltpu.CompilerParams(
            dimension_semantics=("arbitrary", "arbitrary"),
            vmem_limit_bytes=VMEM_LIMIT),
        name="fox_attention",
    )(r_flat, mask, q_aug, k_aug, v_t)


FF_TM = 1024
FF_TF = 1024


def _ffn_kernel(x_ref, oa_ref, obt_ref, woa_ref, wob_ref, nw_ref, wu_ref, wd_ref,
                o_ref, h_ref):
    @pl.when(pl.program_id(2) == 0)
    def _():
        o_b = jnp.concatenate([_dot_tn(obt_ref[j], wob_ref[...])
                               for j in range(FF_TM // AT_BLK)], axis=0)
        x1 = x_ref[...] + _dot(oa_ref[...], woa_ref[...]) + o_b
        ms = jnp.mean(x1 * x1, axis=-1, keepdims=True)
        h_ref[...] = (x1 * lax.rsqrt(ms + NORM_EPS) * nw_ref[...]).astype(BF16)
        o_ref[...] = x1

    u = jnp.maximum(_dot(h_ref[...], wu_ref[...]), 0.0)
    o_ref[...] += _dot((u * u).astype(BF16), wd_ref[...])


def _outproj_ffn(x3d, o_a, o_bt, w_out, norm_w, w_up, w_down, layer):
    bsz, s, _ = x3d.shape
    return pl.pallas_call(
        _ffn_kernel,
        out_shape=jax.ShapeDtypeStruct(x3d.shape, F32),
        grid=(bsz, s // FF_TM, D_FF // FF_TF),
        in_specs=[
            pl.BlockSpec((None, FF_TM, D_MODEL), lambda b, i, j: (b, i, 0)),
            pl.BlockSpec((None, FF_TM, HG_WIDTH), lambda b, i, j: (b, i, 0)),
            pl.BlockSpec((None, FF_TM // AT_BLK, FX_WIDTH, AT_BLK), lambda b, i, j: (b, i, 0, 0)),
            pl.BlockSpec((None, HG_WIDTH, D_MODEL), lambda b, i, j: (layer, 0, 0)),
            pl.BlockSpec((None, FX_WIDTH, D_MODEL), lambda b, i, j: (layer, 1, 0)),
            pl.BlockSpec((1, D_MODEL), lambda b, i, j: (0, 0)),
            pl.BlockSpec((None, D_MODEL, FF_TF), lambda b, i, j: (layer, 0, j)),
            pl.BlockSpec((None, FF_TF, D_MODEL), lambda b, i, j: (layer, j, 0)),
        ],
        out_specs=pl.BlockSpec((None, FF_TM, D_MODEL), lambda b, i, j: (b, i, 0)),
        scratch_shapes=[pltpu.VMEM((FF_TM, D_MODEL), BF16)],
        compiler_params=pltpu.CompilerParams(
            dimension_semantics=("arbitrary", "arbitrary", "arbitrary"),
            vmem_limit_bytes=VMEM_LIMIT),
        name="outproj_ffn",
    )(x3d, o_a, o_bt, w_out, w_out, norm_w, w_up, w_down)


def _layer_lower_bounds(lower_bounds):
    p = jax.nn.softmax(lower_bounds.astype(F32), axis=0)
    c = jnp.cumsum(p, axis=0)
    return c - c[0:1]


def kernel(x, lower_bounds, norm1_w, w_in, fox_f_bias, q_norm_w, k_norm_w,
           hgrn_norm_w, w_out, norm2_w, w_up, w_down):
    bsz, s, d = x.shape
    depth = w_in.shape[0]
    nblk = s // AT_BLK
    lbs = _layer_lower_bounds(lower_bounds)
    w_in_pad = jnp.pad(w_in, ((0, 0), (0, 0), (0, PROJ_PAD - w_in.shape[-1]))).astype(BF16)
    w_out_b = w_out.astype(BF16)
    w_up_b = w_up.astype(BF16)
    w_down_b = w_down.astype(BF16)
    f_bias = jnp.pad(fox_f_bias, ((0, 0), (0, LANE - FX_HEADS)))

    for l in range(depth):
        proj, ff = _inproj(x.reshape(bsz * s, d), norm1_w[l][None, :], w_in_pad, l)
        proj3d = proj.reshape(bsz, s, PROJ_MAIN)
        ff3d = ff.reshape(bsz, s, LANE)
        o_a = _hgrn(proj3d, lbs[l][None, :], hgrn_norm_w[l][None, :])
        q_aug, k_aug, v_t, r_blk = _attn_prep(
            proj3d, ff3d, f_bias[l][None, :],
            jnp.tile(q_norm_w[l], FX_HEADS)[None, :],
            jnp.tile(k_norm_w[l], FX_HEADS)[None, :])
        qk_bound = (1.02 * FX_DH ** 0.5 * LOG2E * jnp.max(jnp.abs(q_norm_w[l]))
                    * jnp.max(jnp.abs(k_norm_w[l])))
        skip_thr = -(F32_EXP2_ZERO + 2.0 * qk_bound)
        r_flat = jnp.concatenate([
            jnp.transpose(r_blk[:, :, 0, :FX_HEADS], (0, 2, 1)).reshape(-1),
            skip_thr.reshape(1).astype(F32)])
        o_bt = _attention(r_flat, q_aug, k_aug, v_t)
        x = _outproj_ffn(x, o_a, o_bt, w_out_b, norm2_w[l][None, :], w_up_b, w_down_b, l)
    return x
```

```python
import functools

import numpy as np
import jax
import jax.numpy as jnp
from jax import lax
from jax.experimental import pallas as pl
from jax.experimental.pallas import tpu as pltpu

F32 = jnp.float32
BF16 = jnp.bfloat16

D_MODEL = 1024
D_FF = 4 * D_MODEL
NORM_EPS = 1e-6

HG_WIDTH = 512
HG_HEADS = 4
HG_DK = 128
HG_CHUNK = 64
HG_LEVELS = 6

FX_WIDTH = 512
FX_HEADS = 8
FX_DH = 64

PROJ_MAIN = 4 * HG_WIDTH + 3 * FX_WIDTH
LANE = 128
PROJ_PAD = PROJ_MAIN + LANE

VMEM_LIMIT = 56 * 1024 * 1024

NEG_BIG = -1e30
LOG2E = 1.4426950408889634

NT_DIMS = (((1,), (1,)), ((), ()))
TN_DIMS = (((0,), (0,)), ((), ()))


def _dot(a, b):
    return jnp.dot(a, b, preferred_element_type=F32)


def _dot_nt(a, b):
    return lax.dot_general(a, b, NT_DIMS, preferred_element_type=F32)


def _dot_tn(a, b):
    return lax.dot_general(a, b, TN_DIMS, preferred_element_type=F32)


def _split2(x):
    hi = x.astype(BF16)
    lo = (x - hi.astype(F32)).astype(BF16)
    return hi, lo


def _split3(x):
    hi = x.astype(BF16)
    r = x - hi.astype(F32)
    mid = r.astype(BF16)
    lo = (r - mid.astype(F32)).astype(BF16)
    return hi, mid, lo


IN_TM = 1024
IN_TN = 512


def _inproj_kernel(x_ref, nw_ref, w_ref, proj_ref, ff_ref):
    x = x_ref[...]
    ms = jnp.mean(x * x, axis=-1, keepdims=True)
    h = (x * lax.rsqrt(ms + NORM_EPS) * nw_ref[...]).astype(BF16)
    for c in range(PROJ_MAIN // IN_TN):
        cols = slice(c * IN_TN, (c + 1) * IN_TN)
        proj_ref[:, cols] = _dot(h, w_ref[:, cols]).astype(BF16)
    ff_ref[...] = _dot(h, w_ref[:, PROJ_MAIN:PROJ_PAD])


def _inproj(x2d, norm_w, w_pad, layer):
    t = x2d.shape[0]
    return pl.pallas_call(
        _inproj_kernel,
        out_shape=(jax.ShapeDtypeStruct((t, PROJ_MAIN), BF16),
                   jax.ShapeDtypeStruct((t, LANE), F32)),
        grid=(t // IN_TM,),
        in_specs=[
            pl.BlockSpec((IN_TM, D_MODEL), lambda i: (i, 0)),
            pl.BlockSpec((1, D_MODEL), lambda i: (0, 0)),
            pl.BlockSpec((None, D_MODEL, PROJ_PAD), lambda i: (layer, 0, 0)),
        ],
        out_specs=(pl.BlockSpec((IN_TM, PROJ_MAIN), lambda i: (i, 0)),
                   pl.BlockSpec((IN_TM, LANE), lambda i: (i, 0))),
        compiler_params=pltpu.CompilerParams(
            dimension_semantics=("arbitrary",), vmem_limit_bytes=VMEM_LIMIT),
        name="inproj",
    )(x2d, norm_w, w_pad)


HG_ROWS = 512
HG_NCHUNK = HG_ROWS // HG_CHUNK
HG_COARSE = 3
HG_E_ROWS = (1 + HG_LEVELS - HG_COARSE) * HG_CHUNK


def _hgrn_constants():
    c = HG_CHUNK
    tri = np.tril(np.ones((c, c), np.float32))
    rows = [tri]
    idx = np.arange(c)
    for lvl in range(HG_COARSE, HG_LEVELS):
        m = c >> (lvl + 1)
        ref = (idx // (2 * m)) * (2 * m) + m - 1
        upper = (idx % (2 * m) >= m)[:, None]
        diff = tri - tri[ref]
        rows.append(np.where(upper, diff, -diff))
    mat = np.concatenate(rows, axis=0)
    mcat = np.concatenate([mat, mat], axis=1)
    x = idx[:, None] ^ idx[None, :]
    top = np.floor(np.log2(np.maximum(x, 1))).astype(np.int32)
    level = (HG_LEVELS - 1) - top
    level = np.where(idx[:, None] == idx[None, :], HG_LEVELS, level)
    level = np.where(idx[:, None] < idx[None, :], -1, level)
    level = np.concatenate([level, level], axis=1)
    return mcat.astype(np.float32), level.astype(np.int32)


def _coarse_level_exponent(b, m):
    parts = []
    for r0 in range(0, HG_CHUNK, 2 * m):
        ref = b[r0 + m - 1:r0 + m, :]
        parts += [ref - b[r0:r0 + m], b[r0 + m:r0 + 2 * m] - ref]
    return jnp.concatenate(parts, axis=0)


def _stack_heads(t):
    return jnp.concatenate([t[:, h * HG_DK:(h + 1) * HG_DK] for h in range(HG_HEADS)], axis=0)


def _hgrn_kernel(q_ref, f_ref, i_ref, g_ref, lb_ref, nw_ref, mcat_ref, lvl_ref,
                 o_ref, st_ref):
    @pl.when(pl.program_id(1) == 0)
    def _():
        st_ref[...] = jnp.zeros_like(st_ref)

    lb = lb_ref[...]
    nw = nw_ref[...]
    mcat = mcat_ref[...]
    level = lvl_ref[...]
    scale = HG_DK ** -0.5

    def chunk(c, carry):
        r0 = pl.multiple_of(c * HG_CHUNK, HG_CHUNK)
        rows = pl.ds(r0, HG_CHUNK)
        fp = f_ref[rows, :].astype(F32)
        f = lb + (1.0 - lb) * jax.nn.sigmoid(fp)
        lf = jnp.log(f) * LOG2E
        kk = 1.0 - f
        hi, lo = _split2(lf)
        e_all = _dot(mcat, jnp.concatenate([hi, lo], axis=0))
        b = e_all[0:HG_CHUNK]
        qs = q_ref[rows, :].astype(F32) * scale
        q_in = (qs * jnp.exp2(b)).astype(BF16)
        k_out = (kk * jnp.exp2(b[HG_CHUNK - 1:HG_CHUNK, :] - b)).astype(BF16)
        dec = jnp.exp2(b[HG_CHUNK - 1:HG_CHUNK, :])
        qb = qs.astype(BF16)
        kb = kk.astype(BF16)
        v = i_ref[rows, :]
        g = g_ref[rows, :].astype(F32)

        a = [jnp.zeros((HG_CHUNK, LANE), F32) for _ in range(HG_HEADS)]
        for lvl in range(HG_LEVELS + 1):
            if lvl < HG_LEVELS:
                if lvl < HG_COARSE:
                    e = _coarse_level_exponent(b, HG_CHUNK >> (lvl + 1))
                else:
                    fine = 1 + lvl - HG_COARSE
                    e = e_all[fine * HG_CHUNK:(fine + 1) * HG_CHUNK]
                x = jnp.exp2(e).astype(BF16)
                ql, kl = qb * x, kb * x
            else:
                ql, kl = qb, kb
            d = _dot_nt(_stack_heads(ql), _stack_heads(kl))
            for h in range(HG_HEADS):
                blk = d[h * HG_CHUNK:(h + 1) * HG_CHUNK, (h // 2) * LANE:(h // 2 + 1) * LANE]
                a[h] = jnp.where(level == lvl, blk, a[h])

        zeros = jnp.zeros((HG_CHUNK, HG_DK), BF16)
        for h in range(HG_HEADS):
            ls = slice(h * HG_DK, (h + 1) * HG_DK)
            v_pad = jnp.concatenate([v[:, ls], zeros] if h % 2 == 0 else [zeros, v[:, ls]], axis=0)
            st = st_ref[h]
            o = _dot(a[h].astype(BF16), v_pad) + _dot_nt(q_in[:, ls], st.astype(BF16))
            st_ref[h] = st * dec[:, ls] + _dot_tn(v[:, ls], k_out[:, ls])
            var = jnp.mean(o * o, axis=-1, keepdims=True)
            gh = g[:, ls]
            y = o * lax.rsqrt(var + NORM_EPS) * nw * (gh * jax.nn.sigmoid(gh))
            o_ref[rows, ls] = y.astype(o_ref.dtype)
        return carry

    lax.fori_loop(0, HG_NCHUNK, chunk, 0, unroll=4)


def _hgrn(proj3d, lb, norm_w):
    bsz, s, _ = proj3d.shape
    mcat, level = _hgrn_constants()

    def col(j):
        return pl.BlockSpec((None, HG_ROWS, HG_WIDTH), lambda b, i, j=j: (b, i, j))

    return pl.pallas_call(
        _hgrn_kernel,
        out_shape=jax.ShapeDtypeStruct((bsz, s, HG_WIDTH), BF16),
        grid=(bsz, s // HG_ROWS),
        in_specs=[
            col(0), col(1), col(2), col(3),
            pl.BlockSpec((1, HG_WIDTH), lambda b, i: (0, 0)),
            pl.BlockSpec((1, HG_DK), lambda b, i: (0, 0)),
            pl.BlockSpec((HG_E_ROWS, 2 * HG_CHUNK), lambda b, i: (0, 0)),
            pl.BlockSpec((HG_CHUNK, 2 * HG_CHUNK), lambda b, i: (0, 0)),
        ],
        out_specs=pl.BlockSpec((None, HG_ROWS, HG_WIDTH), lambda b, i: (b, i, 0)),
        scratch_shapes=[pltpu.VMEM((HG_HEADS, HG_DK, HG_DK), F32)],
        compiler_params=pltpu.CompilerParams(
            dimension_semantics=("arbitrary", "arbitrary"), vmem_limit_bytes=VMEM_LIMIT),
        name="hgrn2",
    )(proj3d, proj3d, proj3d, proj3d, lb, norm_w,
      jnp.asarray(mcat, BF16), jnp.asarray(level))


AT_BLK = 256
PREP_ROWS = 512
PREP_SUB = PREP_ROWS // AT_BLK
AUG_PIECES = 3
VT_ROWS = 80


def _aug_base(h):
    return h * LANE + (FX_DH if h % 2 == 0 else 0)


def _prep_constants():
    grp = np.zeros((FX_WIDTH, FX_WIDTH), np.float32)
    for h in range(FX_HEADS):
        grp[h * FX_DH:(h + 1) * FX_DH, h * FX_DH:(h + 1) * FX_DH] = 1.0 / FX_DH
    tri = np.tril(np.ones((AT_BLK, AT_BLK), np.float32))
    width = FX_HEADS * LANE
    ones_at = FX_HEADS * AUG_PIECES
    pk = np.zeros((LANE, width), np.float32)
    cq = np.zeros((1, LANE), np.float32)
    ck = np.zeros((1, width), np.float32)
    for p in range(AUG_PIECES):
        cq[0, ones_at + p] = 1.0
        for h in range(FX_HEADS):
            base = _aug_base(h)
            ck[0, base + FX_HEADS * p + h] = 1.0
            pk[FX_HEADS * p + h, base + ones_at + p] = -1.0
    return grp, tri, pk, cq, ck


def _prep_kernel(q_ref, k_ref, v_ref, ff_ref, fb_ref, qnw_ref, knw_ref,
                 grp_ref, tri_ref, pk_ref, cq_ref, ck_ref,
                 qa_ref, ka_ref, vt_ref, r_ref, carry_ref):
    @pl.when(pl.program_id(1) == 0)
    def _():
        carry_ref[...] = jnp.zeros_like(carry_ref)

    z = ff_ref[...] + fb_ref[...]
    lf = (jnp.minimum(z, 0.0) - jnp.log(1.0 + jnp.exp(-jnp.abs(z)))) * LOG2E
    grp = grp_ref[...]
    tri = tri_ref[...]
    lane = lax.broadcasted_iota(jnp.int32, (AT_BLK, LANE), 1)
    low_half = lane < FX_DH
    head_lane = lane < FX_HEADS

    def qk_norm(t, w):
        ms = _dot((t * t).astype(BF16), grp)
        return t * lax.rsqrt(ms + NORM_EPS) * w

    for j in range(PREP_SUB):
        rows = slice(j * AT_BLK, (j + 1) * AT_BLK)
        hi, mid, lo = _split3(lf[rows])
        c_rel = _dot(tri, hi) + _dot(tri, mid) + _dot(tri, lo)
        r_ref[j] = carry_ref[...]
        carry_ref[...] = carry_ref[...] + c_rel[AT_BLK - 1:AT_BLK, :]
        packed = None
        for p, piece in enumerate(_split3(c_rel)):
            piece = jnp.where(head_lane, piece.astype(F32), 0.0)
            piece = piece if p == 0 else pltpu.roll(piece, FX_HEADS * p, axis=1)
            packed = piece if packed is None else packed + piece
        aug_q_odd = packed + cq_ref[...]
        aug_q_even = pltpu.roll(aug_q_odd, FX_DH, axis=1)
        aug_k = _dot(packed.astype(BF16), pk_ref[...]) + ck_ref[...]
        qn = qk_norm(q_ref[rows, :].astype(F32), qnw_ref[...]) * (FX_DH ** -0.5 * LOG2E)
        kn = qk_norm(k_ref[rows, :].astype(F32), knw_ref[...])
        for h in range(FX_HEADS):
            pair = slice((h // 2) * LANE, (h // 2 + 1) * LANE)
            head = slice(h * LANE, (h + 1) * LANE)
            data = low_half if h % 2 == 0 else jnp.logical_not(low_half)
            aug_q = aug_q_even if h % 2 == 0 else aug_q_odd
            qa_ref[rows, head] = jnp.where(data, qn[:, pair], aug_q).astype(BF16)
            ka_ref[rows, head] = jnp.where(data, kn[:, pair], aug_k[:, head]).astype(BF16)
        v_t = v_ref[rows, :].astype(F32).T.astype(BF16)
        for h in range(FX_HEADS):
            vt_ref[j, h * VT_ROWS:h * VT_ROWS + FX_DH, :] = v_t[h * FX_DH:(h + 1) * FX_DH]
            vt_ref[j, h * VT_ROWS + FX_DH:(h + 1) * VT_ROWS, :] = jnp.ones(
                (VT_ROWS - FX_DH, AT_BLK), BF16)


def _attn_prep(proj3d, ff3d, f_bias, qnw, knw):
    bsz, s, _ = proj3d.shape
    nblk = s // AT_BLK
    consts = _prep_constants()
    grp, tri, pk = (jnp.asarray(c, BF16) for c in consts[:3])
    cq, ck = (jnp.asarray(c, F32) for c in consts[3:])
    width = FX_HEADS * LANE

    def col(j):
        return pl.BlockSpec((None, PREP_ROWS, FX_WIDTH), lambda b, i, j=j: (b, i, j))

    def full(shape):
        return pl.BlockSpec(shape, lambda b, i: (0,) * len(shape))

    return pl.pallas_call(
        _prep_kernel,
        out_shape=(jax.ShapeDtypeStruct((bsz, s, width), BF16),
                   jax.ShapeDtypeStruct((bsz, s, width), BF16),
                   jax.ShapeDtypeStruct((bsz, nblk, FX_HEADS * VT_ROWS, AT_BLK), BF16),
                   jax.ShapeDtypeStruct((bsz, nblk, 8, LANE), F32)),
        grid=(bsz, s // PREP_ROWS),
        in_specs=[
            col(4), col(5), col(6),
            pl.BlockSpec((None, PREP_ROWS, LANE), lambda b, i: (b, i, 0)),
            full((1, LANE)), full((1, FX_WIDTH)), full((1, FX_WIDTH)),
            full((FX_WIDTH, FX_WIDTH)), full((AT_BLK, AT_BLK)),
            full((LANE, width)), full((1, LANE)), full((1, width)),
        ],
        out_specs=(pl.BlockSpec((None, PREP_ROWS, width), lambda b, i: (b, i, 0)),
                   pl.BlockSpec((None, PREP_ROWS, width), lambda b, i: (b, i, 0)),
                   pl.BlockSpec((None, PREP_SUB, FX_HEADS * VT_ROWS, AT_BLK),
                                lambda b, i: (b, i, 0, 0)),
                   pl.BlockSpec((None, PREP_SUB, 8, LANE), lambda b, i: (b, i, 0, 0))),
        scratch_shapes=[pltpu.VMEM((8, LANE), F32)],
        compiler_params=pltpu.CompilerParams(
            dimension_semantics=("arbitrary", "arbitrary"), vmem_limit_bytes=VMEM_LIMIT),
        name="attn_prep",
    )(proj3d, proj3d, proj3d, ff3d, f_bias, qnw, knw, grp, tri, pk, cq, ck)


AT_HG = 2
AT_SLOTS = 3
AT_TRIP_TICKS = 4 * AT_SLOTS
F32_EXP2_ZERO = 150.0
SUBLANES = 8


def _tree_reduce(op, final, x):
    while x.shape[0] > SUBLANES:
        half = x.shape[0] // 2
        x = op(x[:half], x[half:])
    return final(x, axis=0, keepdims=True)


def _attn_kernel(r_ref, mask_ref, q_ref, k_ref, vt_ref, o_ref,
                 s_ref, p_ref, acc_ref, m_ref, a_ref, start_ref, *, nblk, n_r):
    b = pl.program_id(0)
    hg = pl.program_id(1)
    bases = [(b * FX_HEADS + hg * AT_HG + g) * nblk for g in range(AT_HG)]
    last = nblk - 1

    thr = r_ref[n_r]

    def find_start(qi, n_items):
        def needed(first):
            hit = None
            for base in bases:
                c = r_ref[base + qi] - r_ref[base + first] >= thr
                hit = c if hit is None else jnp.logical_or(hit, c)
            return hit

        first = lax.while_loop(lambda f: jnp.logical_and(f > 0, needed(f)),
                               lambda f: f - 1, qi)
        start_ref[qi] = first
        return n_items + (qi - first + 1)

    n_items = lax.fori_loop(0, nblk, find_start, 0)

    def next_item(item):
        qi, ki, valid = item
        at_diag = ki == qi
        more = jnp.logical_and(valid != 0,
                               jnp.logical_not(jnp.logical_and(at_diag, qi == last)))
        qn = jnp.where(at_diag, jnp.minimum(qi + 1, last), qi)
        kn = jnp.where(at_diag, start_ref[qn], ki + 1)
        return (jnp.where(more, qn, last), jnp.where(more, kn, last), more.astype(jnp.int32))

    def stage_a(slot, item):
        qi, ki, _ = item
        rows_q = pl.ds(pl.multiple_of(qi * AT_BLK, AT_BLK), AT_BLK)
        rows_k = pl.ds(pl.multiple_of(ki * AT_BLK, AT_BLK), AT_BLK)
        mask = mask_ref[(ki == qi).astype(jnp.int32)]
        for g in range(AT_HG):
            lanes = slice(g * LANE, (g + 1) * LANE)
            s_ref[slot, g] = _dot_nt(k_ref[rows_k, lanes], q_ref[rows_q, lanes]) + mask

    def stage_b(slot, item):
        qi, ki, valid = item
        live = valid != 0
        first = jnp.logical_and(live, ki == start_ref[qi])
        for g in range(AT_HG):
            delta = jnp.where(live, r_ref[bases[g] + qi] - r_ref[bases[g] + ki], NEG_BIG)
            s = s_ref[slot, g]
            m = jnp.where(first, NEG_BIG, m_ref[g])
            m_new = jnp.maximum(m, _tree_reduce(jnp.maximum, jnp.max, s) + delta)
            p_ref[slot, g] = jnp.exp2(s - (m_new - delta)).astype(BF16)
            a_ref[g] = jnp.exp2(m - m_new)
            m_ref[g] = m_new

    def stage_c(slot, item):
        qi, ki, _ = item
        for g in range(AT_HG):
            v_t = vt_ref[ki, g * VT_ROWS:(g + 1) * VT_ROWS, :]
            acc = a_ref[g] * acc_ref[g] + _dot(v_t, p_ref[slot, g])
            acc_ref[g] = acc
            o_ref[qi, g * FX_DH:(g + 1) * FX_DH, :] = (
                acc[:FX_DH] / acc[FX_DH:FX_DH + 1]).astype(o_ref.dtype)

    def tick(slot, items):
        item_c, item_b, item_s, item_a = items
        ahead = (slot + 2) % AT_SLOTS
        stage_c(ahead, item_c)
        stage_b(slot, item_b)
        stage_a(ahead, item_a)
        return (item_b, item_s, item_a, next_item(item_a))

    for g in range(AT_HG):
        p_ref[AT_SLOTS - 1, g] = jnp.zeros((AT_BLK, AT_BLK), BF16)
        a_ref[g] = jnp.ones((1, AT_BLK), F32)
        m_ref[g] = jnp.zeros((1, AT_BLK), F32)
        acc_ref[g] = jnp.ones((VT_ROWS, AT_BLK), F32)

    zero = jnp.int32(0)
    item0 = (zero, zero, jnp.int32(1))
    item1 = next_item(item0)
    idle = (zero, zero, zero)
    stage_a(0, item0)
    stage_a(1, item1)

    def trip(_, items):
        for t in range(AT_TRIP_TICKS):
            items = tick(t % AT_SLOTS, items)
        return items

    lax.fori_loop(0, (n_items + AT_TRIP_TICKS) // AT_TRIP_TICKS, trip,
                  (idle, item0, item1, next_item(item1)))


def _attention(r_flat, q_aug, k_aug, v_t):
    bsz, s, _ = q_aug.shape
    nblk = s // AT_BLK
    causal = np.where(np.arange(AT_BLK)[:, None] <= np.arange(AT_BLK)[None, :], 0.0, NEG_BIG)
    mask = jnp.asarray(np.stack([np.zeros_like(causal), causal]), F32)
    return pl.pallas_call(
        functools.partial(_attn_kernel, nblk=nblk, n_r=r_flat.shape[0] - 1),
        out_shape=jax.ShapeDtypeStruct((bsz, nblk, FX_WIDTH, AT_BLK), BF16),
        grid=(bsz, FX_HEADS // AT_HG),
        in_specs=[
            pl.BlockSpec(memory_space=pltpu.SMEM),
            pl.BlockSpec((2, AT_BLK, AT_BLK), lambda b, h: (0, 0, 0)),
            pl.BlockSpec((None, s, AT_HG * LANE), lambda b, h: (b, 0, h)),
            pl.BlockSpec((None, s, AT_HG * LANE), lambda b, h: (b, 0, h)),
            pl.BlockSpec((None, nblk, AT_HG * VT_ROWS, AT_BLK), lambda b, h: (b, 0, h, 0)),
        ],
        out_specs=pl.BlockSpec((None, nblk, AT_HG * FX_DH, AT_BLK), lambda b, h: (b, 0, h, 0)),
        scratch_shapes=[
            pltpu.VMEM((AT_SLOTS, AT_HG, AT_BLK, AT_BLK), F32),
            pltpu.VMEM((AT_SLOTS, AT_HG, AT_BLK, AT_BLK), BF16),
            pltpu.VMEM((AT_HG, VT_ROWS, AT_BLK), F32),
            pltpu.VMEM((AT_HG, 1, AT_BLK), F32),
            pltpu.VMEM((AT_HG, 1, AT_BLK), F32),
            pltpu.SMEM((nblk,), jnp.int32),
        ],
        compiler_params=pltpu.CompilerParams(
            dimension_semantics=("arbitrary", "arbitrary"),
            vmem_limit_bytes=VMEM_LIMIT),
        name="fox_attention",
    )(r_flat, mask, q_aug, k_aug, v_t)


FF_TM = 1024
FF_TF = 1024


def _ffn_kernel(x_ref, oa_ref, obt_ref, woa_ref, wob_ref, nw_ref, wu_ref, wd_ref,
                o_ref, h_ref):
    @pl.when(pl.program_id(2) == 0)
    def _():
        o_b = jnp.concatenate([_dot_tn(obt_ref[j], wob_ref[...])
                               for j in range(FF_TM // AT_BLK)], axis=0)
        x1 = x_ref[...] + _dot(oa_ref[...], woa_ref[...]) + o_b
        ms = jnp.mean(x1 * x1, axis=-1, keepdims=True)
        h_ref[...] = (x1 * lax.rsqrt(ms + NORM_EPS) * nw_ref[...]).astype(BF16)
        o_ref[...] = x1

    u = jnp.maximum(_dot(h_ref[...], wu_ref[...]), 0.0)
    o_ref[...] += _dot((u * u).astype(BF16), wd_ref[...])


def _outproj_ffn(x3d, o_a, o_bt, w_out, norm_w, w_up, w_down, layer):
    bsz, s, _ = x3d.shape
    return pl.pallas_call(
        _ffn_kernel,
        out_shape=jax.ShapeDtypeStruct(x3d.shape, F32),
        grid=(bsz, s // FF_TM, D_FF // FF_TF),
        in_specs=[
            pl.BlockSpec((None, FF_TM, D_MODEL), lambda b, i, j: (b, i, 0)),
            pl.BlockSpec((None, FF_TM, HG_WIDTH), lambda b, i, j: (b, i, 0)),
            pl.BlockSpec((None, FF_TM // AT_BLK, FX_WIDTH, AT_BLK), lambda b, i, j: (b, i, 0, 0)),
            pl.BlockSpec((None, HG_WIDTH, D_MODEL), lambda b, i, j: (layer, 0, 0)),
            pl.BlockSpec((None, FX_WIDTH, D_MODEL), lambda b, i, j: (layer, 1, 0)),
            pl.BlockSpec((1, D_MODEL), lambda b, i, j: (0, 0)),
            pl.BlockSpec((None, D_MODEL, FF_TF), lambda b, i, j: (layer, 0, j)),
            pl.BlockSpec((None, FF_TF, D_MODEL), lambda b, i, j: (layer, j, 0)),
        ],
        out_specs=pl.BlockSpec((None, FF_TM, D_MODEL), lambda b, i, j: (b, i, 0)),
        scratch_shapes=[pltpu.VMEM((FF_TM, D_MODEL), BF16)],
        compiler_params=pltpu.CompilerParams(
            dimension_semantics=("arbitrary", "arbitrary", "arbitrary"),
            vmem_limit_bytes=VMEM_LIMIT),
        name="outproj_ffn",
    )(x3d, o_a, o_bt, w_out, w_out, norm_w, w_up, w_down)


def _layer_lower_bounds(lower_bounds):
    p = jax.nn.softmax(lower_bounds.astype(F32), axis=0)
    c = jnp.cumsum(p, axis=0)
    return c - c[0:1]


def kernel(x, lower_bounds, norm1_w, w_in, fox_f_bias, q_norm_w, k_norm_w,
           hgrn_norm_w, w_out, norm2_w, w_up, w_down):
    bsz, s, d = x.shape
    depth = w_in.shape[0]
    nblk = s // AT_BLK
    lbs = _layer_lower_bounds(lower_bounds)
    w_in_pad = jnp.pad(w_in, ((0, 0), (0, 0), (0, PROJ_PAD - w_in.shape[-1]))).astype(BF16)
    w_out_b = w_out.astype(BF16)
    w_up_b = w_up.astype(BF16)
    w_down_b = w_down.astype(BF16)
    f_bias = jnp.pad(fox_f_bias, ((0, 0), (0, LANE - FX_HEADS)))

    for l in range(depth):
        proj, ff = _inproj(x.reshape(bsz * s, d), norm1_w[l][None, :], w_in_pad, l)
        proj3d = proj.reshape(bsz, s, PROJ_MAIN)
        ff3d = ff.reshape(bsz, s, LANE)
        o_a = _hgrn(proj3d, lbs[l][None, :], hgrn_norm_w[l][None, :])
        q_aug, k_aug, v_t, r_blk = _attn_prep(
            proj3d, ff3d, f_bias[l][None, :],
            jnp.tile(q_norm_w[l], FX_HEADS)[None, :],
            jnp.tile(k_norm_w[l], FX_HEADS)[None, :])
        qk_bound = (1.02 * FX_DH ** 0.5 * LOG2E * jnp.max(jnp.abs(q_norm_w[l]))
                    * jnp.max(jnp.abs(k_norm_w[l])))
        skip_thr = -(F32_EXP2_ZERO + 2.0 * qk_bound)
        r_flat = jnp.concatenate([
            jnp.transpose(r_blk[:, :, 0, :FX_HEADS], (0, 2, 1)).reshape(-1),
            skip_thr.reshape(1).astype(F32)])
        o_bt = _attention(r_flat, q_aug, k_aug, v_t)
        x = _outproj_ffn(x, o_a, o_bt, w_out_b, norm2_w[l][None, :], w_up_b, w_down_b, l)
    return x
```

```python
import functools

import numpy as np
import jax
import jax.numpy as jnp
from jax import lax
from jax.experimental import pallas as pl
from jax.experimental.pallas import tpu as pltpu

F32 = jnp.float32
BF16 = jnp.bfloat16

D_MODEL = 1024
D_FF = 4 * D_MODEL
NORM_EPS = 1e-6

HG_WIDTH = 512
HG_HEADS = 4
HG_DK = 128
HG_CHUNK = 64
HG_LEVELS = 6

FX_WIDTH = 512
FX_HEADS = 8
FX_DH = 64

PROJ_MAIN = 4 * HG_WIDTH + 3 * FX_WIDTH
LANE = 128
PROJ_PAD = PROJ_MAIN + LANE

VMEM_LIMIT = 56 * 1024 * 1024

NEG_BIG = -1e30
LOG2E = 1.4426950408889634

NT_DIMS = (((1,), (1,)), ((), ()))
TN_DIMS = (((0,), (0,)), ((), ()))


def _dot(a, b):
    return jnp.dot(a, b, preferred_element_type=F32)


def _dot_nt(a, b):
    return lax.dot_general(a, b, NT_DIMS, preferred_element_type=F32)


def _dot_tn(a, b):
    return lax.dot_general(a, b, TN_DIMS, preferred_element_type=F32)


def _split2(x):
    hi = x.astype(BF16)
    lo = (x - hi.astype(F32)).astype(BF16)
    return hi, lo


def _split3(x):
    hi = x.astype(BF16)
    r = x - hi.astype(F32)
    mid = r.astype(BF16)
    lo = (r - mid.astype(F32)).astype(BF16)
    return hi, mid, lo


IN_TM = 1024
IN_TN = 512


def _inproj_kernel(x_ref, nw_ref, w_ref, wff_ref, proj_ref, ff_ref):
    x = x_ref[...]
    ms = jnp.mean(x * x, axis=-1, keepdims=True)
    h = (x * lax.rsqrt(ms + NORM_EPS) * nw_ref[...]).astype(BF16)
    for c in range(PROJ_MAIN // IN_TN):
        cols = slice(c * IN_TN, (c + 1) * IN_TN)
        proj_ref[:, cols] = _dot(h, w_ref[:, cols]).astype(BF16)
    ff_ref[...] = _dot(h, wff_ref[...])


def _inproj(x2d, norm_w, w_main, w_ff, layer):
    t = x2d.shape[0]
    return pl.pallas_call(
        _inproj_kernel,
        out_shape=(jax.ShapeDtypeStruct((t, PROJ_MAIN), BF16),
                   jax.ShapeDtypeStruct((t, LANE), F32)),
        grid=(t // IN_TM,),
        in_specs=[
            pl.BlockSpec((IN_TM, D_MODEL), lambda i: (i, 0)),
            pl.BlockSpec((1, D_MODEL), lambda i: (0, 0)),
            pl.BlockSpec((None, D_MODEL, PROJ_MAIN), lambda i: (layer, 0, 0)),
            pl.BlockSpec((None, D_MODEL, LANE), lambda i: (layer, 0, 0)),
        ],
        out_specs=(pl.BlockSpec((IN_TM, PROJ_MAIN), lambda i: (i, 0)),
                   pl.BlockSpec((IN_TM, LANE), lambda i: (i, 0))),
        compiler_params=pltpu.CompilerParams(
            dimension_semantics=("arbitrary",), vmem_limit_bytes=VMEM_LIMIT),
        name="inproj",
    )(x2d, norm_w, w_main, w_ff)


HG_ROWS = 512
HG_NCHUNK = HG_ROWS // HG_CHUNK
HG_COARSE = 0
HG_E_ROWS = (1 + HG_LEVELS - HG_COARSE) * HG_CHUNK


def _hgrn_constants():
    c = HG_CHUNK
    tri = np.tril(np.ones((c, c), np.float32))
    rows = [tri]
    idx = np.arange(c)
    for lvl in range(HG_COARSE, HG_LEVELS):
        m = c >> (lvl + 1)
        ref = (idx // (2 * m)) * (2 * m) + m - 1
        upper = (idx % (2 * m) >= m)[:, None]
        diff = tri - tri[ref]
        rows.append(np.where(upper, diff, -diff))
    mat = np.concatenate(rows, axis=0)
    mcat = np.concatenate([mat, mat], axis=1)
    x = idx[:, None] ^ idx[None, :]
    top = np.floor(np.log2(np.maximum(x, 1))).astype(np.int32)
    level = (HG_LEVELS - 1) - top
    level = np.where(idx[:, None] == idx[None, :], HG_LEVELS, level)
    level = np.where(idx[:, None] < idx[None, :], -1, level)
    level = np.concatenate([level, level], axis=1)
    return mcat.astype(np.float32), level.astype(np.int32)


def _coarse_level_exponent(b, m):
    parts = []
    for r0 in range(0, HG_CHUNK, 2 * m):
        ref = b[r0 + m - 1:r0 + m, :]
        parts += [ref - b[r0:r0 + m], b[r0 + m:r0 + 2 * m] - ref]
    return jnp.concatenate(parts, axis=0)


def _stack_heads(t):
    return jnp.concatenate([t[:, h * HG_DK:(h + 1) * HG_DK] for h in range(HG_HEADS)], axis=0)


def _hgrn_kernel(q_ref, f_ref, i_ref, g_ref, lb_ref, nw_ref, mcat_ref, lvl_ref,
                 o_ref, st_ref):
    @pl.when(pl.program_id(1) == 0)
    def _():
        st_ref[...] = jnp.zeros_like(st_ref)

    lb = lb_ref[...]
    nw = nw_ref[...]
    mcat = mcat_ref[...]
    level = lvl_ref[...]
    scale = HG_DK ** -0.5

    def chunk(c, carry):
        r0 = pl.multiple_of(c * HG_CHUNK, HG_CHUNK)
        rows = pl.ds(r0, HG_CHUNK)
        fp = f_ref[rows, :].astype(F32)
        f = lb + (1.0 - lb) * jax.nn.sigmoid(fp)
        lf = jnp.log(f) * LOG2E
        kk = 1.0 - f
        hi, lo = _split2(lf)
        e_all = _dot(mcat, jnp.concatenate([hi, lo], axis=0))
        b = e_all[0:HG_CHUNK]
        qs = q_ref[rows, :].astype(F32) * scale
        q_in = (qs * jnp.exp2(b)).astype(BF16)
        k_out = (kk * jnp.exp2(b[HG_CHUNK - 1:HG_CHUNK, :] - b)).astype(BF16)
        dec = jnp.exp2(b[HG_CHUNK - 1:HG_CHUNK, :])
        qb = qs.astype(BF16)
        kb = kk.astype(BF16)
        v = i_ref[rows, :]
        g = g_ref[rows, :].astype(F32)

        a = [jnp.zeros((HG_CHUNK, LANE), F32) for _ in range(HG_HEADS)]
        for lvl in range(HG_LEVELS + 1):
            if lvl < HG_LEVELS:
                if lvl < HG_COARSE:
                    e = _coarse_level_exponent(b, HG_CHUNK >> (lvl + 1))
                else:
                    fine = 1 + lvl - HG_COARSE
                    e = e_all[fine * HG_CHUNK:(fine + 1) * HG_CHUNK]
                x = jnp.exp2(e).astype(BF16)
                ql, kl = qb * x, kb * x
            else:
                ql, kl = qb, kb
            d = _dot_nt(_stack_heads(ql), _stack_heads(kl))
            for h in range(HG_HEADS):
                blk = d[h * HG_CHUNK:(h + 1) * HG_CHUNK, (h // 2) * LANE:(h // 2 + 1) * LANE]
                a[h] = jnp.where(level == lvl, blk, a[h])

        zeros = jnp.zeros((HG_CHUNK, HG_DK), BF16)
        for h in range(HG_HEADS):
            ls = slice(h * HG_DK, (h + 1) * HG_DK)
            v_pad = jnp.concatenate([v[:, ls], zeros] if h % 2 == 0 else [zeros, v[:, ls]], axis=0)
            st = st_ref[h]
            o = _dot(a[h].astype(BF16), v_pad) + _dot_nt(q_in[:, ls], st.astype(BF16))
            st_ref[h] = st * dec[:, ls] + _dot_tn(v[:, ls], k_out[:, ls])
            var = jnp.mean(o * o, axis=-1, keepdims=True)
            gh = g[:, ls]
            y = o * lax.rsqrt(var + NORM_EPS) * nw * (gh * jax.nn.sigmoid(gh))
            o_ref[rows, ls] = y.astype(o_ref.dtype)
        return carry

    lax.fori_loop(0, HG_NCHUNK, chunk, 0, unroll=4)


def _hgrn(proj3d, lb, norm_w):
    bsz, s, _ = proj3d.shape
    mcat, level = _hgrn_constants()

    def col(j):
        return pl.BlockSpec((None, HG_ROWS, HG_WIDTH), lambda b, i, j=j: (b, i, j))

    return pl.pallas_call(
        _hgrn_kernel,
        out_shape=jax.ShapeDtypeStruct((bsz, s, HG_WIDTH), BF16),
        grid=(bsz, s // HG_ROWS),
        in_specs=[
            col(0), col(1), col(2), col(3),
            pl.BlockSpec((1, HG_WIDTH), lambda b, i: (0, 0)),
            pl.BlockSpec((1, HG_DK), lambda b, i: (0, 0)),
            pl.BlockSpec((HG_E_ROWS, 2 * HG_CHUNK), lambda b, i: (0, 0)),
            pl.BlockSpec((HG_CHUNK, 2 * HG_CHUNK), lambda b, i: (0, 0)),
        ],
        out_specs=pl.BlockSpec((None, HG_ROWS, HG_WIDTH), lambda b, i: (b, i, 0)),
        scratch_shapes=[pltpu.VMEM((HG_HEADS, HG_DK, HG_DK), F32)],
        compiler_params=pltpu.CompilerParams(
            dimension_semantics=("arbitrary", "arbitrary"), vmem_limit_bytes=VMEM_LIMIT),
        name="hgrn2",
    )(proj3d, proj3d, proj3d, proj3d, lb, norm_w,
      jnp.asarray(mcat, BF16), jnp.asarray(level))


AT_BLK = 256
PREP_ROWS = 512
PREP_SUB = PREP_ROWS // AT_BLK
AUG_PIECES = 3
VT_ROWS = 80


def _aug_base(h):
    return h * LANE + (FX_DH if h % 2 == 0 else 0)


def _prep_constants():
    grp = np.zeros((FX_WIDTH, FX_WIDTH), np.float32)
    for h in range(FX_HEADS):
        grp[h * FX_DH:(h + 1) * FX_DH, h * FX_DH:(h + 1) * FX_DH] = 1.0 / FX_DH
    tri = np.tril(np.ones((AT_BLK, AT_BLK), np.float32))
    width = FX_HEADS * LANE
    ones_at = FX_HEADS * AUG_PIECES
    pk = np.zeros((LANE, width), np.float32)
    cq = np.zeros((1, LANE), np.float32)
    ck = np.zeros((1, width), np.float32)
    for p in range(AUG_PIECES):
        cq[0, ones_at + p] = 1.0
        for h in range(FX_HEADS):
            base = _aug_base(h)
            ck[0, base + FX_HEADS * p + h] = 1.0
            pk[FX_HEADS * p + h, base + ones_at + p] = -1.0
    return grp, tri, pk, cq, ck


def _prep_kernel(q_ref, k_ref, v_ref, ff_ref, fb_ref, qnw_ref, knw_ref,
                 grp_ref, tri_ref, pk_ref, cq_ref, ck_ref,
                 qa_ref, ka_ref, vt_ref, r_ref, carry_ref):
    @pl.when(pl.program_id(1) == 0)
    def _():
        carry_ref[...] = jnp.zeros_like(carry_ref)

    z = ff_ref[...] + fb_ref[...]
    lf = (jnp.minimum(z, 0.0) - jnp.log(1.0 + jnp.exp(-jnp.abs(z)))) * LOG2E
    grp = grp_ref[...]
    tri = tri_ref[...]
    lane = lax.broadcasted_iota(jnp.int32, (AT_BLK, LANE), 1)
    low_half = lane < FX_DH
    head_lane = lane < FX_HEADS

    def qk_norm(t, w):
        ms = _dot((t * t).astype(BF16), grp)
        return t * lax.rsqrt(ms + NORM_EPS) * w

    for j in range(PREP_SUB):
        rows = slice(j * AT_BLK, (j + 1) * AT_BLK)
        hi, mid, lo = _split3(lf[rows])
        c_rel = _dot(tri, hi) + _dot(tri, mid) + _dot(tri, lo)
        r_ref[j] = carry_ref[...]
        carry_ref[...] = carry_ref[...] + c_rel[AT_BLK - 1:AT_BLK, :]
        packed = None
        for p, piece in enumerate(_split3(c_rel)):
            piece = jnp.where(head_lane, piece.astype(F32), 0.0)
            piece = piece if p == 0 else pltpu.roll(piece, FX_HEADS * p, axis=1)
            packed = piece if packed is None else packed + piece
        aug_q_odd = packed + cq_ref[...]
        aug_q_even = pltpu.roll(aug_q_odd, FX_DH, axis=1)
        aug_k = _dot(packed.astype(BF16), pk_ref[...]) + ck_ref[...]
        qn = qk_norm(q_ref[rows, :].astype(F32), qnw_ref[...]) * (FX_DH ** -0.5 * LOG2E)
        kn = qk_norm(k_ref[rows, :].astype(F32), knw_ref[...])
        for h in range(FX_HEADS):
            pair = slice((h // 2) * LANE, (h // 2 + 1) * LANE)
            head = slice(h * LANE, (h + 1) * LANE)
            data = low_half if h % 2 == 0 else jnp.logical_not(low_half)
            aug_q = aug_q_even if h % 2 == 0 else aug_q_odd
            qa_ref[rows, head] = jnp.where(data, qn[:, pair], aug_q).astype(BF16)
            ka_ref[rows, head] = jnp.where(data, kn[:, pair], aug_k[:, head]).astype(BF16)
        v_t = v_ref[rows, :].astype(F32).T.astype(BF16)
        for h in range(FX_HEADS):
            vt_ref[j, h * VT_ROWS:h * VT_ROWS + FX_DH, :] = v_t[h * FX_DH:(h + 1) * FX_DH]
            vt_ref[j, h * VT_ROWS + FX_DH:(h + 1) * VT_ROWS, :] = jnp.ones(
                (VT_ROWS - FX_DH, AT_BLK), BF16)


def _attn_prep(proj3d, ff3d, f_bias, qnw, knw):
    bsz, s, _ = proj3d.shape
    nblk = s // AT_BLK
    consts = _prep_constants()
    grp, tri, pk = (jnp.asarray(c, BF16) for c in consts[:3])
    cq, ck = (jnp.asarray(c, F32) for c in consts[3:])
    width = FX_HEADS * LANE

    def col(j):
        return pl.BlockSpec((None, PREP_ROWS, FX_WIDTH), lambda b, i, j=j: (b, i, j))

    def full(shape):
        return pl.BlockSpec(shape, lambda b, i: (0,) * len(shape))

    return pl.pallas_call(
        _prep_kernel,
        out_shape=(jax.ShapeDtypeStruct((bsz, s, width), BF16),
                   jax.ShapeDtypeStruct((bsz, s, width), BF16),
                   jax.ShapeDtypeStruct((bsz, nblk, FX_HEADS * VT_ROWS, AT_BLK), BF16),
                   jax.ShapeDtypeStruct((bsz, nblk, 8, LANE), F32)),
        grid=(bsz, s // PREP_ROWS),
        in_specs=[
            col(4), col(5), col(6),
            pl.BlockSpec((None, PREP_ROWS, LANE), lambda b, i: (b, i, 0)),
            full((1, LANE)), full((1, FX_WIDTH)), full((1, FX_WIDTH)),
            full((FX_WIDTH, FX_WIDTH)), full((AT_BLK, AT_BLK)),
            full((LANE, width)), full((1, LANE)), full((1, width)),
        ],
        out_specs=(pl.BlockSpec((None, PREP_ROWS, width), lambda b, i: (b, i, 0)),
                   pl.BlockSpec((None, PREP_ROWS, width), lambda b, i: (b, i, 0)),
                   pl.BlockSpec((None, PREP_SUB, FX_HEADS * VT_ROWS, AT_BLK),
                                lambda b, i: (b, i, 0, 0)),
                   pl.BlockSpec((None, PREP_SUB, 8, LANE), lambda b, i: (b, i, 0, 0))),
        scratch_shapes=[pltpu.VMEM((8, LANE), F32)],
        compiler_params=pltpu.CompilerParams(
            dimension_semantics=("arbitrary", "arbitrary"), vmem_limit_bytes=VMEM_LIMIT),
        name="attn_prep",
    )(proj3d, proj3d, proj3d, ff3d, f_bias, qnw, knw, grp, tri, pk, cq, ck)


AT_HG = 2
AT_SLOTS = 3
AT_TRIP_TICKS = 4 * AT_SLOTS
F32_EXP2_ZERO = 150.0
SUBLANES = 8


def _tree_reduce(op, final, x):
    while x.shape[0] > SUBLANES:
        half = x.shape[0] // 2
        x = op(x[:half], x[half:])
    return final(x, axis=0, keepdims=True)


def _attn_kernel(r_ref, mask_ref, q_ref, k_ref, vt_ref, o_ref,
                 s_ref, p_ref, acc_ref, m_ref, a_ref, start_ref, *, nblk, n_r):
    b = pl.program_id(0)
    hg = pl.program_id(1)
    bases = [(b * FX_HEADS + hg * AT_HG + g) * nblk for g in range(AT_HG)]
    last = nblk - 1

    thr = r_ref[n_r]

    def find_start(qi, n_items):
        def needed(first):
            hit = None
            for base in bases:
                c = r_ref[base + qi] - r_ref[base + first] >= thr
                hit = c if hit is None else jnp.logical_or(hit, c)
            return hit

        first = lax.while_loop(lambda f: jnp.logical_and(f > 0, needed(f)),
                               lambda f: f - 1, qi)
        start_ref[qi] = first
        return n_items + (qi - first + 1)

    n_items = lax.fori_loop(0, nblk, find_start, 0)

    def next_item(item):
        qi, ki, valid = item
        at_diag = ki == qi
        more = jnp.logical_and(valid != 0,
                               jnp.logical_not(jnp.logical_and(at_diag, qi == last)))
        qn = jnp.where(at_diag, jnp.minimum(qi + 1, last), qi)
        kn = jnp.where(at_diag, start_ref[qn], ki + 1)
        return (jnp.where(more, qn, last), jnp.where(more, kn, last), more.astype(jnp.int32))

    def stage_a(slot, item):
        qi, ki, _ = item
        rows_q = pl.ds(pl.multiple_of(qi * AT_BLK, AT_BLK), AT_BLK)
        rows_k = pl.ds(pl.multiple_of(ki * AT_BLK, AT_BLK), AT_BLK)
        mask = mask_ref[(ki == qi).astype(jnp.int32)]
        for g in range(AT_HG):
            lanes = slice(g * LANE, (g + 1) * LANE)
            s_ref[slot, g] = _dot_nt(k_ref[rows_k, lanes], q_ref[rows_q, lanes]) + mask

    def stage_b(slot, item):
        qi, ki, valid = item
        live = valid != 0
        first = jnp.logical_and(live, ki == start_ref[qi])
        for g in range(AT_HG):
            delta = jnp.where(live, r_ref[bases[g] + qi] - r_ref[bases[g] + ki], NEG_BIG)
            s = s_ref[slot, g]
            m = jnp.where(first, NEG_BIG, m_ref[g])
            m_new = jnp.maximum(m, _tree_reduce(jnp.maximum, jnp.max, s) + delta)
            p_ref[slot, g] = jnp.exp2(s - (m_new - delta)).astype(BF16)
            a_ref[g] = jnp.exp2(m - m_new)
            m_ref[g] = m_new

    def stage_c(slot, item):
        qi, ki, _ = item
        for g in range(AT_HG):
            v_t = vt_ref[ki, g * VT_ROWS:(g + 1) * VT_ROWS, :]
            acc = a_ref[g] * acc_ref[g] + _dot(v_t, p_ref[slot, g])
            acc_ref[g] = acc
            o_ref[qi, g * FX_DH:(g + 1) * FX_DH, :] = (
                acc[:FX_DH] / acc[FX_DH:FX_DH + 1]).astype(o_ref.dtype)

    def tick(slot, items):
        item_c, item_b, item_s, item_a = items
        ahead = (slot + 2) % AT_SLOTS
        stage_c(ahead, item_c)
        stage_b(slot, item_b)
        stage_a(ahead, item_a)
        return (item_b, item_s, item_a, next_item(item_a))

    for g in range(AT_HG):
        p_ref[AT_SLOTS - 1, g] = jnp.zeros((AT_BLK, AT_BLK), BF16)
        a_ref[g] = jnp.ones((1, AT_BLK), F32)
        m_ref[g] = jnp.zeros((1, AT_BLK), F32)
        acc_ref[g] = jnp.ones((VT_ROWS, AT_BLK), F32)

    zero = jnp.int32(0)
    item0 = (zero, zero, jnp.int32(1))
    item1 = next_item(item0)
    idle = (zero, zero, zero)
    stage_a(0, item0)
    stage_a(1, item1)

    def trip(_, items):
        for t in range(AT_TRIP_TICKS):
            items = tick(t % AT_SLOTS, items)
        return items

    lax.fori_loop(0, (n_items + AT_TRIP_TICKS) // AT_TRIP_TICKS, trip,
                  (idle, item0, item1, next_item(item1)))


def _attention(r_flat, q_aug, k_aug, v_t):
    bsz, s, _ = q_aug.shape
    nblk = s // AT_BLK
    causal = np.where(np.arange(AT_BLK)[:, None] <= np.arange(AT_BLK)[None, :], 0.0, NEG_BIG)
    mask = jnp.asarray(np.stack([np.zeros_like(causal), causal]), F32)
    return pl.pallas_call(
        functools.partial(_attn_kernel, nblk=nblk, n_r=r_flat.shape[0] - 1),
        out_shape=jax.ShapeDtypeStruct((bsz, nblk, FX_WIDTH, AT_BLK), BF16),
        grid=(bsz, FX_HEADS // AT_HG),
        in_specs=[
            pl.BlockSpec(memory_space=pltpu.SMEM),
            pl.BlockSpec((2, AT_BLK, AT_BLK), lambda b, h: (0, 0, 0)),
            pl.BlockSpec((None, s, AT_HG * LANE), lambda b, h: (b, 0, h)),
            pl.BlockSpec((None, s, AT_HG * LANE), lambda b, h: (b, 0, h)),
            pl.BlockSpec((None, nblk, AT_HG * VT_ROWS, AT_BLK), lambda b, h: (b, 0, h, 0)),
        ],
        out_specs=pl.BlockSpec((None, nblk, AT_HG * FX_DH, AT_BLK), lambda b, h: (b, 0, h, 0)),
        scratch_shapes=[
            pltpu.VMEM((AT_SLOTS, AT_HG, AT_BLK, AT_BLK), F32),
            pltpu.VMEM((AT_SLOTS, AT_HG, AT_BLK, AT_BLK), BF16),
            pltpu.VMEM((AT_HG, VT_ROWS, AT_BLK), F32),
            pltpu.VMEM((AT_HG, 1, AT_BLK), F32),
            pltpu.VMEM((AT_HG, 1, AT_BLK), F32),
            pltpu.SMEM((nblk,), jnp.int32),
        ],
        compiler_params=pltpu.CompilerParams(
            dimension_semantics=("arbitrary", "arbitrary"),
            vmem_limit_bytes=VMEM_LIMIT),
        name="fox_attention",
    )(r_flat, mask, q_aug, k_aug, v_t)


FF_TM = 1024
FF_TF = 2048
FF_CHUNK = 1024


def _ffn_kernel(x_ref, oa_ref, obt_ref, woa_ref, wob_ref, nw_ref, wu_ref, wd_ref,
                o_ref, h_ref):
    @pl.when(pl.program_id(2) == 0)
    def _():
        o_b = jnp.concatenate([_dot_tn(obt_ref[j], wob_ref[...])
                               for j in range(FF_TM // AT_BLK)], axis=0)
        x1 = x_ref[...] + _dot(oa_ref[...], woa_ref[...]) + o_b
        ms = jnp.mean(x1 * x1, axis=-1, keepdims=True)
        h_ref[...] = (x1 * lax.rsqrt(ms + NORM_EPS) * nw_ref[...]).astype(BF16)
        o_ref[...] = x1

    for c in range(FF_TF // FF_CHUNK):
        cols = slice(c * FF_CHUNK, (c + 1) * FF_CHUNK)
        u = jnp.maximum(_dot(h_ref[...], wu_ref[:, cols]), 0.0)
        o_ref[...] += _dot((u * u).astype(BF16), wd_ref[cols, :])


def _outproj_ffn(x3d, o_a, o_bt, w_out, norm_w, w_up, w_down, layer):
    bsz, s, _ = x3d.shape
    return pl.pallas_call(
        _ffn_kernel,
        out_shape=jax.ShapeDtypeStruct(x3d.shape, F32),
        grid=(bsz, s // FF_TM, D_FF // FF_TF),
        in_specs=[
            pl.BlockSpec((None, FF_TM, D_MODEL), lambda b, i, j: (b, i, 0)),
            pl.BlockSpec((None, FF_TM, HG_WIDTH), lambda b, i, j: (b, i, 0)),
            pl.BlockSpec((None, FF_TM // AT_BLK, FX_WIDTH, AT_BLK), lambda b, i, j: (b, i, 0, 0)),
            pl.BlockSpec((None, HG_WIDTH, D_MODEL), lambda b, i, j: (layer, 0, 0)),
            pl.BlockSpec((None, FX_WIDTH, D_MODEL), lambda b, i, j: (layer, 1, 0)),
            pl.BlockSpec((1, D_MODEL), lambda b, i, j: (0, 0)),
            pl.BlockSpec((None, D_MODEL, FF_TF), lambda b, i, j: (layer, 0, j)),
            pl.BlockSpec((None, FF_TF, D_MODEL), lambda b, i, j: (layer, j, 0)),
        ],
        out_specs=pl.BlockSpec((None, FF_TM, D_MODEL), lambda b, i, j: (b, i, 0)),
        scratch_shapes=[pltpu.VMEM((FF_TM, D_MODEL), BF16)],
        compiler_params=pltpu.CompilerParams(
            dimension_semantics=("arbitrary", "arbitrary", "arbitrary"),
            vmem_limit_bytes=VMEM_LIMIT),
        name="outproj_ffn",
    )(x3d, o_a, o_bt, w_out, w_out, norm_w, w_up, w_down)


def _layer_lower_bounds(lower_bounds):
    p = jax.nn.softmax(lower_bounds.astype(F32), axis=0)
    c = jnp.cumsum(p, axis=0)
    return c - c[0:1]


def kernel(x, lower_bounds, norm1_w, w_in, fox_f_bias, q_norm_w, k_norm_w,
           hgrn_norm_w, w_out, norm2_w, w_up, w_down):
    bsz, s, d = x.shape
    depth = w_in.shape[0]
    nblk = s // AT_BLK
    lbs = _layer_lower_bounds(lower_bounds)
    w_in_main = w_in[:, :, :PROJ_MAIN].astype(BF16)
    w_in_ff = jnp.pad(w_in[:, :, PROJ_MAIN:].astype(BF16),
                      ((0, 0), (0, 0), (0, LANE - (w_in.shape[-1] - PROJ_MAIN))))
    w_out_b = w_out.astype(BF16)
    w_up_b = w_up.astype(BF16)
    w_down_b = w_down.astype(BF16)
    f_bias = jnp.pad(fox_f_bias, ((0, 0), (0, LANE - FX_HEADS)))

    for l in range(depth):
        proj, ff = _inproj(x.reshape(bsz * s, d), norm1_w[l][None, :], w_in_main, w_in_ff, l)
        proj3d = proj.reshape(bsz, s, PROJ_MAIN)
        ff3d = ff.reshape(bsz, s, LANE)
        o_a = _hgrn(proj3d, lbs[l][None, :], hgrn_norm_w[l][None, :])
        q_aug, k_aug, v_t, r_blk = _attn_prep(
            proj3d, ff3d, f_bias[l][None, :],
            jnp.tile(q_norm_w[l], FX_HEADS)[None, :],
            jnp.tile(k_norm_w[l], FX_HEADS)[None, :])
        qk_bound = (1.02 * FX_DH ** 0.5 * LOG2E * jnp.max(jnp.abs(q_norm_w[l]))
                    * jnp.max(jnp.abs(k_norm_w[l])))
        skip_thr = -(F32_EXP2_ZERO + 2.0 * qk_bound)
        r_flat = jnp.concatenate([
            jnp.transpose(r_blk[:, :, 0, :FX_HEADS], (0, 2, 1)).reshape(-1),
            skip_thr.reshape(1).astype(F32)])
        o_bt = _attention(r_flat, q_aug, k_aug, v_t)
        x = _outproj_ffn(x, o_a, o_bt, w_out_b, norm2_w[l][None, :], w_up_b, w_down_b, l)
    return x
```

```python
import functools

import numpy as np
import jax
import jax.numpy as jnp
from jax import lax
from jax.experimental import pallas as pl
from jax.experimental.pallas import tpu as pltpu

F32 = jnp.float32
BF16 = jnp.bfloat16

D_MODEL = 1024
D_FF = 4 * D_MODEL
NORM_EPS = 1e-6

HG_WIDTH = 512
HG_HEADS = 4
HG_DK = 128
HG_CHUNK = 64
HG_LEVELS = 6

FX_WIDTH = 512
FX_HEADS = 8
FX_DH = 64

PROJ_MAIN = 4 * HG_WIDTH + 3 * FX_WIDTH
LANE = 128
PROJ_PAD = PROJ_MAIN + LANE

VMEM_LIMIT = 56 * 1024 * 1024

NEG_BIG = -1e30
LOG2E = 1.4426950408889634

NT_DIMS = (((1,), (1,)), ((), ()))
TN_DIMS = (((0,), (0,)), ((), ()))


def _dot(a, b):
    return jnp.dot(a, b, preferred_element_type=F32)


def _dot_nt(a, b):
    return lax.dot_general(a, b, NT_DIMS, preferred_element_type=F32)


def _dot_tn(a, b):
    return lax.dot_general(a, b, TN_DIMS, preferred_element_type=F32)


def _split2(x):
    hi = x.astype(BF16)
    lo = (x - hi.astype(F32)).astype(BF16)
    return hi, lo


def _split3(x):
    hi = x.astype(BF16)
    r = x - hi.astype(F32)
    mid = r.astype(BF16)
    lo = (r - mid.astype(F32)).astype(BF16)
    return hi, mid, lo


IN_TM = 1024
IN_TN = 512


def _inproj_kernel(x_ref, nw_ref, w_ref, wff_ref, proj_ref, ff_ref):
    x = x_ref[...]
    ms = jnp.mean(x * x, axis=-1, keepdims=True)
    h = (x * lax.rsqrt(ms + NORM_EPS) * nw_ref[...]).astype(BF16)
    for c in range(PROJ_MAIN // IN_TN):
        cols = slice(c * IN_TN, (c + 1) * IN_TN)
        proj_ref[:, cols] = _dot(h, w_ref[:, cols]).astype(BF16)
    ff_ref[...] = _dot(h, wff_ref[...])


def _inproj(x2d, norm_w, w_main, w_ff, layer):
    t = x2d.shape[0]
    return pl.pallas_call(
        _inproj_kernel,
        out_shape=(jax.ShapeDtypeStruct((t, PROJ_MAIN), BF16),
                   jax.ShapeDtypeStruct((t, LANE), F32)),
        grid=(t // IN_TM,),
        in_specs=[
            pl.BlockSpec((IN_TM, D_MODEL), lambda i: (i, 0)),
            pl.BlockSpec((1, D_MODEL), lambda i: (0, 0)),
            pl.BlockSpec((None, D_MODEL, PROJ_MAIN), lambda i: (layer, 0, 0)),
            pl.BlockSpec((None, D_MODEL, LANE), lambda i: (layer, 0, 0)),
        ],
        out_specs=(pl.BlockSpec((IN_TM, PROJ_MAIN), lambda i: (i, 0)),
                   pl.BlockSpec((IN_TM, LANE), lambda i: (i, 0))),
        compiler_params=pltpu.CompilerParams(
            dimension_semantics=("arbitrary",), vmem_limit_bytes=VMEM_LIMIT),
        name="inproj",
    )(x2d, norm_w, w_main, w_ff)


HG_ROWS = 512
HG_NCHUNK = HG_ROWS // HG_CHUNK
HG_E_ROWS = (2 + HG_LEVELS) * HG_CHUNK


def _hgrn_constants():
    c = HG_CHUNK
    tri = np.tril(np.ones((c, c), np.float32))
    rows = [tri, 1.0 - tri]
    idx = np.arange(c)
    for lvl in range(HG_LEVELS):
        m = c >> (lvl + 1)
        ref = (idx // (2 * m)) * (2 * m) + m - 1
        upper = (idx % (2 * m) >= m)[:, None]
        diff = tri - tri[ref]
        rows.append(np.where(upper, diff, -diff))
    mat = np.concatenate(rows, axis=0)
    mcat = np.concatenate([mat, mat], axis=1)
    x = idx[:, None] ^ idx[None, :]
    top = np.floor(np.log2(np.maximum(x, 1))).astype(np.int32)
    level = (HG_LEVELS - 1) - top
    level = np.where(idx[:, None] == idx[None, :], HG_LEVELS, level)
    level = np.where(idx[:, None] < idx[None, :], -1, level)
    level = np.concatenate([level, level], axis=1)
    return mcat.astype(np.float32), level.astype(np.int32)


def _stack_heads(t):
    return jnp.concatenate([t[:, h * HG_DK:(h + 1) * HG_DK] for h in range(HG_HEADS)], axis=0)


def _hgrn_kernel(q_ref, f_ref, i_ref, g_ref, lb_ref, nw_ref, mcat_ref, lvl_ref,
                 o_ref, st_ref):
    @pl.when(pl.program_id(1) == 0)
    def _():
        st_ref[...] = jnp.zeros_like(st_ref)

    lb = lb_ref[...]
    nw = nw_ref[...]
    mcat = mcat_ref[...]
    level = lvl_ref[...]
    scale = HG_DK ** -0.5

    def chunk(c, carry):
        r0 = pl.multiple_of(c * HG_CHUNK, HG_CHUNK)
        rows = pl.ds(r0, HG_CHUNK)
        fp = f_ref[rows, :].astype(F32)
        f = lb + (1.0 - lb) * jax.nn.sigmoid(fp)
        lf = jnp.log(f) * LOG2E
        kk = 1.0 - f
        hi, lo = _split2(lf)
        e_all = _dot(mcat, jnp.concatenate([hi, lo], axis=0))
        b = e_all[0:HG_CHUNK]
        qs = q_ref[rows, :].astype(F32) * scale
        q_in = (qs * jnp.exp2(b)).astype(BF16)
        k_out = (kk * jnp.exp2(e_all[HG_CHUNK:2 * HG_CHUNK])).astype(BF16)
        dec = jnp.exp2(b[HG_CHUNK - 1:HG_CHUNK, :])
        qb = qs.astype(BF16)
        kb = kk.astype(BF16)
        v = i_ref[rows, :]
        g = g_ref[rows, :].astype(F32)

        a = [jnp.zeros((HG_CHUNK, LANE), F32) for _ in range(HG_HEADS)]
        for lvl in range(HG_LEVELS + 1):
            if lvl < HG_LEVELS:
                x = jnp.exp2(e_all[(2 + lvl) * HG_CHUNK:(3 + lvl) * HG_CHUNK]).astype(BF16)
                ql, kl = qb * x, kb * x
            else:
                ql, kl = qb, kb
            d = _dot_nt(_stack_heads(ql), _stack_heads(kl))
            for h in range(HG_HEADS):
                blk = d[h * HG_CHUNK:(h + 1) * HG_CHUNK, (h // 2) * LANE:(h // 2 + 1) * LANE]
                a[h] = jnp.where(level == lvl, blk, a[h])

        zeros = jnp.zeros((HG_CHUNK, HG_DK), BF16)
        for h in range(HG_HEADS):
            ls = slice(h * HG_DK, (h + 1) * HG_DK)
            v_pad = jnp.concatenate([v[:, ls], zeros] if h % 2 == 0 else [zeros, v[:, ls]], axis=0)
            st = st_ref[h]
            o = _dot(a[h].astype(BF16), v_pad) + _dot_nt(q_in[:, ls], st.astype(BF16))
            st_ref[h] = st * dec[:, ls] + _dot_tn(v[:, ls], k_out[:, ls])
            var = jnp.mean(o * o, axis=-1, keepdims=True)
            gh = g[:, ls]
            y = o * lax.rsqrt(var + NORM_EPS) * nw * (gh * jax.nn.sigmoid(gh))
            o_ref[rows, ls] = y.astype(o_ref.dtype)
        return carry

    lax.fori_loop(0, HG_NCHUNK, chunk, 0, unroll=4)


def _hgrn(proj3d, lb, norm_w):
    bsz, s, _ = proj3d.shape
    mcat, level = _hgrn_constants()

    def col(j):
        return pl.BlockSpec((None, HG_ROWS, HG_WIDTH), lambda b, i, j=j: (b, i, j))

    return pl.pallas_call(
        _hgrn_kernel,
        out_shape=jax.ShapeDtypeStruct((bsz, s, HG_WIDTH), BF16),
        grid=(bsz, s // HG_ROWS),
        in_specs=[
            col(0), col(1), col(2), col(3),
            pl.BlockSpec((1, HG_WIDTH), lambda b, i: (0, 0)),
            pl.BlockSpec((1, HG_DK), lambda b, i: (0, 0)),
            pl.BlockSpec((HG_E_ROWS, 2 * HG_CHUNK), lambda b, i: (0, 0)),
            pl.BlockSpec((HG_CHUNK, 2 * HG_CHUNK), lambda b, i: (0, 0)),
        ],
        out_specs=pl.BlockSpec((None, HG_ROWS, HG_WIDTH), lambda b, i: (b, i, 0)),
        scratch_shapes=[pltpu.VMEM((HG_HEADS, HG_DK, HG_DK), F32)],
        compiler_params=pltpu.CompilerParams(
            dimension_semantics=("arbitrary", "arbitrary"), vmem_limit_bytes=VMEM_LIMIT),
        name="hgrn2",
    )(proj3d, proj3d, proj3d, proj3d, lb, norm_w,
      jnp.asarray(mcat, BF16), jnp.asarray(level))


AT_BLK = 256
PREP_ROWS = 512
PREP_SUB = PREP_ROWS // AT_BLK
AUG_PIECES = 3
VT_ROWS = 80


def _aug_base(h):
    return h * LANE + (FX_DH if h % 2 == 0 else 0)


def _prep_constants():
    grp = np.zeros((FX_WIDTH, FX_WIDTH), np.float32)
    for h in range(FX_HEADS):
        grp[h * FX_DH:(h + 1) * FX_DH, h * FX_DH:(h + 1) * FX_DH] = 1.0 / FX_DH
    tri = np.tril(np.ones((AT_BLK, AT_BLK), np.float32))
    width = FX_HEADS * LANE
    ones_at = FX_HEADS * AUG_PIECES
    pk = np.zeros((LANE, width), np.float32)
    cq = np.zeros((1, LANE), np.float32)
    ck = np.zeros((1, width), np.float32)
    for p in range(AUG_PIECES):
        cq[0, ones_at + p] = 1.0
        for h in range(FX_HEADS):
            base = _aug_base(h)
            ck[0, base + FX_HEADS * p + h] = 1.0
            pk[FX_HEADS * p + h, base + ones_at + p] = -1.0
    return grp, tri, pk, cq, ck


def _prep_kernel(q_ref, k_ref, v_ref, ff_ref, fb_ref, qnw_ref, knw_ref,
                 grp_ref, tri_ref, pk_ref, cq_ref, ck_ref,
                 qa_ref, ka_ref, vt_ref, r_ref, carry_ref):
    @pl.when(pl.program_id(1) == 0)
    def _():
        carry_ref[...] = jnp.zeros_like(carry_ref)

    z = ff_ref[...] + fb_ref[...]
    lf = (jnp.minimum(z, 0.0) - jnp.log(1.0 + jnp.exp(-jnp.abs(z)))) * LOG2E
    grp = grp_ref[...]
    tri = tri_ref[...]
    lane = lax.broadcasted_iota(jnp.int32, (AT_BLK, LANE), 1)
    low_half = lane < FX_DH
    head_lane = lane < FX_HEADS

    def qk_norm(t, w):
        ms = _dot((t * t).astype(BF16), grp)
        return t * lax.rsqrt(ms + NORM_EPS) * w

    for j in range(PREP_SUB):
        rows = slice(j * AT_BLK, (j + 1) * AT_BLK)
        hi, mid, lo = _split3(lf[rows])
        c_rel = _dot(tri, hi) + _dot(tri, mid) + _dot(tri, lo)
        r_ref[j] = carry_ref[...]
        carry_ref[...] = carry_ref[...] + c_rel[AT_BLK - 1:AT_BLK, :]
        packed = None
        for p, piece in enumerate(_split3(c_rel)):
            piece = jnp.where(head_lane, piece.astype(F32), 0.0)
            piece = piece if p == 0 else pltpu.roll(piece, FX_HEADS * p, axis=1)
            packed = piece if packed is None else packed + piece
        aug_q_odd = packed + cq_ref[...]
        aug_q_even = pltpu.roll(aug_q_odd, FX_DH, axis=1)
        aug_k = _dot(packed.astype(BF16), pk_ref[...]) + ck_ref[...]
        qn = qk_norm(q_ref[rows, :].astype(F32), qnw_ref[...]) * (FX_DH ** -0.5 * LOG2E)
        kn = qk_norm(k_ref[rows, :].astype(F32), knw_ref[...])
        for h in range(FX_HEADS):
            pair = slice((h // 2) * LANE, (h // 2 + 1) * LANE)
            head = slice(h * LANE, (h + 1) * LANE)
            data = low_half if h % 2 == 0 else jnp.logical_not(low_half)
            aug_q = aug_q_even if h % 2 == 0 else aug_q_odd
            qa_ref[rows, head] = jnp.where(data, qn[:, pair], aug_q).astype(BF16)
            ka_ref[rows, head] = jnp.where(data, kn[:, pair], aug_k[:, head]).astype(BF16)
        v_t = v_ref[rows, :].astype(F32).T.astype(BF16)
        for h in range(FX_HEADS):
            vt_ref[j, h * VT_ROWS:h * VT_ROWS + FX_DH, :] = v_t[h * FX_DH:(h + 1) * FX_DH]
            vt_ref[j, h * VT_ROWS + FX_DH:(h + 1) * VT_ROWS, :] = jnp.ones(
                (VT_ROWS - FX_DH, AT_BLK), BF16)


def _attn_prep(proj3d, ff3d, f_bias, qnw, knw):
    bsz, s, _ = proj3d.shape
    nblk = s // AT_BLK
    consts = _prep_constants()
    grp, tri, pk = (jnp.asarray(c, BF16) for c in consts[:3])
    cq, ck = (jnp.asarray(c, F32) for c in consts[3:])
    width = FX_HEADS * LANE

    def col(j):
        return pl.BlockSpec((None, PREP_ROWS, FX_WIDTH), lambda b, i, j=j: (b, i, j))

    def full(shape):
        return pl.BlockSpec(shape, lambda b, i: (0,) * len(shape))

    return pl.pallas_call(
        _prep_kernel,
        out_shape=(jax.ShapeDtypeStruct((bsz, s, width), BF16),
                   jax.ShapeDtypeStruct((bsz, s, width), BF16),
                   jax.ShapeDtypeStruct((bsz, nblk, FX_HEADS * VT_ROWS, AT_BLK), BF16),
                   jax.ShapeDtypeStruct((bsz, nblk, 8, LANE), F32)),
        grid=(bsz, s // PREP_ROWS),
        in_specs=[
            col(4), col(5), col(6),
            pl.BlockSpec((None, PREP_ROWS, LANE), lambda b, i: (b, i, 0)),
            full((1, LANE)), full((1, FX_WIDTH)), full((1, FX_WIDTH)),
            full((FX_WIDTH, FX_WIDTH)), full((AT_BLK, AT_BLK)),
            full((LANE, width)), full((1, LANE)), full((1, width)),
        ],
        out_specs=(pl.BlockSpec((None, PREP_ROWS, width), lambda b, i: (b, i, 0)),
                   pl.BlockSpec((None, PREP_ROWS, width), lambda b, i: (b, i, 0)),
                   pl.BlockSpec((None, PREP_SUB, FX_HEADS * VT_ROWS, AT_BLK),
                                lambda b, i: (b, i, 0, 0)),
                   pl.BlockSpec((None, PREP_SUB, 8, LANE), lambda b, i: (b, i, 0, 0))),
        scratch_shapes=[pltpu.VMEM((8, LANE), F32)],
        compiler_params=pltpu.CompilerParams(
            dimension_semantics=("arbitrary", "arbitrary"), vmem_limit_bytes=VMEM_LIMIT),
        name="attn_prep",
    )(proj3d, proj3d, proj3d, ff3d, f_bias, qnw, knw, grp, tri, pk, cq, ck)


AT_HG = 2
AT_SLOTS = 3
AT_TRIP_TICKS = 4 * AT_SLOTS
F32_EXP2_ZERO = 150.0
SUBLANES = 8


def _tree_reduce(op, final, x):
    while x.shape[0] > SUBLANES:
        half = x.shape[0] // 2
        x = op(x[:half], x[half:])
    return final(x, axis=0, keepdims=True)


def _attn_kernel(order_ref, r_ref, mask_ref, *refs, nblk, n_r):
    q_refs, k_refs, vt_refs = (refs[i * AT_HG:(i + 1) * AT_HG] for i in range(3))
    o_ref, s_ref, p_ref, acc_ref, m_ref, a_ref, start_ref = refs[3 * AT_HG:]
    b = pl.program_id(0)
    hg = pl.program_id(1)
    bases = [(b * FX_HEADS + order_ref[hg * AT_HG + g]) * nblk for g in range(AT_HG)]
    last = nblk - 1

    thr = r_ref[n_r]

    def find_start(qi, n_items):
        def needed(first):
            hit = None
            for base in bases:
                c = r_ref[base + qi] - r_ref[base + first] >= thr
                hit = c if hit is None else jnp.logical_or(hit, c)
            return hit

        first = lax.while_loop(lambda f: jnp.logical_and(f > 0, needed(f)),
                               lambda f: f - 1, qi)
        start_ref[qi] = first
        return n_items + (qi - first + 1)

    n_items = lax.fori_loop(0, nblk, find_start, 0)

    def next_item(item):
        qi, ki, valid = item
        at_diag = ki == qi
        more = jnp.logical_and(valid != 0,
                               jnp.logical_not(jnp.logical_and(at_diag, qi == last)))
        qn = jnp.where(at_diag, jnp.minimum(qi + 1, last), qi)
        kn = jnp.where(at_diag, start_ref[qn], ki + 1)
        return (jnp.where(more, qn, last), jnp.where(more, kn, last), more.astype(jnp.int32))

    def stage_a(slot, item):
        qi, ki, _ = item
        rows_q = pl.ds(pl.multiple_of(qi * AT_BLK, AT_BLK), AT_BLK)
        rows_k = pl.ds(pl.multiple_of(ki * AT_BLK, AT_BLK), AT_BLK)
        mask = mask_ref[(ki == qi).astype(jnp.int32)]
        for g in range(AT_HG):
            s_ref[slot, g] = _dot_nt(k_refs[g][rows_k, :], q_refs[g][rows_q, :]) + mask

    def stage_b(slot, item):
        qi, ki, valid = item
        live = valid != 0
        first = jnp.logical_and(live, ki == start_ref[qi])
        for g in range(AT_HG):
            delta = jnp.where(live, r_ref[bases[g] + qi] - r_ref[bases[g] + ki], NEG_BIG)
            s = s_ref[slot, g]
            m = jnp.where(first, NEG_BIG, m_ref[g])
            m_new = jnp.maximum(m, _tree_reduce(jnp.maximum, jnp.max, s) + delta)
            p_ref[slot, g] = jnp.exp2(s - (m_new - delta)).astype(BF16)
            a_ref[g] = jnp.exp2(m - m_new)
            m_ref[g] = m_new

    def stage_c(slot, item):
        qi, ki, _ = item
        for g in range(AT_HG):
            v_t = vt_refs[g][ki]
            acc = a_ref[g] * acc_ref[g] + _dot(v_t, p_ref[slot, g])
            acc_ref[g] = acc
            o_ref[qi, g * FX_DH:(g + 1) * FX_DH, :] = (
                acc[:FX_DH] / acc[FX_DH:FX_DH + 1]).astype(o_ref.dtype)

    def tick(slot, items):
        item_c, item_b, item_s, item_a = items
        ahead = (slot + 2) % AT_SLOTS
        stage_c(ahead, item_c)
        stage_b(slot, item_b)
        stage_a(ahead, item_a)
        return (item_b, item_s, item_a, next_item(item_a))

    for g in range(AT_HG):
        p_ref[AT_SLOTS - 1, g] = jnp.zeros((AT_BLK, AT_BLK), BF16)
        a_ref[g] = jnp.ones((1, AT_BLK), F32)
        m_ref[g] = jnp.zeros((1, AT_BLK), F32)
        acc_ref[g] = jnp.ones((VT_ROWS, AT_BLK), F32)

    zero = jnp.int32(0)
    item0 = (zero, zero, jnp.int32(1))
    item1 = next_item(item0)
    idle = (zero, zero, zero)
    stage_a(0, item0)
    stage_a(1, item1)

    def trip(_, items):
        for t in range(AT_TRIP_TICKS):
            items = tick(t % AT_SLOTS, items)
        return items

    lax.fori_loop(0, (n_items + AT_TRIP_TICKS) // AT_TRIP_TICKS, trip,
                  (idle, item0, item1, next_item(item1)))


def _attention(order, r_flat, q_aug, k_aug, v_t):
    bsz, s, _ = q_aug.shape
    nblk = s // AT_BLK
    causal = np.where(np.arange(AT_BLK)[:, None] <= np.arange(AT_BLK)[None, :], 0.0, NEG_BIG)
    mask = jnp.asarray(np.stack([np.zeros_like(causal), causal]), F32)

    def head_cols(g):
        return pl.BlockSpec((None, s, LANE), lambda b, h, order: (b, 0, order[h * AT_HG + g]))

    def head_vt(g):
        return pl.BlockSpec((None, nblk, VT_ROWS, AT_BLK),
                            lambda b, h, order: (b, 0, order[h * AT_HG + g], 0))

    grid_spec = pltpu.PrefetchScalarGridSpec(
        num_scalar_prefetch=1,
        grid=(bsz, FX_HEADS // AT_HG),
        in_specs=[
            pl.BlockSpec(memory_space=pltpu.SMEM),
            pl.BlockSpec((2, AT_BLK, AT_BLK), lambda b, h, order: (0, 0, 0)),
            *[head_cols(g) for g in range(AT_HG)],
            *[head_cols(g) for g in range(AT_HG)],
            *[head_vt(g) for g in range(AT_HG)],
        ],
        out_specs=pl.BlockSpec((None, nblk, AT_HG * FX_DH, AT_BLK),
                               lambda b, h, order: (b, 0, h, 0)),
        scratch_shapes=[
            pltpu.VMEM((AT_SLOTS, AT_HG, AT_BLK, AT_BLK), F32),
            pltpu.VMEM((AT_SLOTS, AT_HG, AT_BLK, AT_BLK), BF16),
            pltpu.VMEM((AT_HG, VT_ROWS, AT_BLK), F32),
            pltpu.VMEM((AT_HG, 1, AT_BLK), F32),
            pltpu.VMEM((AT_HG, 1, AT_BLK), F32),
            pltpu.SMEM((nblk,), jnp.int32),
        ],
    )
    return pl.pallas_call(
        functools.partial(_attn_kernel, nblk=nblk, n_r=r_flat.shape[0] - 1),
        out_shape=jax.ShapeDtypeStruct((bsz, nblk, FX_WIDTH, AT_BLK), BF16),
        grid_spec=grid_spec,
        compiler_params=pltpu.CompilerParams(
            dimension_semantics=("arbitrary", "arbitrary"),
            vmem_limit_bytes=VMEM_LIMIT),
        name="fox_attention",
    )(order, r_flat, mask, *([q_aug] * AT_HG), *([k_aug] * AT_HG), *([v_t] * AT_HG))


FF_TM = 1024
FF_TF = 2048
FF_CHUNK = 1024


def _ffn_kernel(x_ref, oa_ref, obt_ref, woa_ref, wob_ref, nw_ref, wu_ref, wd_ref,
                o_ref, h_ref):
    @pl.when(pl.program_id(2) == 0)
    def _():
        o_b = jnp.concatenate([_dot_tn(obt_ref[j], wob_ref[...])
                               for j in range(FF_TM // AT_BLK)], axis=0)
        x1 = x_ref[...] + _dot(oa_ref[...], woa_ref[...]) + o_b
        ms = jnp.mean(x1 * x1, axis=-1, keepdims=True)
        h_ref[...] = (x1 * lax.rsqrt(ms + NORM_EPS) * nw_ref[...]).astype(BF16)
        o_ref[...] = x1

    for c in range(FF_TF // FF_CHUNK):
        cols = slice(c * FF_CHUNK, (c + 1) * FF_CHUNK)
        u = jnp.maximum(_dot(h_ref[...], wu_ref[:, cols]), 0.0)
        o_ref[...] += _dot((u * u).astype(BF16), wd_ref[cols, :])


def _outproj_ffn(x3d, o_a, o_bt, w_out, w_out_fox, norm_w, w_up, w_down, layer):
    bsz, s, _ = x3d.shape
    return pl.pallas_call(
        _ffn_kernel,
        out_shape=jax.ShapeDtypeStruct(x3d.shape, F32),
        grid=(bsz, s // FF_TM, D_FF // FF_TF),
        in_specs=[
            pl.BlockSpec((None, FF_TM, D_MODEL), lambda b, i, j: (b, i, 0)),
            pl.BlockSpec((None, FF_TM, HG_WIDTH), lambda b, i, j: (b, i, 0)),
            pl.BlockSpec((None, FF_TM // AT_BLK, FX_WIDTH, AT_BLK), lambda b, i, j: (b, i, 0, 0)),
            pl.BlockSpec((None, HG_WIDTH, D_MODEL), lambda b, i, j: (layer, 0, 0)),
            pl.BlockSpec((FX_WIDTH, D_MODEL), lambda b, i, j: (0, 0)),
            pl.BlockSpec((1, D_MODEL), lambda b, i, j: (0, 0)),
            pl.BlockSpec((None, D_MODEL, FF_TF), lambda b, i, j: (layer, 0, j)),
            pl.BlockSpec((None, FF_TF, D_MODEL), lambda b, i, j: (layer, j, 0)),
        ],
        out_specs=pl.BlockSpec((None, FF_TM, D_MODEL), lambda b, i, j: (b, i, 0)),
        scratch_shapes=[pltpu.VMEM((FF_TM, D_MODEL), BF16)],
        compiler_params=pltpu.CompilerParams(
            dimension_semantics=("arbitrary", "arbitrary", "arbitrary"),
            vmem_limit_bytes=VMEM_LIMIT),
        name="outproj_ffn",
    )(x3d, o_a, o_bt, w_out, w_out_fox, norm_w, w_up, w_down)


def _layer_lower_bounds(lower_bounds):
    p = jax.nn.softmax(lower_bounds.astype(F32), axis=0)
    c = jnp.cumsum(p, axis=0)
    return c - c[0:1]


def kernel(x, lower_bounds, norm1_w, w_in, fox_f_bias, q_norm_w, k_norm_w,
           hgrn_norm_w, w_out, norm2_w, w_up, w_down):
    bsz, s, d = x.shape
    depth = w_in.shape[0]
    nblk = s // AT_BLK
    lbs = _layer_lower_bounds(lower_bounds)
    w_in_main = w_in[:, :, :PROJ_MAIN].astype(BF16)
    w_in_ff = jnp.pad(w_in[:, :, PROJ_MAIN:].astype(BF16),
                      ((0, 0), (0, 0), (0, LANE - (w_in.shape[-1] - PROJ_MAIN))))
    w_out_b = w_out.astype(BF16)
    w_up_b = w_up.astype(BF16)
    w_down_b = w_down.astype(BF16)
    f_bias = jnp.pad(fox_f_bias, ((0, 0), (0, LANE - FX_HEADS)))

    for l in range(depth):
        proj, ff = _inproj(x.reshape(bsz * s, d), norm1_w[l][None, :], w_in_main, w_in_ff, l)
        proj3d = proj.reshape(bsz, s, PROJ_MAIN)
        ff3d = ff.reshape(bsz, s, LANE)
        o_a = _hgrn(proj3d, lbs[l][None, :], hgrn_norm_w[l][None, :])
        q_aug, k_aug, v_t, r_blk = _attn_prep(
            proj3d, ff3d, f_bias[l][None, :],
            jnp.tile(q_norm_w[l], FX_HEADS)[None, :],
            jnp.tile(k_norm_w[l], FX_HEADS)[None, :])
        qk_bound = (1.02 * FX_DH ** 0.5 * LOG2E * jnp.max(jnp.abs(q_norm_w[l]))
                    * jnp.max(jnp.abs(k_norm_w[l])))
        skip_thr = -(F32_EXP2_ZERO + 2.0 * qk_bound)
        r_flat = jnp.concatenate([
            jnp.transpose(r_blk[:, :, 0, :FX_HEADS], (0, 2, 1)).reshape(-1),
            skip_thr.reshape(1).astype(F32)])
        order = jnp.argsort(fox_f_bias[l]).astype(jnp.int32)
        o_bt = _attention(order, r_flat, q_aug, k_aug, v_t)
        fox_rows = (HG_WIDTH + order[:, None] * FX_DH + jnp.arange(FX_DH)[None, :]).reshape(-1)
        x = _outproj_ffn(x, o_a, o_bt, w_out_b, w_out_b[l][fox_rows], norm2_w[l][None, :],
                         w_up_b, w_down_b, l)
    return x
```

```python
import functools

import numpy as np
import jax
import jax.numpy as jnp
from jax import lax
from jax.experimental import pallas as pl
from jax.experimental.pallas import tpu as pltpu

F32 = jnp.float32
BF16 = jnp.bfloat16

D_MODEL = 1024
D_FF = 4 * D_MODEL
NORM_EPS = 1e-6

HG_WIDTH = 512
HG_HEADS = 4
HG_DK = 128
HG_CHUNK = 64
HG_LEVELS = 6

FX_WIDTH = 512
FX_HEADS = 8
FX_DH = 64

PROJ_MAIN = 4 * HG_WIDTH + 3 * FX_WIDTH
LANE = 128
PROJ_PAD = PROJ_MAIN + LANE

VMEM_LIMIT = 56 * 1024 * 1024

NEG_BIG = -1e30
LOG2E = 1.4426950408889634

NT_DIMS = (((1,), (1,)), ((), ()))
TN_DIMS = (((0,), (0,)), ((), ()))


def _dot(a, b):
    return jnp.dot(a, b, preferred_element_type=F32)


def _dot_nt(a, b):
    return lax.dot_general(a, b, NT_DIMS, preferred_element_type=F32)


def _dot_tn(a, b):
    return lax.dot_general(a, b, TN_DIMS, preferred_element_type=F32)


def _split2(x):
    hi = x.astype(BF16)
    lo = (x - hi.astype(F32)).astype(BF16)
    return hi, lo


def _split3(x):
    hi = x.astype(BF16)
    r = x - hi.astype(F32)
    mid = r.astype(BF16)
    lo = (r - mid.astype(F32)).astype(BF16)
    return hi, mid, lo


IN_TM = 1024
IN_TN = 512


def _inproj_kernel(x_ref, nw_ref, w_ref, wff_ref, proj_ref, ff_ref):
    x = x_ref[...]
    ms = jnp.mean(x * x, axis=-1, keepdims=True)
    h = (x * lax.rsqrt(ms + NORM_EPS) * nw_ref[...]).astype(BF16)
    for c in range(PROJ_MAIN // IN_TN):
        cols = slice(c * IN_TN, (c + 1) * IN_TN)
        proj_ref[:, cols] = _dot(h, w_ref[:, cols]).astype(BF16)
    ff_ref[...] = _dot(h, wff_ref[...])


def _inproj(x2d, norm_w, w_pad, layer):
    t = x2d.shape[0]
    return pl.pallas_call(
        _inproj_kernel,
        out_shape=(jax.ShapeDtypeStruct((t, PROJ_MAIN), BF16),
                   jax.ShapeDtypeStruct((t, LANE), F32)),
        grid=(t // IN_TM,),
        in_specs=[
            pl.BlockSpec((IN_TM, D_MODEL), lambda i: (i, 0)),
            pl.BlockSpec((1, D_MODEL), lambda i: (0, 0)),
            pl.BlockSpec((None, D_MODEL, PROJ_MAIN), lambda i: (layer, 0, 0)),
            pl.BlockSpec((None, D_MODEL, LANE), lambda i: (layer, 0, PROJ_MAIN // LANE)),
        ],
        out_specs=(pl.BlockSpec((IN_TM, PROJ_MAIN), lambda i: (i, 0)),
                   pl.BlockSpec((IN_TM, LANE), lambda i: (i, 0))),
        compiler_params=pltpu.CompilerParams(
            dimension_semantics=("arbitrary",), vmem_limit_bytes=VMEM_LIMIT),
        name="inproj",
    )(x2d, norm_w, w_pad, w_pad)


HG_ROWS = 512
HG_NCHUNK = HG_ROWS // HG_CHUNK
HG_E_ROWS = (2 + HG_LEVELS) * HG_CHUNK


def _hgrn_constants():
    c = HG_CHUNK
    tri = np.tril(np.ones((c, c), np.float32))
    rows = [tri, 1.0 - tri]
    idx = np.arange(c)
    for lvl in range(HG_LEVELS):
        m = c >> (lvl + 1)
        ref = (idx // (2 * m)) * (2 * m) + m - 1
        upper = (idx % (2 * m) >= m)[:, None]
        diff = tri - tri[ref]
        rows.append(np.where(upper, diff, -diff))
    mat = np.concatenate(rows, axis=0)
    mcat = np.concatenate([mat, mat], axis=1)
    x = idx[:, None] ^ idx[None, :]
    top = np.floor(np.log2(np.maximum(x, 1))).astype(np.int32)
    level = (HG_LEVELS - 1) - top
    level = np.where(idx[:, None] == idx[None, :], HG_LEVELS, level)
    level = np.where(idx[:, None] < idx[None, :], -1, level)
    level = np.concatenate([level, level], axis=1)
    return mcat.astype(np.float32), level.astype(np.int32)


def _stack_heads(t):
    return jnp.concatenate([t[:, h * HG_DK:(h + 1) * HG_DK] for h in range(HG_HEADS)], axis=0)


def _hgrn_kernel(q_ref, f_ref, i_ref, g_ref, lb_ref, nw_ref, mcat_ref, lvl_ref,
                 o_ref, st_ref):
    @pl.when(pl.program_id(1) == 0)
    def _():
        st_ref[...] = jnp.zeros_like(st_ref)

    lb = lb_ref[...]
    nw = nw_ref[...]
    mcat = mcat_ref[...]
    level = lvl_ref[...]
    scale = HG_DK ** -0.5

    def chunk(c, carry):
        r0 = pl.multiple_of(c * HG_CHUNK, HG_CHUNK)
        rows = pl.ds(r0, HG_CHUNK)
        fp = f_ref[rows, :].astype(F32)
        f = lb + (1.0 - lb) * jax.nn.sigmoid(fp)
        lf = jnp.log(f) * LOG2E
        kk = 1.0 - f
        hi, lo = _split2(lf)
        e_all = _dot(mcat, jnp.concatenate([hi, lo], axis=0))
        b = e_all[0:HG_CHUNK]
        qs = q_ref[rows, :].astype(F32) * scale
        q_in = (qs * jnp.exp2(b)).astype(BF16)
        k_out = (kk * jnp.exp2(e_all[HG_CHUNK:2 * HG_CHUNK])).astype(BF16)
        dec = jnp.exp2(b[HG_CHUNK - 1:HG_CHUNK, :])
        qb = qs.astype(BF16)
        kb = kk.astype(BF16)
        v = i_ref[rows, :]
        g = g_ref[rows, :].astype(F32)

        a = [jnp.zeros((HG_CHUNK, LANE), F32) for _ in range(HG_HEADS)]
        for lvl in range(HG_LEVELS + 1):
            if lvl < HG_LEVELS:
                x = jnp.exp2(e_all[(2 + lvl) * HG_CHUNK:(3 + lvl) * HG_CHUNK]).astype(BF16)
                ql, kl = qb * x, kb * x
            else:
                ql, kl = qb, kb
            d = _dot_nt(_stack_heads(ql), _stack_heads(kl))
            for h in range(HG_HEADS):
                blk = d[h * HG_CHUNK:(h + 1) * HG_CHUNK, (h // 2) * LANE:(h // 2 + 1) * LANE]
                a[h] = jnp.where(level == lvl, blk, a[h])

        zeros = jnp.zeros((HG_CHUNK, HG_DK), BF16)
        for h in range(HG_HEADS):
            ls = slice(h * HG_DK, (h + 1) * HG_DK)
            v_pad = jnp.concatenate([v[:, ls], zeros] if h % 2 == 0 else [zeros, v[:, ls]], axis=0)
            st = st_ref[h]
            o = _dot(a[h].astype(BF16), v_pad) + _dot_nt(q_in[:, ls], st.astype(BF16))
            st_ref[h] = st * dec[:, ls] + _dot_tn(v[:, ls], k_out[:, ls])
            var = jnp.mean(o * o, axis=-1, keepdims=True)
            gh = g[:, ls]
            y = o * lax.rsqrt(var + NORM_EPS) * nw * (gh * jax.nn.sigmoid(gh))
            o_ref[rows, ls] = y.astype(o_ref.dtype)
        return carry

    lax.fori_loop(0, HG_NCHUNK, chunk, 0, unroll=8)


def _hgrn(proj3d, lb, norm_w):
    bsz, s, _ = proj3d.shape
    mcat, level = _hgrn_constants()

    def col(j):
        return pl.BlockSpec((None, HG_ROWS, HG_WIDTH), lambda b, i, j=j: (b, i, j))

    return pl.pallas_call(
        _hgrn_kernel,
        out_shape=jax.ShapeDtypeStruct((bsz, s, HG_WIDTH), BF16),
        grid=(bsz, s // HG_ROWS),
        in_specs=[
            col(0), col(1), col(2), col(3),
            pl.BlockSpec((1, HG_WIDTH), lambda b, i: (0, 0)),
            pl.BlockSpec((1, HG_DK), lambda b, i: (0, 0)),
            pl.BlockSpec((HG_E_ROWS, 2 * HG_CHUNK), lambda b, i: (0, 0)),
            pl.BlockSpec((HG_CHUNK, 2 * HG_CHUNK), lambda b, i: (0, 0)),
        ],
        out_specs=pl.BlockSpec((None, HG_ROWS, HG_WIDTH), lambda b, i: (b, i, 0)),
        scratch_shapes=[pltpu.VMEM((HG_HEADS, HG_DK, HG_DK), F32)],
        compiler_params=pltpu.CompilerParams(
            dimension_semantics=("arbitrary", "arbitrary"), vmem_limit_bytes=VMEM_LIMIT),
        name="hgrn2",
    )(proj3d, proj3d, proj3d, proj3d, lb, norm_w,
      jnp.asarray(mcat, BF16), jnp.asarray(level))


AT_BLK = 256
PREP_ROWS = 512
PREP_SUB = PREP_ROWS // AT_BLK
AUG_PIECES = 3
VT_ROWS = 80


def _aug_base(h):
    return h * LANE + (FX_DH if h % 2 == 0 else 0)


def _prep_constants():
    grp = np.zeros((FX_WIDTH, FX_WIDTH), np.float32)
    for h in range(FX_HEADS):
        grp[h * FX_DH:(h + 1) * FX_DH, h * FX_DH:(h + 1) * FX_DH] = 1.0 / FX_DH
    tri = np.tril(np.ones((AT_BLK, AT_BLK), np.float32))
    width = FX_HEADS * LANE
    ones_at = FX_HEADS * AUG_PIECES
    pk = np.zeros((LANE, width), np.float32)
    cq = np.zeros((1, LANE), np.float32)
    ck = np.zeros((1, width), np.float32)
    for p in range(AUG_PIECES):
        cq[0, ones_at + p] = 1.0
        for h in range(FX_HEADS):
            base = _aug_base(h)
            ck[0, base + FX_HEADS * p + h] = 1.0
            pk[FX_HEADS * p + h, base + ones_at + p] = -1.0
    return grp, tri, pk, cq, ck


def _prep_kernel(q_ref, k_ref, v_ref, ff_ref, fb_ref, qnw_ref, knw_ref,
                 grp_ref, tri_ref, pk_ref, cq_ref, ck_ref,
                 qa_ref, ka_ref, vt_ref, r_ref, carry_ref):
    @pl.when(pl.program_id(1) == 0)
    def _():
        carry_ref[...] = jnp.zeros_like(carry_ref)

    z = ff_ref[...] + fb_ref[...]
    lf = (jnp.minimum(z, 0.0) - jnp.log(1.0 + jnp.exp(-jnp.abs(z)))) * LOG2E
    grp = grp_ref[...]
    tri = tri_ref[...]
    lane = lax.broadcasted_iota(jnp.int32, (AT_BLK, LANE), 1)
    low_half = lane < FX_DH
    head_lane = lane < FX_HEADS

    def qk_norm(t, w):
        ms = _dot((t * t).astype(BF16), grp)
        return t * lax.rsqrt(ms + NORM_EPS) * w

    for j in range(PREP_SUB):
        rows = slice(j * AT_BLK, (j + 1) * AT_BLK)
        hi, mid, lo = _split3(lf[rows])
        c_rel = _dot(tri, hi) + _dot(tri, mid) + _dot(tri, lo)
        r_ref[j] = carry_ref[...]
        carry_ref[...] = carry_ref[...] + c_rel[AT_BLK - 1:AT_BLK, :]
        packed = None
        for p, piece in enumerate(_split3(c_rel)):
            piece = jnp.where(head_lane, piece.astype(F32), 0.0)
            piece = piece if p == 0 else pltpu.roll(piece, FX_HEADS * p, axis=1)
            packed = piece if packed is None else packed + piece
        aug_q_odd = packed + cq_ref[...]
        aug_q_even = pltpu.roll(aug_q_odd, FX_DH, axis=1)
        aug_k = _dot(packed.astype(BF16), pk_ref[...]) + ck_ref[...]
        qn = qk_norm(q_ref[rows, :].astype(F32), qnw_ref[...]) * (FX_DH ** -0.5 * LOG2E)
        kn = qk_norm(k_ref[rows, :].astype(F32), knw_ref[...])
        for h in range(FX_HEADS):
            pair = slice((h // 2) * LANE, (h // 2 + 1) * LANE)
            head = slice(h * LANE, (h + 1) * LANE)
            data = low_half if h % 2 == 0 else jnp.logical_not(low_half)
            aug_q = aug_q_even if h % 2 == 0 else aug_q_odd
            qa_ref[rows, head] = jnp.where(data, qn[:, pair], aug_q).astype(BF16)
            ka_ref[rows, head] = jnp.where(data, kn[:, pair], aug_k[:, head]).astype(BF16)
        v_t = v_ref[rows, :].astype(F32).T.astype(BF16)
        for h in range(FX_HEADS):
            vt_ref[j, h * VT_ROWS:h * VT_ROWS + FX_DH, :] = v_t[h * FX_DH:(h + 1) * FX_DH]
            vt_ref[j, h * VT_ROWS + FX_DH:(h + 1) * VT_ROWS, :] = jnp.ones(
                (VT_ROWS - FX_DH, AT_BLK), BF16)


def _attn_prep(proj3d, ff3d, f_bias, qnw, knw):
    bsz, s, _ = proj3d.shape
    nblk = s // AT_BLK
    consts = _prep_constants()
    grp, tri, pk = (jnp.asarray(c, BF16) for c in consts[:3])
    cq, ck = (jnp.asarray(c, F32) for c in consts[3:])
    width = FX_HEADS * LANE

    def col(j):
        return pl.BlockSpec((None, PREP_ROWS, FX_WIDTH), lambda b, i, j=j: (b, i, j))

    def full(shape):
        return pl.BlockSpec(shape, lambda b, i: (0,) * len(shape))

    return pl.pallas_call(
        _prep_kernel,
        out_shape=(jax.ShapeDtypeStruct((bsz, s, width), BF16),
                   jax.ShapeDtypeStruct((bsz, s, width), BF16),
                   jax.ShapeDtypeStruct((bsz, nblk, FX_HEADS * VT_ROWS, AT_BLK), BF16),
                   jax.ShapeDtypeStruct((bsz, nblk, 8, LANE), F32)),
        grid=(bsz, s // PREP_ROWS),
        in_specs=[
            col(4), col(5), col(6),
            pl.BlockSpec((None, PREP_ROWS, LANE), lambda b, i: (b, i, 0)),
            full((1, LANE)), full((1, FX_WIDTH)), full((1, FX_WIDTH)),
            full((FX_WIDTH, FX_WIDTH)), full((AT_BLK, AT_BLK)),
            full((LANE, width)), full((1, LANE)), full((1, width)),
        ],
        out_specs=(pl.BlockSpec((None, PREP_ROWS, width), lambda b, i: (b, i, 0)),
                   pl.BlockSpec((None, PREP_ROWS, width), lambda b, i: (b, i, 0)),
                   pl.BlockSpec((None, PREP_SUB, FX_HEADS * VT_ROWS, AT_BLK),
                                lambda b, i: (b, i, 0, 0)),
                   pl.BlockSpec((None, PREP_SUB, 8, LANE), lambda b, i: (b, i, 0, 0))),
        scratch_shapes=[pltpu.VMEM((8, LANE), F32)],
        compiler_params=pltpu.CompilerParams(
            dimension_semantics=("arbitrary", "arbitrary"), vmem_limit_bytes=VMEM_LIMIT),
        name="attn_prep",
    )(proj3d, proj3d, proj3d, ff3d, f_bias, qnw, knw, grp, tri, pk, cq, ck)


AT_HG = 2
AT_SLOTS = 3
AT_TRIP_TICKS = 4 * AT_SLOTS
F32_EXP2_ZERO = 150.0
SUBLANES = 8


def _tree_reduce(op, final, x):
    while x.shape[0] > SUBLANES:
        half = x.shape[0] // 2
        x = op(x[:half], x[half:])
    return final(x, axis=0, keepdims=True)


def _attn_kernel(order_ref, r_ref, mask_ref, *refs, nblk, n_r):
    q_refs, k_refs, vt_refs = (refs[i * AT_HG:(i + 1) * AT_HG] for i in range(3))
    o_ref, s_ref, p_ref, acc_ref, m_ref, a_ref, start_ref = refs[3 * AT_HG:]
    b = pl.program_id(0)
    hg = pl.program_id(1)
    bases = [(b * FX_HEADS + order_ref[hg * AT_HG + g]) * nblk for g in range(AT_HG)]
    last = nblk - 1

    thr = r_ref[n_r]

    def find_start(qi, n_items):
        def needed(first):
            hit = None
            for base in bases:
                c = r_ref[base + qi] - r_ref[base + first] >= thr
                hit = c if hit is None else jnp.logical_or(hit, c)
            return hit

        first = lax.while_loop(lambda f: jnp.logical_and(f > 0, needed(f)),
                               lambda f: f - 1, qi)
        start_ref[qi] = first
        return n_items + (qi - first + 1)

    n_items = lax.fori_loop(0, nblk, find_start, 0)

    def next_item(item):
        qi, ki, valid = item
        at_diag = ki == qi
        more = jnp.logical_and(valid != 0,
                               jnp.logical_not(jnp.logical_and(at_diag, qi == last)))
        qn = jnp.where(at_diag, jnp.minimum(qi + 1, last), qi)
        kn = jnp.where(at_diag, start_ref[qn], ki + 1)
        return (jnp.where(more, qn, last), jnp.where(more, kn, last), more.astype(jnp.int32))

    def stage_a(slot, item):
        qi, ki, _ = item
        rows_q = pl.ds(pl.multiple_of(qi * AT_BLK, AT_BLK), AT_BLK)
        rows_k = pl.ds(pl.multiple_of(ki * AT_BLK, AT_BLK), AT_BLK)
        mask = mask_ref[(ki == qi).astype(jnp.int32)]
        for g in range(AT_HG):
            s_ref[slot, g] = _dot_nt(k_refs[g][rows_k, :], q_refs[g][rows_q, :]) + mask

    def stage_b(slot, item):
        qi, ki, valid = item
        live = valid != 0
        first = jnp.logical_and(live, ki == start_ref[qi])
        for g in range(AT_HG):
            delta = jnp.where(live, r_ref[bases[g] + qi] - r_ref[bases[g] + ki], NEG_BIG)
            s = s_ref[slot, g]
            m = jnp.where(first, NEG_BIG, m_ref[g])
            m_new = jnp.maximum(m, _tree_reduce(jnp.maximum, jnp.max, s) + delta)
            p_ref[slot, g] = jnp.exp2(s - (m_new - delta)).astype(BF16)
            a_ref[g] = jnp.exp2(m - m_new)
            m_ref[g] = m_new

    def stage_c(slot, item):
        qi, ki, _ = item
        for g in range(AT_HG):
            v_t = vt_refs[g][ki]
            acc = a_ref[g] * acc_ref[g] + _dot(v_t, p_ref[slot, g])
            acc_ref[g] = acc
            o_ref[qi, g * FX_DH:(g + 1) * FX_DH, :] = (
                acc[:FX_DH] / acc[FX_DH:FX_DH + 1]).astype(o_ref.dtype)

    def tick(slot, items):
        item_c, item_b, item_s, item_a = items
        ahead = (slot + 2) % AT_SLOTS
        stage_c(ahead, item_c)
        stage_b(slot, item_b)
        stage_a(ahead, item_a)
        return (item_b, item_s, item_a, next_item(item_a))

    for g in range(AT_HG):
        p_ref[AT_SLOTS - 1, g] = jnp.zeros((AT_BLK, AT_BLK), BF16)
        a_ref[g] = jnp.ones((1, AT_BLK), F32)
        m_ref[g] = jnp.zeros((1, AT_BLK), F32)
        acc_ref[g] = jnp.ones((VT_ROWS, AT_BLK), F32)

    zero = jnp.int32(0)
    item0 = (zero, zero, jnp.int32(1))
    item1 = next_item(item0)
    idle = (zero, zero, zero)
    stage_a(0, item0)
    stage_a(1, item1)

    def trip(_, items):
        for t in range(AT_TRIP_TICKS):
            items = tick(t % AT_SLOTS, items)
        return items

    lax.fori_loop(0, (n_items + AT_TRIP_TICKS) // AT_TRIP_TICKS, trip,
                  (idle, item0, item1, next_item(item1)))


def _attention(order, r_flat, q_aug, k_aug, v_t):
    bsz, s, _ = q_aug.shape
    nblk = s // AT_BLK
    causal = np.where(np.arange(AT_BLK)[:, None] <= np.arange(AT_BLK)[None, :], 0.0, NEG_BIG)
    mask = jnp.asarray(np.stack([np.zeros_like(causal), causal]), F32)

    def head_cols(g):
        return pl.BlockSpec((None, s, LANE), lambda b, h, order: (b, 0, order[h * AT_HG + g]))

    def head_vt(g):
        return pl.BlockSpec((None, nblk, VT_ROWS, AT_BLK),
                            lambda b, h, order: (b, 0, order[h * AT_HG + g], 0))

    grid_spec = pltpu.PrefetchScalarGridSpec(
        num_scalar_prefetch=1,
        grid=(bsz, FX_HEADS // AT_HG),
        in_specs=[
            pl.BlockSpec(memory_space=pltpu.SMEM),
            pl.BlockSpec((2, AT_BLK, AT_BLK), lambda b, h, order: (0, 0, 0)),
            *[head_cols(g) for g in range(AT_HG)],
            *[head_cols(g) for g in range(AT_HG)],
            *[head_vt(g) for g in range(AT_HG)],
        ],
        out_specs=pl.BlockSpec((None, nblk, AT_HG * FX_DH, AT_BLK),
                               lambda b, h, order: (b, 0, h, 0)),
        scratch_shapes=[
            pltpu.VMEM((AT_SLOTS, AT_HG, AT_BLK, AT_BLK), F32),
            pltpu.VMEM((AT_SLOTS, AT_HG, AT_BLK, AT_BLK), BF16),
            pltpu.VMEM((AT_HG, VT_ROWS, AT_BLK), F32),
            pltpu.VMEM((AT_HG, 1, AT_BLK), F32),
            pltpu.VMEM((AT_HG, 1, AT_BLK), F32),
            pltpu.SMEM((nblk,), jnp.int32),
        ],
    )
    return pl.pallas_call(
        functools.partial(_attn_kernel, nblk=nblk, n_r=r_flat.shape[0] - 1),
        out_shape=jax.ShapeDtypeStruct((bsz, nblk, FX_WIDTH, AT_BLK), BF16),
        grid_spec=grid_spec,
        compiler_params=pltpu.CompilerParams(
            dimension_semantics=("arbitrary", "arbitrary"),
            vmem_limit_bytes=VMEM_LIMIT),
        name="fox_attention",
    )(order, r_flat, mask, *([q_aug] * AT_HG), *([k_aug] * AT_HG), *([v_t] * AT_HG))


FF_TM = 512
FF_TF = D_FF
FF_CHUNK = 1024


def _ffn_kernel(x_ref, oa_ref, obt_ref, woa_ref, wob_ref, nw_ref, wu_ref, wd_ref,
                o_ref, h_ref):
    @pl.when(pl.program_id(2) == 0)
    def _():
        o_b = jnp.concatenate([_dot_tn(obt_ref[j], wob_ref[...])
                               for j in range(FF_TM // AT_BLK)], axis=0)
        x1 = x_ref[...] + _dot(oa_ref[...], woa_ref[...]) + o_b
        ms = jnp.mean(x1 * x1, axis=-1, keepdims=True)
        h_ref[...] = (x1 * lax.rsqrt(ms + NORM_EPS) * nw_ref[...]).astype(BF16)
        o_ref[...] = x1

    for c in range(FF_TF // FF_CHUNK):
        cols = slice(c * FF_CHUNK, (c + 1) * FF_CHUNK)
        u = jnp.maximum(_dot(h_ref[...], wu_ref[:, cols]), 0.0)
        o_ref[...] += _dot((u * u).astype(BF16), wd_ref[cols, :])


def _outproj_ffn(x3d, o_a, o_bt, w_out, w_out_fox, norm_w, w_up, w_down, layer):
    bsz, s, _ = x3d.shape
    resident = dict(pipeline_mode=pl.Buffered(1)) if FF_TF == D_FF else {}
    return pl.pallas_call(
        _ffn_kernel,
        out_shape=jax.ShapeDtypeStruct(x3d.shape, F32),
        grid=(bsz, s // FF_TM, D_FF // FF_TF),
        in_specs=[
            pl.BlockSpec((None, FF_TM, D_MODEL), lambda b, i, j: (b, i, 0)),
            pl.BlockSpec((None, FF_TM, HG_WIDTH), lambda b, i, j: (b, i, 0)),
            pl.BlockSpec((None, FF_TM // AT_BLK, FX_WIDTH, AT_BLK), lambda b, i, j: (b, i, 0, 0)),
            pl.BlockSpec((None, HG_WIDTH, D_MODEL), lambda b, i, j: (layer, 0, 0), **resident),
            pl.BlockSpec((FX_WIDTH, D_MODEL), lambda b, i, j: (0, 0), **resident),
            pl.BlockSpec((1, D_MODEL), lambda b, i, j: (0, 0)),
            pl.BlockSpec((None, D_MODEL, FF_TF), lambda b, i, j: (layer, 0, j), **resident),
            pl.BlockSpec((None, FF_TF, D_MODEL), lambda b, i, j: (layer, j, 0), **resident),
        ],
        out_specs=pl.BlockSpec((None, FF_TM, D_MODEL), lambda b, i, j: (b, i, 0)),
        scratch_shapes=[pltpu.VMEM((FF_TM, D_MODEL), BF16)],
        compiler_params=pltpu.CompilerParams(
            dimension_semantics=("arbitrary", "arbitrary", "arbitrary"),
            vmem_limit_bytes=VMEM_LIMIT),
        name="outproj_ffn",
    )(x3d, o_a, o_bt, w_out, w_out_fox, norm_w, w_up, w_down)


def _layer_lower_bounds(lower_bounds):
    p = jax.nn.softmax(lower_bounds.astype(F32), axis=0)
    c = jnp.cumsum(p, axis=0)
    return c - c[0:1]


def kernel(x, lower_bounds, norm1_w, w_in, fox_f_bias, q_norm_w, k_norm_w,
           hgrn_norm_w, w_out, norm2_w, w_up, w_down):
    bsz, s, d = x.shape
    depth = w_in.shape[0]
    nblk = s // AT_BLK
    lbs = _layer_lower_bounds(lower_bounds)
    w_in_pad = jnp.pad(w_in, ((0, 0), (0, 0), (0, PROJ_PAD - w_in.shape[-1]))).astype(BF16)
    w_out_b = w_out.astype(BF16)
    w_up_b = w_up.astype(BF16)
    w_down_b = w_down.astype(BF16)
    f_bias = jnp.pad(fox_f_bias, ((0, 0), (0, LANE - FX_HEADS)))

    for l in range(depth):
        proj, ff = _inproj(x.reshape(bsz * s, d), norm1_w[l][None, :], w_in_pad, l)
        proj3d = proj.reshape(bsz, s, PROJ_MAIN)
        ff3d = ff.reshape(bsz, s, LANE)
        o_a = _hgrn(proj3d, lbs[l][None, :], hgrn_norm_w[l][None, :])
        q_aug, k_aug, v_t, r_blk = _attn_prep(
            proj3d, ff3d, f_bias[l][None, :],
            jnp.tile(q_norm_w[l], FX_HEADS)[None, :],
            jnp.tile(k_norm_w[l], FX_HEADS)[None, :])
        qk_bound = (1.02 * FX_DH ** 0.5 * LOG2E * jnp.max(jnp.abs(q_norm_w[l]))
                    * jnp.max(jnp.abs(k_norm_w[l])))
        skip_thr = -(F32_EXP2_ZERO + 2.0 * qk_bound)
        r_flat = jnp.concatenate([
            jnp.transpose(r_blk[:, :, 0, :FX_HEADS], (0, 2, 1)).reshape(-1),
            skip_thr.reshape(1).astype(F32)])
        order = jnp.argsort(fox_f_bias[l]).astype(jnp.int32)
        o_bt = _attention(order, r_flat, q_aug, k_aug, v_t)
        fox_rows = (HG_WIDTH + order[:, None] * FX_DH + jnp.arange(FX_DH)[None, :]).reshape(-1)
        x = _outproj_ffn(x, o_a, o_bt, w_out_b, w_out_b[l][fox_rows], norm2_w[l][None, :],
                         w_up_b, w_down_b, l)
    return x
```

```python
import functools

import numpy as np
import jax
import jax.numpy as jnp
from jax import lax
from jax.experimental import pallas as pl
from jax.experimental.pallas import tpu as pltpu

F32 = jnp.float32
BF16 = jnp.bfloat16

D_MODEL = 1024
D_FF = 4 * D_MODEL
NORM_EPS = 1e-6

HG_WIDTH = 512
HG_HEADS = 4
HG_DK = 128
HG_CHUNK = 64
HG_LEVELS = 6

FX_WIDTH = 512
FX_HEADS = 8
FX_DH = 64

PROJ_MAIN = 4 * HG_WIDTH + 3 * FX_WIDTH
LANE = 128
PROJ_PAD = PROJ_MAIN + LANE

VMEM_LIMIT = 56 * 1024 * 1024

NEG_BIG = -1e30
LOG2E = 1.4426950408889634

NT_DIMS = (((1,), (1,)), ((), ()))
TN_DIMS = (((0,), (0,)), ((), ()))


def _dot(a, b):
    return jnp.dot(a, b, preferred_element_type=F32)


def _dot_nt(a, b):
    return lax.dot_general(a, b, NT_DIMS, preferred_element_type=F32)


def _dot_tn(a, b):
    return lax.dot_general(a, b, TN_DIMS, preferred_element_type=F32)


def _split2(x):
    hi = x.astype(BF16)
    lo = (x - hi.astype(F32)).astype(BF16)
    return hi, lo


def _split3(x):
    hi = x.astype(BF16)
    r = x - hi.astype(F32)
    mid = r.astype(BF16)
    lo = (r - mid.astype(F32)).astype(BF16)
    return hi, mid, lo


IN_TM = 512
IN_TN = 512
HG_COLS = 4 * HG_WIDTH


HG_ROWS = 512
HG_NCHUNK = HG_ROWS // HG_CHUNK
HG_E_ROWS = (2 + HG_LEVELS) * HG_CHUNK


def _hgrn_constants():
    c = HG_CHUNK
    tri = np.tril(np.ones((c, c), np.float32))
    rows = [tri, 1.0 - tri]
    idx = np.arange(c)
    for lvl in range(HG_LEVELS):
        m = c >> (lvl + 1)
        ref = (idx // (2 * m)) * (2 * m) + m - 1
        upper = (idx % (2 * m) >= m)[:, None]
        diff = tri - tri[ref]
        rows.append(np.where(upper, diff, -diff))
    mat = np.concatenate(rows, axis=0)
    mcat = np.concatenate([mat, mat], axis=1)
    x = idx[:, None] ^ idx[None, :]
    top = np.floor(np.log2(np.maximum(x, 1))).astype(np.int32)
    level = (HG_LEVELS - 1) - top
    level = np.where(idx[:, None] == idx[None, :], HG_LEVELS, level)
    level = np.where(idx[:, None] < idx[None, :], -1, level)
    level = np.concatenate([level, level], axis=1)
    return mcat.astype(np.float32), level.astype(np.int32)


def _stack_heads(t):
    return jnp.concatenate([t[:, h * HG_DK:(h + 1) * HG_DK] for h in range(HG_HEADS)], axis=0)


def _hgrn_kernel(q_ref, f_ref, i_ref, g_ref, lb_ref, nw_ref, mcat_ref, lvl_ref,
                 o_ref, st_ref):
    @pl.when(pl.program_id(1) == 0)
    def _():
        st_ref[...] = jnp.zeros_like(st_ref)

    lb = lb_ref[...]
    nw = nw_ref[...]
    mcat = mcat_ref[...]
    level = lvl_ref[...]
    scale = HG_DK ** -0.5

    def chunk(c, carry):
        r0 = pl.multiple_of(c * HG_CHUNK, HG_CHUNK)
        rows = pl.ds(r0, HG_CHUNK)
        fp = f_ref[rows, :].astype(F32)
        f = lb + (1.0 - lb) * jax.nn.sigmoid(fp)
        lf = jnp.log(f) * LOG2E
        kk = 1.0 - f
        hi, lo = _split2(lf)
        e_all = _dot(mcat, jnp.concatenate([hi, lo], axis=0))
        b = e_all[0:HG_CHUNK]
        qs = q_ref[rows, :].astype(F32) * scale
        q_in = (qs * jnp.exp2(b)).astype(BF16)
        k_out = (kk * jnp.exp2(e_all[HG_CHUNK:2 * HG_CHUNK])).astype(BF16)
        dec = jnp.exp2(b[HG_CHUNK - 1:HG_CHUNK, :])
        qb = qs.astype(BF16)
        kb = kk.astype(BF16)
        v = i_ref[rows, :]
        g = g_ref[rows, :].astype(F32)

        a = [jnp.zeros((HG_CHUNK, LANE), F32) for _ in range(HG_HEADS)]
        for lvl in range(HG_LEVELS + 1):
            if lvl < HG_LEVELS:
                x = jnp.exp2(e_all[(2 + lvl) * HG_CHUNK:(3 + lvl) * HG_CHUNK]).astype(BF16)
                ql, kl = qb * x, kb * x
            else:
                ql, kl = qb, kb
            d = _dot_nt(_stack_heads(ql), _stack_heads(kl))
            for h in range(HG_HEADS):
                blk = d[h * HG_CHUNK:(h + 1) * HG_CHUNK, (h // 2) * LANE:(h // 2 + 1) * LANE]
                a[h] = jnp.where(level == lvl, blk, a[h])

        zeros = jnp.zeros((HG_CHUNK, HG_DK), BF16)
        for h in range(HG_HEADS):
            ls = slice(h * HG_DK, (h + 1) * HG_DK)
            v_pad = jnp.concatenate([v[:, ls], zeros] if h % 2 == 0 else [zeros, v[:, ls]], axis=0)
            st = st_ref[h]
            o = _dot(a[h].astype(BF16), v_pad) + _dot_nt(q_in[:, ls], st.astype(BF16))
            st_ref[h] = st * dec[:, ls] + _dot_tn(v[:, ls], k_out[:, ls])
            var = jnp.mean(o * o, axis=-1, keepdims=True)
            gh = g[:, ls]
            y = o * lax.rsqrt(var + NORM_EPS) * nw * (gh * jax.nn.sigmoid(gh))
            o_ref[rows, ls] = y.astype(o_ref.dtype)
        return carry

    lax.fori_loop(0, HG_NCHUNK, chunk, 0, unroll=8)


def _hgrn(proj3d, lb, norm_w):
    bsz, s, _ = proj3d.shape
    mcat, level = _hgrn_constants()

    def col(j):
        return pl.BlockSpec((None, HG_ROWS, HG_WIDTH), lambda b, i, j=j: (b, i, j))

    return pl.pallas_call(
        _hgrn_kernel,
        out_shape=jax.ShapeDtypeStruct((bsz, s, HG_WIDTH), BF16),
        grid=(bsz, s // HG_ROWS),
        in_specs=[
            col(0), col(1), col(2), col(3),
            pl.BlockSpec((1, HG_WIDTH), lambda b, i: (0, 0)),
            pl.BlockSpec((1, HG_DK), lambda b, i: (0, 0)),
            pl.BlockSpec((HG_E_ROWS, 2 * HG_CHUNK), lambda b, i: (0, 0)),
            pl.BlockSpec((HG_CHUNK, 2 * HG_CHUNK), lambda b, i: (0, 0)),
        ],
        out_specs=pl.BlockSpec((None, HG_ROWS, HG_WIDTH), lambda b, i: (b, i, 0)),
        scratch_shapes=[pltpu.VMEM((HG_HEADS, HG_DK, HG_DK), F32)],
        compiler_params=pltpu.CompilerParams(
            dimension_semantics=("arbitrary", "arbitrary"), vmem_limit_bytes=VMEM_LIMIT),
        name="hgrn2",
    )(proj3d, proj3d, proj3d, proj3d, lb, norm_w,
      jnp.asarray(mcat, BF16), jnp.asarray(level))


AT_BLK = 256
PREP_SUB = IN_TM // AT_BLK
AUG_PIECES = 3
VT_ROWS = 80


def _aug_base(h):
    return h * LANE + (FX_DH if h % 2 == 0 else 0)


def _prep_constants():
    grp = np.zeros((FX_WIDTH // 2, FX_WIDTH // 2), np.float32)
    for h in range(FX_HEADS // 2):
        grp[h * FX_DH:(h + 1) * FX_DH, h * FX_DH:(h + 1) * FX_DH] = 1.0 / FX_DH
    tri = np.tril(np.ones((AT_BLK, AT_BLK), np.float32))
    width = FX_HEADS * LANE
    ones_at = FX_HEADS * AUG_PIECES
    pk = np.zeros((LANE, width), np.float32)
    cq = np.zeros((1, LANE), np.float32)
    ck = np.zeros((1, width), np.float32)
    for p in range(AUG_PIECES):
        cq[0, ones_at + p] = 1.0
        for h in range(FX_HEADS):
            base = _aug_base(h)
            ck[0, base + FX_HEADS * p + h] = 1.0
            pk[FX_HEADS * p + h, base + ones_at + p] = -1.0
    return grp, tri, pk, cq, ck


def _inproj_prep_kernel(x_ref, nw_ref, w_ref, wff_ref, fb_ref, qnw_ref, knw_ref,
                        grp_ref, tri_ref, pk_ref, cq_ref, ck_ref,
                        proj_ref, qa_ref, ka_ref, vt_ref, r_ref, carry_ref, *, steps_per_seq):
    @pl.when(pl.program_id(0) % steps_per_seq == 0)
    def _():
        carry_ref[...] = jnp.zeros_like(carry_ref)

    x = x_ref[...]
    ms = jnp.mean(x * x, axis=-1, keepdims=True)
    h = (x * lax.rsqrt(ms + NORM_EPS) * nw_ref[...]).astype(BF16)

    def proj(c):
        return _dot(h, w_ref[:, c * IN_TN:(c + 1) * IN_TN])

    for c in range(HG_COLS // IN_TN):
        proj_ref[:, c * IN_TN:(c + 1) * IN_TN] = proj(c).astype(BF16)
    q_all, k_all, v_all = (proj(HG_COLS // IN_TN + c) for c in range(3))

    z = _dot(h, wff_ref[...]) + fb_ref[...]
    lf = (jnp.minimum(z, 0.0) - jnp.log(1.0 + jnp.exp(-jnp.abs(z)))) * LOG2E
    grp = grp_ref[...]
    tri = tri_ref[...]
    lane = lax.broadcasted_iota(jnp.int32, (AT_BLK, LANE), 1)
    low_half = lane < FX_DH
    head_lane = lane < FX_HEADS

    def qk_norm(t, w):
        sq = (t * t).astype(BF16)
        half = FX_WIDTH // 2
        ms = jnp.concatenate([_dot(sq[:, :half], grp), _dot(sq[:, half:], grp)], axis=1)
        return t * lax.rsqrt(ms + NORM_EPS) * w

    for j in range(PREP_SUB):
        rows = slice(j * AT_BLK, (j + 1) * AT_BLK)
        hi, mid, lo = _split3(lf[rows])
        c_rel = _dot(tri, hi) + _dot(tri, mid) + _dot(tri, lo)
        r_ref[j] = carry_ref[...]
        carry_ref[...] = carry_ref[...] + c_rel[AT_BLK - 1:AT_BLK, :]
        packed = None
        for p, piece in enumerate(_split3(c_rel)):
            piece = jnp.where(head_lane, piece.astype(F32), 0.0)
            piece = piece if p == 0 else pltpu.roll(piece, FX_HEADS * p, axis=1)
            packed = piece if packed is None else packed + piece
        aug_q_odd = packed + cq_ref[...]
        aug_q_even = pltpu.roll(aug_q_odd, FX_DH, axis=1)
        aug_k = _dot(packed.astype(BF16), pk_ref[...]) + ck_ref[...]
        qn = qk_norm(q_all[rows], qnw_ref[...]) * (FX_DH ** -0.5 * LOG2E)
        kn = qk_norm(k_all[rows], knw_ref[...])
        for h in range(FX_HEADS):
            pair = slice((h // 2) * LANE, (h // 2 + 1) * LANE)
            head = slice(h * LANE, (h + 1) * LANE)
            data = low_half if h % 2 == 0 else jnp.logical_not(low_half)
            aug_q = aug_q_even if h % 2 == 0 else aug_q_odd
            qa_ref[rows, head] = jnp.where(data, qn[:, pair], aug_q).astype(BF16)
            ka_ref[rows, head] = jnp.where(data, kn[:, pair], aug_k[:, head]).astype(BF16)
        v_t = v_all[rows].T.astype(BF16)
        for h in range(FX_HEADS):
            vt_ref[j, h * VT_ROWS:h * VT_ROWS + FX_DH, :] = v_t[h * FX_DH:(h + 1) * FX_DH]
            vt_ref[j, h * VT_ROWS + FX_DH:(h + 1) * VT_ROWS, :] = jnp.ones(
                (VT_ROWS - FX_DH, AT_BLK), BF16)


def _inproj_prep(x2d, seq_len, norm_w, w_pad, layer, f_bias, qnw, knw):
    t = x2d.shape[0]
    consts = _prep_constants()
    grp, tri, pk = (jnp.asarray(c, BF16) for c in consts[:3])
    cq, ck = (jnp.asarray(c, F32) for c in consts[3:])
    width = FX_HEADS * LANE

    def full(shape):
        return pl.BlockSpec(shape, lambda i: (0,) * len(shape))

    return pl.pallas_call(
        functools.partial(_inproj_prep_kernel, steps_per_seq=seq_len // IN_TM),
        out_shape=(jax.ShapeDtypeStruct((t, HG_COLS), BF16),
                   jax.ShapeDtypeStruct((t, width), BF16),
                   jax.ShapeDtypeStruct((t, width), BF16),
                   jax.ShapeDtypeStruct((t // AT_BLK, FX_HEADS * VT_ROWS, AT_BLK), BF16),
                   jax.ShapeDtypeStruct((t // AT_BLK, 8, LANE), F32)),
        grid=(t // IN_TM,),
        in_specs=[
            pl.BlockSpec((IN_TM, D_MODEL), lambda i: (i, 0)),
            full((1, D_MODEL)),
            pl.BlockSpec((None, D_MODEL, PROJ_MAIN), lambda i: (layer, 0, 0)),
            pl.BlockSpec((None, D_MODEL, LANE), lambda i: (layer, 0, PROJ_MAIN // LANE)),
            full((1, LANE)), full((1, FX_WIDTH)), full((1, FX_WIDTH)),
            full((FX_WIDTH // 2, FX_WIDTH // 2)), full((AT_BLK, AT_BLK)),
            full((LANE, width)), full((1, LANE)), full((1, width)),
        ],
        out_specs=(pl.BlockSpec((IN_TM, HG_COLS), lambda i: (i, 0)),
                   pl.BlockSpec((IN_TM, width), lambda i: (i, 0)),
                   pl.BlockSpec((IN_TM, width), lambda i: (i, 0)),
                   pl.BlockSpec((PREP_SUB, FX_HEADS * VT_ROWS, AT_BLK), lambda i: (i, 0, 0)),
                   pl.BlockSpec((PREP_SUB, 8, LANE), lambda i: (i, 0, 0))),
        scratch_shapes=[pltpu.VMEM((8, LANE), F32)],
        compiler_params=pltpu.CompilerParams(
            dimension_semantics=("arbitrary",), vmem_limit_bytes=VMEM_LIMIT),
        name="inproj_prep",
    )(x2d, norm_w, w_pad, w_pad, f_bias, qnw, knw, grp, tri, pk, cq, ck)


AT_HG = 2
AT_SLOTS = 3
AT_TRIP_TICKS = 4 * AT_SLOTS
F32_EXP2_ZERO = 150.0
SUBLANES = 8


def _tree_reduce(op, final, x):
    while x.shape[0] > SUBLANES:
        half = x.shape[0] // 2
        x = op(x[:half], x[half:])
    return final(x, axis=0, keepdims=True)


def _attn_kernel(order_ref, r_ref, mask_ref, *refs, nblk, n_r):
    q_refs, k_refs, vt_refs = (refs[i * AT_HG:(i + 1) * AT_HG] for i in range(3))
    o_ref, s_ref, p_ref, acc_ref, m_ref, a_ref, start_ref = refs[3 * AT_HG:]
    b = pl.program_id(0)
    hg = pl.program_id(1)
    bases = [(b * FX_HEADS + order_ref[hg * AT_HG + g]) * nblk for g in range(AT_HG)]
    last = nblk - 1

    thr = r_ref[n_r]

    def find_start(qi, n_items):
        def needed(first):
            hit = None
            for base in bases:
                c = r_ref[base + qi] - r_ref[base + first] >= thr
                hit = c if hit is None else jnp.logical_or(hit, c)
            return hit

        first = lax.while_loop(lambda f: jnp.logical_and(f > 0, needed(f)),
                               lambda f: f - 1, qi)
        start_ref[qi] = first
        return n_items + (qi - first + 1)

    n_items = lax.fori_loop(0, nblk, find_start, 0)

    def next_item(item):
        qi, ki, valid = item
        at_diag = ki == qi
        more = jnp.logical_and(valid != 0,
                               jnp.logical_not(jnp.logical_and(at_diag, qi == last)))
        qn = jnp.where(at_diag, jnp.minimum(qi + 1, last), qi)
        kn = jnp.where(at_diag, start_ref[qn], ki + 1)
        return (jnp.where(more, qn, last), jnp.where(more, kn, last), more.astype(jnp.int32))

    def stage_a(slot, item):
        qi, ki, _ = item
        rows_q = pl.ds(pl.multiple_of(qi * AT_BLK, AT_BLK), AT_BLK)
        rows_k = pl.ds(pl.multiple_of(ki * AT_BLK, AT_BLK), AT_BLK)
        mask = mask_ref[(ki == qi).astype(jnp.int32)]
        for g in range(AT_HG):
            s_ref[slot, g] = _dot_nt(k_refs[g][rows_k, :], q_refs[g][rows_q, :]) + mask

    def stage_b(slot, item):
        qi, ki, valid = item
        live = valid != 0
        first = jnp.logical_and(live, ki == start_ref[qi])
        for g in range(AT_HG):
            delta = jnp.where(live, r_ref[bases[g] + qi] - r_ref[bases[g] + ki], NEG_BIG)
            s = s_ref[slot, g]
            m = jnp.where(first, NEG_BIG, m_ref[g])
            m_new = jnp.maximum(m, _tree_reduce(jnp.maximum, jnp.max, s) + delta)
            p_ref[slot, g] = jnp.exp2(s - (m_new - delta)).astype(BF16)
            a_ref[g] = jnp.exp2(m - m_new)
            m_ref[g] = m_new

    def stage_c(slot, item):
        qi, ki, _ = item
        for g in range(AT_HG):
            v_t = vt_refs[g][ki]
            acc = a_ref[g] * acc_ref[g] + _dot(v_t, p_ref[slot, g])
            acc_ref[g] = acc
            o_ref[qi, g * FX_DH:(g + 1) * FX_DH, :] = (
                acc[:FX_DH] / acc[FX_DH:FX_DH + 1]).astype(o_ref.dtype)

    def tick(slot, items):
        item_c, item_b, item_s, item_a = items
        ahead = (slot + 2) % AT_SLOTS
        stage_c(ahead, item_c)
        stage_b(slot, item_b)
        stage_a(ahead, item_a)
        return (item_b, item_s, item_a, next_item(item_a))

    for g in range(AT_HG):
        p_ref[AT_SLOTS - 1, g] = jnp.zeros((AT_BLK, AT_BLK), BF16)
        a_ref[g] = jnp.ones((1, AT_BLK), F32)
        m_ref[g] = jnp.zeros((1, AT_BLK), F32)
        acc_ref[g] = jnp.ones((VT_ROWS, AT_BLK), F32)

    zero = jnp.int32(0)
    item0 = (zero, zero, jnp.int32(1))
    item1 = next_item(item0)
    idle = (zero, zero, zero)
    stage_a(0, item0)
    stage_a(1, item1)

    def trip(_, items):
        for t in range(AT_TRIP_TICKS):
            items = tick(t % AT_SLOTS, items)
        return items

    lax.fori_loop(0, (n_items + AT_TRIP_TICKS) // AT_TRIP_TICKS, trip,
                  (idle, item0, item1, next_item(item1)))


def _attention(order, r_flat, q_aug, k_aug, v_t):
    bsz, s, _ = q_aug.shape
    nblk = s // AT_BLK
    causal = np.where(np.arange(AT_BLK)[:, None] <= np.arange(AT_BLK)[None, :], 0.0, NEG_BIG)
    mask = jnp.asarray(np.stack([np.zeros_like(causal), causal]), F32)

    def head_cols(g):
        return pl.BlockSpec((None, s, LANE), lambda b, h, order: (b, 0, order[h * AT_HG + g]))

    def head_vt(g):
        return pl.BlockSpec((None, nblk, VT_ROWS, AT_BLK),
                            lambda b, h, order: (b, 0, order[h * AT_HG + g], 0))

    grid_spec = pltpu.PrefetchScalarGridSpec(
        num_scalar_prefetch=1,
        grid=(bsz, FX_HEADS // AT_HG),
        in_specs=[
            pl.BlockSpec(memory_space=pltpu.SMEM),
            pl.BlockSpec((2, AT_BLK, AT_BLK), lambda b, h, order: (0, 0, 0)),
            *[head_cols(g) for g in range(AT_HG)],
            *[head_cols(g) for g in range(AT_HG)],
            *[head_vt(g) for g in range(AT_HG)],
        ],
        out_specs=pl.BlockSpec((None, nblk, AT_HG * FX_DH, AT_BLK),
                               lambda b, h, order: (b, 0, h, 0)),
        scratch_shapes=[
            pltpu.VMEM((AT_SLOTS, AT_HG, AT_BLK, AT_BLK), F32),
            pltpu.VMEM((AT_SLOTS, AT_HG, AT_BLK, AT_BLK), BF16),
            pltpu.VMEM((AT_HG, VT_ROWS, AT_BLK), F32),
            pltpu.VMEM((AT_HG, 1, AT_BLK), F32),
            pltpu.VMEM((AT_HG, 1, AT_BLK), F32),
            pltpu.SMEM((nblk,), jnp.int32),
        ],
    )
    return pl.pallas_call(
        functools.partial(_attn_kernel, nblk=nblk, n_r=r_flat.shape[0] - 1),
        out_shape=jax.ShapeDtypeStruct((bsz, nblk, FX_WIDTH, AT_BLK), BF16),
        grid_spec=grid_spec,
        compiler_params=pltpu.CompilerParams(
            dimension_semantics=("arbitrary", "arbitrary"),
            vmem_limit_bytes=VMEM_LIMIT),
        name="fox_attention",
    )(order, r_flat, mask, *([q_aug] * AT_HG), *([k_aug] * AT_HG), *([v_t] * AT_HG))


FF_TM = 512
FF_TF = D_FF
FF_CHUNK = 1024


def _ffn_kernel(x_ref, oa_ref, obt_ref, woa_ref, wob_ref, nw_ref, wu_ref, wd_ref,
                o_ref, h_ref):
    @pl.when(pl.program_id(2) == 0)
    def _():
        o_b = jnp.concatenate([_dot_tn(obt_ref[j], wob_ref[...])
                               for j in range(FF_TM // AT_BLK)], axis=0)
        x1 = x_ref[...] + _dot(oa_ref[...], woa_ref[...]) + o_b
        ms = jnp.mean(x1 * x1, axis=-1, keepdims=True)
        h_ref[...] = (x1 * lax.rsqrt(ms + NORM_EPS) * nw_ref[...]).astype(BF16)
        o_ref[...] = x1

    for c in range(FF_TF // FF_CHUNK):
        cols = slice(c * FF_CHUNK, (c + 1) * FF_CHUNK)
        u = jnp.maximum(_dot(h_ref[...], wu_ref[:, cols]), 0.0)
        o_ref[...] += _dot((u * u).astype(BF16), wd_ref[cols, :])


def _outproj_ffn(x3d, o_a, o_bt, w_out, w_out_fox, norm_w, w_up, w_down, layer):
    bsz, s, _ = x3d.shape
    resident = dict(pipeline_mode=pl.Buffered(1)) if FF_TF == D_FF else {}
    return pl.pallas_call(
        _ffn_kernel,
        out_shape=jax.ShapeDtypeStruct(x3d.shape, F32),
        grid=(bsz, s // FF_TM, D_FF // FF_TF),
        in_specs=[
            pl.BlockSpec((None, FF_TM, D_MODEL), lambda b, i, j: (b, i, 0)),
            pl.BlockSpec((None, FF_TM, HG_WIDTH), lambda b, i, j: (b, i, 0)),
            pl.BlockSpec((None, FF_TM // AT_BLK, FX_WIDTH, AT_BLK), lambda b, i, j: (b, i, 0, 0)),
            pl.BlockSpec((None, HG_WIDTH, D_MODEL), lambda b, i, j: (layer, 0, 0), **resident),
            pl.BlockSpec((FX_WIDTH, D_MODEL), lambda b, i, j: (0, 0), **resident),
            pl.BlockSpec((1, D_MODEL), lambda b, i, j: (0, 0)),
            pl.BlockSpec((None, D_MODEL, FF_TF), lambda b, i, j: (layer, 0, j), **resident),
            pl.BlockSpec((None, FF_TF, D_MODEL), lambda b, i, j: (layer, j, 0), **resident),
        ],
        out_specs=pl.BlockSpec((None, FF_TM, D_MODEL), lambda b, i, j: (b, i, 0)),
        scratch_shapes=[pltpu.VMEM((FF_TM, D_MODEL), BF16)],
        compiler_params=pltpu.CompilerParams(
            dimension_semantics=("arbitrary", "arbitrary", "arbitrary"),
            vmem_limit_bytes=VMEM_LIMIT),
        name="outproj_ffn",
    )(x3d, o_a, o_bt, w_out, w_out_fox, norm_w, w_up, w_down)


def _layer_lower_bounds(lower_bounds):
    p = jax.nn.softmax(lower_bounds.astype(F32), axis=0)
    c = jnp.cumsum(p, axis=0)
    return c - c[0:1]


def kernel(x, lower_bounds, norm1_w, w_in, fox_f_bias, q_norm_w, k_norm_w,
           hgrn_norm_w, w_out, norm2_w, w_up, w_down):
    bsz, s, d = x.shape
    depth = w_in.shape[0]
    nblk = s // AT_BLK
    lbs = _layer_lower_bounds(lower_bounds)
    w_in_pad = jnp.pad(w_in, ((0, 0), (0, 0), (0, PROJ_PAD - w_in.shape[-1]))).astype(BF16)
    w_out_b = w_out.astype(BF16)
    w_up_b = w_up.astype(BF16)
    w_down_b = w_down.astype(BF16)
    f_bias = jnp.pad(fox_f_bias, ((0, 0), (0, LANE - FX_HEADS)))

    for l in range(depth):
        proj, q_aug, k_aug, v_t, r_blk = _inproj_prep(
            x.reshape(bsz * s, d), s, norm1_w[l][None, :], w_in_pad, l, f_bias[l][None, :],
            jnp.tile(q_norm_w[l], FX_HEADS)[None, :],
            jnp.tile(k_norm_w[l], FX_HEADS)[None, :])
        width = FX_HEADS * LANE
        q_aug = q_aug.reshape(bsz, s, width)
        k_aug = k_aug.reshape(bsz, s, width)
        v_t = v_t.reshape(bsz, nblk, FX_HEADS * VT_ROWS, AT_BLK)
        r_blk = r_blk.reshape(bsz, nblk, 8, LANE)
        o_a = _hgrn(proj.reshape(bsz, s, HG_COLS), lbs[l][None, :], hgrn_norm_w[l][None, :])
        qk_bound = (1.02 * FX_DH ** 0.5 * LOG2E * jnp.max(jnp.abs(q_norm_w[l]))
                    * jnp.max(jnp.abs(k_norm_w[l])))
        skip_thr = -(F32_EXP2_ZERO + 2.0 * qk_bound)
        r_flat = jnp.concatenate([
            jnp.transpose(r_blk[:, :, 0, :FX_HEADS], (0, 2, 1)).reshape(-1),
            skip_thr.reshape(1).astype(F32)])
        order = jnp.argsort(fox_f_bias[l]).astype(jnp.int32)
        o_bt = _attention(order, r_flat, q_aug, k_aug, v_t)
        fox_rows = (HG_WIDTH + order[:, None] * FX_DH + jnp.arange(FX_DH)[None, :]).reshape(-1)
        x = _outproj_ffn(x, o_a, o_bt, w_out_b, w_out_b[l][fox_rows], norm2_w[l][None, :],
                         w_up_b, w_down_b, l)
    return x
```

```python
import functools

import numpy as np
import jax
import jax.numpy as jnp
from jax import lax
from jax.experimental import pallas as pl
from jax.experimental.pallas import tpu as pltpu

F32 = jnp.float32
BF16 = jnp.bfloat16

D_MODEL = 1024
D_FF = 4 * D_MODEL
NORM_EPS = 1e-6

HG_WIDTH = 512
HG_HEADS = 4
HG_DK = 128
HG_CHUNK = 64
HG_LEVELS = 6

FX_WIDTH = 512
FX_HEADS = 8
FX_DH = 64

PROJ_MAIN = 4 * HG_WIDTH + 3 * FX_WIDTH
LANE = 128
PROJ_PAD = PROJ_MAIN + LANE

VMEM_LIMIT = 56 * 1024 * 1024

NEG_BIG = -1e30
LOG2E = 1.4426950408889634

NT_DIMS = (((1,), (1,)), ((), ()))
TN_DIMS = (((0,), (0,)), ((), ()))


def _dot(a, b):
    return jnp.dot(a, b, preferred_element_type=F32)


def _dot_nt(a, b):
    return lax.dot_general(a, b, NT_DIMS, preferred_element_type=F32)


def _dot_tn(a, b):
    return lax.dot_general(a, b, TN_DIMS, preferred_element_type=F32)


def _split2(x):
    hi = x.astype(BF16)
    lo = (x - hi.astype(F32)).astype(BF16)
    return hi, lo


def _split3(x):
    hi = x.astype(BF16)
    r = x - hi.astype(F32)
    mid = r.astype(BF16)
    lo = (r - mid.astype(F32)).astype(BF16)
    return hi, mid, lo


IN_TM = 1024
IN_TN = 512
HG_COLS = 4 * HG_WIDTH


HG_ROWS = 1024
HG_NCHUNK = HG_ROWS // HG_CHUNK
HG_E_ROWS = (2 + HG_LEVELS) * HG_CHUNK


def _hgrn_constants():
    c = HG_CHUNK
    tri = np.tril(np.ones((c, c), np.float32))
    rows = [tri, 1.0 - tri]
    idx = np.arange(c)
    for lvl in range(HG_LEVELS):
        m = c >> (lvl + 1)
        ref = (idx // (2 * m)) * (2 * m) + m - 1
        upper = (idx % (2 * m) >= m)[:, None]
        diff = tri - tri[ref]
        rows.append(np.where(upper, diff, -diff))
    mat = np.concatenate(rows, axis=0)
    mcat = np.concatenate([mat, mat], axis=1)
    x = idx[:, None] ^ idx[None, :]
    top = np.floor(np.log2(np.maximum(x, 1))).astype(np.int32)
    level = (HG_LEVELS - 1) - top
    level = np.where(idx[:, None] == idx[None, :], HG_LEVELS, level)
    level = np.where(idx[:, None] < idx[None, :], -1, level)
    level = np.concatenate([level, level], axis=1)
    return mcat.astype(np.float32), level.astype(np.int32)


def _stack_heads(t, heads):
    return jnp.concatenate([t[:, h * HG_DK:(h + 1) * HG_DK] for h in heads], axis=0)


def _hgrn_kernel(q_ref, f_ref, i_ref, g_ref, lb_ref, nw_ref, mcat_ref, lvl_ref,
                 o_ref, st_ref):
    @pl.when(pl.program_id(1) == 0)
    def _():
        st_ref[...] = jnp.zeros_like(st_ref)

    lb = lb_ref[...]
    nw = nw_ref[...]
    mcat = mcat_ref[...]
    level = lvl_ref[...]
    scale = HG_DK ** -0.5

    def chunk(c, carry):
        r0 = pl.multiple_of(c * HG_CHUNK, HG_CHUNK)
        rows = pl.ds(r0, HG_CHUNK)
        fp = f_ref[rows, :].astype(F32)
        f = lb + (1.0 - lb) * jax.nn.sigmoid(fp)
        lf = jnp.log(f) * LOG2E
        kk = 1.0 - f
        hi, lo = _split2(lf)
        e_all = _dot(mcat, jnp.concatenate([hi, lo], axis=0))
        b = e_all[0:HG_CHUNK]
        qs = q_ref[rows, :].astype(F32) * scale
        q_in = (qs * jnp.exp2(b)).astype(BF16)
        k_out = (kk * jnp.exp2(e_all[HG_CHUNK:2 * HG_CHUNK])).astype(BF16)
        dec = jnp.exp2(b[HG_CHUNK - 1:HG_CHUNK, :])
        qb = qs.astype(BF16)
        kb = kk.astype(BF16)
        v = i_ref[rows, :]
        g = g_ref[rows, :].astype(F32)

        a = [jnp.zeros((HG_CHUNK, LANE), F32) for _ in range(HG_HEADS)]
        for lvl in range(HG_LEVELS + 1):
            if lvl < HG_LEVELS:
                x = jnp.exp2(e_all[(2 + lvl) * HG_CHUNK:(3 + lvl) * HG_CHUNK]).astype(BF16)
                ql, kl = qb * x, kb * x
            else:
                ql, kl = qb, kb
            for pair in range(HG_HEADS // 2):
                heads = (2 * pair, 2 * pair + 1)
                d = _dot_nt(_stack_heads(ql, heads), _stack_heads(kl, heads))
                for i, h in enumerate(heads):
                    blk = d[i * HG_CHUNK:(i + 1) * HG_CHUNK, :]
                    a[h] = jnp.where(level == lvl, blk, a[h])

        zeros = jnp.zeros((HG_CHUNK, HG_DK), BF16)
        for h in range(HG_HEADS):
            ls = slice(h * HG_DK, (h + 1) * HG_DK)
            v_pad = jnp.concatenate([v[:, ls], zeros] if h % 2 == 0 else [zeros, v[:, ls]], axis=0)
            st = st_ref[h]
            o = _dot(a[h].astype(BF16), v_pad) + _dot_nt(q_in[:, ls], st.astype(BF16))
            st_ref[h] = st * dec[:, ls] + _dot_tn(v[:, ls], k_out[:, ls])
            var = jnp.mean(o * o, axis=-1, keepdims=True)
            gh = g[:, ls]
            y = o * lax.rsqrt(var + NORM_EPS) * nw * (gh * jax.nn.sigmoid(gh))
            o_ref[rows, ls] = y.astype(o_ref.dtype)
        return carry

    lax.fori_loop(0, HG_NCHUNK, chunk, 0, unroll=HG_NCHUNK)


def _hgrn(proj3d, lb, norm_w):
    bsz, s, _ = proj3d.shape
    mcat, level = _hgrn_constants()

    def col(j):
        return pl.BlockSpec((None, HG_ROWS, HG_WIDTH), lambda b, i, j=j: (b, i, j))

    return pl.pallas_call(
        _hgrn_kernel,
        out_shape=jax.ShapeDtypeStruct((bsz, s, HG_WIDTH), BF16),
        grid=(bsz, s // HG_ROWS),
        in_specs=[
            col(0), col(1), col(2), col(3),
            pl.BlockSpec((1, HG_WIDTH), lambda b, i: (0, 0)),
            pl.BlockSpec((1, HG_DK), lambda b, i: (0, 0)),
            pl.BlockSpec((HG_E_ROWS, 2 * HG_CHUNK), lambda b, i: (0, 0)),
            pl.BlockSpec((HG_CHUNK, 2 * HG_CHUNK), lambda b, i: (0, 0)),
        ],
        out_specs=pl.BlockSpec((None, HG_ROWS, HG_WIDTH), lambda b, i: (b, i, 0)),
        scratch_shapes=[pltpu.VMEM((HG_HEADS, HG_DK, HG_DK), F32)],
        compiler_params=pltpu.CompilerParams(
            dimension_semantics=("arbitrary", "arbitrary"), vmem_limit_bytes=VMEM_LIMIT),
        name="hgrn2",
    )(proj3d, proj3d, proj3d, proj3d, lb, norm_w,
      jnp.asarray(mcat, BF16), jnp.asarray(level))


AT_BLK = 256
PREP_SUB = IN_TM // AT_BLK
AUG_PIECES = 3
VT_ROWS = 80


def _aug_base(h):
    return h * LANE + (FX_DH if h % 2 == 0 else 0)


def _prep_constants():
    grp = np.zeros((FX_WIDTH // 2, FX_WIDTH // 2), np.float32)
    for h in range(FX_HEADS // 2):
        grp[h * FX_DH:(h + 1) * FX_DH, h * FX_DH:(h + 1) * FX_DH] = 1.0 / FX_DH
    tri = np.tril(np.ones((AT_BLK, AT_BLK), np.float32))
    width = FX_HEADS * LANE
    ones_at = FX_HEADS * AUG_PIECES
    pk = np.zeros((LANE, width), np.float32)
    cq = np.zeros((1, LANE), np.float32)
    ck = np.zeros((1, width), np.float32)
    for p in range(AUG_PIECES):
        cq[0, ones_at + p] = 1.0
        for h in range(FX_HEADS):
            base = _aug_base(h)
            ck[0, base + FX_HEADS * p + h] = 1.0
            pk[FX_HEADS * p + h, base + ones_at + p] = -1.0
    return grp, tri, pk, cq, ck


def _inproj_prep_kernel(x_ref, nw_ref, w_ref, wff_ref, fb_ref, qnw_ref, knw_ref,
                        grp_ref, tri_ref, pk_ref, cq_ref, ck_ref,
                        proj_ref, qa_ref, ka_ref, vt_ref, r_ref, carry_ref, *, steps_per_seq):
    @pl.when(pl.program_id(0) % steps_per_seq == 0)
    def _():
        carry_ref[...] = jnp.zeros_like(carry_ref)

    x = x_ref[...]
    ms = jnp.mean(x * x, axis=-1, keepdims=True)
    h = (x * lax.rsqrt(ms + NORM_EPS) * nw_ref[...]).astype(BF16)

    def proj(c):
        return _dot(h, w_ref[:, c * IN_TN:(c + 1) * IN_TN])

    for c in range(HG_COLS // IN_TN):
        proj_ref[:, c * IN_TN:(c + 1) * IN_TN] = proj(c).astype(BF16)
    q_all, k_all, v_all = (proj(HG_COLS // IN_TN + c) for c in range(3))

    z = _dot(h, wff_ref[...]) + fb_ref[...]
    lf = (jnp.minimum(z, 0.0) - jnp.log(1.0 + jnp.exp(-jnp.abs(z)))) * LOG2E
    grp = grp_ref[...]
    tri = tri_ref[...]
    lane = lax.broadcasted_iota(jnp.int32, (AT_BLK, LANE), 1)
    low_half = lane < FX_DH
    head_lane = lane < FX_HEADS

    def qk_norm(t, w):
        sq = (t * t).astype(BF16)
        half = FX_WIDTH // 2
        ms = jnp.concatenate([_dot(sq[:, :half], grp), _dot(sq[:, half:], grp)], axis=1)
        return t * lax.rsqrt(ms + NORM_EPS) * w

    for j in range(PREP_SUB):
        rows = slice(j * AT_BLK, (j + 1) * AT_BLK)
        hi, mid, lo = _split3(lf[rows])
        c_rel = _dot(tri, hi) + _dot(tri, mid) + _dot(tri, lo)
        r_ref[j] = carry_ref[...]
        carry_ref[...] = carry_ref[...] + c_rel[AT_BLK - 1:AT_BLK, :]
        packed = None
        for p, piece in enumerate(_split3(c_rel)):
            piece = jnp.where(head_lane, piece.astype(F32), 0.0)
            piece = piece if p == 0 else pltpu.roll(piece, FX_HEADS * p, axis=1)
            packed = piece if packed is None else packed + piece
        aug_q_odd = packed + cq_ref[...]
        aug_q_even = pltpu.roll(aug_q_odd, FX_DH, axis=1)
        aug_k = _dot(packed.astype(BF16), pk_ref[...]) + ck_ref[...]
        qn = qk_norm(q_all[rows], qnw_ref[...]) * (FX_DH ** -0.5 * LOG2E)
        kn = qk_norm(k_all[rows], knw_ref[...])
        for h in range(FX_HEADS):
            pair = slice((h // 2) * LANE, (h // 2 + 1) * LANE)
            head = slice(h * LANE, (h + 1) * LANE)
            data = low_half if h % 2 == 0 else jnp.logical_not(low_half)
            aug_q = aug_q_even if h % 2 == 0 else aug_q_odd
            qa_ref[rows, head] = jnp.where(data, qn[:, pair], aug_q).astype(BF16)
            ka_ref[rows, head] = jnp.where(data, kn[:, pair], aug_k[:, head]).astype(BF16)
        v_t = v_all[rows].T.astype(BF16)
        for h in range(FX_HEADS):
            vt_ref[j, h * VT_ROWS:h * VT_ROWS + FX_DH, :] = v_t[h * FX_DH:(h + 1) * FX_DH]
            vt_ref[j, h * VT_ROWS + FX_DH:(h + 1) * VT_ROWS, :] = jnp.ones(
                (VT_ROWS - FX_DH, AT_BLK), BF16)


def _inproj_prep(x2d, seq_len, norm_w, w_pad, layer, f_bias, qnw, knw):
    t = x2d.shape[0]
    consts = _prep_constants()
    grp, tri, pk = (jnp.asarray(c, BF16) for c in consts[:3])
    cq, ck = (jnp.asarray(c, F32) for c in consts[3:])
    width = FX_HEADS * LANE

    def full(shape):
        return pl.BlockSpec(shape, lambda i: (0,) * len(shape))

    return pl.pallas_call(
        functools.partial(_inproj_prep_kernel, steps_per_seq=seq_len // IN_TM),
        out_shape=(jax.ShapeDtypeStruct((t, HG_COLS), BF16),
                   jax.ShapeDtypeStruct((t, width), BF16),
                   jax.ShapeDtypeStruct((t, width), BF16),
                   jax.ShapeDtypeStruct((t // AT_BLK, FX_HEADS * VT_ROWS, AT_BLK), BF16),
                   jax.ShapeDtypeStruct((t // AT_BLK, 8, LANE), F32)),
        grid=(t // IN_TM,),
        in_specs=[
            pl.BlockSpec((IN_TM, D_MODEL), lambda i: (i, 0)),
            full((1, D_MODEL)),
            pl.BlockSpec((None, D_MODEL, PROJ_MAIN), lambda i: (layer, 0, 0),
                         pipeline_mode=pl.Buffered(1)),
            pl.BlockSpec((None, D_MODEL, LANE), lambda i: (layer, 0, PROJ_MAIN // LANE)),
            full((1, LANE)), full((1, FX_WIDTH)), full((1, FX_WIDTH)),
            full((FX_WIDTH // 2, FX_WIDTH // 2)), full((AT_BLK, AT_BLK)),
            full((LANE, width)), full((1, LANE)), full((1, width)),
        ],
        out_specs=(pl.BlockSpec((IN_TM, HG_COLS), lambda i: (i, 0)),
                   pl.BlockSpec((IN_TM, width), lambda i: (i, 0)),
                   pl.BlockSpec((IN_TM, width), lambda i: (i, 0)),
                   pl.BlockSpec((PREP_SUB, FX_HEADS * VT_ROWS, AT_BLK), lambda i: (i, 0, 0)),
                   pl.BlockSpec((PREP_SUB, 8, LANE), lambda i: (i, 0, 0))),
        scratch_shapes=[pltpu.VMEM((8, LANE), F32)],
        compiler_params=pltpu.CompilerParams(
            dimension_semantics=("arbitrary",), vmem_limit_bytes=VMEM_LIMIT),
        name="inproj_prep",
    )(x2d, norm_w, w_pad, w_pad, f_bias, qnw, knw, grp, tri, pk, cq, ck)


AT_HG = 2
AT_SLOTS = 3
AT_TRIP_TICKS = 4 * AT_SLOTS
F32_EXP2_ZERO = 150.0
SUBLANES = 8


def _tree_reduce(op, final, x):
    while x.shape[0] > SUBLANES:
        half = x.shape[0] // 2
        x = op(x[:half], x[half:])
    return final(x, axis=0, keepdims=True)


def _attn_kernel(order_ref, r_ref, mask_ref, *refs, nblk, n_r):
    q_refs, k_refs, vt_refs = (refs[i * AT_HG:(i + 1) * AT_HG] for i in range(3))
    o_ref, s_ref, p_ref, acc_ref, m_ref, a_ref, start_ref = refs[3 * AT_HG:]
    b = pl.program_id(0)
    hg = pl.program_id(1)
    bases = [(b * FX_HEADS + order_ref[hg * AT_HG + g]) * nblk for g in range(AT_HG)]
    last = nblk - 1

    thr = r_ref[n_r]

    def find_start(qi, n_items):
        def needed(first):
            hit = None
            for base in bases:
                c = r_ref[base + qi] - r_ref[base + first] >= thr
                hit = c if hit is None else jnp.logical_or(hit, c)
            return hit

        first = lax.while_loop(lambda f: jnp.logical_and(f > 0, needed(f)),
                               lambda f: f - 1, qi)
        start_ref[qi] = first
        return n_items + (qi - first + 1)

    n_items = lax.fori_loop(0, nblk, find_start, 0)

    def next_item(item):
        qi, ki, valid = item
        at_diag = ki == qi
        more = jnp.logical_and(valid != 0,
                               jnp.logical_not(jnp.logical_and(at_diag, qi == last)))
        qn = jnp.where(at_diag, jnp.minimum(qi + 1, last), qi)
        kn = jnp.where(at_diag, start_ref[qn], ki + 1)
        return (jnp.where(more, qn, last), jnp.where(more, kn, last), more.astype(jnp.int32))

    def stage_a(slot, item):
        qi, ki, _ = item
        rows_q = pl.ds(pl.multiple_of(qi * AT_BLK, AT_BLK), AT_BLK)
        rows_k = pl.ds(pl.multiple_of(ki * AT_BLK, AT_BLK), AT_BLK)
        mask = mask_ref[(ki == qi).astype(jnp.int32)]
        for g in range(AT_HG):
            s_ref[slot, g] = _dot_nt(k_refs[g][rows_k, :], q_refs[g][rows_q, :]) + mask

    def stage_b(slot, item):
        qi, ki, valid = item
        live = valid != 0
        first = jnp.logical_and(live, ki == start_ref[qi])
        for g in range(AT_HG):
            delta = jnp.where(live, r_ref[bases[g] + qi] - r_ref[bases[g] + ki], NEG_BIG)
            s = s_ref[slot, g]
            m = jnp.where(first, NEG_BIG, m_ref[g])
            m_new = jnp.maximum(m, _tree_reduce(jnp.maximum, jnp.max, s) + delta)
            p_ref[slot, g] = jnp.exp2(s - (m_new - delta)).astype(BF16)
            a_ref[g] = jnp.exp2(m - m_new)
            m_ref[g] = m_new

    def stage_c(slot, item):
        qi, ki, _ = item
        for g in range(AT_HG):
            v_t = vt_refs[g][ki]
            acc = a_ref[g] * acc_ref[g] + _dot(v_t, p_ref[slot, g])
            acc_ref[g] = acc
            o_ref[qi, g * FX_DH:(g + 1) * FX_DH, :] = (
                acc[:FX_DH] / acc[FX_DH:FX_DH + 1]).astype(o_ref.dtype)

    def tick(slot, items):
        item_c, item_b, item_s, item_a = items
        ahead = (slot + 2) % AT_SLOTS
        stage_c(ahead, item_c)
        stage_b(slot, item_b)
        stage_a(ahead, item_a)
        return (item_b, item_s, item_a, next_item(item_a))

    for g in range(AT_HG):
        p_ref[AT_SLOTS - 1, g] = jnp.zeros((AT_BLK, AT_BLK), BF16)
        a_ref[g] = jnp.ones((1, AT_BLK), F32)
        m_ref[g] = jnp.zeros((1, AT_BLK), F32)
        acc_ref[g] = jnp.ones((VT_ROWS, AT_BLK), F32)

    zero = jnp.int32(0)
    item0 = (zero, zero, jnp.int32(1))
    item1 = next_item(item0)
    idle = (zero, zero, zero)
    stage_a(0, item0)
    stage_a(1, item1)

    def trip(_, items):
        for t in range(AT_TRIP_TICKS):
            items = tick(t % AT_SLOTS, items)
        return items

    lax.fori_loop(0, (n_items + AT_TRIP_TICKS) // AT_TRIP_TICKS, trip,
                  (idle, item0, item1, next_item(item1)))


def _attention(order, r_flat, q_aug, k_aug, v_t):
    bsz, s, _ = q_aug.shape
    nblk = s // AT_BLK
    causal = np.where(np.arange(AT_BLK)[:, None] <= np.arange(AT_BLK)[None, :], 0.0, NEG_BIG)
    mask = jnp.asarray(np.stack([np.zeros_like(causal), causal]), F32)

    def head_cols(g):
        return pl.BlockSpec((None, s, LANE), lambda b, h, order: (b, 0, order[h * AT_HG + g]))

    def head_vt(g):
        return pl.BlockSpec((None, nblk, VT_ROWS, AT_BLK),
                            lambda b, h, order: (b, 0, order[h * AT_HG + g], 0))

    grid_spec = pltpu.PrefetchScalarGridSpec(
        num_scalar_prefetch=1,
        grid=(bsz, FX_HEADS // AT_HG),
        in_specs=[
            pl.BlockSpec(memory_space=pltpu.SMEM),
            pl.BlockSpec((2, AT_BLK, AT_BLK), lambda b, h, order: (0, 0, 0)),
            *[head_cols(g) for g in range(AT_HG)],
            *[head_cols(g) for g in range(AT_HG)],
            *[head_vt(g) for g in range(AT_HG)],
        ],
        out_specs=pl.BlockSpec((None, nblk, AT_HG * FX_DH, AT_BLK),
                               lambda b, h, order: (b, 0, h, 0)),
        scratch_shapes=[
            pltpu.VMEM((AT_SLOTS, AT_HG, AT_BLK, AT_BLK), F32),
            pltpu.VMEM((AT_SLOTS, AT_HG, AT_BLK, AT_BLK), BF16),
            pltpu.VMEM((AT_HG, VT_ROWS, AT_BLK), F32),
            pltpu.VMEM((AT_HG, 1, AT_BLK), F32),
            pltpu.VMEM((AT_HG, 1, AT_BLK), F32),
            pltpu.SMEM((nblk,), jnp.int32),
        ],
    )
    return pl.pallas_call(
        functools.partial(_attn_kernel, nblk=nblk, n_r=r_flat.shape[0] - 1),
        out_shape=jax.ShapeDtypeStruct((bsz, nblk, FX_WIDTH, AT_BLK), BF16),
        grid_spec=grid_spec,
        compiler_params=pltpu.CompilerParams(
            dimension_semantics=("arbitrary", "arbitrary"),
            vmem_limit_bytes=VMEM_LIMIT),
        name="fox_attention",
    )(order, r_flat, mask, *([q_aug] * AT_HG), *([k_aug] * AT_HG), *([v_t] * AT_HG))


FF_TM = 512
FF_TF = D_FF
FF_CHUNK = 1024


def _ffn_kernel(x_ref, oa_ref, obt_ref, woa_ref, wob_ref, nw_ref, wu_ref, wd_ref,
                o_ref, h_ref):
    @pl.when(pl.program_id(2) == 0)
    def _():
        o_b = jnp.concatenate([_dot_tn(obt_ref[j], wob_ref[...])
                               for j in range(FF_TM // AT_BLK)], axis=0)
        x1 = x_ref[...] + _dot(oa_ref[...], woa_ref[...]) + o_b
        ms = jnp.mean(x1 * x1, axis=-1, keepdims=True)
        h_ref[...] = (x1 * lax.rsqrt(ms + NORM_EPS) * nw_ref[...]).astype(BF16)
        o_ref[...] = x1

    for c in range(FF_TF // FF_CHUNK):
        cols = slice(c * FF_CHUNK, (c + 1) * FF_CHUNK)
        u = jnp.maximum(_dot(h_ref[...], wu_ref[:, cols]), 0.0)
        o_ref[...] += _dot((u * u).astype(BF16), wd_ref[cols, :])


def _outproj_ffn(x3d, o_a, o_bt, w_out, w_out_fox, norm_w, w_up, w_down, layer):
    bsz, s, _ = x3d.shape
    resident = dict(pipeline_mode=pl.Buffered(1)) if FF_TF == D_FF else {}
    return pl.pallas_call(
        _ffn_kernel,
        out_shape=jax.ShapeDtypeStruct(x3d.shape, F32),
        grid=(bsz, s // FF_TM, D_FF // FF_TF),
        in_specs=[
            pl.BlockSpec((None, FF_TM, D_MODEL), lambda b, i, j: (b, i, 0)),
            pl.BlockSpec((None, FF_TM, HG_WIDTH), lambda b, i, j: (b, i, 0)),
            pl.BlockSpec((None, FF_TM // AT_BLK, FX_WIDTH, AT_BLK), lambda b, i, j: (b, i, 0, 0)),
            pl.BlockSpec((None, HG_WIDTH, D_MODEL), lambda b, i, j: (layer, 0, 0), **resident),
            pl.BlockSpec((FX_WIDTH, D_MODEL), lambda b, i, j: (0, 0), **resident),
            pl.BlockSpec((1, D_MODEL), lambda b, i, j: (0, 0)),
            pl.BlockSpec((None, D_MODEL, FF_TF), lambda b, i, j: (layer, 0, j), **resident),
            pl.BlockSpec((None, FF_TF, D_MODEL), lambda b, i, j: (layer, j, 0), **resident),
        ],
        out_specs=pl.BlockSpec((None, FF_TM, D_MODEL), lambda b, i, j: (b, i, 0)),
        scratch_shapes=[pltpu.VMEM((FF_TM, D_MODEL), BF16)],
        compiler_params=pltpu.CompilerParams(
            dimension_semantics=("arbitrary", "arbitrary", "arbitrary"),
            vmem_limit_bytes=VMEM_LIMIT),
        name="outproj_ffn",
    )(x3d, o_a, o_bt, w_out, w_out_fox, norm_w, w_up, w_down)


def _layer_lower_bounds(lower_bounds):
    p = jax.nn.softmax(lower_bounds.astype(F32), axis=0)
    c = jnp.cumsum(p, axis=0)
    return c - c[0:1]


def kernel(x, lower_bounds, norm1_w, w_in, fox_f_bias, q_norm_w, k_norm_w,
           hgrn_norm_w, w_out, norm2_w, w_up, w_down):
    bsz, s, d = x.shape
    depth = w_in.shape[0]
    nblk = s // AT_BLK
    lbs = _layer_lower_bounds(lower_bounds)
    w_in_pad = jnp.pad(w_in, ((0, 0), (0, 0), (0, PROJ_PAD - w_in.shape[-1]))).astype(BF16)
    w_out_b = w_out.astype(BF16)
    w_up_b = w_up.astype(BF16)
    w_down_b = w_down.astype(BF16)
    f_bias = jnp.pad(fox_f_bias, ((0, 0), (0, LANE - FX_HEADS)))

    for l in range(depth):
        proj, q_aug, k_aug, v_t, r_blk = _inproj_prep(
            x.reshape(bsz * s, d), s, norm1_w[l][None, :], w_in_pad, l, f_bias[l][None, :],
            jnp.tile(q_norm_w[l], FX_HEADS)[None, :],
            jnp.tile(k_norm_w[l], FX_HEADS)[None, :])
        width = FX_HEADS * LANE
        q_aug = q_aug.reshape(bsz, s, width)
        k_aug = k_aug.reshape(bsz, s, width)
        v_t = v_t.reshape(bsz, nblk, FX_HEADS * VT_ROWS, AT_BLK)
        r_blk = r_blk.reshape(bsz, nblk, 8, LANE)
        o_a = _hgrn(proj.reshape(bsz, s, HG_COLS), lbs[l][None, :], hgrn_norm_w[l][None, :])
        qk_bound = (1.02 * FX_DH ** 0.5 * LOG2E * jnp.max(jnp.abs(q_norm_w[l]))
                    * jnp.max(jnp.abs(k_norm_w[l])))
        skip_thr = -(F32_EXP2_ZERO + 2.0 * qk_bound)
        r_flat = jnp.concatenate([
            jnp.transpose(r_blk[:, :, 0, :FX_HEADS], (0, 2, 1)).reshape(-1),
            skip_thr.reshape(1).astype(F32)])
        order = jnp.argsort(fox_f_bias[l]).astype(jnp.int32)
        o_bt = _attention(order, r_flat, q_aug, k_aug, v_t)
        fox_rows = (HG_WIDTH + order[:, None] * FX_DH + jnp.arange(FX_DH)[None, :]).reshape(-1)
        x = _outproj_ffn(x, o_a, o_bt, w_out_b, w_out_b[l][fox_rows], norm2_w[l][None, :],
                         w_up_b, w_down_b, l)
    return x
```

```python
import functools

import numpy as np
import jax
import jax.numpy as jnp
from jax import lax
from jax.experimental import pallas as pl
from jax.experimental.pallas import tpu as pltpu

F32 = jnp.float32
BF16 = jnp.bfloat16

D_MODEL = 1024
D_FF = 4 * D_MODEL
NORM_EPS = 1e-6

HG_WIDTH = 512
HG_HEADS = 4
HG_DK = 128
HG_CHUNK = 64
HG_LEVELS = 6

FX_WIDTH = 512
FX_HEADS = 8
FX_DH = 64

PROJ_MAIN = 4 * HG_WIDTH + 3 * FX_WIDTH
LANE = 128
PROJ_PAD = PROJ_MAIN + LANE

VMEM_LIMIT = 56 * 1024 * 1024

NEG_BIG = -1e30
LOG2E = 1.4426950408889634

NT_DIMS = (((1,), (1,)), ((), ()))
TN_DIMS = (((0,), (0,)), ((), ()))


def _dot(a, b):
    return jnp.dot(a, b, preferred_element_type=F32)


def _dot_nt(a, b):
    return lax.dot_general(a, b, NT_DIMS, preferred_element_type=F32)


def _dot_tn(a, b):
    return lax.dot_general(a, b, TN_DIMS, preferred_element_type=F32)


def _split2(x):
    hi = x.astype(BF16)
    lo = (x - hi.astype(F32)).astype(BF16)
    return hi, lo


def _split3(x):
    hi = x.astype(BF16)
    r = x - hi.astype(F32)
    mid = r.astype(BF16)
    lo = (r - mid.astype(F32)).astype(BF16)
    return hi, mid, lo


IN_TM = 1024
IN_TN = 512
HG_COLS = 4 * HG_WIDTH


HG_ROWS = 1024
HG_NCHUNK = HG_ROWS // HG_CHUNK
HG_E_ROWS = (2 + HG_LEVELS) * HG_CHUNK


def _hgrn_constants():
    c = HG_CHUNK
    tri = np.tril(np.ones((c, c), np.float32))
    rows = [tri, 1.0 - tri]
    idx = np.arange(c)
    for lvl in range(HG_LEVELS):
        m = c >> (lvl + 1)
        ref = (idx // (2 * m)) * (2 * m) + m - 1
        upper = (idx % (2 * m) >= m)[:, None]
        diff = tri - tri[ref]
        rows.append(np.where(upper, diff, -diff))
    mat = np.concatenate(rows, axis=0)
    mcat = np.concatenate([mat, mat], axis=1)
    x = idx[:, None] ^ idx[None, :]
    top = np.floor(np.log2(np.maximum(x, 1))).astype(np.int32)
    level = (HG_LEVELS - 1) - top
    level = np.where(idx[:, None] == idx[None, :], HG_LEVELS, level)
    level = np.where(idx[:, None] < idx[None, :], -1, level)
    level = np.concatenate([level, level], axis=1)
    return mcat.astype(np.float32), level.astype(np.int32)


def _stack_heads(t, heads):
    return jnp.concatenate([t[:, h * HG_DK:(h + 1) * HG_DK] for h in heads], axis=0)


def _hgrn_kernel(q_ref, f_ref, i_ref, g_ref, lb_ref, nw_ref, mcat_ref, lvl_ref,
                 o_ref, st_ref):
    @pl.when(pl.program_id(1) == 0)
    def _():
        st_ref[...] = jnp.zeros_like(st_ref)

    lb = lb_ref[...]
    nw = nw_ref[...]
    mcat = mcat_ref[...]
    level = lvl_ref[...]
    scale = HG_DK ** -0.5

    def chunk(c, carry):
        r0 = pl.multiple_of(c * HG_CHUNK, HG_CHUNK)
        rows = pl.ds(r0, HG_CHUNK)
        fp = f_ref[rows, :].astype(F32)
        f = lb + (1.0 - lb) * jax.nn.sigmoid(fp)
        lf = jnp.log(f) * LOG2E
        kk = 1.0 - f
        hi, lo = _split2(lf)
        e_all = _dot(mcat, jnp.concatenate([hi, lo], axis=0))
        b = e_all[0:HG_CHUNK]
        qs = q_ref[rows, :].astype(F32) * scale
        q_in = (qs * jnp.exp2(b)).astype(BF16)
        k_out = (kk * jnp.exp2(e_all[HG_CHUNK:2 * HG_CHUNK])).astype(BF16)
        dec = jnp.exp2(b[HG_CHUNK - 1:HG_CHUNK, :])
        qb = qs.astype(BF16)
        kb = kk.astype(BF16)
        v = i_ref[rows, :]
        g = g_ref[rows, :].astype(F32)

        a = [jnp.zeros((HG_CHUNK, LANE), F32) for _ in range(HG_HEADS)]
        for lvl in range(HG_LEVELS + 1):
            if lvl < HG_LEVELS:
                x = jnp.exp2(e_all[(2 + lvl) * HG_CHUNK:(3 + lvl) * HG_CHUNK]).astype(BF16)
                ql, kl = qb * x, kb * x
            else:
                ql, kl = qb, kb
            for pair in range(HG_HEADS // 2):
                heads = (2 * pair, 2 * pair + 1)
                d = _dot_nt(_stack_heads(ql, heads), _stack_heads(kl, heads))
                for i, h in enumerate(heads):
                    blk = d[i * HG_CHUNK:(i + 1) * HG_CHUNK, :]
                    a[h] = jnp.where(level == lvl, blk, a[h])

        zeros = jnp.zeros((HG_CHUNK, HG_DK), BF16)
        for h in range(HG_HEADS):
            ls = slice(h * HG_DK, (h + 1) * HG_DK)
            v_pad = jnp.concatenate([v[:, ls], zeros] if h % 2 == 0 else [zeros, v[:, ls]], axis=0)
            st = st_ref[h]
            o = _dot(a[h].astype(BF16), v_pad) + _dot_nt(q_in[:, ls], st.astype(BF16))
            st_ref[h] = st * dec[:, ls] + _dot_tn(v[:, ls], k_out[:, ls])
            var = jnp.mean(o * o, axis=-1, keepdims=True)
            gh = g[:, ls]
            y = o * lax.rsqrt(var + NORM_EPS) * nw * (gh * jax.nn.sigmoid(gh))
            o_ref[rows, ls] = y.astype(o_ref.dtype)
        return carry

    lax.fori_loop(0, HG_NCHUNK, chunk, 0, unroll=HG_NCHUNK)


def _hgrn(proj3d, lb, norm_w):
    bsz, s, _ = proj3d.shape
    mcat, level = _hgrn_constants()

    def col(j):
        return pl.BlockSpec((None, HG_ROWS, HG_WIDTH), lambda b, i, j=j: (b, i, j))

    return pl.pallas_call(
        _hgrn_kernel,
        out_shape=jax.ShapeDtypeStruct((bsz, s, HG_WIDTH), BF16),
        grid=(bsz, s // HG_ROWS),
        in_specs=[
            col(0), col(1), col(2), col(3),
            pl.BlockSpec((1, HG_WIDTH), lambda b, i: (0, 0)),
            pl.BlockSpec((1, HG_DK), lambda b, i: (0, 0)),
            pl.BlockSpec((HG_E_ROWS, 2 * HG_CHUNK), lambda b, i: (0, 0)),
            pl.BlockSpec((HG_CHUNK, 2 * HG_CHUNK), lambda b, i: (0, 0)),
        ],
        out_specs=pl.BlockSpec((None, HG_ROWS, HG_WIDTH), lambda b, i: (b, i, 0)),
        scratch_shapes=[pltpu.VMEM((HG_HEADS, HG_DK, HG_DK), F32)],
        compiler_params=pltpu.CompilerParams(
            dimension_semantics=("arbitrary", "arbitrary"), vmem_limit_bytes=VMEM_LIMIT),
        name="hgrn2",
    )(proj3d, proj3d, proj3d, proj3d, lb, norm_w,
      jnp.asarray(mcat, BF16), jnp.asarray(level))


AT_BLK = 256
PREP_SUB = IN_TM // AT_BLK
AUG_PIECES = 3
VT_ROWS = 80


def _prep_constants():
    grp = np.zeros((FX_WIDTH // 2, FX_WIDTH // 2), np.float32)
    for h in range(FX_HEADS // 2):
        grp[h * FX_DH:(h + 1) * FX_DH, h * FX_DH:(h + 1) * FX_DH] = 1.0 / FX_DH
    tri = np.tril(np.ones((AT_BLK, AT_BLK), np.float32))
    ones_at = FX_HEADS * AUG_PIECES
    pk = np.zeros((LANE, FX_WIDTH), np.float32)
    cq = np.zeros((1, LANE), np.float32)
    ck = np.zeros((1, FX_WIDTH), np.float32)
    for p in range(AUG_PIECES):
        cq[0, ones_at + p] = 1.0
        for h in range(FX_HEADS):
            base = (h // 2) * LANE + (FX_DH if h % 2 == 0 else 0)
            ck[0, base + FX_HEADS * p + h] = 1.0
            pk[FX_HEADS * p + h, base + ones_at + p] = -1.0
    return grp, tri, pk, cq, ck


def _inproj_prep_kernel(x_ref, nw_ref, w_ref, wff_ref, fb_ref, qnw_ref, knw_ref,
                        grp_ref, tri_ref, pk_ref, cq_ref, ck_ref,
                        proj_ref, qa_ref, ka_ref, vt_ref, r_ref, carry_ref, *, steps_per_seq):
    @pl.when(pl.program_id(0) % steps_per_seq == 0)
    def _():
        carry_ref[...] = jnp.zeros_like(carry_ref)

    x = x_ref[...]
    ms = jnp.mean(x * x, axis=-1, keepdims=True)
    xn = (x * lax.rsqrt(ms + NORM_EPS) * nw_ref[...]).astype(BF16)

    def proj(c):
        return _dot(xn, w_ref[:, c * IN_TN:(c + 1) * IN_TN])

    hg_groups = list(range(HG_COLS // IN_TN))
    q_all, k_all, v_all = (proj(len(hg_groups) + c) for c in range(3))

    z = _dot(xn, wff_ref[...]) + fb_ref[...]
    lf = (jnp.minimum(z, 0.0) - jnp.log(1.0 + jnp.exp(-jnp.abs(z)))) * LOG2E
    grp = grp_ref[...]
    tri = tri_ref[...]
    lane = lax.broadcasted_iota(jnp.int32, (AT_BLK, LANE), 1)
    low_half = lane < FX_DH
    head_lane = lane < FX_HEADS

    def qk_norm(t, w):
        sq = (t * t).astype(BF16)
        half = FX_WIDTH // 2
        ms = jnp.concatenate([_dot(sq[:, :half], grp), _dot(sq[:, half:], grp)], axis=1)
        return t * lax.rsqrt(ms + NORM_EPS) * w

    for j in range(PREP_SUB):
        rows = slice(j * AT_BLK, (j + 1) * AT_BLK)
        hi, mid, lo = _split3(lf[rows])
        c_rel = _dot(tri, hi) + _dot(tri, mid) + _dot(tri, lo)
        r_ref[j] = carry_ref[...]
        carry_ref[...] = carry_ref[...] + c_rel[AT_BLK - 1:AT_BLK, :]
        packed = None
        for p, piece in enumerate(_split3(c_rel)):
            piece = jnp.where(head_lane, piece.astype(F32), 0.0)
            piece = piece if p == 0 else pltpu.roll(piece, FX_HEADS * p, axis=1)
            packed = piece if packed is None else packed + piece
        aug_q_odd = packed + cq_ref[...]
        aug_q_even = pltpu.roll(aug_q_odd, FX_DH, axis=1)
        aug_k = _dot(packed.astype(BF16), pk_ref[...]) + ck_ref[...]
        qn = qk_norm(q_all[rows], qnw_ref[...]) * (FX_DH ** -0.5 * LOG2E)
        kn = qk_norm(k_all[rows], knw_ref[...])
        for h in range(FX_HEADS):
            pair = slice((h // 2) * LANE, (h // 2 + 1) * LANE)
            head = slice(h * LANE, (h + 1) * LANE)
            data = low_half if h % 2 == 0 else jnp.logical_not(low_half)
            aug_q = aug_q_even if h % 2 == 0 else aug_q_odd
            qa_ref[rows, head] = jnp.where(data, qn[:, pair], aug_q).astype(BF16)
            ka_ref[rows, head] = jnp.where(data, kn[:, pair], aug_k[:, pair]).astype(BF16)
        v_t = v_all[rows].T.astype(BF16)
        for h in range(FX_HEADS):
            vt_ref[j, h * VT_ROWS:h * VT_ROWS + FX_DH, :] = v_t[h * FX_DH:(h + 1) * FX_DH]
            vt_ref[j, h * VT_ROWS + FX_DH:(h + 1) * VT_ROWS, :] = jnp.ones(
                (VT_ROWS - FX_DH, AT_BLK), BF16)
        for c in hg_groups[j::PREP_SUB]:
            proj_ref[:, c * IN_TN:(c + 1) * IN_TN] = proj(c).astype(BF16)


def _inproj_prep(x2d, seq_len, norm_w, w_pad, layer, f_bias, qnw, knw):
    t = x2d.shape[0]
    consts = _prep_constants()
    grp, tri, pk = (jnp.asarray(c, BF16) for c in consts[:3])
    cq, ck = (jnp.asarray(c, F32) for c in consts[3:])
    width = FX_HEADS * LANE

    def full(shape):
        return pl.BlockSpec(shape, lambda i: (0,) * len(shape))

    return pl.pallas_call(
        functools.partial(_inproj_prep_kernel, steps_per_seq=seq_len // IN_TM),
        out_shape=(jax.ShapeDtypeStruct((t, HG_COLS), BF16),
                   jax.ShapeDtypeStruct((t, width), BF16),
                   jax.ShapeDtypeStruct((t, width), BF16),
                   jax.ShapeDtypeStruct((t // AT_BLK, FX_HEADS * VT_ROWS, AT_BLK), BF16),
                   jax.ShapeDtypeStruct((t // AT_BLK, 8, LANE), F32)),
        grid=(t // IN_TM,),
        in_specs=[
            pl.BlockSpec((IN_TM, D_MODEL), lambda i: (i, 0)),
            full((1, D_MODEL)),
            pl.BlockSpec((None, D_MODEL, PROJ_MAIN), lambda i: (layer, 0, 0),
                         pipeline_mode=pl.Buffered(1)),
            pl.BlockSpec((None, D_MODEL, LANE), lambda i: (layer, 0, PROJ_MAIN // LANE)),
            full((1, LANE)), full((1, FX_WIDTH)), full((1, FX_WIDTH)),
            full((FX_WIDTH // 2, FX_WIDTH // 2)), full((AT_BLK, AT_BLK)),
            full((LANE, FX_WIDTH)), full((1, LANE)), full((1, FX_WIDTH)),
        ],
        out_specs=(pl.BlockSpec((IN_TM, HG_COLS), lambda i: (i, 0)),
                   pl.BlockSpec((IN_TM, width), lambda i: (i, 0)),
                   pl.BlockSpec((IN_TM, width), lambda i: (i, 0)),
                   pl.BlockSpec((PREP_SUB, FX_HEADS * VT_ROWS, AT_BLK), lambda i: (i, 0, 0)),
                   pl.BlockSpec((PREP_SUB, 8, LANE), lambda i: (i, 0, 0))),
        scratch_shapes=[pltpu.VMEM((8, LANE), F32)],
        compiler_params=pltpu.CompilerParams(
            dimension_semantics=("arbitrary",), vmem_limit_bytes=VMEM_LIMIT),
        name="inproj_prep",
    )(x2d, norm_w, w_pad, w_pad, f_bias, qnw, knw, grp, tri, pk, cq, ck)


AT_HG = 2
AT_SLOTS = 3
AT_TRIP_TICKS = 4 * AT_SLOTS
F32_EXP2_ZERO = 150.0
SUBLANES = 8


def _tree_reduce(op, final, x):
    while x.shape[0] > SUBLANES:
        half = x.shape[0] // 2
        x = op(x[:half], x[half:])
    return final(x, axis=0, keepdims=True)


def _attn_kernel(order_ref, r_ref, mask_ref, *refs, nblk, n_r):
    q_refs, k_refs, vt_refs = (refs[i * AT_HG:(i + 1) * AT_HG] for i in range(3))
    o_ref, s_ref, p_ref, acc_ref, m_ref, a_ref, start_ref = refs[3 * AT_HG:]
    b = pl.program_id(0)
    hg = pl.program_id(1)
    bases = [(b * FX_HEADS + order_ref[hg * AT_HG + g]) * nblk for g in range(AT_HG)]
    last = nblk - 1

    thr = r_ref[n_r]

    def find_start(qi, n_items):
        def needed(first):
            hit = None
            for base in bases:
                c = r_ref[base + qi] - r_ref[base + first] >= thr
                hit = c if hit is None else jnp.logical_or(hit, c)
            return hit

        first = lax.while_loop(lambda f: jnp.logical_and(f > 0, needed(f)),
                               lambda f: f - 1, qi)
        start_ref[qi] = first
        return n_items + (qi - first + 1)

    n_items = lax.fori_loop(0, nblk, find_start, 0)

    def next_item(item):
        qi, ki, valid = item
        at_diag = ki == qi
        more = jnp.logical_and(valid != 0,
                               jnp.logical_not(jnp.logical_and(at_diag, qi == last)))
        qn = jnp.where(at_diag, jnp.minimum(qi + 1, last), qi)
        kn = jnp.where(at_diag, start_ref[qn], ki + 1)
        return (jnp.where(more, qn, last), jnp.where(more, kn, last), more.astype(jnp.int32))

    def stage_a(slot, item):
        qi, ki, _ = item
        rows_q = pl.ds(pl.multiple_of(qi * AT_BLK, AT_BLK), AT_BLK)
        rows_k = pl.ds(pl.multiple_of(ki * AT_BLK, AT_BLK), AT_BLK)
        mask = mask_ref[(ki == qi).astype(jnp.int32)]
        for g in range(AT_HG):
            s_ref[slot, g] = _dot_nt(k_refs[g][rows_k, :], q_refs[g][rows_q, :]) + mask

    def stage_b(slot, item):
        qi, ki, valid = item
        live = valid != 0
        first = jnp.logical_and(live, ki == start_ref[qi])
        for g in range(AT_HG):
            delta = jnp.where(live, r_ref[bases[g] + qi] - r_ref[bases[g] + ki], NEG_BIG)
            s = s_ref[slot, g]
            m = jnp.where(first, NEG_BIG, m_ref[g])
            m_new = jnp.maximum(m, _tree_reduce(jnp.maximum, jnp.max, s) + delta)
            p_ref[slot, g] = jnp.exp2(s - (m_new - delta)).astype(BF16)
            a_ref[g] = jnp.exp2(m - m_new)
            m_ref[g] = m_new

    def stage_c(slot, item):
        qi, ki, _ = item
        for g in range(AT_HG):
            v_t = vt_refs[g][ki]
            acc = a_ref[g] * acc_ref[g] + _dot(v_t, p_ref[slot, g])
            acc_ref[g] = acc
            o_ref[qi, g * FX_DH:(g + 1) * FX_DH, :] = (
                acc[:FX_DH] / acc[FX_DH:FX_DH + 1]).astype(o_ref.dtype)

    def tick(slot, items):
        item_c, item_b, item_s, item_a = items
        ahead = (slot + 2) % AT_SLOTS
        stage_c(ahead, item_c)
        stage_b(slot, item_b)
        stage_a(ahead, item_a)
        return (item_b, item_s, item_a, next_item(item_a))

    for g in range(AT_HG):
        p_ref[AT_SLOTS - 1, g] = jnp.zeros((AT_BLK, AT_BLK), BF16)
        a_ref[g] = jnp.ones((1, AT_BLK), F32)
        m_ref[g] = jnp.zeros((1, AT_BLK), F32)
        acc_ref[g] = jnp.ones((VT_ROWS, AT_BLK), F32)

    zero = jnp.int32(0)
    item0 = (zero, zero, jnp.int32(1))
    item1 = next_item(item0)
    idle = (zero, zero, zero)
    stage_a(0, item0)
    stage_a(1, item1)

    def trip(_, items):
        for t in range(AT_TRIP_TICKS):
            items = tick(t % AT_SLOTS, items)
        return items

    lax.fori_loop(0, (n_items + AT_TRIP_TICKS) // AT_TRIP_TICKS, trip,
                  (idle, item0, item1, next_item(item1)))


def _attention(order, r_flat, q_aug, k_aug, v_t):
    bsz, s, _ = q_aug.shape
    nblk = s // AT_BLK
    causal = np.where(np.arange(AT_BLK)[:, None] <= np.arange(AT_BLK)[None, :], 0.0, NEG_BIG)
    mask = jnp.asarray(np.stack([np.zeros_like(causal), causal]), F32)

    def head_cols(g):
        return pl.BlockSpec((None, s, LANE), lambda b, h, order: (b, 0, order[h * AT_HG + g]))

    def head_vt(g):
        return pl.BlockSpec((None, nblk, VT_ROWS, AT_BLK),
                            lambda b, h, order: (b, 0, order[h * AT_HG + g], 0))

    grid_spec = pltpu.PrefetchScalarGridSpec(
        num_scalar_prefetch=1,
        grid=(bsz, FX_HEADS // AT_HG),
        in_specs=[
            pl.BlockSpec(memory_space=pltpu.SMEM),
            pl.BlockSpec((2, AT_BLK, AT_BLK), lambda b, h, order: (0, 0, 0)),
            *[head_cols(g) for g in range(AT_HG)],
            *[head_cols(g) for g in range(AT_HG)],
            *[head_vt(g) for g in range(AT_HG)],
        ],
        out_specs=pl.BlockSpec((None, nblk, AT_HG * FX_DH, AT_BLK),
                               lambda b, h, order: (b, 0, h, 0)),
        scratch_shapes=[
            pltpu.VMEM((AT_SLOTS, AT_HG, AT_BLK, AT_BLK), F32),
            pltpu.VMEM((AT_SLOTS, AT_HG, AT_BLK, AT_BLK), BF16),
            pltpu.VMEM((AT_HG, VT_ROWS, AT_BLK), F32),
            pltpu.VMEM((AT_HG, 1, AT_BLK), F32),
            pltpu.VMEM((AT_HG, 1, AT_BLK), F32),
            pltpu.SMEM((nblk,), jnp.int32),
        ],
    )
    return pl.pallas_call(
        functools.partial(_attn_kernel, nblk=nblk, n_r=r_flat.shape[0] - 1),
        out_shape=jax.ShapeDtypeStruct((bsz, nblk, FX_WIDTH, AT_BLK), BF16),
        grid_spec=grid_spec,
        compiler_params=pltpu.CompilerParams(
            dimension_semantics=("arbitrary", "arbitrary"),
            vmem_limit_bytes=VMEM_LIMIT),
        name="fox_attention",
    )(order, r_flat, mask, *([q_aug] * AT_HG), *([k_aug] * AT_HG), *([v_t] * AT_HG))


FF_TM = 512
FF_TF = D_FF
FF_CHUNK = 1024


def _ffn_kernel(x_ref, oa_ref, obt_ref, woa_ref, wob_ref, nw_ref, wu_ref, wd_ref,
                o_ref, h_ref):
    @pl.when(pl.program_id(2) == 0)
    def _():
        o_b = jnp.concatenate([_dot_tn(obt_ref[j], wob_ref[...])
                               for j in range(FF_TM // AT_BLK)], axis=0)
        x1 = x_ref[...] + _dot(oa_ref[...], woa_ref[...]) + o_b
        ms = jnp.mean(x1 * x1, axis=-1, keepdims=True)
        h_ref[...] = (x1 * lax.rsqrt(ms + NORM_EPS) * nw_ref[...]).astype(BF16)
        o_ref[...] = x1

    for c in range(FF_TF // FF_CHUNK):
        cols = slice(c * FF_CHUNK, (c + 1) * FF_CHUNK)
        u = jnp.maximum(_dot(h_ref[...], wu_ref[:, cols]), 0.0)
        o_ref[...] += _dot((u * u).astype(BF16), wd_ref[cols, :])


def _outproj_ffn(x3d, o_a, o_bt, w_out, w_out_fox, norm_w, w_up, w_down, layer):
    bsz, s, _ = x3d.shape
    resident = dict(pipeline_mode=pl.Buffered(1)) if FF_TF == D_FF else {}
    return pl.pallas_call(
        _ffn_kernel,
        out_shape=jax.ShapeDtypeStruct(x3d.shape, F32),
        grid=(bsz, s // FF_TM, D_FF // FF_TF),
        in_specs=[
            pl.BlockSpec((None, FF_TM, D_MODEL), lambda b, i, j: (b, i, 0)),
            pl.BlockSpec((None, FF_TM, HG_WIDTH), lambda b, i, j: (b, i, 0)),
            pl.BlockSpec((None, FF_TM // AT_BLK, FX_WIDTH, AT_BLK), lambda b, i, j: (b, i, 0, 0)),
            pl.BlockSpec((None, HG_WIDTH, D_MODEL), lambda b, i, j: (layer, 0, 0), **resident),
            pl.BlockSpec((FX_WIDTH, D_MODEL), lambda b, i, j: (0, 0), **resident),
            pl.BlockSpec((1, D_MODEL), lambda b, i, j: (0, 0)),
            pl.BlockSpec((None, D_MODEL, FF_TF), lambda b, i, j: (layer, 0, j), **resident),
            pl.BlockSpec((None, FF_TF, D_MODEL), lambda b, i, j: (layer, j, 0), **resident),
        ],
        out_specs=pl.BlockSpec((None, FF_TM, D_MODEL), lambda b, i, j: (b, i, 0)),
        scratch_shapes=[pltpu.VMEM((FF_TM, D_MODEL), BF16)],
        compiler_params=pltpu.CompilerParams(
            dimension_semantics=("arbitrary", "arbitrary", "arbitrary"),
            vmem_limit_bytes=VMEM_LIMIT),
        name="outproj_ffn",
    )(x3d, o_a, o_bt, w_out, w_out_fox, norm_w, w_up, w_down)


def _layer_lower_bounds(lower_bounds):
    p = jax.nn.softmax(lower_bounds.astype(F32), axis=0)
    c = jnp.cumsum(p, axis=0)
    return c - c[0:1]


def kernel(x, lower_bounds, norm1_w, w_in, fox_f_bias, q_norm_w, k_norm_w,
           hgrn_norm_w, w_out, norm2_w, w_up, w_down):
    bsz, s, d = x.shape
    depth = w_in.shape[0]
    nblk = s // AT_BLK
    lbs = _layer_lower_bounds(lower_bounds)
    w_in_pad = jnp.pad(w_in, ((0, 0), (0, 0), (0, PROJ_PAD - w_in.shape[-1]))).astype(BF16)
    w_out_b = w_out.astype(BF16)
    w_up_b = w_up.astype(BF16)
    w_down_b = w_down.astype(BF16)
    f_bias = jnp.pad(fox_f_bias, ((0, 0), (0, LANE - FX_HEADS)))

    for l in range(depth):
        proj, q_aug, k_aug, v_t, r_blk = _inproj_prep(
            x.reshape(bsz * s, d), s, norm1_w[l][None, :], w_in_pad, l, f_bias[l][None, :],
            jnp.tile(q_norm_w[l], FX_HEADS)[None, :],
            jnp.tile(k_norm_w[l], FX_HEADS)[None, :])
        width = FX_HEADS * LANE
        q_aug = q_aug.reshape(bsz, s, width)
        k_aug = k_aug.reshape(bsz, s, width)
        v_t = v_t.reshape(bsz, nblk, FX_HEADS * VT_ROWS, AT_BLK)
        r_blk = r_blk.reshape(bsz, nblk, 8, LANE)
        o_a = _hgrn(proj.reshape(bsz, s, HG_COLS), lbs[l][None, :], hgrn_norm_w[l][None, :])
        qk_bound = (1.02 * FX_DH ** 0.5 * LOG2E * jnp.max(jnp.abs(q_norm_w[l]))
                    * jnp.max(jnp.abs(k_norm_w[l])))
        skip_thr = -(F32_EXP2_ZERO + 2.0 * qk_bound)
        r_flat = jnp.concatenate([
            jnp.transpose(r_blk[:, :, 0, :FX_HEADS], (0, 2, 1)).reshape(-1),
            skip_thr.reshape(1).astype(F32)])
        order = jnp.argsort(fox_f_bias[l]).astype(jnp.int32)
        o_bt = _attention(order, r_flat, q_aug, k_aug, v_t)
        fox_rows = (HG_WIDTH + order[:, None] * FX_DH + jnp.arange(FX_DH)[None, :]).reshape(-1)
        x = _outproj_ffn(x, o_a, o_bt, w_out_b, w_out_b[l][fox_rows], norm2_w[l][None, :],
                         w_up_b, w_down_b, l)
    return x
```

```python
import functools

import numpy as np
import jax
import jax.numpy as jnp
from jax import lax
from jax.experimental import pallas as pl
from jax.experimental.pallas import tpu as pltpu

F32 = jnp.float32
BF16 = jnp.bfloat16

D_MODEL = 1024
D_FF = 4 * D_MODEL
NORM_EPS = 1e-6

HG_WIDTH = 512
HG_HEADS = 4
HG_DK = 128
HG_CHUNK = 64
HG_LEVELS = 6

FX_WIDTH = 512
FX_HEADS = 8
FX_DH = 64

PROJ_MAIN = 4 * HG_WIDTH + 3 * FX_WIDTH
LANE = 128
PROJ_PAD = PROJ_MAIN + LANE

VMEM_LIMIT = 56 * 1024 * 1024

NEG_BIG = -1e30
LOG2E = 1.4426950408889634

NT_DIMS = (((1,), (1,)), ((), ()))
TN_DIMS = (((0,), (0,)), ((), ()))


def _dot(a, b):
    return jnp.dot(a, b, preferred_element_type=F32)


def _dot_nt(a, b):
    return lax.dot_general(a, b, NT_DIMS, preferred_element_type=F32)


def _dot_tn(a, b):
    return lax.dot_general(a, b, TN_DIMS, preferred_element_type=F32)


def _split2(x):
    hi = x.astype(BF16)
    lo = (x - hi.astype(F32)).astype(BF16)
    return hi, lo


def _split3(x):
    hi = x.astype(BF16)
    r = x - hi.astype(F32)
    mid = r.astype(BF16)
    lo = (r - mid.astype(F32)).astype(BF16)
    return hi, mid, lo


IN_TM = 1024
IN_TN = 512
HG_COLS = 4 * HG_WIDTH


HG_ROWS = 1024
HG_NCHUNK = HG_ROWS // HG_CHUNK
HG_E_ROWS = (2 + HG_LEVELS) * HG_CHUNK


def _hgrn_constants():
    c = HG_CHUNK
    tri = np.tril(np.ones((c, c), np.float32))
    rows = [tri, 1.0 - tri]
    idx = np.arange(c)
    for lvl in range(HG_LEVELS):
        m = c >> (lvl + 1)
        ref = (idx // (2 * m)) * (2 * m) + m - 1
        upper = (idx % (2 * m) >= m)[:, None]
        diff = tri - tri[ref]
        rows.append(np.where(upper, diff, -diff))
    mat = np.concatenate(rows, axis=0)
    mcat = np.concatenate([mat, mat], axis=1)
    x = idx[:, None] ^ idx[None, :]
    top = np.floor(np.log2(np.maximum(x, 1))).astype(np.int32)
    level = (HG_LEVELS - 1) - top
    level = np.where(idx[:, None] == idx[None, :], HG_LEVELS, level)
    level = np.where(idx[:, None] < idx[None, :], -1, level)
    level = np.concatenate([level, level], axis=1)
    return mcat.astype(np.float32), level.astype(np.int32)


def _stack_heads(t, heads):
    return jnp.concatenate([t[:, h * HG_DK:(h + 1) * HG_DK] for h in heads], axis=0)


def _hgrn_kernel(q_ref, f_ref, i_ref, g_ref, lb_ref, nw_ref, mcat_ref, lvl_ref,
                 o_ref, st_ref):
    @pl.when(pl.program_id(1) == 0)
    def _():
        st_ref[...] = jnp.zeros_like(st_ref)

    lb = lb_ref[...]
    nw = nw_ref[...]
    mcat = mcat_ref[...]
    level = lvl_ref[...]
    scale = HG_DK ** -0.5

    def chunk(c, carry):
        r0 = pl.multiple_of(c * HG_CHUNK, HG_CHUNK)
        rows = pl.ds(r0, HG_CHUNK)
        fp = f_ref[rows, :].astype(F32)
        f = lb + (1.0 - lb) * jax.nn.sigmoid(fp)
        lf = jnp.log(f) * LOG2E
        kk = 1.0 - f
        hi, lo = _split2(lf)
        e_all = _dot(mcat, jnp.concatenate([hi, lo], axis=0))
        b = e_all[0:HG_CHUNK]
        qs = q_ref[rows, :].astype(F32) * scale
        q_in = (qs * jnp.exp2(b)).astype(BF16)
        k_out = (kk * jnp.exp2(e_all[HG_CHUNK:2 * HG_CHUNK])).astype(BF16)
        dec = jnp.exp2(b[HG_CHUNK - 1:HG_CHUNK, :])
        qb = qs.astype(BF16)
        kb = kk.astype(BF16)
        v = i_ref[rows, :]
        g = g_ref[rows, :].astype(F32)

        a = [jnp.zeros((HG_CHUNK, LANE), F32) for _ in range(HG_HEADS)]
        for lvl in range(HG_LEVELS + 1):
            if lvl < HG_LEVELS:
                x = jnp.exp2(e_all[(2 + lvl) * HG_CHUNK:(3 + lvl) * HG_CHUNK]).astype(BF16)
                ql, kl = qb * x, kb * x
            else:
                ql, kl = qb, kb
            for pair in range(HG_HEADS // 2):
                heads = (2 * pair, 2 * pair + 1)
                d = _dot_nt(_stack_heads(ql, heads), _stack_heads(kl, heads))
                for i, h in enumerate(heads):
                    blk = d[i * HG_CHUNK:(i + 1) * HG_CHUNK, :]
                    a[h] = jnp.where(level == lvl, blk, a[h])

        zeros = jnp.zeros((HG_CHUNK, HG_DK), BF16)
        for h in range(HG_HEADS):
            ls = slice(h * HG_DK, (h + 1) * HG_DK)
            v_pad = jnp.concatenate([v[:, ls], zeros] if h % 2 == 0 else [zeros, v[:, ls]], axis=0)
            st = st_ref[h]
            o = _dot(a[h].astype(BF16), v_pad) + _dot_nt(q_in[:, ls], st.astype(BF16))
            st_ref[h] = st * dec[:, ls] + _dot_tn(v[:, ls], k_out[:, ls])
            var = jnp.mean(o * o, axis=-1, keepdims=True)
            gh = g[:, ls]
            y = o * lax.rsqrt(var + NORM_EPS) * nw * (gh * jax.nn.sigmoid(gh))
            o_ref[rows, ls] = y.astype(o_ref.dtype)
        return carry

    lax.fori_loop(0, HG_NCHUNK, chunk, 0, unroll=HG_NCHUNK)


def _hgrn(proj3d, lb, norm_w):
    bsz, s, _ = proj3d.shape
    mcat, level = _hgrn_constants()

    def col(j):
        return pl.BlockSpec((None, HG_ROWS, HG_WIDTH), lambda b, i, j=j: (b, i, j))

    return pl.pallas_call(
        _hgrn_kernel,
        out_shape=jax.ShapeDtypeStruct((bsz, s, HG_WIDTH), BF16),
        grid=(bsz, s // HG_ROWS),
        in_specs=[
            col(0), col(1), col(2), col(3),
            pl.BlockSpec((1, HG_WIDTH), lambda b, i: (0, 0)),
            pl.BlockSpec((1, HG_DK), lambda b, i: (0, 0)),
            pl.BlockSpec((HG_E_ROWS, 2 * HG_CHUNK), lambda b, i: (0, 0)),
            pl.BlockSpec((HG_CHUNK, 2 * HG_CHUNK), lambda b, i: (0, 0)),
        ],
        out_specs=pl.BlockSpec((None, HG_ROWS, HG_WIDTH), lambda b, i: (b, i, 0)),
        scratch_shapes=[pltpu.VMEM((HG_HEADS, HG_DK, HG_DK), F32)],
        compiler_params=pltpu.CompilerParams(
            dimension_semantics=("arbitrary", "arbitrary"), vmem_limit_bytes=VMEM_LIMIT),
        name="hgrn2",
    )(proj3d, proj3d, proj3d, proj3d, lb, norm_w,
      jnp.asarray(mcat, BF16), jnp.asarray(level))


AT_BLK = 256
PREP_SUB = IN_TM // AT_BLK
AUG_PIECES = 3
VT_ROWS = 80


def _prep_constants():
    grp = np.zeros((FX_WIDTH // 2, FX_WIDTH // 2), np.float32)
    for h in range(FX_HEADS // 2):
        grp[h * FX_DH:(h + 1) * FX_DH, h * FX_DH:(h + 1) * FX_DH] = 1.0 / FX_DH
    tri = np.tril(np.ones((AT_BLK, AT_BLK), np.float32))
    ones_at = FX_HEADS * AUG_PIECES
    pk = np.zeros((LANE, FX_WIDTH), np.float32)
    cq = np.zeros((1, LANE), np.float32)
    ck = np.zeros((1, FX_WIDTH), np.float32)
    for p in range(AUG_PIECES):
        cq[0, ones_at + p] = 1.0
        for h in range(FX_HEADS):
            base = (h // 2) * LANE + (FX_DH if h % 2 == 0 else 0)
            ck[0, base + FX_HEADS * p + h] = 1.0
            pk[FX_HEADS * p + h, base + ones_at + p] = -1.0
    return grp, tri, pk, cq, ck


def _inproj_prep_kernel(x_ref, nw_ref, w_ref, wff_ref, fb_ref, qnw_ref, knw_ref,
                        grp_ref, tri_ref, pk_ref, cq_ref, ck_ref,
                        proj_ref, qa_ref, ka_ref, vt_ref, r_ref, carry_ref, *, steps_per_seq):
    @pl.when(pl.program_id(0) % steps_per_seq == 0)
    def _():
        carry_ref[...] = jnp.zeros_like(carry_ref)

    x = x_ref[...]
    ms = jnp.mean(x * x, axis=-1, keepdims=True)
    xn = (x * lax.rsqrt(ms + NORM_EPS) * nw_ref[...]).astype(BF16)

    def proj(c):
        return _dot(xn, w_ref[:, c * IN_TN:(c + 1) * IN_TN])

    hg_groups = list(range(HG_COLS // IN_TN))
    q_all, k_all, v_all = (proj(len(hg_groups) + c) for c in range(3))

    z = _dot(xn, wff_ref[...]) + fb_ref[...]
    lf = (jnp.minimum(z, 0.0) - jnp.log(1.0 + jnp.exp(-jnp.abs(z)))) * LOG2E
    grp = grp_ref[...]
    tri = tri_ref[...]
    lane = lax.broadcasted_iota(jnp.int32, (AT_BLK, LANE), 1)
    low_half = lane < FX_DH
    head_lane = lane < FX_HEADS

    def qk_norm(t, w):
        sq = (t * t).astype(BF16)
        half = FX_WIDTH // 2
        ms = jnp.concatenate([_dot(sq[:, :half], grp), _dot(sq[:, half:], grp)], axis=1)
        return t * lax.rsqrt(ms + NORM_EPS) * w

    for j in range(PREP_SUB):
        rows = slice(j * AT_BLK, (j + 1) * AT_BLK)
        hi, mid, lo = _split3(lf[rows])
        c_rel = _dot(tri, hi) + _dot(tri, mid) + _dot(tri, lo)
        r_ref[j] = carry_ref[...]
        carry_ref[...] = carry_ref[...] + c_rel[AT_BLK - 1:AT_BLK, :]
        packed = None
        for p, piece in enumerate(_split3(c_rel)):
            piece = jnp.where(head_lane, piece.astype(F32), 0.0)
            piece = piece if p == 0 else pltpu.roll(piece, FX_HEADS * p, axis=1)
            packed = piece if packed is None else packed + piece
        aug_q_odd = packed + cq_ref[...]
        aug_q_even = pltpu.roll(aug_q_odd, FX_DH, axis=1)
        aug_k = _dot(packed.astype(BF16), pk_ref[...]) + ck_ref[...]
        qn = qk_norm(q_all[rows], qnw_ref[...]) * (FX_DH ** -0.5 * LOG2E)
        kn = qk_norm(k_all[rows], knw_ref[...])
        for h in range(FX_HEADS):
            pair = slice((h // 2) * LANE, (h // 2 + 1) * LANE)
            head = slice(h * LANE, (h + 1) * LANE)
            data = low_half if h % 2 == 0 else jnp.logical_not(low_half)
            aug_q = aug_q_even if h % 2 == 0 else aug_q_odd
            qa_ref[rows, head] = jnp.where(data, qn[:, pair], aug_q).astype(BF16)
            ka_ref[rows, head] = jnp.where(data, kn[:, pair], aug_k[:, pair]).astype(BF16)
        v_t = v_all[rows].T.astype(BF16)
        for h in range(FX_HEADS):
            vt_ref[j, h * VT_ROWS:h * VT_ROWS + FX_DH, :] = v_t[h * FX_DH:(h + 1) * FX_DH]
            vt_ref[j, h * VT_ROWS + FX_DH:(h + 1) * VT_ROWS, :] = jnp.ones(
                (VT_ROWS - FX_DH, AT_BLK), BF16)
        for c in hg_groups[j::PREP_SUB]:
            proj_ref[:, c * IN_TN:(c + 1) * IN_TN] = proj(c).astype(BF16)


def _inproj_prep(x2d, seq_len, norm_w, w_pad, layer, f_bias, qnw, knw):
    t = x2d.shape[0]
    consts = _prep_constants()
    grp, tri, pk = (jnp.asarray(c, BF16) for c in consts[:3])
    cq, ck = (jnp.asarray(c, F32) for c in consts[3:])
    width = FX_HEADS * LANE

    def full(shape):
        return pl.BlockSpec(shape, lambda i: (0,) * len(shape))

    return pl.pallas_call(
        functools.partial(_inproj_prep_kernel, steps_per_seq=seq_len // IN_TM),
        out_shape=(jax.ShapeDtypeStruct((t, HG_COLS), BF16),
                   jax.ShapeDtypeStruct((t, width), BF16),
                   jax.ShapeDtypeStruct((t, width), BF16),
                   jax.ShapeDtypeStruct((t // AT_BLK, FX_HEADS * VT_ROWS, AT_BLK), BF16),
                   jax.ShapeDtypeStruct((t // AT_BLK, 8, LANE), F32)),
        grid=(t // IN_TM,),
        in_specs=[
            pl.BlockSpec((IN_TM, D_MODEL), lambda i: (i, 0)),
            full((1, D_MODEL)),
            pl.BlockSpec((None, D_MODEL, PROJ_MAIN), lambda i: (layer, 0, 0),
                         pipeline_mode=pl.Buffered(1)),
            pl.BlockSpec((None, D_MODEL, LANE), lambda i: (layer, 0, PROJ_MAIN // LANE)),
            full((1, LANE)), full((1, FX_WIDTH)), full((1, FX_WIDTH)),
            full((FX_WIDTH // 2, FX_WIDTH // 2)), full((AT_BLK, AT_BLK)),
            full((LANE, FX_WIDTH)), full((1, LANE)), full((1, FX_WIDTH)),
        ],
        out_specs=(pl.BlockSpec((IN_TM, HG_COLS), lambda i: (i, 0)),
                   pl.BlockSpec((IN_TM, width), lambda i: (i, 0)),
                   pl.BlockSpec((IN_TM, width), lambda i: (i, 0)),
                   pl.BlockSpec((PREP_SUB, FX_HEADS * VT_ROWS, AT_BLK), lambda i: (i, 0, 0)),
                   pl.BlockSpec((PREP_SUB, 8, LANE), lambda i: (i, 0, 0))),
        scratch_shapes=[pltpu.VMEM((8, LANE), F32)],
        compiler_params=pltpu.CompilerParams(
            dimension_semantics=("arbitrary",), vmem_limit_bytes=VMEM_LIMIT),
        name="inproj_prep",
    )(x2d, norm_w, w_pad, w_pad, f_bias, qnw, knw, grp, tri, pk, cq, ck)


AT_HG = 2
AT_SLOTS = 3
AT_TRIP_TICKS = 4 * AT_SLOTS
F32_EXP2_ZERO = 150.0
SUBLANES = 8


def _tree_reduce(op, final, x):
    while x.shape[0] > SUBLANES:
        half = x.shape[0] // 2
        x = op(x[:half], x[half:])
    return final(x, axis=0, keepdims=True)


def _attn_kernel(order_ref, r_ref, mask_ref, *refs, nblk, n_r):
    q_refs, k_refs, vt_refs = (refs[i * AT_HG:(i + 1) * AT_HG] for i in range(3))
    o_ref, s_ref, p_ref, acc_ref, m_ref, a_ref, start_ref = refs[3 * AT_HG:]
    b = pl.program_id(0)
    hg = pl.program_id(1)
    bases = [(b * FX_HEADS + order_ref[hg * AT_HG + g]) * nblk for g in range(AT_HG)]
    last = nblk - 1

    thr = r_ref[n_r]

    def find_start(qi, n_items):
        def needed(first):
            hit = None
            for base in bases:
                c = r_ref[base + qi] - r_ref[base + first] >= thr
                hit = c if hit is None else jnp.logical_or(hit, c)
            return hit

        first = lax.while_loop(lambda f: jnp.logical_and(f > 0, needed(f)),
                               lambda f: f - 1, qi)
        start_ref[qi] = first
        return n_items + (qi - first + 1)

    n_items = lax.fori_loop(0, nblk, find_start, 0)

    def next_item(item):
        qi, ki, valid = item
        at_diag = ki == qi
        more = jnp.logical_and(valid != 0,
                               jnp.logical_not(jnp.logical_and(at_diag, qi == last)))
        qn = jnp.where(at_diag, jnp.minimum(qi + 1, last), qi)
        kn = jnp.where(at_diag, start_ref[qn], ki + 1)
        return (jnp.where(more, qn, last), jnp.where(more, kn, last), more.astype(jnp.int32))

    def stage_a(slot, item):
        qi, ki, _ = item
        rows_q = pl.ds(pl.multiple_of(qi * AT_BLK, AT_BLK), AT_BLK)
        rows_k = pl.ds(pl.multiple_of(ki * AT_BLK, AT_BLK), AT_BLK)
        mask = mask_ref[(ki == qi).astype(jnp.int32)]
        for g in range(AT_HG):
            s_ref[slot, g] = _dot_nt(k_refs[g][rows_k, :], q_refs[g][rows_q, :]) + mask

    def stage_b(slot, item):
        qi, ki, valid = item
        live = valid != 0
        first = jnp.logical_and(live, ki == start_ref[qi])
        for g in range(AT_HG):
            delta = jnp.where(live, r_ref[bases[g] + qi] - r_ref[bases[g] + ki], NEG_BIG)
            s = s_ref[slot, g]
            m = jnp.where(first, NEG_BIG, m_ref[g])
            m_new = jnp.maximum(m, _tree_reduce(jnp.maximum, jnp.max, s) + delta)
            p_ref[slot, g] = jnp.exp2(s - (m_new - delta)).astype(BF16)
            a_ref[g] = jnp.exp2(m - m_new)
            m_ref[g] = m_new

    def stage_c(slot, item):
        qi, ki, _ = item
        for g in range(AT_HG):
            v_t = vt_refs[g][ki]
            acc_ref[qi, g] = a_ref[g] * acc_ref[qi, g] + _dot(v_t, p_ref[slot, g])

    def tick(slot, items):
        item_c, item_b, item_s, item_a = items
        ahead = (slot + 2) % AT_SLOTS
        stage_c(ahead, item_c)
        stage_b(slot, item_b)
        stage_a(ahead, item_a)
        return (item_b, item_s, item_a, next_item(item_a))

    for g in range(AT_HG):
        p_ref[AT_SLOTS - 1, g] = jnp.zeros((AT_BLK, AT_BLK), BF16)
        a_ref[g] = jnp.ones((1, AT_BLK), F32)
        m_ref[g] = jnp.zeros((1, AT_BLK), F32)
    acc_ref[...] = jnp.ones(acc_ref.shape, F32)

    zero = jnp.int32(0)
    item0 = (zero, zero, jnp.int32(1))
    item1 = next_item(item0)
    idle = (zero, zero, zero)
    stage_a(0, item0)
    stage_a(1, item1)

    def trip(_, items):
        for t in range(AT_TRIP_TICKS):
            items = tick(t % AT_SLOTS, items)
        return items

    lax.fori_loop(0, (n_items + AT_TRIP_TICKS) // AT_TRIP_TICKS, trip,
                  (idle, item0, item1, next_item(item1)))

    def normalize(qi, carry):
        for g in range(AT_HG):
            acc = acc_ref[qi, g]
            o_ref[qi, g * FX_DH:(g + 1) * FX_DH, :] = (
                acc[:FX_DH] / acc[FX_DH:FX_DH + 1]).astype(o_ref.dtype)
        return carry

    lax.fori_loop(0, nblk, normalize, 0)


def _attention(order, r_flat, q_aug, k_aug, v_t):
    bsz, s, _ = q_aug.shape
    nblk = s // AT_BLK
    causal = np.where(np.arange(AT_BLK)[:, None] <= np.arange(AT_BLK)[None, :], 0.0, NEG_BIG)
    mask = jnp.asarray(np.stack([np.zeros_like(causal), causal]), F32)

    def head_cols(g):
        return pl.BlockSpec((None, s, LANE), lambda b, h, order: (b, 0, order[h * AT_HG + g]))

    def head_vt(g):
        return pl.BlockSpec((None, nblk, VT_ROWS, AT_BLK),
                            lambda b, h, order: (b, 0, order[h * AT_HG + g], 0))

    grid_spec = pltpu.PrefetchScalarGridSpec(
        num_scalar_prefetch=1,
        grid=(bsz, FX_HEADS // AT_HG),
        in_specs=[
            pl.BlockSpec(memory_space=pltpu.SMEM),
            pl.BlockSpec((2, AT_BLK, AT_BLK), lambda b, h, order: (0, 0, 0)),
            *[head_cols(g) for g in range(AT_HG)],
            *[head_cols(g) for g in range(AT_HG)],
            *[head_vt(g) for g in range(AT_HG)],
        ],
        out_specs=pl.BlockSpec((None, nblk, AT_HG * FX_DH, AT_BLK),
                               lambda b, h, order: (b, 0, h, 0)),
        scratch_shapes=[
            pltpu.VMEM((AT_SLOTS, AT_HG, AT_BLK, AT_BLK), F32),
            pltpu.VMEM((AT_SLOTS, AT_HG, AT_BLK, AT_BLK), BF16),
            pltpu.VMEM((nblk, AT_HG, VT_ROWS, AT_BLK), F32),
            pltpu.VMEM((AT_HG, 1, AT_BLK), F32),
            pltpu.VMEM((AT_HG, 1, AT_BLK), F32),
            pltpu.SMEM((nblk,), jnp.int32),
        ],
    )
    return pl.pallas_call(
        functools.partial(_attn_kernel, nblk=nblk, n_r=r_flat.shape[0] - 1),
        out_shape=jax.ShapeDtypeStruct((bsz, nblk, FX_WIDTH, AT_BLK), BF16),
        grid_spec=grid_spec,
        compiler_params=pltpu.CompilerParams(
            dimension_semantics=("arbitrary", "arbitrary"),
            vmem_limit_bytes=VMEM_LIMIT),
        name="fox_attention",
    )(order, r_flat, mask, *([q_aug] * AT_HG), *([k_aug] * AT_HG), *([v_t] * AT_HG))


FF_TM = 512
FF_TF = D_FF
FF_CHUNK = 1024


def _ffn_kernel(x_ref, oa_ref, obt_ref, woa_ref, wob_ref, nw_ref, wu_ref, wd_ref,
                o_ref, h_ref):
    @pl.when(pl.program_id(2) == 0)
    def _():
        o_b = jnp.concatenate([_dot_tn(obt_ref[j], wob_ref[...])
                               for j in range(FF_TM // AT_BLK)], axis=0)
        x1 = x_ref[...] + _dot(oa_ref[...], woa_ref[...]) + o_b
        ms = jnp.mean(x1 * x1, axis=-1, keepdims=True)
        h_ref[...] = (x1 * lax.rsqrt(ms + NORM_EPS) * nw_ref[...]).astype(BF16)
        o_ref[...] = x1

    for c in range(FF_TF // FF_CHUNK):
        cols = slice(c * FF_CHUNK, (c + 1) * FF_CHUNK)
        u = jnp.maximum(_dot(h_ref[...], wu_ref[:, cols]), 0.0)
        o_ref[...] += _dot((u * u).astype(BF16), wd_ref[cols, :])


def _outproj_ffn(x3d, o_a, o_bt, w_out, w_out_fox, norm_w, w_up, w_down, layer):
    bsz, s, _ = x3d.shape
    resident = dict(pipeline_mode=pl.Buffered(1)) if FF_TF == D_FF else {}
    return pl.pallas_call(
        _ffn_kernel,
        out_shape=jax.ShapeDtypeStruct(x3d.shape, F32),
        grid=(bsz, s // FF_TM, D_FF // FF_TF),
        in_specs=[
            pl.BlockSpec((None, FF_TM, D_MODEL), lambda b, i, j: (b, i, 0)),
            pl.BlockSpec((None, FF_TM, HG_WIDTH), lambda b, i, j: (b, i, 0)),
            pl.BlockSpec((None, FF_TM // AT_BLK, FX_WIDTH, AT_BLK), lambda b, i, j: (b, i, 0, 0)),
            pl.BlockSpec((None, HG_WIDTH, D_MODEL), lambda b, i, j: (layer, 0, 0), **resident),
            pl.BlockSpec((FX_WIDTH, D_MODEL), lambda b, i, j: (0, 0), **resident),
            pl.BlockSpec((1, D_MODEL), lambda b, i, j: (0, 0)),
            pl.BlockSpec((None, D_MODEL, FF_TF), lambda b, i, j: (layer, 0, j), **resident),
            pl.BlockSpec((None, FF_TF, D_MODEL), lambda b, i, j: (layer, j, 0), **resident),
        ],
        out_specs=pl.BlockSpec((None, FF_TM, D_MODEL), lambda b, i, j: (b, i, 0)),
        scratch_shapes=[pltpu.VMEM((FF_TM, D_MODEL), BF16)],
        compiler_params=pltpu.CompilerParams(
            dimension_semantics=("arbitrary", "arbitrary", "arbitrary"),
            vmem_limit_bytes=VMEM_LIMIT),
        name="outproj_ffn",
    )(x3d, o_a, o_bt, w_out, w_out_fox, norm_w, w_up, w_down)


def _layer_lower_bounds(lower_bounds):
    p = jax.nn.softmax(lower_bounds.astype(F32), axis=0)
    c = jnp.cumsum(p, axis=0)
    return c - c[0:1]


def kernel(x, lower_bounds, norm1_w, w_in, fox_f_bias, q_norm_w, k_norm_w,
           hgrn_norm_w, w_out, norm2_w, w_up, w_down):
    bsz, s, d = x.shape
    depth = w_in.shape[0]
    nblk = s // AT_BLK
    lbs = _layer_lower_bounds(lower_bounds)
    w_in_pad = jnp.pad(w_in, ((0, 0), (0, 0), (0, PROJ_PAD - w_in.shape[-1]))).astype(BF16)
    w_out_b = w_out.astype(BF16)
    w_up_b = w_up.astype(BF16)
    w_down_b = w_down.astype(BF16)
    f_bias = jnp.pad(fox_f_bias, ((0, 0), (0, LANE - FX_HEADS)))

    for l in range(depth):
        proj, q_aug, k_aug, v_t, r_blk = _inproj_prep(
            x.reshape(bsz * s, d), s, norm1_w[l][None, :], w_in_pad, l, f_bias[l][None, :],
            jnp.tile(q_norm_w[l], FX_HEADS)[None, :],
            jnp.tile(k_norm_w[l], FX_HEADS)[None, :])
        width = FX_HEADS * LANE
        q_aug = q_aug.reshape(bsz, s, width)
        k_aug = k_aug.reshape(bsz, s, width)
        v_t = v_t.reshape(bsz, nblk, FX_HEADS * VT_ROWS, AT_BLK)
        r_blk = r_blk.reshape(bsz, nblk, 8, LANE)
        o_a = _hgrn(proj.reshape(bsz, s, HG_COLS), lbs[l][None, :], hgrn_norm_w[l][None, :])
        qk_bound = (1.02 * FX_DH ** 0.5 * LOG2E * jnp.max(jnp.abs(q_norm_w[l]))
                    * jnp.max(jnp.abs(k_norm_w[l])))
        skip_thr = -(F32_EXP2_ZERO + 2.0 * qk_bound)
        r_flat = jnp.concatenate([
            jnp.transpose(r_blk[:, :, 0, :FX_HEADS], (0, 2, 1)).reshape(-1),
            skip_thr.reshape(1).astype(F32)])
        order = jnp.argsort(fox_f_bias[l]).astype(jnp.int32)
        o_bt = _attention(order, r_flat, q_aug, k_aug, v_t)
        fox_rows = (HG_WIDTH + order[:, None] * FX_DH + jnp.arange(FX_DH)[None, :]).reshape(-1)
        x = _outproj_ffn(x, o_a, o_bt, w_out_b, w_out_b[l][fox_rows], norm2_w[l][None, :],
                         w_up_b, w_down_b, l)
    return x
```

```python
import functools

import numpy as np
import jax
import jax.numpy as jnp
from jax import lax
from jax.experimental import pallas as pl
from jax.experimental.pallas import tpu as pltpu

F32 = jnp.float32
BF16 = jnp.bfloat16

D_MODEL = 1024
D_FF = 4 * D_MODEL
NORM_EPS = 1e-6

HG_WIDTH = 512
HG_HEADS = 4
HG_DK = 128
HG_CHUNK = 64
HG_LEVELS = 6

FX_WIDTH = 512
FX_HEADS = 8
FX_DH = 64

PROJ_MAIN = 4 * HG_WIDTH + 3 * FX_WIDTH
LANE = 128
PROJ_PAD = PROJ_MAIN + LANE

VMEM_LIMIT = 56 * 1024 * 1024

NEG_BIG = -1e30
LOG2E = 1.4426950408889634

NT_DIMS = (((1,), (1,)), ((), ()))
TN_DIMS = (((0,), (0,)), ((), ()))


def _dot(a, b):
    return jnp.dot(a, b, preferred_element_type=F32)


def _dot_nt(a, b):
    return lax.dot_general(a, b, NT_DIMS, preferred_element_type=F32)


def _dot_tn(a, b):
    return lax.dot_general(a, b, TN_DIMS, preferred_element_type=F32)


def _split2(x):
    hi = x.astype(BF16)
    lo = (x - hi.astype(F32)).astype(BF16)
    return hi, lo


def _split3(x):
    hi = x.astype(BF16)
    r = x - hi.astype(F32)
    mid = r.astype(BF16)
    lo = (r - mid.astype(F32)).astype(BF16)
    return hi, mid, lo


IN_TM = 1024
IN_TN = 512
HG_COLS = 4 * HG_WIDTH


HG_ROWS = 1024
HG_NCHUNK = HG_ROWS // HG_CHUNK
HG_E_ROWS = (2 + HG_LEVELS) * HG_CHUNK


def _hgrn_constants():
    c = HG_CHUNK
    tri = np.tril(np.ones((c, c), np.float32))
    rows = [tri, 1.0 - tri]
    idx = np.arange(c)
    for lvl in range(HG_LEVELS):
        m = c >> (lvl + 1)
        ref = (idx // (2 * m)) * (2 * m) + m - 1
        upper = (idx % (2 * m) >= m)[:, None]
        diff = tri - tri[ref]
        rows.append(np.where(upper, diff, -diff))
    mat = np.concatenate(rows, axis=0)
    mcat = np.concatenate([mat, mat], axis=1)
    x = idx[:, None] ^ idx[None, :]
    top = np.floor(np.log2(np.maximum(x, 1))).astype(np.int32)
    level = (HG_LEVELS - 1) - top
    level = np.where(idx[:, None] == idx[None, :], HG_LEVELS, level)
    level = np.where(idx[:, None] < idx[None, :], -1, level)
    level = np.concatenate([level, level], axis=1)
    return mcat.astype(np.float32), level.astype(np.int32)


def _stack_heads(t, heads):
    return jnp.concatenate([t[:, h * HG_DK:(h + 1) * HG_DK] for h in heads], axis=0)


def _hgrn_kernel(q_ref, f_ref, i_ref, g_ref, lb_ref, nw_ref, mcat_ref, lvl_ref,
                 o_ref, st_ref):
    @pl.when(pl.program_id(1) == 0)
    def _():
        st_ref[...] = jnp.zeros_like(st_ref)

    lb = lb_ref[...]
    nw = nw_ref[...]
    mcat = mcat_ref[...]
    level = lvl_ref[...]
    scale = HG_DK ** -0.5

    def chunk(c, carry):
        r0 = pl.multiple_of(c * HG_CHUNK, HG_CHUNK)
        rows = pl.ds(r0, HG_CHUNK)
        fp = f_ref[rows, :].astype(F32)
        f = lb + (1.0 - lb) * jax.nn.sigmoid(fp)
        lf = jnp.log(f) * LOG2E
        kk = 1.0 - f
        hi, lo = _split2(lf)
        e_all = _dot(mcat, jnp.concatenate([hi, lo], axis=0))
        b = e_all[0:HG_CHUNK]
        qs = q_ref[rows, :].astype(F32) * scale
        q_in = (qs * jnp.exp2(b)).astype(BF16)
        k_out = (kk * jnp.exp2(e_all[HG_CHUNK:2 * HG_CHUNK])).astype(BF16)
        dec = jnp.exp2(b[HG_CHUNK - 1:HG_CHUNK, :])
        qb = qs.astype(BF16)
        kb = kk.astype(BF16)
        v = i_ref[rows, :]
        g = g_ref[rows, :].astype(F32)

        a = [jnp.zeros((HG_CHUNK, LANE), F32) for _ in range(HG_HEADS)]
        for lvl in range(HG_LEVELS + 1):
            if lvl < HG_LEVELS:
                x = jnp.exp2(e_all[(2 + lvl) * HG_CHUNK:(3 + lvl) * HG_CHUNK]).astype(BF16)
                ql, kl = qb * x, kb * x
            else:
                ql, kl = qb, kb
            for pair in range(HG_HEADS // 2):
                heads = (2 * pair, 2 * pair + 1)
                d = _dot_nt(_stack_heads(ql, heads), _stack_heads(kl, heads))
                for i, h in enumerate(heads):
                    blk = d[i * HG_CHUNK:(i + 1) * HG_CHUNK, :]
                    a[h] = jnp.where(level == lvl, blk, a[h])

        zeros = jnp.zeros((HG_CHUNK, HG_DK), BF16)
        for h in range(HG_HEADS):
            ls = slice(h * HG_DK, (h + 1) * HG_DK)
            v_pad = jnp.concatenate([v[:, ls], zeros] if h % 2 == 0 else [zeros, v[:, ls]], axis=0)
            st = st_ref[h]
            o = _dot(a[h].astype(BF16), v_pad) + _dot_nt(q_in[:, ls], st.astype(BF16))
            st_ref[h] = st * dec[:, ls] + _dot_tn(v[:, ls], k_out[:, ls])
            var = jnp.mean(o * o, axis=-1, keepdims=True)
            gh = g[:, ls]
            y = o * lax.rsqrt(var + NORM_EPS) * nw * (gh * jax.nn.sigmoid(gh))
            o_ref[rows, ls] = y.astype(o_ref.dtype)
        return carry

    lax.fori_loop(0, HG_NCHUNK, chunk, 0, unroll=HG_NCHUNK)


def _hgrn(proj3d, lb, norm_w):
    bsz, s, _ = proj3d.shape
    mcat, level = _hgrn_constants()

    def col(j):
        return pl.BlockSpec((None, HG_ROWS, HG_WIDTH), lambda b, i, j=j: (b, i, j))

    return pl.pallas_call(
        _hgrn_kernel,
        out_shape=jax.ShapeDtypeStruct((bsz, s, HG_WIDTH), BF16),
        grid=(bsz, s // HG_ROWS),
        in_specs=[
            col(0), col(1), col(2), col(3),
            pl.BlockSpec((1, HG_WIDTH), lambda b, i: (0, 0)),
            pl.BlockSpec((1, HG_DK), lambda b, i: (0, 0)),
            pl.BlockSpec((HG_E_ROWS, 2 * HG_CHUNK), lambda b, i: (0, 0)),
            pl.BlockSpec((HG_CHUNK, 2 * HG_CHUNK), lambda b, i: (0, 0)),
        ],
        out_specs=pl.BlockSpec((None, HG_ROWS, HG_WIDTH), lambda b, i: (b, i, 0)),
        scratch_shapes=[pltpu.VMEM((HG_HEADS, HG_DK, HG_DK), F32)],
        compiler_params=pltpu.CompilerParams(
            dimension_semantics=("arbitrary", "arbitrary"), vmem_limit_bytes=VMEM_LIMIT),
        name="hgrn2",
    )(proj3d, proj3d, proj3d, proj3d, lb, norm_w,
      jnp.asarray(mcat, BF16), jnp.asarray(level))


AT_BLK = 256
PREP_SUB = IN_TM // AT_BLK
AUG_PIECES = 3
VT_ROWS = 80


def _prep_constants():
    grp = np.zeros((FX_WIDTH // 2, FX_WIDTH // 2), np.float32)
    for h in range(FX_HEADS // 2):
        grp[h * FX_DH:(h + 1) * FX_DH, h * FX_DH:(h + 1) * FX_DH] = 1.0 / FX_DH
    tri = np.tril(np.ones((AT_BLK, AT_BLK), np.float32))
    ones_at = FX_HEADS * AUG_PIECES
    pk = np.zeros((LANE, FX_WIDTH), np.float32)
    cq = np.zeros((1, LANE), np.float32)
    ck = np.zeros((1, FX_WIDTH), np.float32)
    for p in range(AUG_PIECES):
        cq[0, ones_at + p] = 1.0
        for h in range(FX_HEADS):
            base = (h // 2) * LANE + (FX_DH if h % 2 == 0 else 0)
            ck[0, base + FX_HEADS * p + h] = 1.0
            pk[FX_HEADS * p + h, base + ones_at + p] = -1.0
    return grp, tri, pk, cq, ck


def _inproj_prep_kernel(x_ref, nw_ref, w_ref, wff_ref, fb_ref, qnw_ref, knw_ref,
                        grp_ref, tri_ref, pk_ref, cq_ref, ck_ref,
                        proj_ref, qa_ref, ka_ref, vt_ref, r_ref, carry_ref, *, steps_per_seq):
    @pl.when(pl.program_id(0) % steps_per_seq == 0)
    def _():
        carry_ref[...] = jnp.zeros_like(carry_ref)

    x = x_ref[...]
    ms = jnp.mean(x * x, axis=-1, keepdims=True)
    xn = (x * lax.rsqrt(ms + NORM_EPS) * nw_ref[...]).astype(BF16)

    def proj(c):
        return _dot(xn, w_ref[:, c * IN_TN:(c + 1) * IN_TN])

    hg_groups = list(range(HG_COLS // IN_TN))
    q_all, k_all, v_all = (proj(len(hg_groups) + c) for c in range(3))

    z = _dot(xn, wff_ref[...]) + fb_ref[...]
    lf = (jnp.minimum(z, 0.0) - jnp.log(1.0 + jnp.exp(-jnp.abs(z)))) * LOG2E
    grp = grp_ref[...]
    tri = tri_ref[...]
    lane = lax.broadcasted_iota(jnp.int32, (AT_BLK, LANE), 1)
    low_half = lane < FX_DH
    head_lane = lane < FX_HEADS

    def qk_norm(t, w):
        sq = (t * t).astype(BF16)
        half = FX_WIDTH // 2
        ms = jnp.concatenate([_dot(sq[:, :half], grp), _dot(sq[:, half:], grp)], axis=1)
        return t * lax.rsqrt(ms + NORM_EPS) * w

    for j in range(PREP_SUB):
        rows = slice(j * AT_BLK, (j + 1) * AT_BLK)
        hi, mid, lo = _split3(lf[rows])
        c_rel = _dot(tri, hi) + _dot(tri, mid) + _dot(tri, lo)
        r_ref[j] = carry_ref[...]
        carry_ref[...] = carry_ref[...] + c_rel[AT_BLK - 1:AT_BLK, :]
        packed = None
        for p, piece in enumerate(_split3(c_rel)):
            piece = jnp.where(head_lane, piece.astype(F32), 0.0)
            piece = piece if p == 0 else pltpu.roll(piece, FX_HEADS * p, axis=1)
            packed = piece if packed is None else packed + piece
        aug_q_odd = packed + cq_ref[...]
        aug_q_even = pltpu.roll(aug_q_odd, FX_DH, axis=1)
        aug_k = _dot(packed.astype(BF16), pk_ref[...]) + ck_ref[...]
        qn = qk_norm(q_all[rows], qnw_ref[...]) * (FX_DH ** -0.5 * LOG2E)
        kn = qk_norm(k_all[rows], knw_ref[...])
        for h in range(FX_HEADS):
            pair = slice((h // 2) * LANE, (h // 2 + 1) * LANE)
            head = slice(h * LANE, (h + 1) * LANE)
            data = low_half if h % 2 == 0 else jnp.logical_not(low_half)
            aug_q = aug_q_even if h % 2 == 0 else aug_q_odd
            qa_ref[rows, head] = jnp.where(data, qn[:, pair], aug_q).astype(BF16)
            ka_ref[rows, head] = jnp.where(data, kn[:, pair], aug_k[:, pair]).astype(BF16)
        v_t = v_all[rows].T.astype(BF16)
        for h in range(FX_HEADS):
            vt_ref[j, h * VT_ROWS:h * VT_ROWS + FX_DH, :] = v_t[h * FX_DH:(h + 1) * FX_DH]
            vt_ref[j, h * VT_ROWS + FX_DH:(h + 1) * VT_ROWS, :] = jnp.ones(
                (VT_ROWS - FX_DH, AT_BLK), BF16)
        for c in hg_groups[j::PREP_SUB]:
            proj_ref[:, c * IN_TN:(c + 1) * IN_TN] = proj(c).astype(BF16)


def _inproj_prep(x2d, seq_len, norm_w, w_pad, layer, f_bias, qnw, knw):
    t = x2d.shape[0]
    consts = _prep_constants()
    grp, tri, pk = (jnp.asarray(c, BF16) for c in consts[:3])
    cq, ck = (jnp.asarray(c, F32) for c in consts[3:])
    width = FX_HEADS * LANE

    def full(shape):
        return pl.BlockSpec(shape, lambda i: (0,) * len(shape))

    return pl.pallas_call(
        functools.partial(_inproj_prep_kernel, steps_per_seq=seq_len // IN_TM),
        out_shape=(jax.ShapeDtypeStruct((t, HG_COLS), BF16),
                   jax.ShapeDtypeStruct((t, width), BF16),
                   jax.ShapeDtypeStruct((t, width), BF16),
                   jax.ShapeDtypeStruct((t // AT_BLK, FX_HEADS * VT_ROWS, AT_BLK), BF16),
                   jax.ShapeDtypeStruct((t // AT_BLK, 8, LANE), F32)),
        grid=(t // IN_TM,),
        in_specs=[
            pl.BlockSpec((IN_TM, D_MODEL), lambda i: (i, 0)),
            full((1, D_MODEL)),
            pl.BlockSpec((None, D_MODEL, PROJ_MAIN), lambda i: (layer, 0, 0),
                         pipeline_mode=pl.Buffered(1)),
            pl.BlockSpec((None, D_MODEL, LANE), lambda i: (layer, 0, PROJ_MAIN // LANE)),
            full((1, LANE)), full((1, FX_WIDTH)), full((1, FX_WIDTH)),
            full((FX_WIDTH // 2, FX_WIDTH // 2)), full((AT_BLK, AT_BLK)),
            full((LANE, FX_WIDTH)), full((1, LANE)), full((1, FX_WIDTH)),
        ],
        out_specs=(pl.BlockSpec((IN_TM, HG_COLS), lambda i: (i, 0)),
                   pl.BlockSpec((IN_TM, width), lambda i: (i, 0)),
                   pl.BlockSpec((IN_TM, width), lambda i: (i, 0)),
                   pl.BlockSpec((PREP_SUB, FX_HEADS * VT_ROWS, AT_BLK), lambda i: (i, 0, 0)),
                   pl.BlockSpec((PREP_SUB, 8, LANE), lambda i: (i, 0, 0))),
        scratch_shapes=[pltpu.VMEM((8, LANE), F32)],
        compiler_params=pltpu.CompilerParams(
            dimension_semantics=("arbitrary",), vmem_limit_bytes=VMEM_LIMIT),
        name="inproj_prep",
    )(x2d, norm_w, w_pad, w_pad, f_bias, qnw, knw, grp, tri, pk, cq, ck)


AT_HG = 2
AT_SLOTS = 3
AT_TRIP_TICKS = 4 * AT_SLOTS
F32_EXP2_ZERO = 150.0
SUBLANES = 8


def _tree_reduce(op, final, x):
    while x.shape[0] > SUBLANES:
        half = x.shape[0] // 2
        x = op(x[:half], x[half:])
    return final(x, axis=0, keepdims=True)


def _attn_kernel(order_ref, r_ref, mask_ref, *refs, nblk, n_r):
    q_refs, k_refs, vt_refs = (refs[i * AT_HG:(i + 1) * AT_HG] for i in range(3))
    o_ref, s_ref, p_ref, acc_ref, m_ref, a_ref, start_ref = refs[3 * AT_HG:]
    b = pl.program_id(0)
    hg = pl.program_id(1)
    bases = [(b * FX_HEADS + order_ref[hg * AT_HG + g]) * nblk for g in range(AT_HG)]
    last = nblk - 1

    thr = r_ref[n_r]

    def find_start(qi, n_items):
        def needed(first):
            hit = None
            for base in bases:
                c = r_ref[base + qi] - r_ref[base + first] >= thr
                hit = c if hit is None else jnp.logical_or(hit, c)
            return hit

        first = lax.while_loop(lambda f: jnp.logical_and(f > 0, needed(f)),
                               lambda f: f - 1, qi)
        start_ref[qi] = first
        return n_items + (qi - first + 1)

    n_items = lax.fori_loop(0, nblk, find_start, 0)

    def next_item(item):
        qi, ki, valid = item
        at_diag = ki == qi
        more = jnp.logical_and(valid != 0,
                               jnp.logical_not(jnp.logical_and(at_diag, qi == last)))
        qn = jnp.where(at_diag, jnp.minimum(qi + 1, last), qi)
        kn = jnp.where(at_diag, start_ref[qn], ki + 1)
        return (jnp.where(more, qn, last), jnp.where(more, kn, last), more.astype(jnp.int32))

    def stage_a(slot, item):
        qi, ki, _ = item
        rows_q = pl.ds(pl.multiple_of(qi * AT_BLK, AT_BLK), AT_BLK)
        rows_k = pl.ds(pl.multiple_of(ki * AT_BLK, AT_BLK), AT_BLK)
        mask = mask_ref[(ki == qi).astype(jnp.int32)]
        for g in range(AT_HG):
            s_ref[slot, g] = _dot_nt(k_refs[g][rows_k, :], q_refs[g][rows_q, :]) + mask

    def stage_b(slot, item):
        qi, ki, valid = item
        live = valid != 0
        first = jnp.logical_and(live, ki == start_ref[qi])
        for g in range(AT_HG):
            delta = jnp.where(live, r_ref[bases[g] + qi] - r_ref[bases[g] + ki], NEG_BIG)
            s = s_ref[slot, g]
            m = jnp.where(first, NEG_BIG, m_ref[g])
            m_new = jnp.maximum(m, _tree_reduce(jnp.maximum, jnp.max, s) + delta)
            p_ref[slot, g] = jnp.exp2(s - (m_new - delta)).astype(BF16)
            a_ref[g] = jnp.exp2(m - m_new)
            m_ref[g] = m_new

    def stage_c(slot, item):
        qi, ki, _ = item
        for g in range(AT_HG):
            v_t = vt_refs[g][ki]
            acc_ref[qi, g] = a_ref[g] * acc_ref[qi, g] + _dot(v_t, p_ref[slot, g])

    def tick(slot, items):
        item_c, item_b, item_s, item_a = items
        ahead = (slot + 2) % AT_SLOTS
        stage_c(ahead, item_c)
        stage_b(slot, item_b)
        stage_a(ahead, item_a)
        return (item_b, item_s, item_a, next_item(item_a))

    for g in range(AT_HG):
        p_ref[AT_SLOTS - 1, g] = jnp.zeros((AT_BLK, AT_BLK), BF16)
        a_ref[g] = jnp.ones((1, AT_BLK), F32)
        m_ref[g] = jnp.zeros((1, AT_BLK), F32)
    acc_ref[...] = jnp.ones(acc_ref.shape, F32)

    zero = jnp.int32(0)
    item0 = (zero, zero, jnp.int32(1))
    item1 = next_item(item0)
    idle = (zero, zero, zero)
    stage_a(0, item0)
    stage_a(1, item1)

    def trip(_, items):
        for t in range(AT_TRIP_TICKS):
            items = tick(t % AT_SLOTS, items)
        return items

    lax.fori_loop(0, (n_items + AT_TRIP_TICKS) // AT_TRIP_TICKS, trip,
                  (idle, item0, item1, next_item(item1)))

    def normalize(qi, carry):
        for g in range(AT_HG):
            acc = acc_ref[qi, g]
            o_ref[qi, g * FX_DH:(g + 1) * FX_DH, :] = (
                acc[:FX_DH] / acc[FX_DH:FX_DH + 1]).astype(o_ref.dtype)
        return carry

    lax.fori_loop(0, nblk, normalize, 0)


def _attention(order, r_flat, q_aug, k_aug, v_t):
    bsz, s, _ = q_aug.shape
    nblk = s // AT_BLK
    causal = np.where(np.arange(AT_BLK)[:, None] <= np.arange(AT_BLK)[None, :], 0.0, NEG_BIG)
    mask = jnp.asarray(np.stack([np.zeros_like(causal), causal]), F32)

    def head_cols(g):
        return pl.BlockSpec((None, s, LANE), lambda b, h, order: (b, 0, order[h * AT_HG + g]))

    def head_vt(g):
        return pl.BlockSpec((None, nblk, VT_ROWS, AT_BLK),
                            lambda b, h, order: (b, 0, order[h * AT_HG + g], 0))

    grid_spec = pltpu.PrefetchScalarGridSpec(
        num_scalar_prefetch=1,
        grid=(bsz, FX_HEADS // AT_HG),
        in_specs=[
            pl.BlockSpec(memory_space=pltpu.SMEM),
            pl.BlockSpec((2, AT_BLK, AT_BLK), lambda b, h, order: (0, 0, 0)),
            *[head_cols(g) for g in range(AT_HG)],
            *[head_cols(g) for g in range(AT_HG)],
            *[head_vt(g) for g in range(AT_HG)],
        ],
        out_specs=pl.BlockSpec((None, nblk, AT_HG * FX_DH, AT_BLK),
                               lambda b, h, order: (b, 0, h, 0)),
        scratch_shapes=[
            pltpu.VMEM((AT_SLOTS, AT_HG, AT_BLK, AT_BLK), F32),
            pltpu.VMEM((AT_SLOTS, AT_HG, AT_BLK, AT_BLK), BF16),
            pltpu.VMEM((nblk, AT_HG, VT_ROWS, AT_BLK), F32),
            pltpu.VMEM((AT_HG, 1, AT_BLK), F32),
            pltpu.VMEM((AT_HG, 1, AT_BLK), F32),
            pltpu.SMEM((nblk,), jnp.int32),
        ],
    )
    return pl.pallas_call(
        functools.partial(_attn_kernel, nblk=nblk, n_r=r_flat.shape[0] - 1),
        out_shape=jax.ShapeDtypeStruct((bsz, nblk, FX_WIDTH, AT_BLK), BF16),
        grid_spec=grid_spec,
        compiler_params=pltpu.CompilerParams(
            dimension_semantics=("arbitrary", "arbitrary"),
            vmem_limit_bytes=VMEM_LIMIT),
        name="fox_attention",
    )(order, r_flat, mask, *([q_aug] * AT_HG), *([k_aug] * AT_HG), *([v_t] * AT_HG))


FF_TM = 1024
FF_TF = D_FF
FF_CHUNK = 1024


def _ffn_kernel(x_ref, oa_ref, obt_ref, woa_ref, wob_ref, nw_ref, wu_ref, wd_ref,
                o_ref, h_ref):
    @pl.when(pl.program_id(2) == 0)
    def _():
        o_b = jnp.concatenate([_dot_tn(obt_ref[j], wob_ref[...])
                               for j in range(FF_TM // AT_BLK)], axis=0)
        x1 = x_ref[...] + _dot(oa_ref[...], woa_ref[...]) + o_b
        ms = jnp.mean(x1 * x1, axis=-1, keepdims=True)
        h_ref[...] = (x1 * lax.rsqrt(ms + NORM_EPS) * nw_ref[...]).astype(BF16)
        o_ref[...] = x1

    for c in range(FF_TF // FF_CHUNK):
        cols = slice(c * FF_CHUNK, (c + 1) * FF_CHUNK)
        u = jnp.maximum(_dot(h_ref[...], wu_ref[:, cols]), 0.0)
        o_ref[...] += _dot((u * u).astype(BF16), wd_ref[cols, :])


def _outproj_ffn(x3d, o_a, o_bt, w_out, w_out_fox, norm_w, w_up, w_down, layer):
    bsz, s, _ = x3d.shape
    resident = dict(pipeline_mode=pl.Buffered(1)) if FF_TF == D_FF else {}
    return pl.pallas_call(
        _ffn_kernel,
        out_shape=jax.ShapeDtypeStruct(x3d.shape, F32),
        grid=(bsz, s // FF_TM, D_FF // FF_TF),
        in_specs=[
            pl.BlockSpec((None, FF_TM, D_MODEL), lambda b, i, j: (b, i, 0)),
            pl.BlockSpec((None, FF_TM, HG_WIDTH), lambda b, i, j: (b, i, 0)),
            pl.BlockSpec((None, FF_TM // AT_BLK, FX_WIDTH, AT_BLK), lambda b, i, j: (b, i, 0, 0)),
            pl.BlockSpec((None, HG_WIDTH, D_MODEL), lambda b, i, j: (layer, 0, 0), **resident),
            pl.BlockSpec((FX_WIDTH, D_MODEL), lambda b, i, j: (0, 0), **resident),
            pl.BlockSpec((1, D_MODEL), lambda b, i, j: (0, 0)),
            pl.BlockSpec((None, D_MODEL, FF_TF), lambda b, i, j: (layer, 0, j), **resident),
            pl.BlockSpec((None, FF_TF, D_MODEL), lambda b, i, j: (layer, j, 0), **resident),
        ],
        out_specs=pl.BlockSpec((None, FF_TM, D_MODEL), lambda b, i, j: (b, i, 0)),
        scratch_shapes=[pltpu.VMEM((FF_TM, D_MODEL), BF16)],
        compiler_params=pltpu.CompilerParams(
            dimension_semantics=("arbitrary", "arbitrary", "arbitrary"),
            vmem_limit_bytes=VMEM_LIMIT),
        name="outproj_ffn",
    )(x3d, o_a, o_bt, w_out, w_out_fox, norm_w, w_up, w_down)


def _layer_lower_bounds(lower_bounds):
    p = jax.nn.softmax(lower_bounds.astype(F32), axis=0)
    c = jnp.cumsum(p, axis=0)
    return c - c[0:1]


def kernel(x, lower_bounds, norm1_w, w_in, fox_f_bias, q_norm_w, k_norm_w,
           hgrn_norm_w, w_out, norm2_w, w_up, w_down):
    bsz, s, d = x.shape
    depth = w_in.shape[0]
    nblk = s // AT_BLK
    lbs = _layer_lower_bounds(lower_bounds)
    w_in_pad = jnp.pad(w_in, ((0, 0), (0, 0), (0, PROJ_PAD - w_in.shape[-1]))).astype(BF16)
    w_out_b = w_out.astype(BF16)
    w_up_b = w_up.astype(BF16)
    w_down_b = w_down.astype(BF16)
    f_bias = jnp.pad(fox_f_bias, ((0, 0), (0, LANE - FX_HEADS)))

    for l in range(depth):
        proj, q_aug, k_aug, v_t, r_blk = _inproj_prep(
            x.reshape(bsz * s, d), s, norm1_w[l][None, :], w_in_pad, l, f_bias[l][None, :],
            jnp.tile(q_norm_w[l], FX_HEADS)[None, :],
            jnp.tile(k_norm_w[l], FX_HEADS)[None, :])
        width = FX_HEADS * LANE
        q_aug = q_aug.reshape(bsz, s, width)
        k_aug = k_aug.reshape(bsz, s, width)
        v_t = v_t.reshape(bsz, nblk, FX_HEADS * VT_ROWS, AT_BLK)
        r_blk = r_blk.reshape(bsz, nblk, 8, LANE)
        o_a = _hgrn(proj.reshape(bsz, s, HG_COLS), lbs[l][None, :], hgrn_norm_w[l][None, :])
        qk_bound = (1.02 * FX_DH ** 0.5 * LOG2E * jnp.max(jnp.abs(q_norm_w[l]))
                    * jnp.max(jnp.abs(k_norm_w[l])))
        skip_thr = -(F32_EXP2_ZERO + 2.0 * qk_bound)
        r_flat = jnp.concatenate([
            jnp.transpose(r_blk[:, :, 0, :FX_HEADS], (0, 2, 1)).reshape(-1),
            skip_thr.reshape(1).astype(F32)])
        order = jnp.argsort(fox_f_bias[l]).astype(jnp.int32)
        o_bt = _attention(order, r_flat, q_aug, k_aug, v_t)
        fox_rows = (HG_WIDTH + order[:, None] * FX_DH + jnp.arange(FX_DH)[None, :]).reshape(-1)
        x = _outproj_ffn(x, o_a, o_bt, w_out_b, w_out_b[l][fox_rows], norm2_w[l][None, :],
                         w_up_b, w_down_b, l)
    return x
```

```python
import functools

import numpy as np
import jax
import jax.numpy as jnp
from jax import lax
from jax.experimental import pallas as pl
from jax.experimental.pallas import tpu as pltpu

F32 = jnp.float32
BF16 = jnp.bfloat16

D_MODEL = 1024
D_FF = 4 * D_MODEL
NORM_EPS = 1e-6

HG_WIDTH = 512
HG_HEADS = 4
HG_DK = 128
HG_CHUNK = 64
HG_LEVELS = 6

FX_WIDTH = 512
FX_HEADS = 8
FX_DH = 64

PROJ_MAIN = 4 * HG_WIDTH + 3 * FX_WIDTH
LANE = 128
PROJ_PAD = PROJ_MAIN + LANE

VMEM_LIMIT = 56 * 1024 * 1024

NEG_BIG = -1e30
LOG2E = 1.4426950408889634

NT_DIMS = (((1,), (1,)), ((), ()))
TN_DIMS = (((0,), (0,)), ((), ()))


def _dot(a, b):
    return jnp.dot(a, b, preferred_element_type=F32)


def _dot_nt(a, b):
    return lax.dot_general(a, b, NT_DIMS, preferred_element_type=F32)


def _dot_tn(a, b):
    return lax.dot_general(a, b, TN_DIMS, preferred_element_type=F32)


def _split2(x):
    hi = x.astype(BF16)
    lo = (x - hi.astype(F32)).astype(BF16)
    return hi, lo


def _split3(x):
    hi = x.astype(BF16)
    r = x - hi.astype(F32)
    mid = r.astype(BF16)
    lo = (r - mid.astype(F32)).astype(BF16)
    return hi, mid, lo


IN_TM = 1024
IN_TN = 512
HG_COLS = 4 * HG_WIDTH


HG_ROWS = 1024
HG_NCHUNK = HG_ROWS // HG_CHUNK
HG_E_ROWS = (2 + HG_LEVELS) * HG_CHUNK


def _hgrn_constants():
    c = HG_CHUNK
    tri = np.tril(np.ones((c, c), np.float32))
    rows = [tri, 1.0 - tri]
    idx = np.arange(c)
    for lvl in range(HG_LEVELS):
        m = c >> (lvl + 1)
        ref = (idx // (2 * m)) * (2 * m) + m - 1
        upper = (idx % (2 * m) >= m)[:, None]
        diff = tri - tri[ref]
        rows.append(np.where(upper, diff, -diff))
    mat = np.concatenate(rows, axis=0)
    mcat = np.concatenate([mat, mat], axis=1)
    x = idx[:, None] ^ idx[None, :]
    top = np.floor(np.log2(np.maximum(x, 1))).astype(np.int32)
    level = (HG_LEVELS - 1) - top
    level = np.where(idx[:, None] == idx[None, :], HG_LEVELS, level)
    level = np.where(idx[:, None] < idx[None, :], -1, level)
    level = np.concatenate([level, level], axis=1)
    return mcat.astype(np.float32), level.astype(np.int32)


def _stack_heads(t, heads):
    return jnp.concatenate([t[:, h * HG_DK:(h + 1) * HG_DK] for h in heads], axis=0)


def _hgrn_kernel(q_ref, f_ref, i_ref, g_ref, lb_ref, nw_ref, mcat_ref, lvl_ref,
                 o_ref, st_ref):
    @pl.when(pl.program_id(1) == 0)
    def _():
        st_ref[...] = jnp.zeros_like(st_ref)

    lb = lb_ref[...]
    nw = nw_ref[...]
    mcat = mcat_ref[...]
    level = lvl_ref[...]
    scale = HG_DK ** -0.5

    def chunk(c, carry):
        r0 = pl.multiple_of(c * HG_CHUNK, HG_CHUNK)
        rows = pl.ds(r0, HG_CHUNK)
        fp = f_ref[rows, :].astype(F32)
        f = lb + (1.0 - lb) * jax.nn.sigmoid(fp)
        lf = jnp.log(f) * LOG2E
        kk = 1.0 - f
        hi, lo = _split2(lf)
        e_all = _dot(mcat, jnp.concatenate([hi, lo], axis=0))
        b = e_all[0:HG_CHUNK]
        qs = q_ref[rows, :].astype(F32) * scale
        q_in = (qs * jnp.exp2(b)).astype(BF16)
        k_out = (kk * jnp.exp2(e_all[HG_CHUNK:2 * HG_CHUNK])).astype(BF16)
        dec = jnp.exp2(b[HG_CHUNK - 1:HG_CHUNK, :])
        qb = qs.astype(BF16)
        kb = kk.astype(BF16)
        v = i_ref[rows, :]
        g = g_ref[rows, :].astype(F32)

        a = [jnp.zeros((HG_CHUNK, LANE), F32) for _ in range(HG_HEADS)]
        for lvl in range(HG_LEVELS + 1):
            if lvl < HG_LEVELS:
                x = jnp.exp2(e_all[(2 + lvl) * HG_CHUNK:(3 + lvl) * HG_CHUNK]).astype(BF16)
                ql, kl = qb * x, kb * x
            else:
                ql, kl = qb, kb
            for pair in range(HG_HEADS // 2):
                heads = (2 * pair, 2 * pair + 1)
                d = _dot_nt(_stack_heads(ql, heads), _stack_heads(kl, heads))
                for i, h in enumerate(heads):
                    blk = d[i * HG_CHUNK:(i + 1) * HG_CHUNK, :]
                    a[h] = jnp.where(level == lvl, blk, a[h])

        zeros = jnp.zeros((HG_CHUNK, HG_DK), BF16)
        for h in range(HG_HEADS):
            ls = slice(h * HG_DK, (h + 1) * HG_DK)
            v_pad = jnp.concatenate([v[:, ls], zeros] if h % 2 == 0 else [zeros, v[:, ls]], axis=0)
            st = st_ref[h]
            o = _dot(a[h].astype(BF16), v_pad) + _dot_nt(q_in[:, ls], st.astype(BF16))
            st_ref[h] = st * dec[:, ls] + _dot_tn(v[:, ls], k_out[:, ls])
            var = jnp.mean(o * o, axis=-1, keepdims=True)
            gh = g[:, ls]
            y = o * lax.rsqrt(var + NORM_EPS) * nw * (gh * jax.nn.sigmoid(gh))
            o_ref[rows, ls] = y.astype(o_ref.dtype)
        return carry

    lax.fori_loop(0, HG_NCHUNK, chunk, 0, unroll=HG_NCHUNK)


def _hgrn(proj3d, lb, norm_w):
    bsz, s, _ = proj3d.shape
    mcat, level = _hgrn_constants()

    def col(j):
        return pl.BlockSpec((None, HG_ROWS, HG_WIDTH), lambda b, i, j=j: (b, i, j))

    return pl.pallas_call(
        _hgrn_kernel,
        out_shape=jax.ShapeDtypeStruct((bsz, s, HG_WIDTH), BF16),
        grid=(bsz, s // HG_ROWS),
        in_specs=[
            col(0), col(1), col(2), col(3),
            pl.BlockSpec((1, HG_WIDTH), lambda b, i: (0, 0)),
            pl.BlockSpec((1, HG_DK), lambda b, i: (0, 0)),
            pl.BlockSpec((HG_E_ROWS, 2 * HG_CHUNK), lambda b, i: (0, 0)),
            pl.BlockSpec((HG_CHUNK, 2 * HG_CHUNK), lambda b, i: (0, 0)),
        ],
        out_specs=pl.BlockSpec((None, HG_ROWS, HG_WIDTH), lambda b, i: (b, i, 0)),
        scratch_shapes=[pltpu.VMEM((HG_HEADS, HG_DK, HG_DK), F32)],
        compiler_params=pltpu.CompilerParams(
            dimension_semantics=("arbitrary", "arbitrary"), vmem_limit_bytes=VMEM_LIMIT),
        name="hgrn2",
    )(proj3d, proj3d, proj3d, proj3d, lb, norm_w,
      jnp.asarray(mcat, BF16), jnp.asarray(level))


AT_BLK = 256
PREP_SUB = IN_TM // AT_BLK
AUG_PIECES = 3
VT_ROWS = 80


def _prep_constants():
    grp = np.zeros((FX_WIDTH // 2, FX_WIDTH // 2), np.float32)
    for h in range(FX_HEADS // 2):
        grp[h * FX_DH:(h + 1) * FX_DH, h * FX_DH:(h + 1) * FX_DH] = 1.0 / FX_DH
    tri = np.tril(np.ones((AT_BLK, AT_BLK), np.float32))
    ones_at = FX_HEADS * AUG_PIECES
    pk = np.zeros((LANE, FX_WIDTH), np.float32)
    cq = np.zeros((1, LANE), np.float32)
    ck = np.zeros((1, FX_WIDTH), np.float32)
    for p in range(AUG_PIECES):
        cq[0, ones_at + p] = 1.0
        for h in range(FX_HEADS):
            base = (h // 2) * LANE + (FX_DH if h % 2 == 0 else 0)
            ck[0, base + FX_HEADS * p + h] = 1.0
            pk[FX_HEADS * p + h, base + ones_at + p] = -1.0
    return grp, tri, pk, cq, ck


def _inproj_prep_kernel(x_ref, nw_ref, w_ref, wff_ref, fb_ref, qnw_ref, knw_ref,
                        grp_ref, tri_ref, pk_ref, cq_ref, ck_ref,
                        proj_ref, qa_ref, ka_ref, vt_ref, r_ref, carry_ref, *, steps_per_seq):
    @pl.when(pl.program_id(0) % steps_per_seq == 0)
    def _():
        carry_ref[...] = jnp.zeros_like(carry_ref)

    x = x_ref[...]
    ms = jnp.mean(x * x, axis=-1, keepdims=True)
    xn = (x * lax.rsqrt(ms + NORM_EPS) * nw_ref[...]).astype(BF16)

    def proj(c):
        return _dot(xn, w_ref[:, c * IN_TN:(c + 1) * IN_TN])

    hg_groups = list(range(HG_COLS // IN_TN))
    q_all, k_all, v_all = (proj(len(hg_groups) + c) for c in range(3))

    z = _dot(xn, wff_ref[...]) + fb_ref[...]
    lf = (jnp.minimum(z, 0.0) - jnp.log(1.0 + jnp.exp(-jnp.abs(z)))) * LOG2E
    grp = grp_ref[...]
    tri = tri_ref[...]
    lane = lax.broadcasted_iota(jnp.int32, (AT_BLK, LANE), 1)
    low_half = lane < FX_DH
    head_lane = lane < FX_HEADS

    def qk_norm(t, w):
        sq = (t * t).astype(BF16)
        half = FX_WIDTH // 2
        ms = jnp.concatenate([_dot(sq[:, :half], grp), _dot(sq[:, half:], grp)], axis=1)
        return t * lax.rsqrt(ms + NORM_EPS) * w

    for j in range(PREP_SUB):
        rows = slice(j * AT_BLK, (j + 1) * AT_BLK)
        hi, mid, lo = _split3(lf[rows])
        c_rel = _dot(tri, hi) + _dot(tri, mid) + _dot(tri, lo)
        r_ref[j] = carry_ref[...]
        carry_ref[...] = carry_ref[...] + c_rel[AT_BLK - 1:AT_BLK, :]
        packed = None
        for p, piece in enumerate(_split3(c_rel)):
            piece = jnp.where(head_lane, piece.astype(F32), 0.0)
            piece = piece if p == 0 else pltpu.roll(piece, FX_HEADS * p, axis=1)
            packed = piece if packed is None else packed + piece
        aug_q_odd = packed + cq_ref[...]
        aug_q_even = pltpu.roll(aug_q_odd, FX_DH, axis=1)
        aug_k = _dot(packed.astype(BF16), pk_ref[...]) + ck_ref[...]
        qn = qk_norm(q_all[rows], qnw_ref[...]) * (FX_DH ** -0.5 * LOG2E)
        kn = qk_norm(k_all[rows], knw_ref[...])
        for h in range(FX_HEADS):
            pair = slice((h // 2) * LANE, (h // 2 + 1) * LANE)
            head = slice(h * LANE, (h + 1) * LANE)
            data = low_half if h % 2 == 0 else jnp.logical_not(low_half)
            aug_q = aug_q_even if h % 2 == 0 else aug_q_odd
            qa_ref[rows, head] = jnp.where(data, qn[:, pair], aug_q).astype(BF16)
            ka_ref[rows, head] = jnp.where(data, kn[:, pair], aug_k[:, pair]).astype(BF16)
        v_t = v_all[rows].T.astype(BF16)
        for h in range(FX_HEADS):
            vt_ref[j, h * VT_ROWS:h * VT_ROWS + FX_DH, :] = v_t[h * FX_DH:(h + 1) * FX_DH]
            vt_ref[j, h * VT_ROWS + FX_DH:(h + 1) * VT_ROWS, :] = jnp.ones(
                (VT_ROWS - FX_DH, AT_BLK), BF16)
        for c in hg_groups[j::PREP_SUB]:
            proj_ref[:, c * IN_TN:(c + 1) * IN_TN] = proj(c).astype(BF16)


def _inproj_prep(x2d, seq_len, norm_w, w_pad, layer, f_bias, qnw, knw):
    t = x2d.shape[0]
    consts = _prep_constants()
    grp, tri, pk = (jnp.asarray(c, BF16) for c in consts[:3])
    cq, ck = (jnp.asarray(c, F32) for c in consts[3:])
    width = FX_HEADS * LANE

    def full(shape):
        return pl.BlockSpec(shape, lambda i: (0,) * len(shape))

    return pl.pallas_call(
        functools.partial(_inproj_prep_kernel, steps_per_seq=seq_len // IN_TM),
        out_shape=(jax.ShapeDtypeStruct((t, HG_COLS), BF16),
                   jax.ShapeDtypeStruct((t, width), BF16),
                   jax.ShapeDtypeStruct((t, width), BF16),
                   jax.ShapeDtypeStruct((t // AT_BLK, FX_HEADS * VT_ROWS, AT_BLK), BF16),
                   jax.ShapeDtypeStruct((t // AT_BLK, 8, LANE), F32)),
        grid=(t // IN_TM,),
        in_specs=[
            pl.BlockSpec((IN_TM, D_MODEL), lambda i: (i, 0)),
            full((1, D_MODEL)),
            pl.BlockSpec((None, D_MODEL, PROJ_MAIN), lambda i: (layer, 0, 0),
                         pipeline_mode=pl.Buffered(1)),
            pl.BlockSpec((None, D_MODEL, LANE), lambda i: (layer, 0, PROJ_MAIN // LANE)),
            full((1, LANE)), full((1, FX_WIDTH)), full((1, FX_WIDTH)),
            full((FX_WIDTH // 2, FX_WIDTH // 2)), full((AT_BLK, AT_BLK)),
            full((LANE, FX_WIDTH)), full((1, LANE)), full((1, FX_WIDTH)),
        ],
        out_specs=(pl.BlockSpec((IN_TM, HG_COLS), lambda i: (i, 0)),
                   pl.BlockSpec((IN_TM, width), lambda i: (i, 0)),
                   pl.BlockSpec((IN_TM, width), lambda i: (i, 0)),
                   pl.BlockSpec((PREP_SUB, FX_HEADS * VT_ROWS, AT_BLK), lambda i: (i, 0, 0)),
                   pl.BlockSpec((PREP_SUB, 8, LANE), lambda i: (i, 0, 0))),
        scratch_shapes=[pltpu.VMEM((8, LANE), F32)],
        compiler_params=pltpu.CompilerParams(
            dimension_semantics=("arbitrary",), vmem_limit_bytes=VMEM_LIMIT),
        name="inproj_prep",
    )(x2d, norm_w, w_pad, w_pad, f_bias, qnw, knw, grp, tri, pk, cq, ck)


AT_HG = 2
AT_SLOTS = 3
AT_TRIP_TICKS = 4 * AT_SLOTS
F32_EXP2_ZERO = 150.0
FIXED_STABILISER_MAX = 32.0
SUBLANES = 8


def _tree_reduce(op, final, x):
    while x.shape[0] > SUBLANES:
        half = x.shape[0] // 2
        x = op(x[:half], x[half:])
    return final(x, axis=0, keepdims=True)


def _attn_kernel(order_ref, r_ref, mask_ref, *refs, nblk, n_r):
    q_refs, k_refs, vt_refs = (refs[i * AT_HG:(i + 1) * AT_HG] for i in range(3))
    o_ref, s_ref, p_ref, acc_ref, m_ref, a_ref, start_ref = refs[3 * AT_HG:]
    b = pl.program_id(0)
    hg = pl.program_id(1)
    bases = [(b * FX_HEADS + order_ref[hg * AT_HG + g]) * nblk for g in range(AT_HG)]
    last = nblk - 1

    thr = r_ref[n_r]

    def find_start(qi, n_items):
        def needed(first):
            hit = None
            for base in bases:
                c = r_ref[base + qi] - r_ref[base + first] >= thr
                hit = c if hit is None else jnp.logical_or(hit, c)
            return hit

        first = lax.while_loop(lambda f: jnp.logical_and(f > 0, needed(f)),
                               lambda f: f - 1, qi)
        start_ref[qi] = first
        return n_items + (qi - first + 1)

    n_items = lax.fori_loop(0, nblk, find_start, 0)

    def next_item(item):
        qi, ki, valid = item
        at_diag = ki == qi
        more = jnp.logical_and(valid != 0,
                               jnp.logical_not(jnp.logical_and(at_diag, qi == last)))
        qn = jnp.where(at_diag, jnp.minimum(qi + 1, last), qi)
        kn = jnp.where(at_diag, start_ref[qn], ki + 1)
        return (jnp.where(more, qn, last), jnp.where(more, kn, last), more.astype(jnp.int32))

    def stage_a(slot, item):
        qi, ki, _ = item
        rows_q = pl.ds(pl.multiple_of(qi * AT_BLK, AT_BLK), AT_BLK)
        rows_k = pl.ds(pl.multiple_of(ki * AT_BLK, AT_BLK), AT_BLK)
        mask = mask_ref[(ki == qi).astype(jnp.int32)]
        for g in range(AT_HG):
            s_ref[slot, g] = _dot_nt(k_refs[g][rows_k, :], q_refs[g][rows_q, :]) + mask

    qk_bound = r_ref[n_r + 1]

    def stage_b(slot, item, fixed):
        qi, ki, valid = item
        live = valid != 0
        first = jnp.logical_and(live, ki == start_ref[qi])
        for g in range(AT_HG):
            delta = jnp.where(live, r_ref[bases[g] + qi] - r_ref[bases[g] + ki], NEG_BIG)
            s = s_ref[slot, g]
            if fixed:
                p_ref[slot, g] = jnp.exp2(s + (delta - qk_bound)).astype(BF16)
            else:
                m = jnp.where(first, NEG_BIG, m_ref[g])
                m_new = jnp.maximum(m, _tree_reduce(jnp.maximum, jnp.max, s) + delta)
                p_ref[slot, g] = jnp.exp2(s - (m_new - delta)).astype(BF16)
                a_ref[g] = jnp.exp2(m - m_new)
                m_ref[g] = m_new

    def stage_c(slot, item, fixed):
        qi, ki, _ = item
        for g in range(AT_HG):
            pv = _dot(vt_refs[g][ki], p_ref[slot, g])
            acc_ref[qi, g] = acc_ref[qi, g] + pv if fixed else a_ref[g] * acc_ref[qi, g] + pv

    def tick(slot, items, fixed):
        item_c, item_b, item_s, item_a = items
        ahead = (slot + 2) % AT_SLOTS
        stage_c(ahead, item_c, fixed)
        stage_b(slot, item_b, fixed)
        stage_a(ahead, item_a)
        return (item_b, item_s, item_a, next_item(item_a))

    zero = jnp.int32(0)
    item0 = (zero, zero, jnp.int32(1))
    item1 = next_item(item0)
    idle = (zero, zero, zero)
    stage_a(0, item0)
    stage_a(1, item1)

    def sweep(fixed):
        for g in range(AT_HG):
            p_ref[AT_SLOTS - 1, g] = jnp.zeros((AT_BLK, AT_BLK), BF16)
            a_ref[g] = jnp.ones((1, AT_BLK), F32)
            m_ref[g] = jnp.zeros((1, AT_BLK), F32)
        acc_ref[...] = jnp.zeros(acc_ref.shape, F32) if fixed else jnp.ones(acc_ref.shape, F32)

        def trip(_, items):
            for t in range(AT_TRIP_TICKS):
                items = tick(t % AT_SLOTS, items, fixed)
            return items

        lax.fori_loop(0, (n_items + AT_TRIP_TICKS) // AT_TRIP_TICKS, trip,
                      (idle, item0, item1, next_item(item1)))

    use_fixed = order_ref[FX_HEADS] != 0

    @pl.when(use_fixed)
    def _():
        sweep(True)

    @pl.when(jnp.logical_not(use_fixed))
    def _():
        sweep(False)

    def normalize(qi, carry):
        for g in range(AT_HG):
            acc = acc_ref[qi, g]
            o_ref[qi, g * FX_DH:(g + 1) * FX_DH, :] = (
                acc[:FX_DH] / acc[FX_DH:FX_DH + 1]).astype(o_ref.dtype)
        return carry

    lax.fori_loop(0, nblk, normalize, 0)


def _attention(order, r_flat, q_aug, k_aug, v_t):
    bsz, s, _ = q_aug.shape
    nblk = s // AT_BLK
    causal = np.where(np.arange(AT_BLK)[:, None] <= np.arange(AT_BLK)[None, :], 0.0, NEG_BIG)
    mask = jnp.asarray(np.stack([np.zeros_like(causal), causal]), F32)

    def head_cols(g):
        return pl.BlockSpec((None, s, LANE), lambda b, h, order: (b, 0, order[h * AT_HG + g]))

    def head_vt(g):
        return pl.BlockSpec((None, nblk, VT_ROWS, AT_BLK),
                            lambda b, h, order: (b, 0, order[h * AT_HG + g], 0))

    grid_spec = pltpu.PrefetchScalarGridSpec(
        num_scalar_prefetch=1,
        grid=(bsz, FX_HEADS // AT_HG),
        in_specs=[
            pl.BlockSpec(memory_space=pltpu.SMEM),
            pl.BlockSpec((2, AT_BLK, AT_BLK), lambda b, h, order: (0, 0, 0)),
            *[head_cols(g) for g in range(AT_HG)],
            *[head_cols(g) for g in range(AT_HG)],
            *[head_vt(g) for g in range(AT_HG)],
        ],
        out_specs=pl.BlockSpec((None, nblk, AT_HG * FX_DH, AT_BLK),
                               lambda b, h, order: (b, 0, h, 0)),
        scratch_shapes=[
            pltpu.VMEM((AT_SLOTS, AT_HG, AT_BLK, AT_BLK), F32),
            pltpu.VMEM((AT_SLOTS, AT_HG, AT_BLK, AT_BLK), BF16),
            pltpu.VMEM((nblk, AT_HG, VT_ROWS, AT_BLK), F32),
            pltpu.VMEM((AT_HG, 1, AT_BLK), F32),
            pltpu.VMEM((AT_HG, 1, AT_BLK), F32),
            pltpu.SMEM((nblk,), jnp.int32),
        ],
    )
    return pl.pallas_call(
        functools.partial(_attn_kernel, nblk=nblk, n_r=r_flat.shape[0] - 2),
        out_shape=jax.ShapeDtypeStruct((bsz, nblk, FX_WIDTH, AT_BLK), BF16),
        grid_spec=grid_spec,
        compiler_params=pltpu.CompilerParams(
            dimension_semantics=("arbitrary", "arbitrary"),
            vmem_limit_bytes=VMEM_LIMIT),
        name="fox_attention",
    )(order, r_flat, mask, *([q_aug] * AT_HG), *([k_aug] * AT_HG), *([v_t] * AT_HG))


FF_TM = 1024
FF_TF = D_FF
FF_CHUNK = 1024


def _ffn_kernel(x_ref, oa_ref, obt_ref, woa_ref, wob_ref, nw_ref, wu_ref, wd_ref,
                o_ref, h_ref):
    @pl.when(pl.program_id(2) == 0)
    def _():
        o_b = jnp.concatenate([_dot_tn(obt_ref[j], wob_ref[...])
                               for j in range(FF_TM // AT_BLK)], axis=0)
        x1 = x_ref[...] + _dot(oa_ref[...], woa_ref[...]) + o_b
        ms = jnp.mean(x1 * x1, axis=-1, keepdims=True)
        h_ref[...] = (x1 * lax.rsqrt(ms + NORM_EPS) * nw_ref[...]).astype(BF16)
        o_ref[...] = x1

    for c in range(FF_TF // FF_CHUNK):
        cols = slice(c * FF_CHUNK, (c + 1) * FF_CHUNK)
        u = jnp.maximum(_dot(h_ref[...], wu_ref[:, cols]), 0.0)
        o_ref[...] += _dot((u * u).astype(BF16), wd_ref[cols, :])


def _outproj_ffn(x3d, o_a, o_bt, w_out, w_out_fox, norm_w, w_up, w_down, layer):
    bsz, s, _ = x3d.shape
    resident = dict(pipeline_mode=pl.Buffered(1)) if FF_TF == D_FF else {}
    return pl.pallas_call(
        _ffn_kernel,
        out_shape=jax.ShapeDtypeStruct(x3d.shape, F32),
        grid=(bsz, s // FF_TM, D_FF // FF_TF),
        in_specs=[
            pl.BlockSpec((None, FF_TM, D_MODEL), lambda b, i, j: (b, i, 0)),
            pl.BlockSpec((None, FF_TM, HG_WIDTH), lambda b, i, j: (b, i, 0)),
            pl.BlockSpec((None, FF_TM // AT_BLK, FX_WIDTH, AT_BLK), lambda b, i, j: (b, i, 0, 0)),
            pl.BlockSpec((None, HG_WIDTH, D_MODEL), lambda b, i, j: (layer, 0, 0), **resident),
            pl.BlockSpec((FX_WIDTH, D_MODEL), lambda b, i, j: (0, 0), **resident),
            pl.BlockSpec((1, D_MODEL), lambda b, i, j: (0, 0)),
            pl.BlockSpec((None, D_MODEL, FF_TF), lambda b, i, j: (layer, 0, j), **resident),
            pl.BlockSpec((None, FF_TF, D_MODEL), lambda b, i, j: (layer, j, 0), **resident),
        ],
        out_specs=pl.BlockSpec((None, FF_TM, D_MODEL), lambda b, i, j: (b, i, 0)),
        scratch_shapes=[pltpu.VMEM((FF_TM, D_MODEL), BF16)],
        compiler_params=pltpu.CompilerParams(
            dimension_semantics=("arbitrary", "arbitrary", "arbitrary"),
            vmem_limit_bytes=VMEM_LIMIT),
        name="outproj_ffn",
    )(x3d, o_a, o_bt, w_out, w_out_fox, norm_w, w_up, w_down)


def _layer_lower_bounds(lower_bounds):
    p = jax.nn.softmax(lower_bounds.astype(F32), axis=0)
    c = jnp.cumsum(p, axis=0)
    return c - c[0:1]


def kernel(x, lower_bounds, norm1_w, w_in, fox_f_bias, q_norm_w, k_norm_w,
           hgrn_norm_w, w_out, norm2_w, w_up, w_down):
    bsz, s, d = x.shape
    depth = w_in.shape[0]
    nblk = s // AT_BLK
    lbs = _layer_lower_bounds(lower_bounds)
    w_in_pad = jnp.pad(w_in, ((0, 0), (0, 0), (0, PROJ_PAD - w_in.shape[-1]))).astype(BF16)
    w_out_b = w_out.astype(BF16)
    w_up_b = w_up.astype(BF16)
    w_down_b = w_down.astype(BF16)
    f_bias = jnp.pad(fox_f_bias, ((0, 0), (0, LANE - FX_HEADS)))

    for l in range(depth):
        proj, q_aug, k_aug, v_t, r_blk = _inproj_prep(
            x.reshape(bsz * s, d), s, norm1_w[l][None, :], w_in_pad, l, f_bias[l][None, :],
            jnp.tile(q_norm_w[l], FX_HEADS)[None, :],
            jnp.tile(k_norm_w[l], FX_HEADS)[None, :])
        width = FX_HEADS * LANE
        q_aug = q_aug.reshape(bsz, s, width)
        k_aug = k_aug.reshape(bsz, s, width)
        v_t = v_t.reshape(bsz, nblk, FX_HEADS * VT_ROWS, AT_BLK)
        r_blk = r_blk.reshape(bsz, nblk, 8, LANE)
        o_a = _hgrn(proj.reshape(bsz, s, HG_COLS), lbs[l][None, :], hgrn_norm_w[l][None, :])
        qk_bound = (1.02 * FX_DH ** 0.5 * LOG2E * jnp.max(jnp.abs(q_norm_w[l]))
                    * jnp.max(jnp.abs(k_norm_w[l])))
        skip_thr = -(F32_EXP2_ZERO + 2.0 * qk_bound)
        r_flat = jnp.concatenate([
            jnp.transpose(r_blk[:, :, 0, :FX_HEADS], (0, 2, 1)).reshape(-1),
            skip_thr.reshape(1).astype(F32), qk_bound.reshape(1).astype(F32)])
        order = jnp.argsort(fox_f_bias[l]).astype(jnp.int32)
        use_fixed = (qk_bound <= FIXED_STABILISER_MAX).astype(jnp.int32).reshape(1)
        o_bt = _attention(jnp.concatenate([order, use_fixed]), r_flat, q_aug, k_aug, v_t)
        fox_rows = (HG_WIDTH + order[:, None] * FX_DH + jnp.arange(FX_DH)[None, :]).reshape(-1)
        x = _outproj_ffn(x, o_a, o_bt, w_out_b, w_out_b[l][fox_rows], norm2_w[l][None, :],
                         w_up_b, w_down_b, l)
    return x
```

```python
import functools

import numpy as np
import jax
import jax.numpy as jnp
from jax import lax
from jax.experimental import pallas as pl
from jax.experimental.pallas import tpu as pltpu

F32 = jnp.float32
BF16 = jnp.bfloat16

D_MODEL = 1024
D_FF = 4 * D_MODEL
NORM_EPS = 1e-6

HG_WIDTH = 512
HG_HEADS = 4
HG_DK = 128
HG_CHUNK = 64
HG_LEVELS = 6

FX_WIDTH = 512
FX_HEADS = 8
FX_DH = 64

PROJ_MAIN = 4 * HG_WIDTH + 3 * FX_WIDTH
LANE = 128
PROJ_PAD = PROJ_MAIN + LANE

VMEM_LIMIT = 56 * 1024 * 1024

NEG_BIG = -1e30
LOG2E = 1.4426950408889634

NT_DIMS = (((1,), (1,)), ((), ()))
TN_DIMS = (((0,), (0,)), ((), ()))


def _dot(a, b):
    return jnp.dot(a, b, preferred_element_type=F32)


def _dot_nt(a, b):
    return lax.dot_general(a, b, NT_DIMS, preferred_element_type=F32)


def _dot_tn(a, b):
    return lax.dot_general(a, b, TN_DIMS, preferred_element_type=F32)


def _split2(x):
    hi = x.astype(BF16)
    lo = (x - hi.astype(F32)).astype(BF16)
    return hi, lo


def _split3(x):
    hi = x.astype(BF16)
    r = x - hi.astype(F32)
    mid = r.astype(BF16)
    lo = (r - mid.astype(F32)).astype(BF16)
    return hi, mid, lo


IN_TM = 1024
IN_TN = 512
HG_COLS = 4 * HG_WIDTH


HG_ROWS = 1024
HG_NCHUNK = HG_ROWS // HG_CHUNK
HG_E_ROWS = (2 + HG_LEVELS) * HG_CHUNK


def _hgrn_constants():
    c = HG_CHUNK
    tri = np.tril(np.ones((c, c), np.float32))
    rows = [tri, 1.0 - tri]
    idx = np.arange(c)
    for lvl in range(HG_LEVELS):
        m = c >> (lvl + 1)
        ref = (idx // (2 * m)) * (2 * m) + m - 1
        upper = (idx % (2 * m) >= m)[:, None]
        diff = tri - tri[ref]
        rows.append(np.where(upper, diff, -diff))
    mat = np.concatenate(rows, axis=0)
    mcat = np.concatenate([mat, mat], axis=1)
    x = idx[:, None] ^ idx[None, :]
    top = np.floor(np.log2(np.maximum(x, 1))).astype(np.int32)
    level = (HG_LEVELS - 1) - top
    level = np.where(idx[:, None] == idx[None, :], HG_LEVELS, level)
    level = np.where(idx[:, None] < idx[None, :], -1, level)
    level = np.concatenate([level, level], axis=1)
    return mcat.astype(np.float32), level.astype(np.int32)


def _stack_heads(t, heads):
    return jnp.concatenate([t[:, h * HG_DK:(h + 1) * HG_DK] for h in heads], axis=0)


def _hgrn_kernel(q_ref, f_ref, i_ref, g_ref, lb_ref, nw_ref, mcat_ref, lvl_ref,
                 o_ref, st_ref):
    @pl.when(pl.program_id(1) == 0)
    def _():
        st_ref[...] = jnp.zeros_like(st_ref)

    lb = lb_ref[...]
    nw = nw_ref[...]
    mcat = mcat_ref[...]
    level = lvl_ref[...]
    scale = HG_DK ** -0.5

    def chunk(c, carry):
        r0 = pl.multiple_of(c * HG_CHUNK, HG_CHUNK)
        rows = pl.ds(r0, HG_CHUNK)
        fp = f_ref[rows, :].astype(F32)
        f = lb + (1.0 - lb) * jax.nn.sigmoid(fp)
        lf = jnp.log(f) * LOG2E
        kk = 1.0 - f
        hi, lo = _split2(lf)
        e_all = _dot(mcat, jnp.concatenate([hi, lo], axis=0))
        b = e_all[0:HG_CHUNK]
        qs = q_ref[rows, :].astype(F32) * scale
        q_in = (qs * jnp.exp2(b)).astype(BF16)
        k_out = (kk * jnp.exp2(e_all[HG_CHUNK:2 * HG_CHUNK])).astype(BF16)
        dec = jnp.exp2(b[HG_CHUNK - 1:HG_CHUNK, :])
        qb = qs.astype(BF16)
        kb = kk.astype(BF16)
        v = i_ref[rows, :]
        g = g_ref[rows, :].astype(F32)

        a = [jnp.zeros((HG_CHUNK, LANE), F32) for _ in range(HG_HEADS)]
        for lvl in range(HG_LEVELS + 1):
            if lvl < HG_LEVELS:
                x = jnp.exp2(e_all[(2 + lvl) * HG_CHUNK:(3 + lvl) * HG_CHUNK]).astype(BF16)
                ql, kl = qb * x, kb * x
            else:
                ql, kl = qb, kb
            for pair in range(HG_HEADS // 2):
                heads = (2 * pair, 2 * pair + 1)
                d = _dot_nt(_stack_heads(ql, heads), _stack_heads(kl, heads))
                for i, h in enumerate(heads):
                    blk = d[i * HG_CHUNK:(i + 1) * HG_CHUNK, :]
                    a[h] = jnp.where(level == lvl, blk, a[h])

        zeros = jnp.zeros((HG_CHUNK, HG_DK), BF16)
        for h in range(HG_HEADS):
            ls = slice(h * HG_DK, (h + 1) * HG_DK)
            v_pad = jnp.concatenate([v[:, ls], zeros] if h % 2 == 0 else [zeros, v[:, ls]], axis=0)
            st = st_ref[h]
            o = _dot(a[h].astype(BF16), v_pad) + _dot_nt(q_in[:, ls], st.astype(BF16))
            st_ref[h] = st * dec[:, ls] + _dot_tn(v[:, ls], k_out[:, ls])
            var = jnp.mean(o * o, axis=-1, keepdims=True)
            gh = g[:, ls]
            y = o * lax.rsqrt(var + NORM_EPS) * nw * (gh * jax.nn.sigmoid(gh))
            o_ref[rows, ls] = y.astype(o_ref.dtype)
        return carry

    lax.fori_loop(0, HG_NCHUNK, chunk, 0, unroll=HG_NCHUNK)


def _hgrn(proj3d, lb, norm_w):
    bsz, s, _ = proj3d.shape
    mcat, level = _hgrn_constants()

    def col(j):
        return pl.BlockSpec((None, HG_ROWS, HG_WIDTH), lambda b, i, j=j: (b, i, j))

    return pl.pallas_call(
        _hgrn_kernel,
        out_shape=jax.ShapeDtypeStruct((bsz, s, HG_WIDTH), BF16),
        grid=(bsz, s // HG_ROWS),
        in_specs=[
            col(0), col(1), col(2), col(3),
            pl.BlockSpec((1, HG_WIDTH), lambda b, i: (0, 0)),
            pl.BlockSpec((1, HG_DK), lambda b, i: (0, 0)),
            pl.BlockSpec((HG_E_ROWS, 2 * HG_CHUNK), lambda b, i: (0, 0)),
            pl.BlockSpec((HG_CHUNK, 2 * HG_CHUNK), lambda b, i: (0, 0)),
        ],
        out_specs=pl.BlockSpec((None, HG_ROWS, HG_WIDTH), lambda b, i: (b, i, 0)),
        scratch_shapes=[pltpu.VMEM((HG_HEADS, HG_DK, HG_DK), F32)],
        compiler_params=pltpu.CompilerParams(
            dimension_semantics=("arbitrary", "arbitrary"), vmem_limit_bytes=VMEM_LIMIT),
        name="hgrn2",
    )(proj3d, proj3d, proj3d, proj3d, lb, norm_w,
      jnp.asarray(mcat, BF16), jnp.asarray(level))


AT_BLK = 256
PREP_SUB = IN_TM // AT_BLK
AUG_PIECES = 3
VT_ROWS = 80


def _prep_constants():
    grp = np.zeros((FX_WIDTH // 2, FX_WIDTH // 2), np.float32)
    for h in range(FX_HEADS // 2):
        grp[h * FX_DH:(h + 1) * FX_DH, h * FX_DH:(h + 1) * FX_DH] = 1.0 / FX_DH
    tri = np.tril(np.ones((AT_BLK, AT_BLK), np.float32))
    ones_at = FX_HEADS * AUG_PIECES
    pk = np.zeros((LANE, FX_WIDTH), np.float32)
    cq = np.zeros((1, LANE), np.float32)
    ck = np.zeros((1, FX_WIDTH), np.float32)
    for p in range(AUG_PIECES):
        cq[0, ones_at + p] = 1.0
        for h in range(FX_HEADS):
            base = (h // 2) * LANE + (FX_DH if h % 2 == 0 else 0)
            ck[0, base + FX_HEADS * p + h] = 1.0
            pk[FX_HEADS * p + h, base + ones_at + p] = -1.0
    return grp, tri, pk, cq, ck


def _inproj_prep_kernel(x_ref, nw_ref, w_ref, wff_ref, fb_ref, qnw_ref, knw_ref,
                        grp_ref, tri_ref, pk_ref, cq_ref, ck_ref,
                        proj_ref, qa_ref, ka_ref, vt_ref, r_ref, carry_ref, *, steps_per_seq):
    @pl.when(pl.program_id(0) % steps_per_seq == 0)
    def _():
        carry_ref[...] = jnp.zeros_like(carry_ref)

    x = x_ref[...]
    ms = jnp.mean(x * x, axis=-1, keepdims=True)
    xn = (x * lax.rsqrt(ms + NORM_EPS) * nw_ref[...]).astype(BF16)

    def proj(c):
        return _dot(xn, w_ref[:, c * IN_TN:(c + 1) * IN_TN])

    hg_groups = list(range(HG_COLS // IN_TN))
    q_all, k_all, v_all = (proj(len(hg_groups) + c) for c in range(3))

    z = _dot(xn, wff_ref[...]) + fb_ref[...]
    lf = (jnp.minimum(z, 0.0) - jnp.log(1.0 + jnp.exp(-jnp.abs(z)))) * LOG2E
    grp = grp_ref[...]
    tri = tri_ref[...]
    lane = lax.broadcasted_iota(jnp.int32, (AT_BLK, LANE), 1)
    low_half = lane < FX_DH
    head_lane = lane < FX_HEADS

    def qk_norm(t, w):
        sq = (t * t).astype(BF16)
        half = FX_WIDTH // 2
        ms = jnp.concatenate([_dot(sq[:, :half], grp), _dot(sq[:, half:], grp)], axis=1)
        return t * lax.rsqrt(ms + NORM_EPS) * w

    def pack_pieces(t):
        packed = None
        for p, piece in enumerate(_split3(t)):
            piece = jnp.where(head_lane, piece.astype(F32), 0.0)
            piece = piece if p == 0 else pltpu.roll(piece, FX_HEADS * p, axis=1)
            packed = piece if packed is None else packed + piece
        return packed

    for j in range(PREP_SUB):
        rows = slice(j * AT_BLK, (j + 1) * AT_BLK)
        sums = _dot(tri, jnp.concatenate(_split3(lf[rows]), axis=1))
        c_rel = sums[:, :LANE] + sums[:, LANE:2 * LANE] + sums[:, 2 * LANE:]
        r_ref[j] = carry_ref[...]
        carry_ref[...] = carry_ref[...] + c_rel[AT_BLK - 1:AT_BLK, :]
        packed = pack_pieces(c_rel)
        aug_q_odd = packed + cq_ref[...]
        aug_q_even = pltpu.roll(aug_q_odd, FX_DH, axis=1)
        aug_k = _dot(packed.astype(BF16), pk_ref[...]) + ck_ref[...]
        qn = qk_norm(q_all[rows], qnw_ref[...])
        kn = qk_norm(k_all[rows], knw_ref[...])
        for h in range(FX_HEADS):
            pair = slice((h // 2) * LANE, (h // 2 + 1) * LANE)
            head = slice(h * LANE, (h + 1) * LANE)
            data = low_half if h % 2 == 0 else jnp.logical_not(low_half)
            aug_q = aug_q_even if h % 2 == 0 else aug_q_odd
            qa_ref[rows, head] = jnp.where(data, qn[:, pair], aug_q).astype(BF16)
            ka_ref[rows, head] = jnp.where(data, kn[:, pair], aug_k[:, pair]).astype(BF16)
        v_t = v_all[rows].T.astype(BF16)
        for h in range(FX_HEADS):
            vt_ref[j, h * VT_ROWS:h * VT_ROWS + FX_DH, :] = v_t[h * FX_DH:(h + 1) * FX_DH]
            vt_ref[j, h * VT_ROWS + FX_DH:(h + 1) * VT_ROWS, :] = jnp.ones(
                (VT_ROWS - FX_DH, AT_BLK), BF16)
        for c in hg_groups[j::PREP_SUB]:
            proj_ref[:, c * IN_TN:(c + 1) * IN_TN] = proj(c).astype(BF16)


def _inproj_prep(x2d, seq_len, norm_w, w_pad, layer, f_bias, qnw, knw):
    t = x2d.shape[0]
    consts = _prep_constants()
    grp, tri, pk = (jnp.asarray(c, BF16) for c in consts[:3])
    cq, ck = (jnp.asarray(c, F32) for c in consts[3:])
    width = FX_HEADS * LANE

    def full(shape):
        return pl.BlockSpec(shape, lambda i: (0,) * len(shape))

    return pl.pallas_call(
        functools.partial(_inproj_prep_kernel, steps_per_seq=seq_len // IN_TM),
        out_shape=(jax.ShapeDtypeStruct((t, HG_COLS), BF16),
                   jax.ShapeDtypeStruct((t, width), BF16),
                   jax.ShapeDtypeStruct((t, width), BF16),
                   jax.ShapeDtypeStruct((t // AT_BLK, FX_HEADS * VT_ROWS, AT_BLK), BF16),
                   jax.ShapeDtypeStruct((t // AT_BLK, 8, LANE), F32)),
        grid=(t // IN_TM,),
        in_specs=[
            pl.BlockSpec((IN_TM, D_MODEL), lambda i: (i, 0)),
            full((1, D_MODEL)),
            pl.BlockSpec((None, D_MODEL, PROJ_MAIN), lambda i: (layer, 0, 0),
                         pipeline_mode=pl.Buffered(1)),
            pl.BlockSpec((None, D_MODEL, LANE), lambda i: (layer, 0, PROJ_MAIN // LANE)),
            full((1, LANE)), full((1, FX_WIDTH)), full((1, FX_WIDTH)),
            full((FX_WIDTH // 2, FX_WIDTH // 2)), full((AT_BLK, AT_BLK)),
            full((LANE, FX_WIDTH)), full((1, LANE)), full((1, FX_WIDTH)),
        ],
        out_specs=(pl.BlockSpec((IN_TM, HG_COLS), lambda i: (i, 0)),
                   pl.BlockSpec((IN_TM, width), lambda i: (i, 0)),
                   pl.BlockSpec((IN_TM, width), lambda i: (i, 0)),
                   pl.BlockSpec((PREP_SUB, FX_HEADS * VT_ROWS, AT_BLK), lambda i: (i, 0, 0)),
                   pl.BlockSpec((PREP_SUB, 8, LANE), lambda i: (i, 0, 0))),
        scratch_shapes=[pltpu.VMEM((8, LANE), F32)],
        compiler_params=pltpu.CompilerParams(
            dimension_semantics=("arbitrary",), vmem_limit_bytes=VMEM_LIMIT),
        name="inproj_prep",
    )(x2d, norm_w, w_pad, w_pad, f_bias, qnw, knw, grp, tri, pk, cq, ck)


AT_HG = 2
AT_SLOTS = 3
AT_TRIP_TICKS = 4 * AT_SLOTS
F32_EXP2_ZERO = 150.0
FIXED_STABILISER_MAX = 32.0
SUBLANES = 8


def _tree_reduce(op, final, x):
    while x.shape[0] > SUBLANES:
        half = x.shape[0] // 2
        x = op(x[:half], x[half:])
    return final(x, axis=0, keepdims=True)


def _attn_kernel(order_ref, r_ref, mask_ref, *refs, nblk, n_r):
    q_refs, k_refs, vt_refs = (refs[i * AT_HG:(i + 1) * AT_HG] for i in range(3))
    o_ref, s_ref, p_ref, acc_ref, m_ref, a_ref, start_ref = refs[3 * AT_HG:]
    b = pl.program_id(0)
    hg = pl.program_id(1)
    bases = [(b * FX_HEADS + order_ref[hg * AT_HG + g]) * nblk for g in range(AT_HG)]
    last = nblk - 1

    thr = r_ref[n_r]

    def find_start(qi, n_items):
        def needed(first):
            hit = None
            for base in bases:
                c = r_ref[base + qi] - r_ref[base + first] >= thr
                hit = c if hit is None else jnp.logical_or(hit, c)
            return hit

        first = lax.while_loop(lambda f: jnp.logical_and(f > 0, needed(f)),
                               lambda f: f - 1, qi)
        start_ref[qi] = first
        return n_items + (qi - first + 1)

    n_items = lax.fori_loop(0, nblk, find_start, 0)

    def next_item(item):
        qi, ki, valid = item
        at_diag = ki == qi
        more = jnp.logical_and(valid != 0,
                               jnp.logical_not(jnp.logical_and(at_diag, qi == last)))
        qn = jnp.where(at_diag, jnp.minimum(qi + 1, last), qi)
        kn = jnp.where(at_diag, start_ref[qn], ki + 1)
        return (jnp.where(more, qn, last), jnp.where(more, kn, last), more.astype(jnp.int32))

    def stage_a(slot, item):
        qi, ki, _ = item
        rows_q = pl.ds(pl.multiple_of(qi * AT_BLK, AT_BLK), AT_BLK)
        rows_k = pl.ds(pl.multiple_of(ki * AT_BLK, AT_BLK), AT_BLK)
        mask = mask_ref[(ki == qi).astype(jnp.int32)]
        for g in range(AT_HG):
            s_ref[slot, g] = _dot_nt(k_refs[g][rows_k, :], q_refs[g][rows_q, :]) + mask

    qk_bound = r_ref[n_r + 1]

    def stage_b(slot, item, fixed):
        qi, ki, valid = item
        live = valid != 0
        first = jnp.logical_and(live, ki == start_ref[qi])
        for g in range(AT_HG):
            delta = jnp.where(live, r_ref[bases[g] + qi] - r_ref[bases[g] + ki], NEG_BIG)
            s = s_ref[slot, g]
            if fixed:
                p_ref[slot, g] = jnp.exp2(s + (delta - qk_bound)).astype(BF16)
            else:
                m = jnp.where(first, NEG_BIG, m_ref[g])
                m_new = jnp.maximum(m, _tree_reduce(jnp.maximum, jnp.max, s) + delta)
                p_ref[slot, g] = jnp.exp2(s - (m_new - delta)).astype(BF16)
                a_ref[g] = jnp.exp2(m - m_new)
                m_ref[g] = m_new

    def stage_c(slot, item, fixed):
        qi, ki, _ = item
        for g in range(AT_HG):
            pv = _dot(vt_refs[g][ki], p_ref[slot, g])
            acc_ref[qi, g] = acc_ref[qi, g] + pv if fixed else a_ref[g] * acc_ref[qi, g] + pv

    def tick(slot, items, fixed):
        item_c, item_b, item_s, item_a = items
        ahead = (slot + 2) % AT_SLOTS
        stage_c(ahead, item_c, fixed)
        stage_b(slot, item_b, fixed)
        stage_a(ahead, item_a)
        return (item_b, item_s, item_a, next_item(item_a))

    zero = jnp.int32(0)
    item0 = (zero, zero, jnp.int32(1))
    item1 = next_item(item0)
    idle = (zero, zero, zero)
    stage_a(0, item0)
    stage_a(1, item1)

    def sweep(fixed):
        for g in range(AT_HG):
            p_ref[AT_SLOTS - 1, g] = jnp.zeros((AT_BLK, AT_BLK), BF16)
            a_ref[g] = jnp.ones((1, AT_BLK), F32)
            m_ref[g] = jnp.zeros((1, AT_BLK), F32)
        acc_ref[...] = jnp.zeros(acc_ref.shape, F32) if fixed else jnp.ones(acc_ref.shape, F32)

        def trip(_, items):
            for t in range(AT_TRIP_TICKS):
                items = tick(t % AT_SLOTS, items, fixed)
            return items

        lax.fori_loop(0, (n_items + AT_TRIP_TICKS) // AT_TRIP_TICKS, trip,
                      (idle, item0, item1, next_item(item1)))

    use_fixed = order_ref[FX_HEADS] != 0

    @pl.when(use_fixed)
    def _():
        sweep(True)

    @pl.when(jnp.logical_not(use_fixed))
    def _():
        sweep(False)

    def normalize(qi, carry):
        for g in range(AT_HG):
            acc = acc_ref[qi, g]
            o_ref[qi, g * FX_DH:(g + 1) * FX_DH, :] = (
                acc[:FX_DH] / acc[FX_DH:FX_DH + 1]).astype(o_ref.dtype)
        return carry

    lax.fori_loop(0, nblk, normalize, 0)


def _attention(order, r_flat, q_aug, k_aug, v_t):
    bsz, s, _ = q_aug.shape
    nblk = s // AT_BLK
    causal = np.where(np.arange(AT_BLK)[:, None] <= np.arange(AT_BLK)[None, :], 0.0, NEG_BIG)
    mask = jnp.asarray(np.stack([np.zeros_like(causal), causal]), F32)

    def head_cols(g):
        return pl.BlockSpec((None, s, LANE), lambda b, h, order: (b, 0, order[h * AT_HG + g]))

    def head_vt(g):
        return pl.BlockSpec((None, nblk, VT_ROWS, AT_BLK),
                            lambda b, h, order: (b, 0, order[h * AT_HG + g], 0))

    grid_spec = pltpu.PrefetchScalarGridSpec(
        num_scalar_prefetch=1,
        grid=(bsz, FX_HEADS // AT_HG),
        in_specs=[
            pl.BlockSpec(memory_space=pltpu.SMEM),
            pl.BlockSpec((2, AT_BLK, AT_BLK), lambda b, h, order: (0, 0, 0)),
            *[head_cols(g) for g in range(AT_HG)],
            *[head_cols(g) for g in range(AT_HG)],
            *[head_vt(g) for g in range(AT_HG)],
        ],
        out_specs=pl.BlockSpec((None, nblk, AT_HG * FX_DH, AT_BLK),
                               lambda b, h, order: (b, 0, h, 0)),
        scratch_shapes=[
            pltpu.VMEM((AT_SLOTS, AT_HG, AT_BLK, AT_BLK), F32),
            pltpu.VMEM((AT_SLOTS, AT_HG, AT_BLK, AT_BLK), BF16),
            pltpu.VMEM((nblk, AT_HG, VT_ROWS, AT_BLK), F32),
            pltpu.VMEM((AT_HG, 1, AT_BLK), F32),
            pltpu.VMEM((AT_HG, 1, AT_BLK), F32),
            pltpu.SMEM((nblk,), jnp.int32),
        ],
    )
    return pl.pallas_call(
        functools.partial(_attn_kernel, nblk=nblk, n_r=r_flat.shape[0] - 2),
        out_shape=jax.ShapeDtypeStruct((bsz, nblk, FX_WIDTH, AT_BLK), BF16),
        grid_spec=grid_spec,
        compiler_params=pltpu.CompilerParams(
            dimension_semantics=("arbitrary", "arbitrary"),
            vmem_limit_bytes=VMEM_LIMIT),
        name="fox_attention",
    )(order, r_flat, mask, *([q_aug] * AT_HG), *([k_aug] * AT_HG), *([v_t] * AT_HG))


FF_TM = 1024
FF_TF = D_FF
FF_CHUNK = 1024


def _ffn_kernel(x_ref, oa_ref, obt_ref, woa_ref, wob_ref, nw_ref, wu_ref, wd_ref,
                o_ref, h_ref):
    @pl.when(pl.program_id(2) == 0)
    def _():
        o_b = jnp.concatenate([_dot_tn(obt_ref[j], wob_ref[...])
                               for j in range(FF_TM // AT_BLK)], axis=0)
        x1 = x_ref[...] + _dot(oa_ref[...], woa_ref[...]) + o_b
        ms = jnp.mean(x1 * x1, axis=-1, keepdims=True)
        h_ref[...] = (x1 * lax.rsqrt(ms + NORM_EPS) * nw_ref[...]).astype(BF16)
        o_ref[...] = x1

    for c in range(FF_TF // FF_CHUNK):
        cols = slice(c * FF_CHUNK, (c + 1) * FF_CHUNK)
        u = jnp.maximum(_dot(h_ref[...], wu_ref[:, cols]), 0.0)
        o_ref[...] += _dot((u * u).astype(BF16), wd_ref[cols, :])


def _outproj_ffn(x3d, o_a, o_bt, w_out, w_out_fox, norm_w, w_up, w_down, layer):
    bsz, s, _ = x3d.shape
    resident = dict(pipeline_mode=pl.Buffered(1)) if FF_TF == D_FF else {}
    return pl.pallas_call(
        _ffn_kernel,
        out_shape=jax.ShapeDtypeStruct(x3d.shape, F32),
        grid=(bsz, s // FF_TM, D_FF // FF_TF),
        in_specs=[
            pl.BlockSpec((None, FF_TM, D_MODEL), lambda b, i, j: (b, i, 0)),
            pl.BlockSpec((None, FF_TM, HG_WIDTH), lambda b, i, j: (b, i, 0)),
            pl.BlockSpec((None, FF_TM // AT_BLK, FX_WIDTH, AT_BLK), lambda b, i, j: (b, i, 0, 0)),
            pl.BlockSpec((None, HG_WIDTH, D_MODEL), lambda b, i, j: (layer, 0, 0), **resident),
            pl.BlockSpec((FX_WIDTH, D_MODEL), lambda b, i, j: (0, 0), **resident),
            pl.BlockSpec((1, D_MODEL), lambda b, i, j: (0, 0)),
            pl.BlockSpec((None, D_MODEL, FF_TF), lambda b, i, j: (layer, 0, j), **resident),
            pl.BlockSpec((None, FF_TF, D_MODEL), lambda b, i, j: (layer, j, 0), **resident),
        ],
        out_specs=pl.BlockSpec((None, FF_TM, D_MODEL), lambda b, i, j: (b, i, 0)),
        scratch_shapes=[pltpu.VMEM((FF_TM, D_MODEL), BF16)],
        compiler_params=pltpu.CompilerParams(
            dimension_semantics=("arbitrary", "arbitrary", "arbitrary"),
            vmem_limit_bytes=VMEM_LIMIT),
        name="outproj_ffn",
    )(x3d, o_a, o_bt, w_out, w_out_fox, norm_w, w_up, w_down)


def _layer_lower_bounds(lower_bounds):
    p = jax.nn.softmax(lower_bounds.astype(F32), axis=0)
    c = jnp.cumsum(p, axis=0)
    return c - c[0:1]


def kernel(x, lower_bounds, norm1_w, w_in, fox_f_bias, q_norm_w, k_norm_w,
           hgrn_norm_w, w_out, norm2_w, w_up, w_down):
    bsz, s, d = x.shape
    depth = w_in.shape[0]
    nblk = s // AT_BLK
    lbs = _layer_lower_bounds(lower_bounds)
    w_in_pad = jnp.pad(w_in.astype(BF16), ((0, 0), (0, 0), (0, PROJ_PAD - w_in.shape[-1])))
    w_out_b = w_out.astype(BF16)
    w_up_b = w_up.astype(BF16)
    w_down_b = w_down.astype(BF16)
    f_bias = jnp.pad(fox_f_bias, ((0, 0), (0, LANE - FX_HEADS)))

    for l in range(depth):
        proj, q_aug, k_aug, v_t, r_blk = _inproj_prep(
            x.reshape(bsz * s, d), s, norm1_w[l][None, :], w_in_pad, l, f_bias[l][None, :],
            jnp.tile(q_norm_w[l] * (FX_DH ** -0.5 * LOG2E), FX_HEADS)[None, :],
            jnp.tile(k_norm_w[l], FX_HEADS)[None, :])
        width = FX_HEADS * LANE
        q_aug = q_aug.reshape(bsz, s, width)
        k_aug = k_aug.reshape(bsz, s, width)
        v_t = v_t.reshape(bsz, nblk, FX_HEADS * VT_ROWS, AT_BLK)
        r_blk = r_blk.reshape(bsz, nblk, 8, LANE)
        o_a = _hgrn(proj.reshape(bsz, s, HG_COLS), lbs[l][None, :], hgrn_norm_w[l][None, :])
        qk_bound = (1.02 * FX_DH ** 0.5 * LOG2E * jnp.max(jnp.abs(q_norm_w[l]))
                    * jnp.max(jnp.abs(k_norm_w[l])))
        skip_thr = -(F32_EXP2_ZERO + 2.0 * qk_bound)
        r_flat = jnp.concatenate([
            jnp.transpose(r_blk[:, :, 0, :FX_HEADS], (0, 2, 1)).reshape(-1),
            skip_thr.reshape(1).astype(F32), qk_bound.reshape(1).astype(F32)])
        order = jnp.argsort(fox_f_bias[l]).astype(jnp.int32)
        use_fixed = (qk_bound <= FIXED_STABILISER_MAX).astype(jnp.int32).reshape(1)
        o_bt = _attention(jnp.concatenate([order, use_fixed]), r_flat, q_aug, k_aug, v_t)
        fox_rows = (HG_WIDTH + order[:, None] * FX_DH + jnp.arange(FX_DH)[None, :]).reshape(-1)
        x = _outproj_ffn(x, o_a, o_bt, w_out_b, w_out_b[l][fox_rows], norm2_w[l][None, :],
                         w_up_b, w_down_b, l)
    return x
```

```python
import functools

import numpy as np
import jax
import jax.numpy as jnp
from jax import lax
from jax.experimental import pallas as pl
from jax.experimental.pallas import tpu as pltpu

F32 = jnp.float32
BF16 = jnp.bfloat16

D_MODEL = 1024
D_FF = 4 * D_MODEL
NORM_EPS = 1e-6

HG_WIDTH = 512
HG_HEADS = 4
HG_DK = 128
HG_CHUNK = 64
HG_LEVELS = 6

FX_WIDTH = 512
FX_HEADS = 8
FX_DH = 64

PROJ_MAIN = 4 * HG_WIDTH + 3 * FX_WIDTH
LANE = 128
PROJ_PAD = PROJ_MAIN + LANE

VMEM_LIMIT = 56 * 1024 * 1024

NEG_BIG = -1e30
LOG2E = 1.4426950408889634

NT_DIMS = (((1,), (1,)), ((), ()))
TN_DIMS = (((0,), (0,)), ((), ()))


def _dot(a, b):
    return jnp.dot(a, b, preferred_element_type=F32)


def _dot_nt(a, b):
    return lax.dot_general(a, b, NT_DIMS, preferred_element_type=F32)


def _dot_tn(a, b):
    return lax.dot_general(a, b, TN_DIMS, preferred_element_type=F32)


def _split2(x):
    hi = x.astype(BF16)
    lo = (x - hi.astype(F32)).astype(BF16)
    return hi, lo


def _split3(x):
    hi = x.astype(BF16)
    r = x - hi.astype(F32)
    mid = r.astype(BF16)
    lo = (r - mid.astype(F32)).astype(BF16)
    return hi, mid, lo


IN_TM = 1024
IN_TN = 512
HG_COLS = 4 * HG_WIDTH


HG_ROWS = 1024
HG_NCHUNK = HG_ROWS // HG_CHUNK
HG_E_ROWS = (2 + HG_LEVELS) * HG_CHUNK


def _hgrn_constants():
    c = HG_CHUNK
    tri = np.tril(np.ones((c, c), np.float32))
    rows = [tri, 1.0 - tri]
    idx = np.arange(c)
    for lvl in range(HG_LEVELS):
        m = c >> (lvl + 1)
        ref = (idx // (2 * m)) * (2 * m) + m - 1
        upper = (idx % (2 * m) >= m)[:, None]
        diff = tri - tri[ref]
        rows.append(np.where(upper, diff, -diff))
    mat = np.concatenate(rows, axis=0)
    mcat = np.concatenate([mat, mat], axis=1)
    x = idx[:, None] ^ idx[None, :]
    top = np.floor(np.log2(np.maximum(x, 1))).astype(np.int32)
    level = (HG_LEVELS - 1) - top
    level = np.where(idx[:, None] == idx[None, :], HG_LEVELS, level)
    level = np.where(idx[:, None] < idx[None, :], -1, level)
    level = np.concatenate([level, level], axis=1)
    return mcat.astype(np.float32), level.astype(np.int32)


def _stack_heads(t, heads):
    return jnp.concatenate([t[:, h * HG_DK:(h + 1) * HG_DK] for h in heads], axis=0)


def _hgrn_kernel(q_ref, f_ref, i_ref, g_ref, lb_ref, nw_ref, mcat_ref, lvl_ref,
                 o_ref, st_ref):
    @pl.when(pl.program_id(1) == 0)
    def _():
        st_ref[...] = jnp.zeros_like(st_ref)

    lb = lb_ref[...]
    nw = nw_ref[...]
    mcat = mcat_ref[...]
    level = lvl_ref[...]
    scale = HG_DK ** -0.5

    def chunk(c, carry):
        r0 = pl.multiple_of(c * HG_CHUNK, HG_CHUNK)
        rows = pl.ds(r0, HG_CHUNK)
        fp = f_ref[rows, :].astype(F32)
        f = lb + (1.0 - lb) * jax.nn.sigmoid(fp)
        lf = jnp.log(f) * LOG2E
        kk = 1.0 - f
        hi, lo = _split2(lf)
        e_all = _dot(mcat, jnp.concatenate([hi, lo], axis=0))
        b = e_all[0:HG_CHUNK]
        qs = q_ref[rows, :].astype(F32) * scale
        q_in = (qs * jnp.exp2(b)).astype(BF16)
        k_out = (kk * jnp.exp2(e_all[HG_CHUNK:2 * HG_CHUNK])).astype(BF16)
        dec = jnp.exp2(b[HG_CHUNK - 1:HG_CHUNK, :])
        qb = qs.astype(BF16)
        kb = kk.astype(BF16)
        v = i_ref[rows, :]
        g = g_ref[rows, :].astype(F32)

        a = [jnp.zeros((HG_CHUNK, LANE), F32) for _ in range(HG_HEADS)]
        for lvl in range(HG_LEVELS + 1):
            if lvl < HG_LEVELS:
                x = jnp.exp2(e_all[(2 + lvl) * HG_CHUNK:(3 + lvl) * HG_CHUNK]).astype(BF16)
                ql, kl = qb * x, kb * x
            else:
                ql, kl = qb, kb
            for pair in range(HG_HEADS // 2):
                heads = (2 * pair, 2 * pair + 1)
                d = _dot_nt(_stack_heads(ql, heads), _stack_heads(kl, heads))
                for i, h in enumerate(heads):
                    blk = d[i * HG_CHUNK:(i + 1) * HG_CHUNK, :]
                    a[h] = jnp.where(level == lvl, blk, a[h])

        zeros = jnp.zeros((HG_CHUNK, HG_DK), BF16)
        for h in range(HG_HEADS):
            ls = slice(h * HG_DK, (h + 1) * HG_DK)
            v_pad = jnp.concatenate([v[:, ls], zeros] if h % 2 == 0 else [zeros, v[:, ls]], axis=0)
            st = st_ref[h]
            o = _dot(a[h].astype(BF16), v_pad) + _dot_nt(q_in[:, ls], st.astype(BF16))
            st_ref[h] = st * dec[:, ls] + _dot_tn(v[:, ls], k_out[:, ls])
            var = jnp.mean(o * o, axis=-1, keepdims=True)
            gh = g[:, ls]
            y = o * lax.rsqrt(var + NORM_EPS) * nw * (gh * jax.nn.sigmoid(gh))
            o_ref[rows, ls] = y.astype(o_ref.dtype)
        return carry

    lax.fori_loop(0, HG_NCHUNK, chunk, 0, unroll=HG_NCHUNK)


def _hgrn(proj3d, lb, norm_w):
    bsz, s, _ = proj3d.shape
    mcat, level = _hgrn_constants()

    def col(j):
        return pl.BlockSpec((None, HG_ROWS, HG_WIDTH), lambda b, i, j=j: (b, i, j))

    return pl.pallas_call(
        _hgrn_kernel,
        out_shape=jax.ShapeDtypeStruct((bsz, s, HG_WIDTH), BF16),
        grid=(bsz, s // HG_ROWS),
        in_specs=[
            col(0), col(1), col(2), col(3),
            pl.BlockSpec((1, HG_WIDTH), lambda b, i: (0, 0)),
            pl.BlockSpec((1, HG_DK), lambda b, i: (0, 0)),
            pl.BlockSpec((HG_E_ROWS, 2 * HG_CHUNK), lambda b, i: (0, 0)),
            pl.BlockSpec((HG_CHUNK, 2 * HG_CHUNK), lambda b, i: (0, 0)),
        ],
        out_specs=pl.BlockSpec((None, HG_ROWS, HG_WIDTH), lambda b, i: (b, i, 0)),
        scratch_shapes=[pltpu.VMEM((HG_HEADS, HG_DK, HG_DK), F32)],
        compiler_params=pltpu.CompilerParams(
            dimension_semantics=("arbitrary", "arbitrary"), vmem_limit_bytes=VMEM_LIMIT),
        name="hgrn2",
    )(proj3d, proj3d, proj3d, proj3d, lb, norm_w,
      jnp.asarray(mcat, BF16), jnp.asarray(level))


AT_BLK = 256
PREP_SUB = IN_TM // AT_BLK
AUG_PIECES = 3
VT_ROWS = 80


def _prep_constants():
    grp = np.zeros((FX_WIDTH // 2, FX_WIDTH // 2), np.float32)
    for h in range(FX_HEADS // 2):
        grp[h * FX_DH:(h + 1) * FX_DH, h * FX_DH:(h + 1) * FX_DH] = 1.0 / FX_DH
    tri = np.tril(np.ones((AT_BLK, AT_BLK), np.float32))
    ones_at = FX_HEADS * AUG_PIECES
    pk = np.zeros((LANE, FX_WIDTH), np.float32)
    cq = np.zeros((1, LANE), np.float32)
    ck = np.zeros((1, FX_WIDTH), np.float32)
    for p in range(AUG_PIECES):
        cq[0, ones_at + p] = 1.0
        for h in range(FX_HEADS):
            base = (h // 2) * LANE + (FX_DH if h % 2 == 0 else 0)
            ck[0, base + FX_HEADS * p + h] = 1.0
            pk[FX_HEADS * p + h, base + ones_at + p] = -1.0
    return grp, tri, pk, cq, ck


def _inproj_prep_kernel(x_ref, nw_ref, w_ref, wff_ref, fb_ref, qnw_ref, knw_ref,
                        grp_ref, tri_ref, pk_ref, cq_ref, ck_ref,
                        proj_ref, qa_ref, ka_ref, vt_ref, r_ref, carry_ref, *, steps_per_seq):
    @pl.when(pl.program_id(0) % steps_per_seq == 0)
    def _():
        carry_ref[...] = jnp.zeros_like(carry_ref)

    x = x_ref[...]
    ms = jnp.mean(x * x, axis=-1, keepdims=True)
    xn = (x * lax.rsqrt(ms + NORM_EPS) * nw_ref[...]).astype(BF16)

    def proj(c):
        return _dot(xn, w_ref[:, c * IN_TN:(c + 1) * IN_TN])

    hg_groups = list(range(HG_COLS // IN_TN))
    q_all, k_all, v_all = (proj(len(hg_groups) + c) for c in range(3))

    z = _dot(xn, wff_ref[...]) + fb_ref[...]
    lf = (jnp.minimum(z, 0.0) - jnp.log(1.0 + jnp.exp(-jnp.abs(z)))) * LOG2E
    grp = grp_ref[...]
    tri = tri_ref[...]
    lane = lax.broadcasted_iota(jnp.int32, (AT_BLK, LANE), 1)
    low_half = lane < FX_DH
    head_lane = lane < FX_HEADS

    def qk_norm(t, w):
        sq = (t * t).astype(BF16)
        half = FX_WIDTH // 2
        ms = jnp.concatenate([_dot(sq[:, :half], grp), _dot(sq[:, half:], grp)], axis=1)
        return t * lax.rsqrt(ms + NORM_EPS) * w

    def pack_pieces(t):
        packed = None
        for p, piece in enumerate(_split3(t)):
            piece = jnp.where(head_lane, piece.astype(F32), 0.0)
            piece = piece if p == 0 else pltpu.roll(piece, FX_HEADS * p, axis=1)
            packed = piece if packed is None else packed + piece
        return packed

    for j in range(PREP_SUB):
        rows = slice(j * AT_BLK, (j + 1) * AT_BLK)
        sums = _dot(tri, jnp.concatenate(_split3(lf[rows]), axis=1))
        c_rel = sums[:, :LANE] + sums[:, LANE:2 * LANE] + sums[:, 2 * LANE:]
        r_ref[j] = carry_ref[...]
        carry_ref[...] = carry_ref[...] + c_rel[AT_BLK - 1:AT_BLK, :]
        packed = pack_pieces(c_rel)
        aug_q_odd = packed + cq_ref[...]
        aug_q_even = pltpu.roll(aug_q_odd, FX_DH, axis=1)
        aug_k = _dot(packed.astype(BF16), pk_ref[...]) + ck_ref[...]
        qn = qk_norm(q_all[rows], qnw_ref[...])
        kn = qk_norm(k_all[rows], knw_ref[...])
        for h in range(FX_HEADS):
            pair = slice((h // 2) * LANE, (h // 2 + 1) * LANE)
            head = slice(h * LANE, (h + 1) * LANE)
            data = low_half if h % 2 == 0 else jnp.logical_not(low_half)
            aug_q = aug_q_even if h % 2 == 0 else aug_q_odd
            qa_ref[rows, head] = jnp.where(data, qn[:, pair], aug_q).astype(BF16)
            ka_ref[rows, head] = jnp.where(data, kn[:, pair], aug_k[:, pair]).astype(BF16)
        v_t = v_all[rows].T.astype(BF16)
        for h in range(FX_HEADS):
            vt_ref[j, h * VT_ROWS:h * VT_ROWS + FX_DH, :] = v_t[h * FX_DH:(h + 1) * FX_DH]
            vt_ref[j, h * VT_ROWS + FX_DH:(h + 1) * VT_ROWS, :] = jnp.ones(
                (VT_ROWS - FX_DH, AT_BLK), BF16)
        for c in hg_groups[j::PREP_SUB]:
            proj_ref[:, c * IN_TN:(c + 1) * IN_TN] = proj(c).astype(BF16)


def _inproj_prep(x2d, seq_len, norm_w, w_all, w_ff, layer, f_bias, qnw, knw):
    t = x2d.shape[0]
    consts = _prep_constants()
    grp, tri, pk = (jnp.asarray(c, BF16) for c in consts[:3])
    cq, ck = (jnp.asarray(c, F32) for c in consts[3:])
    width = FX_HEADS * LANE

    def full(shape):
        return pl.BlockSpec(shape, lambda i: (0,) * len(shape))

    return pl.pallas_call(
        functools.partial(_inproj_prep_kernel, steps_per_seq=seq_len // IN_TM),
        out_shape=(jax.ShapeDtypeStruct((t, HG_COLS), BF16),
                   jax.ShapeDtypeStruct((t, width), BF16),
                   jax.ShapeDtypeStruct((t, width), BF16),
                   jax.ShapeDtypeStruct((t // AT_BLK, FX_HEADS * VT_ROWS, AT_BLK), BF16),
                   jax.ShapeDtypeStruct((t // AT_BLK, 8, LANE), F32)),
        grid=(t // IN_TM,),
        in_specs=[
            pl.BlockSpec((IN_TM, D_MODEL), lambda i: (i, 0)),
            full((1, D_MODEL)),
            pl.BlockSpec((None, D_MODEL, PROJ_MAIN), lambda i: (layer, 0, 0),
                         pipeline_mode=pl.Buffered(1)),
            pl.BlockSpec((None, D_MODEL, LANE), lambda i: (layer, 0, 0)),
            full((1, LANE)), full((1, FX_WIDTH)), full((1, FX_WIDTH)),
            full((FX_WIDTH // 2, FX_WIDTH // 2)), full((AT_BLK, AT_BLK)),
            full((LANE, FX_WIDTH)), full((1, LANE)), full((1, FX_WIDTH)),
        ],
        out_specs=(pl.BlockSpec((IN_TM, HG_COLS), lambda i: (i, 0)),
                   pl.BlockSpec((IN_TM, width), lambda i: (i, 0)),
                   pl.BlockSpec((IN_TM, width), lambda i: (i, 0)),
                   pl.BlockSpec((PREP_SUB, FX_HEADS * VT_ROWS, AT_BLK), lambda i: (i, 0, 0)),
                   pl.BlockSpec((PREP_SUB, 8, LANE), lambda i: (i, 0, 0))),
        scratch_shapes=[pltpu.VMEM((8, LANE), F32)],
        compiler_params=pltpu.CompilerParams(
            dimension_semantics=("arbitrary",), vmem_limit_bytes=VMEM_LIMIT),
        name="inproj_prep",
    )(x2d, norm_w, w_all, w_ff, f_bias, qnw, knw, grp, tri, pk, cq, ck)


AT_HG = 2
AT_SLOTS = 3
AT_TRIP_TICKS = 4 * AT_SLOTS
F32_EXP2_ZERO = 150.0
FIXED_STABILISER_MAX = 32.0
SUBLANES = 8


def _tree_reduce(op, final, x):
    while x.shape[0] > SUBLANES:
        half = x.shape[0] // 2
        x = op(x[:half], x[half:])
    return final(x, axis=0, keepdims=True)


def _attn_kernel(order_ref, r_ref, mask_ref, *refs, nblk, n_r):
    q_refs, k_refs, vt_refs = (refs[i * AT_HG:(i + 1) * AT_HG] for i in range(3))
    o_ref, s_ref, p_ref, acc_ref, m_ref, a_ref, start_ref = refs[3 * AT_HG:]
    b = pl.program_id(0)
    hg = pl.program_id(1)
    bases = [(b * FX_HEADS + order_ref[hg * AT_HG + g]) * nblk for g in range(AT_HG)]
    last = nblk - 1

    thr = r_ref[n_r]

    def find_start(qi, n_items):
        def needed(first):
            hit = None
            for base in bases:
                c = r_ref[base + qi] - r_ref[base + first] >= thr
                hit = c if hit is None else jnp.logical_or(hit, c)
            return hit

        first = lax.while_loop(lambda f: jnp.logical_and(f > 0, needed(f)),
                               lambda f: f - 1, qi)
        start_ref[qi] = first
        return n_items + (qi - first + 1)

    n_items = lax.fori_loop(0, nblk, find_start, 0)

    def next_item(item):
        qi, ki, valid = item
        at_diag = ki == qi
        more = jnp.logical_and(valid != 0,
                               jnp.logical_not(jnp.logical_and(at_diag, qi == last)))
        qn = jnp.where(at_diag, jnp.minimum(qi + 1, last), qi)
        kn = jnp.where(at_diag, start_ref[qn], ki + 1)
        return (jnp.where(more, qn, last), jnp.where(more, kn, last), more.astype(jnp.int32))

    def stage_a(slot, item):
        qi, ki, _ = item
        rows_q = pl.ds(pl.multiple_of(qi * AT_BLK, AT_BLK), AT_BLK)
        rows_k = pl.ds(pl.multiple_of(ki * AT_BLK, AT_BLK), AT_BLK)
        mask = mask_ref[(ki == qi).astype(jnp.int32)]
        for g in range(AT_HG):
            s_ref[slot, g] = _dot_nt(k_refs[g][rows_k, :], q_refs[g][rows_q, :]) + mask

    qk_bound = r_ref[n_r + 1]

    def stage_b(slot, item, fixed):
        qi, ki, valid = item
        live = valid != 0
        first = jnp.logical_and(live, ki == start_ref[qi])
        for g in range(AT_HG):
            delta = jnp.where(live, r_ref[bases[g] + qi] - r_ref[bases[g] + ki], NEG_BIG)
            s = s_ref[slot, g]
            if fixed:
                p_ref[slot, g] = jnp.exp2(s + (delta - qk_bound)).astype(BF16)
            else:
                m = jnp.where(first, NEG_BIG, m_ref[g])
                m_new = jnp.maximum(m, _tree_reduce(jnp.maximum, jnp.max, s) + delta)
                p_ref[slot, g] = jnp.exp2(s - (m_new - delta)).astype(BF16)
                a_ref[g] = jnp.exp2(m - m_new)
                m_ref[g] = m_new

    def stage_c(slot, item, fixed):
        qi, ki, _ = item
        for g in range(AT_HG):
            pv = _dot(vt_refs[g][ki], p_ref[slot, g])
            acc_ref[qi, g] = acc_ref[qi, g] + pv if fixed else a_ref[g] * acc_ref[qi, g] + pv

    def tick(slot, items, fixed):
        item_c, item_b, item_s, item_a = items
        ahead = (slot + 2) % AT_SLOTS
        stage_c(ahead, item_c, fixed)
        stage_b(slot, item_b, fixed)
        stage_a(ahead, item_a)
        return (item_b, item_s, item_a, next_item(item_a))

    zero = jnp.int32(0)
    item0 = (zero, zero, jnp.int32(1))
    item1 = next_item(item0)
    idle = (zero, zero, zero)
    stage_a(0, item0)
    stage_a(1, item1)

    def sweep(fixed):
        for g in range(AT_HG):
            p_ref[AT_SLOTS - 1, g] = jnp.zeros((AT_BLK, AT_BLK), BF16)
            a_ref[g] = jnp.ones((1, AT_BLK), F32)
            m_ref[g] = jnp.zeros((1, AT_BLK), F32)
        acc_ref[...] = jnp.zeros(acc_ref.shape, F32) if fixed else jnp.ones(acc_ref.shape, F32)

        def trip(_, items):
            for t in range(AT_TRIP_TICKS):
                items = tick(t % AT_SLOTS, items, fixed)
            return items

        lax.fori_loop(0, (n_items + AT_TRIP_TICKS) // AT_TRIP_TICKS, trip,
                      (idle, item0, item1, next_item(item1)))

    use_fixed = order_ref[FX_HEADS] != 0

    @pl.when(use_fixed)
    def _():
        sweep(True)

    @pl.when(jnp.logical_not(use_fixed))
    def _():
        sweep(False)

    def normalize(qi, carry):
        for g in range(AT_HG):
            acc = acc_ref[qi, g]
            o_ref[qi, g * FX_DH:(g + 1) * FX_DH, :] = (
                acc[:FX_DH] / acc[FX_DH:FX_DH + 1]).astype(o_ref.dtype)
        return carry

    lax.fori_loop(0, nblk, normalize, 0)


def _attention(order, r_flat, q_aug, k_aug, v_t):
    bsz, s, _ = q_aug.shape
    nblk = s // AT_BLK
    causal = np.where(np.arange(AT_BLK)[:, None] <= np.arange(AT_BLK)[None, :], 0.0, NEG_BIG)
    mask = jnp.asarray(np.stack([np.zeros_like(causal), causal]), F32)

    def head_cols(g):
        return pl.BlockSpec((None, s, LANE), lambda b, h, order: (b, 0, order[h * AT_HG + g]))

    def head_vt(g):
        return pl.BlockSpec((None, nblk, VT_ROWS, AT_BLK),
                            lambda b, h, order: (b, 0, order[h * AT_HG + g], 0))

    grid_spec = pltpu.PrefetchScalarGridSpec(
        num_scalar_prefetch=1,
        grid=(bsz, FX_HEADS // AT_HG),
        in_specs=[
            pl.BlockSpec(memory_space=pltpu.SMEM),
            pl.BlockSpec((2, AT_BLK, AT_BLK), lambda b, h, order: (0, 0, 0)),
            *[head_cols(g) for g in range(AT_HG)],
            *[head_cols(g) for g in range(AT_HG)],
            *[head_vt(g) for g in range(AT_HG)],
        ],
        out_specs=pl.BlockSpec((None, nblk, AT_HG * FX_DH, AT_BLK),
                               lambda b, h, order: (b, 0, h, 0)),
        scratch_shapes=[
            pltpu.VMEM((AT_SLOTS, AT_HG, AT_BLK, AT_BLK), F32),
            pltpu.VMEM((AT_SLOTS, AT_HG, AT_BLK, AT_BLK), BF16),
            pltpu.VMEM((nblk, AT_HG, VT_ROWS, AT_BLK), F32),
            pltpu.VMEM((AT_HG, 1, AT_BLK), F32),
            pltpu.VMEM((AT_HG, 1, AT_BLK), F32),
            pltpu.SMEM((nblk,), jnp.int32),
        ],
    )
    return pl.pallas_call(
        functools.partial(_attn_kernel, nblk=nblk, n_r=r_flat.shape[0] - 2),
        out_shape=jax.ShapeDtypeStruct((bsz, nblk, FX_WIDTH, AT_BLK), BF16),
        grid_spec=grid_spec,
        compiler_params=pltpu.CompilerParams(
            dimension_semantics=("arbitrary", "arbitrary"),
            vmem_limit_bytes=VMEM_LIMIT),
        name="fox_attention",
    )(order, r_flat, mask, *([q_aug] * AT_HG), *([k_aug] * AT_HG), *([v_t] * AT_HG))


FF_TM = 1024
FF_TF = D_FF
FF_CHUNK = 1024


def _ffn_kernel(x_ref, oa_ref, obt_ref, woa_ref, wob_ref, nw_ref, wu_ref, wd_ref,
                o_ref, h_ref):
    @pl.when(pl.program_id(2) == 0)
    def _():
        o_b = jnp.concatenate([_dot_tn(obt_ref[j], wob_ref[...])
                               for j in range(FF_TM // AT_BLK)], axis=0)
        x1 = x_ref[...] + _dot(oa_ref[...], woa_ref[...]) + o_b
        ms = jnp.mean(x1 * x1, axis=-1, keepdims=True)
        h_ref[...] = (x1 * lax.rsqrt(ms + NORM_EPS) * nw_ref[...]).astype(BF16)
        o_ref[...] = x1

    for c in range(FF_TF // FF_CHUNK):
        cols = slice(c * FF_CHUNK, (c + 1) * FF_CHUNK)
        u = jnp.maximum(_dot(h_ref[...], wu_ref[:, cols]), 0.0)
        o_ref[...] += _dot((u * u).astype(BF16), wd_ref[cols, :])


def _outproj_ffn(x3d, o_a, o_bt, w_out, w_out_fox, norm_w, w_up, w_down, layer):
    bsz, s, _ = x3d.shape
    resident = dict(pipeline_mode=pl.Buffered(1)) if FF_TF == D_FF else {}
    return pl.pallas_call(
        _ffn_kernel,
        out_shape=jax.ShapeDtypeStruct(x3d.shape, F32),
        grid=(bsz, s // FF_TM, D_FF // FF_TF),
        in_specs=[
            pl.BlockSpec((None, FF_TM, D_MODEL), lambda b, i, j: (b, i, 0)),
            pl.BlockSpec((None, FF_TM, HG_WIDTH), lambda b, i, j: (b, i, 0)),
            pl.BlockSpec((None, FF_TM // AT_BLK, FX_WIDTH, AT_BLK), lambda b, i, j: (b, i, 0, 0)),
            pl.BlockSpec((None, HG_WIDTH, D_MODEL), lambda b, i, j: (layer, 0, 0), **resident),
            pl.BlockSpec((FX_WIDTH, D_MODEL), lambda b, i, j: (0, 0), **resident),
            pl.BlockSpec((1, D_MODEL), lambda b, i, j: (0, 0)),
            pl.BlockSpec((None, D_MODEL, FF_TF), lambda b, i, j: (layer, 0, j), **resident),
            pl.BlockSpec((None, FF_TF, D_MODEL), lambda b, i, j: (layer, j, 0), **resident),
        ],
        out_specs=pl.BlockSpec((None, FF_TM, D_MODEL), lambda b, i, j: (b, i, 0)),
        scratch_shapes=[pltpu.VMEM((FF_TM, D_MODEL), BF16)],
        compiler_params=pltpu.CompilerParams(
            dimension_semantics=("arbitrary", "arbitrary", "arbitrary"),
            vmem_limit_bytes=VMEM_LIMIT),
        name="outproj_ffn",
    )(x3d, o_a, o_bt, w_out, w_out_fox, norm_w, w_up, w_down)


def _layer_lower_bounds(lower_bounds):
    p = jax.nn.softmax(lower_bounds.astype(F32), axis=0)
    c = jnp.cumsum(p, axis=0)
    return c - c[0:1]


def kernel(x, lower_bounds, norm1_w, w_in, fox_f_bias, q_norm_w, k_norm_w,
           hgrn_norm_w, w_out, norm2_w, w_up, w_down):
    bsz, s, d = x.shape
    depth = w_in.shape[0]
    nblk = s // AT_BLK
    lbs = _layer_lower_bounds(lower_bounds)
    w_in_b = w_in.astype(BF16)
    w_in_ff = jnp.pad(w_in_b[:, :, PROJ_MAIN:], ((0, 0), (0, 0), (0, PROJ_PAD - w_in.shape[-1])))
    w_out_b = w_out.astype(BF16)
    w_up_b = w_up.astype(BF16)
    w_down_b = w_down.astype(BF16)
    f_bias = jnp.pad(fox_f_bias, ((0, 0), (0, LANE - FX_HEADS)))

    for l in range(depth):
        proj, q_aug, k_aug, v_t, r_blk = _inproj_prep(
            x.reshape(bsz * s, d), s, norm1_w[l][None, :], w_in_b, w_in_ff, l, f_bias[l][None, :],
            jnp.tile(q_norm_w[l] * (FX_DH ** -0.5 * LOG2E), FX_HEADS)[None, :],
            jnp.tile(k_norm_w[l], FX_HEADS)[None, :])
        width = FX_HEADS * LANE
        q_aug = q_aug.reshape(bsz, s, width)
        k_aug = k_aug.reshape(bsz, s, width)
        v_t = v_t.reshape(bsz, nblk, FX_HEADS * VT_ROWS, AT_BLK)
        r_blk = r_blk.reshape(bsz, nblk, 8, LANE)
        o_a = _hgrn(proj.reshape(bsz, s, HG_COLS), lbs[l][None, :], hgrn_norm_w[l][None, :])
        qk_bound = (1.02 * FX_DH ** 0.5 * LOG2E * jnp.max(jnp.abs(q_norm_w[l]))
                    * jnp.max(jnp.abs(k_norm_w[l])))
        skip_thr = -(F32_EXP2_ZERO + 2.0 * qk_bound)
        r_flat = jnp.concatenate([
            jnp.transpose(r_blk[:, :, 0, :FX_HEADS], (0, 2, 1)).reshape(-1),
            skip_thr.reshape(1).astype(F32), qk_bound.reshape(1).astype(F32)])
        order = jnp.argsort(fox_f_bias[l]).astype(jnp.int32)
        use_fixed = (qk_bound <= FIXED_STABILISER_MAX).astype(jnp.int32).reshape(1)
        o_bt = _attention(jnp.concatenate([order, use_fixed]), r_flat, q_aug, k_aug, v_t)
        fox_rows = (HG_WIDTH + order[:, None] * FX_DH + jnp.arange(FX_DH)[None, :]).reshape(-1)
        x = _outproj_ffn(x, o_a, o_bt, w_out_b, w_out_b[l][fox_rows], norm2_w[l][None, :],
                         w_up_b, w_down_b, l)
    return x
```

```python
import functools

import numpy as np
import jax
import jax.numpy as jnp
from jax import lax
from jax.experimental import pallas as pl
from jax.experimental.pallas import tpu as pltpu

F32 = jnp.float32
BF16 = jnp.bfloat16

D_MODEL = 1024
D_FF = 4 * D_MODEL
NORM_EPS = 1e-6

HG_WIDTH = 512
HG_HEADS = 4
HG_DK = 128
HG_CHUNK = 64
HG_LEVELS = 6

FX_WIDTH = 512
FX_HEADS = 8
FX_DH = 64

PROJ_MAIN = 4 * HG_WIDTH + 3 * FX_WIDTH
LANE = 128
PROJ_PAD = PROJ_MAIN + LANE

VMEM_LIMIT = 56 * 1024 * 1024

NEG_BIG = -1e30
LOG2E = 1.4426950408889634

NT_DIMS = (((1,), (1,)), ((), ()))
TN_DIMS = (((0,), (0,)), ((), ()))


def _dot(a, b):
    return jnp.dot(a, b, preferred_element_type=F32)


def _dot_nt(a, b):
    return lax.dot_general(a, b, NT_DIMS, preferred_element_type=F32)


def _dot_tn(a, b):
    return lax.dot_general(a, b, TN_DIMS, preferred_element_type=F32)


def _sigmoid(x):
    return 0.5 * jnp.tanh(0.5 * x) + 0.5


def _split2(x):
    hi = x.astype(BF16)
    lo = (x - hi.astype(F32)).astype(BF16)
    return hi, lo


def _split3(x):
    hi = x.astype(BF16)
    r = x - hi.astype(F32)
    mid = r.astype(BF16)
    lo = (r - mid.astype(F32)).astype(BF16)
    return hi, mid, lo


IN_TM = 1024
IN_TN = 512
HG_COLS = 4 * HG_WIDTH


HG_ROWS = 2048
HG_NCHUNK = HG_ROWS // HG_CHUNK
HG_E_ROWS = (2 + HG_LEVELS) * HG_CHUNK


def _hgrn_constants():
    c = HG_CHUNK
    tri = np.tril(np.ones((c, c), np.float32))
    rows = [tri, 1.0 - tri]
    idx = np.arange(c)
    for lvl in range(HG_LEVELS):
        m = c >> (lvl + 1)
        ref = (idx // (2 * m)) * (2 * m) + m - 1
        upper = (idx % (2 * m) >= m)[:, None]
        diff = tri - tri[ref]
        rows.append(np.where(upper, diff, -diff))
    mat = np.concatenate(rows, axis=0)
    mcat = np.concatenate([mat, mat], axis=1)
    x = idx[:, None] ^ idx[None, :]
    top = np.floor(np.log2(np.maximum(x, 1))).astype(np.int32)
    level = (HG_LEVELS - 1) - top
    level = np.where(idx[:, None] == idx[None, :], HG_LEVELS, level)
    level = np.where(idx[:, None] < idx[None, :], -1, level)
    level = np.concatenate([level, level], axis=1)
    return mcat.astype(np.float32), level.astype(np.int32)


def _stack_heads(t, heads):
    return jnp.concatenate([t[:, h * HG_DK:(h + 1) * HG_DK] for h in heads], axis=0)


def _hgrn_kernel(q_ref, f_ref, i_ref, g_ref, lb_ref, nw_ref, mcat_ref, lvl_ref,
                 o_ref, st_ref):
    @pl.when(pl.program_id(1) == 0)
    def _():
        st_ref[...] = jnp.zeros_like(st_ref)

    lb = lb_ref[...]
    nw = nw_ref[...]
    mcat = mcat_ref[...]
    level = lvl_ref[...]
    scale = HG_DK ** -0.5

    def chunk(c, carry):
        r0 = pl.multiple_of(c * HG_CHUNK, HG_CHUNK)
        rows = pl.ds(r0, HG_CHUNK)
        fp = f_ref[rows, :].astype(F32)
        f = lb + (1.0 - lb) * _sigmoid(fp)
        lf = jnp.log(f) * LOG2E
        kk = 1.0 - f
        hi, lo = _split2(lf)
        e_all = _dot(mcat, jnp.concatenate([hi, lo], axis=0))
        b = e_all[0:HG_CHUNK]
        qs = q_ref[rows, :].astype(F32) * scale
        q_in = (qs * jnp.exp2(b)).astype(BF16)
        k_out = (kk * jnp.exp2(e_all[HG_CHUNK:2 * HG_CHUNK])).astype(BF16)
        dec = jnp.exp2(b[HG_CHUNK - 1:HG_CHUNK, :])
        qb = qs.astype(BF16)
        kb = kk.astype(BF16)
        v = i_ref[rows, :]
        g = g_ref[rows, :].astype(F32)

        a = [jnp.zeros((HG_CHUNK, LANE), F32) for _ in range(HG_HEADS)]
        for lvl in range(HG_LEVELS + 1):
            if lvl < HG_LEVELS:
                x = jnp.exp2(e_all[(2 + lvl) * HG_CHUNK:(3 + lvl) * HG_CHUNK]).astype(BF16)
                ql, kl = qb * x, kb * x
            else:
                ql, kl = qb, kb
            for pair in range(HG_HEADS // 2):
                heads = (2 * pair, 2 * pair + 1)
                d = _dot_nt(_stack_heads(ql, heads), _stack_heads(kl, heads))
                for i, h in enumerate(heads):
                    blk = d[i * HG_CHUNK:(i + 1) * HG_CHUNK, :]
                    a[h] = jnp.where(level == lvl, blk, a[h])

        zeros = jnp.zeros((HG_CHUNK, HG_DK), BF16)
        for h in range(HG_HEADS):
            ls = slice(h * HG_DK, (h + 1) * HG_DK)
            v_pad = jnp.concatenate([v[:, ls], zeros] if h % 2 == 0 else [zeros, v[:, ls]], axis=0)
            st = st_ref[h]
            o = _dot(a[h].astype(BF16), v_pad) + _dot_nt(q_in[:, ls], st.astype(BF16))
            st_ref[h] = st * dec[:, ls] + _dot_tn(v[:, ls], k_out[:, ls])
            var = jnp.mean(o * o, axis=-1, keepdims=True)
            gh = g[:, ls]
            y = o * lax.rsqrt(var + NORM_EPS) * nw * (gh * _sigmoid(gh))
            o_ref[rows, ls] = y.astype(o_ref.dtype)
        return carry

    lax.fori_loop(0, HG_NCHUNK, chunk, 0, unroll=HG_NCHUNK)


def _hgrn(proj3d, lb, norm_w):
    bsz, s, _ = proj3d.shape
    mcat, level = _hgrn_constants()

    def col(j):
        return pl.BlockSpec((None, HG_ROWS, HG_WIDTH), lambda b, i, j=j: (b, i, j))

    return pl.pallas_call(
        _hgrn_kernel,
        out_shape=jax.ShapeDtypeStruct((bsz, s, HG_WIDTH), BF16),
        grid=(bsz, s // HG_ROWS),
        in_specs=[
            col(0), col(1), col(2), col(3),
            pl.BlockSpec((1, HG_WIDTH), lambda b, i: (0, 0)),
            pl.BlockSpec((1, HG_DK), lambda b, i: (0, 0)),
            pl.BlockSpec((HG_E_ROWS, 2 * HG_CHUNK), lambda b, i: (0, 0)),
            pl.BlockSpec((HG_CHUNK, 2 * HG_CHUNK), lambda b, i: (0, 0)),
        ],
        out_specs=pl.BlockSpec((None, HG_ROWS, HG_WIDTH), lambda b, i: (b, i, 0)),
        scratch_shapes=[pltpu.VMEM((HG_HEADS, HG_DK, HG_DK), F32)],
        compiler_params=pltpu.CompilerParams(
            dimension_semantics=("arbitrary", "arbitrary"), vmem_limit_bytes=VMEM_LIMIT),
        name="hgrn2",
    )(proj3d, proj3d, proj3d, proj3d, lb, norm_w,
      jnp.asarray(mcat, BF16), jnp.asarray(level))


AT_BLK = 256
PREP_SUB = IN_TM // AT_BLK
AUG_PIECES = 3
VT_ROWS = 80


def _prep_constants():
    grp = np.zeros((FX_WIDTH // 2, FX_WIDTH // 2), np.float32)
    for h in range(FX_HEADS // 2):
        grp[h * FX_DH:(h + 1) * FX_DH, h * FX_DH:(h + 1) * FX_DH] = 1.0 / FX_DH
    tri = np.tril(np.ones((AT_BLK, AT_BLK), np.float32))
    ones_at = FX_HEADS * AUG_PIECES
    pk = np.zeros((LANE, FX_WIDTH), np.float32)
    cq = np.zeros((1, LANE), np.float32)
    ck = np.zeros((1, FX_WIDTH), np.float32)
    for p in range(AUG_PIECES):
        cq[0, ones_at + p] = 1.0
        for h in range(FX_HEADS):
            base = (h // 2) * LANE + (FX_DH if h % 2 == 0 else 0)
            ck[0, base + FX_HEADS * p + h] = 1.0
            pk[FX_HEADS * p + h, base + ones_at + p] = -1.0
    return grp, tri, pk, cq, ck


def _inproj_prep_kernel(x_ref, nw_ref, w_ref, wff_ref, fb_ref, qnw_ref, knw_ref,
                        grp_ref, tri_ref, pk_ref, cq_ref, ck_ref,
                        proj_ref, qa_ref, ka_ref, vt_ref, r_ref, carry_ref, *, steps_per_seq):
    @pl.when(pl.program_id(0) % steps_per_seq == 0)
    def _():
        carry_ref[...] = jnp.zeros_like(carry_ref)

    x = x_ref[...]
    ms = jnp.mean(x * x, axis=-1, keepdims=True)
    xn = (x * lax.rsqrt(ms + NORM_EPS) * nw_ref[...]).astype(BF16)

    def proj(c):
        return _dot(xn, w_ref[:, c * IN_TN:(c + 1) * IN_TN])

    hg_groups = list(range(HG_COLS // IN_TN))
    q_all, k_all, v_all = (proj(len(hg_groups) + c) for c in range(3))

    z = _dot(xn, wff_ref[...]) + fb_ref[...]
    lf = (jnp.minimum(z, 0.0) - jnp.log(1.0 + jnp.exp(-jnp.abs(z)))) * LOG2E
    grp = grp_ref[...]
    tri = tri_ref[...]
    lane = lax.broadcasted_iota(jnp.int32, (AT_BLK, LANE), 1)
    low_half = lane < FX_DH
    head_lane = lane < FX_HEADS

    def qk_norm(t, w):
        sq = (t * t).astype(BF16)
        half = FX_WIDTH // 2
        ms = jnp.concatenate([_dot(sq[:, :half], grp), _dot(sq[:, half:], grp)], axis=1)
        return t * lax.rsqrt(ms + NORM_EPS) * w

    def pack_pieces(t):
        packed = None
        for p, piece in enumerate(_split3(t)):
            piece = jnp.where(head_lane, piece.astype(F32), 0.0)
            piece = piece if p == 0 else pltpu.roll(piece, FX_HEADS * p, axis=1)
            packed = piece if packed is None else packed + piece
        return packed

    for j in range(PREP_SUB):
        rows = slice(j * AT_BLK, (j + 1) * AT_BLK)
        sums = _dot(tri, jnp.concatenate(_split3(lf[rows]), axis=1))
        c_rel = sums[:, :LANE] + sums[:, LANE:2 * LANE] + sums[:, 2 * LANE:]
        r_ref[j] = carry_ref[...]
        carry_ref[...] = carry_ref[...] + c_rel[AT_BLK - 1:AT_BLK, :]
        packed = pack_pieces(c_rel)
        aug_q_odd = packed + cq_ref[...]
        aug_q_even = pltpu.roll(aug_q_odd, FX_DH, axis=1)
        aug_k = _dot(packed.astype(BF16), pk_ref[...]) + ck_ref[...]
        qn = qk_norm(q_all[rows], qnw_ref[...])
        kn = qk_norm(k_all[rows], knw_ref[...])
        for h in range(FX_HEADS):
            pair = slice((h // 2) * LANE, (h // 2 + 1) * LANE)
            head = slice(h * LANE, (h + 1) * LANE)
            data = low_half if h % 2 == 0 else jnp.logical_not(low_half)
            aug_q = aug_q_even if h % 2 == 0 else aug_q_odd
            qa_ref[rows, head] = jnp.where(data, qn[:, pair], aug_q).astype(BF16)
            ka_ref[rows, head] = jnp.where(data, kn[:, pair], aug_k[:, pair]).astype(BF16)
        v_t = v_all[rows].T.astype(BF16)
        for h in range(FX_HEADS):
            vt_ref[j, h * VT_ROWS:h * VT_ROWS + FX_DH, :] = v_t[h * FX_DH:(h + 1) * FX_DH]
            vt_ref[j, h * VT_ROWS + FX_DH:(h + 1) * VT_ROWS, :] = jnp.ones(
                (VT_ROWS - FX_DH, AT_BLK), BF16)
        for c in hg_groups[j::PREP_SUB]:
            proj_ref[:, c * IN_TN:(c + 1) * IN_TN] = proj(c).astype(BF16)


def _inproj_prep(x2d, seq_len, norm_w, w_all, w_ff, layer, f_bias, qnw, knw):
    t = x2d.shape[0]
    consts = _prep_constants()
    grp, tri, pk = (jnp.asarray(c, BF16) for c in consts[:3])
    cq, ck = (jnp.asarray(c, F32) for c in consts[3:])
    width = FX_HEADS * LANE

    def full(shape):
        return pl.BlockSpec(shape, lambda i: (0,) * len(shape))

    return pl.pallas_call(
        functools.partial(_inproj_prep_kernel, steps_per_seq=seq_len // IN_TM),
        out_shape=(jax.ShapeDtypeStruct((t, HG_COLS), BF16),
                   jax.ShapeDtypeStruct((t, width), BF16),
                   jax.ShapeDtypeStruct((t, width), BF16),
                   jax.ShapeDtypeStruct((t // AT_BLK, FX_HEADS * VT_ROWS, AT_BLK), BF16),
                   jax.ShapeDtypeStruct((t // AT_BLK, 8, LANE), F32)),
        grid=(t // IN_TM,),
        in_specs=[
            pl.BlockSpec((IN_TM, D_MODEL), lambda i: (i, 0)),
            full((1, D_MODEL)),
            pl.BlockSpec((None, D_MODEL, PROJ_MAIN), lambda i: (layer, 0, 0),
                         pipeline_mode=pl.Buffered(1)),
            pl.BlockSpec((None, D_MODEL, LANE), lambda i: (layer, 0, 0)),
            full((1, LANE)), full((1, FX_WIDTH)), full((1, FX_WIDTH)),
            full((FX_WIDTH // 2, FX_WIDTH // 2)), full((AT_BLK, AT_BLK)),
            full((LANE, FX_WIDTH)), full((1, LANE)), full((1, FX_WIDTH)),
        ],
        out_specs=(pl.BlockSpec((IN_TM, HG_COLS), lambda i: (i, 0)),
                   pl.BlockSpec((IN_TM, width), lambda i: (i, 0)),
                   pl.BlockSpec((IN_TM, width), lambda i: (i, 0)),
                   pl.BlockSpec((PREP_SUB, FX_HEADS * VT_ROWS, AT_BLK), lambda i: (i, 0, 0)),
                   pl.BlockSpec((PREP_SUB, 8, LANE), lambda i: (i, 0, 0))),
        scratch_shapes=[pltpu.VMEM((8, LANE), F32)],
        compiler_params=pltpu.CompilerParams(
            dimension_semantics=("arbitrary",), vmem_limit_bytes=VMEM_LIMIT),
        name="inproj_prep",
    )(x2d, norm_w, w_all, w_ff, f_bias, qnw, knw, grp, tri, pk, cq, ck)


AT_HG = 2
AT_SLOTS = 3
AT_TRIP_TICKS = 4 * AT_SLOTS
F32_EXP2_ZERO = 150.0
FIXED_STABILISER_MAX = 32.0
SUBLANES = 8


def _tree_reduce(op, final, x):
    while x.shape[0] > SUBLANES:
        half = x.shape[0] // 2
        x = op(x[:half], x[half:])
    return final(x, axis=0, keepdims=True)


def _attn_kernel(order_ref, r_ref, mask_ref, *refs, nblk, n_r):
    q_refs, k_refs, vt_refs = (refs[i * AT_HG:(i + 1) * AT_HG] for i in range(3))
    o_ref, s_ref, p_ref, acc_ref, m_ref, a_ref, start_ref = refs[3 * AT_HG:]
    b = pl.program_id(0)
    hg = pl.program_id(1)
    bases = [(b * FX_HEADS + order_ref[hg * AT_HG + g]) * nblk for g in range(AT_HG)]
    last = nblk - 1

    thr = r_ref[n_r]

    def find_start(qi, n_items):
        def needed(first):
            hit = None
            for base in bases:
                c = r_ref[base + qi] - r_ref[base + first] >= thr
                hit = c if hit is None else jnp.logical_or(hit, c)
            return hit

        first = lax.while_loop(lambda f: jnp.logical_and(f > 0, needed(f)),
                               lambda f: f - 1, qi)
        start_ref[qi] = first
        return n_items + (qi - first + 1)

    n_items = lax.fori_loop(0, nblk, find_start, 0)

    def next_item(item):
        qi, ki, valid = item
        at_diag = ki == qi
        more = jnp.logical_and(valid != 0,
                               jnp.logical_not(jnp.logical_and(at_diag, qi == last)))
        qn = jnp.where(at_diag, jnp.minimum(qi + 1, last), qi)
        kn = jnp.where(at_diag, start_ref[qn], ki + 1)
        return (jnp.where(more, qn, last), jnp.where(more, kn, last), more.astype(jnp.int32))

    def stage_a(slot, item):
        qi, ki, _ = item
        rows_q = pl.ds(pl.multiple_of(qi * AT_BLK, AT_BLK), AT_BLK)
        rows_k = pl.ds(pl.multiple_of(ki * AT_BLK, AT_BLK), AT_BLK)
        mask = mask_ref[(ki == qi).astype(jnp.int32)]
        for g in range(AT_HG):
            s_ref[slot, g] = _dot_nt(k_refs[g][rows_k, :], q_refs[g][rows_q, :]) + mask

    qk_bound = r_ref[n_r + 1]

    def stage_b(slot, item, fixed):
        qi, ki, valid = item
        live = valid != 0
        first = jnp.logical_and(live, ki == start_ref[qi])
        for g in range(AT_HG):
            delta = jnp.where(live, r_ref[bases[g] + qi] - r_ref[bases[g] + ki], NEG_BIG)
            s = s_ref[slot, g]
            if fixed:
                p_ref[slot, g] = jnp.exp2(s + (delta - qk_bound)).astype(BF16)
            else:
                m = jnp.where(first, NEG_BIG, m_ref[g])
                m_new = jnp.maximum(m, _tree_reduce(jnp.maximum, jnp.max, s) + delta)
                p_ref[slot, g] = jnp.exp2(s - (m_new - delta)).astype(BF16)
                a_ref[g] = jnp.exp2(m - m_new)
                m_ref[g] = m_new

    def stage_c(slot, item, fixed):
        qi, ki, _ = item
        for g in range(AT_HG):
            pv = _dot(vt_refs[g][ki], p_ref[slot, g])
            acc_ref[qi, g] = acc_ref[qi, g] + pv if fixed else a_ref[g] * acc_ref[qi, g] + pv

    def tick(slot, items, fixed):
        item_c, item_b, item_s, item_a = items
        ahead = (slot + 2) % AT_SLOTS
        stage_c(ahead, item_c, fixed)
        stage_b(slot, item_b, fixed)
        stage_a(ahead, item_a)
        return (item_b, item_s, item_a, next_item(item_a))

    zero = jnp.int32(0)
    item0 = (zero, zero, jnp.int32(1))
    item1 = next_item(item0)
    idle = (zero, zero, zero)
    stage_a(0, item0)
    stage_a(1, item1)

    def sweep(fixed):
        for g in range(AT_HG):
            p_ref[AT_SLOTS - 1, g] = jnp.zeros((AT_BLK, AT_BLK), BF16)
            a_ref[g] = jnp.ones((1, AT_BLK), F32)
            m_ref[g] = jnp.zeros((1, AT_BLK), F32)
        acc_ref[...] = jnp.zeros(acc_ref.shape, F32) if fixed else jnp.ones(acc_ref.shape, F32)

        def trip(_, items):
            for t in range(AT_TRIP_TICKS):
                items = tick(t % AT_SLOTS, items, fixed)
            return items

        lax.fori_loop(0, (n_items + AT_TRIP_TICKS) // AT_TRIP_TICKS, trip,
                      (idle, item0, item1, next_item(item1)))

    use_fixed = order_ref[FX_HEADS] != 0

    @pl.when(use_fixed)
    def _():
        sweep(True)

    @pl.when(jnp.logical_not(use_fixed))
    def _():
        sweep(False)

    def normalize(qi, carry):
        for g in range(AT_HG):
            acc = acc_ref[qi, g]
            o_ref[qi, g * FX_DH:(g + 1) * FX_DH, :] = (
                acc[:FX_DH] / acc[FX_DH:FX_DH + 1]).astype(o_ref.dtype)
        return carry

    lax.fori_loop(0, nblk, normalize, 0)


def _attention(order, r_flat, q_aug, k_aug, v_t):
    bsz, s, _ = q_aug.shape
    nblk = s // AT_BLK
    causal = np.where(np.arange(AT_BLK)[:, None] <= np.arange(AT_BLK)[None, :], 0.0, NEG_BIG)
    mask = jnp.asarray(np.stack([np.zeros_like(causal), causal]), F32)

    def head_cols(g):
        return pl.BlockSpec((None, s, LANE), lambda b, h, order: (b, 0, order[h * AT_HG + g]))

    def head_vt(g):
        return pl.BlockSpec((None, nblk, VT_ROWS, AT_BLK),
                            lambda b, h, order: (b, 0, order[h * AT_HG + g], 0))

    grid_spec = pltpu.PrefetchScalarGridSpec(
        num_scalar_prefetch=1,
        grid=(bsz, FX_HEADS // AT_HG),
        in_specs=[
            pl.BlockSpec(memory_space=pltpu.SMEM),
            pl.BlockSpec((2, AT_BLK, AT_BLK), lambda b, h, order: (0, 0, 0)),
            *[head_cols(g) for g in range(AT_HG)],
            *[head_cols(g) for g in range(AT_HG)],
            *[head_vt(g) for g in range(AT_HG)],
        ],
        out_specs=pl.BlockSpec((None, nblk, AT_HG * FX_DH, AT_BLK),
                               lambda b, h, order: (b, 0, h, 0)),
        scratch_shapes=[
            pltpu.VMEM((AT_SLOTS, AT_HG, AT_BLK, AT_BLK), F32),
            pltpu.VMEM((AT_SLOTS, AT_HG, AT_BLK, AT_BLK), BF16),
            pltpu.VMEM((nblk, AT_HG, VT_ROWS, AT_BLK), F32),
            pltpu.VMEM((AT_HG, 1, AT_BLK), F32),
            pltpu.VMEM((AT_HG, 1, AT_BLK), F32),
            pltpu.SMEM((nblk,), jnp.int32),
        ],
    )
    return pl.pallas_call(
        functools.partial(_attn_kernel, nblk=nblk, n_r=r_flat.shape[0] - 2),
        out_shape=jax.ShapeDtypeStruct((bsz, nblk, FX_WIDTH, AT_BLK), BF16),
        grid_spec=grid_spec,
        compiler_params=pltpu.CompilerParams(
            dimension_semantics=("arbitrary", "arbitrary"),
            vmem_limit_bytes=VMEM_LIMIT),
        name="fox_attention",
    )(order, r_flat, mask, *([q_aug] * AT_HG), *([k_aug] * AT_HG), *([v_t] * AT_HG))


FF_TM = 1024
FF_TF = D_FF
FF_CHUNK = 1024


def _ffn_kernel(x_ref, oa_ref, obt_ref, woa_ref, wob_ref, nw_ref, wu_ref, wd_ref,
                o_ref, h_ref):
    @pl.when(pl.program_id(2) == 0)
    def _():
        o_b = jnp.concatenate([_dot_tn(obt_ref[j], wob_ref[...])
                               for j in range(FF_TM // AT_BLK)], axis=0)
        x1 = x_ref[...] + _dot(oa_ref[...], woa_ref[...]) + o_b
        ms = jnp.mean(x1 * x1, axis=-1, keepdims=True)
        h_ref[...] = (x1 * lax.rsqrt(ms + NORM_EPS) * nw_ref[...]).astype(BF16)
        o_ref[...] = x1

    for c in range(FF_TF // FF_CHUNK):
        cols = slice(c * FF_CHUNK, (c + 1) * FF_CHUNK)
        u = jnp.maximum(_dot(h_ref[...], wu_ref[:, cols]), 0.0)
        o_ref[...] += _dot((u * u).astype(BF16), wd_ref[cols, :])


def _outproj_ffn(x3d, o_a, o_bt, w_out, w_out_fox, norm_w, w_up, w_down, layer):
    bsz, s, _ = x3d.shape
    resident = dict(pipeline_mode=pl.Buffered(1)) if FF_TF == D_FF else {}
    return pl.pallas_call(
        _ffn_kernel,
        out_shape=jax.ShapeDtypeStruct(x3d.shape, F32),
        grid=(bsz, s // FF_TM, D_FF // FF_TF),
        in_specs=[
            pl.BlockSpec((None, FF_TM, D_MODEL), lambda b, i, j: (b, i, 0)),
            pl.BlockSpec((None, FF_TM, HG_WIDTH), lambda b, i, j: (b, i, 0)),
            pl.BlockSpec((None, FF_TM // AT_BLK, FX_WIDTH, AT_BLK), lambda b, i, j: (b, i, 0, 0)),
            pl.BlockSpec((None, HG_WIDTH, D_MODEL), lambda b, i, j: (layer, 0, 0), **resident),
            pl.BlockSpec((FX_WIDTH, D_MODEL), lambda b, i, j: (0, 0), **resident),
            pl.BlockSpec((1, D_MODEL), lambda b, i, j: (0, 0)),
            pl.BlockSpec((None, D_MODEL, FF_TF), lambda b, i, j: (layer, 0, j), **resident),
            pl.BlockSpec((None, FF_TF, D_MODEL), lambda b, i, j: (layer, j, 0), **resident),
        ],
        out_specs=pl.BlockSpec((None, FF_TM, D_MODEL), lambda b, i, j: (b, i, 0)),
        scratch_shapes=[pltpu.VMEM((FF_TM, D_MODEL), BF16)],
        compiler_params=pltpu.CompilerParams(
            dimension_semantics=("arbitrary", "arbitrary", "arbitrary"),
            vmem_limit_bytes=VMEM_LIMIT),
        name="outproj_ffn",
    )(x3d, o_a, o_bt, w_out, w_out_fox, norm_w, w_up, w_down)


def _layer_lower_bounds(lower_bounds):
    p = jax.nn.softmax(lower_bounds.astype(F32), axis=0)
    c = jnp.cumsum(p, axis=0)
    return c - c[0:1]


def kernel(x, lower_bounds, norm1_w, w_in, fox_f_bias, q_norm_w, k_norm_w,
           hgrn_norm_w, w_out, norm2_w, w_up, w_down):
    bsz, s, d = x.shape
    depth = w_in.shape[0]
    nblk = s // AT_BLK
    lbs = _layer_lower_bounds(lower_bounds)
    w_in_b = w_in.astype(BF16)
    w_in_ff = jnp.pad(w_in_b[:, :, PROJ_MAIN:], ((0, 0), (0, 0), (0, PROJ_PAD - w_in.shape[-1])))
    w_out_b = w_out.astype(BF16)
    w_up_b = w_up.astype(BF16)
    w_down_b = w_down.astype(BF16)
    f_bias = jnp.pad(fox_f_bias, ((0, 0), (0, LANE - FX_HEADS)))

    for l in range(depth):
        proj, q_aug, k_aug, v_t, r_blk = _inproj_prep(
            x.reshape(bsz * s, d), s, norm1_w[l][None, :], w_in_b, w_in_ff, l, f_bias[l][None, :],
            jnp.tile(q_norm_w[l] * (FX_DH ** -0.5 * LOG2E), FX_HEADS)[None, :],
            jnp.tile(k_norm_w[l], FX_HEADS)[None, :])
        width = FX_HEADS * LANE
        q_aug = q_aug.reshape(bsz, s, width)
        k_aug = k_aug.reshape(bsz, s, width)
        v_t = v_t.reshape(bsz, nblk, FX_HEADS * VT_ROWS, AT_BLK)
        r_blk = r_blk.reshape(bsz, nblk, 8, LANE)
        o_a = _hgrn(proj.reshape(bsz, s, HG_COLS), lbs[l][None, :], hgrn_norm_w[l][None, :])
        qk_bound = (1.02 * FX_DH ** 0.5 * LOG2E * jnp.max(jnp.abs(q_norm_w[l]))
                    * jnp.max(jnp.abs(k_norm_w[l])))
        skip_thr = -(F32_EXP2_ZERO + 2.0 * qk_bound)
        r_flat = jnp.concatenate([
            jnp.transpose(r_blk[:, :, 0, :FX_HEADS], (0, 2, 1)).reshape(-1),
            skip_thr.reshape(1).astype(F32), qk_bound.reshape(1).astype(F32)])
        order = jnp.argsort(fox_f_bias[l]).astype(jnp.int32)
        use_fixed = (qk_bound <= FIXED_STABILISER_MAX).astype(jnp.int32).reshape(1)
        o_bt = _attention(jnp.concatenate([order, use_fixed]), r_flat, q_aug, k_aug, v_t)
        fox_rows = (HG_WIDTH + order[:, None] * FX_DH + jnp.arange(FX_DH)[None, :]).reshape(-1)
        x = _outproj_ffn(x, o_a, o_bt, w_out_b, w_out_b[l][fox_rows], norm2_w[l][None, :],
                         w_up_b, w_down_b, l)
    return x
```

```python
import functools

import numpy as np
import jax
import jax.numpy as jnp
from jax import lax
from jax.experimental import pallas as pl
from jax.experimental.pallas import tpu as pltpu

F32 = jnp.float32
BF16 = jnp.bfloat16

D_MODEL = 1024
D_FF = 4 * D_MODEL
NORM_EPS = 1e-6

HG_WIDTH = 512
HG_HEADS = 4
HG_DK = 128
HG_CHUNK = 64
HG_LEVELS = 6

FX_WIDTH = 512
FX_HEADS = 8
FX_DH = 64

PROJ_MAIN = 4 * HG_WIDTH + 3 * FX_WIDTH
LANE = 128
PROJ_PAD = PROJ_MAIN + LANE

VMEM_LIMIT = 56 * 1024 * 1024

NEG_BIG = -1e30
LOG2E = 1.4426950408889634

NT_DIMS = (((1,), (1,)), ((), ()))
TN_DIMS = (((0,), (0,)), ((), ()))


def _dot(a, b):
    return jnp.dot(a, b, preferred_element_type=F32)


def _dot_nt(a, b):
    return lax.dot_general(a, b, NT_DIMS, preferred_element_type=F32)


def _dot_tn(a, b):
    return lax.dot_general(a, b, TN_DIMS, preferred_element_type=F32)


def _sigmoid(x):
    return 0.5 * jnp.tanh(0.5 * x) + 0.5


def _split2(x):
    hi = x.astype(BF16)
    lo = (x - hi.astype(F32)).astype(BF16)
    return hi, lo


def _split3(x):
    hi = x.astype(BF16)
    r = x - hi.astype(F32)
    mid = r.astype(BF16)
    lo = (r - mid.astype(F32)).astype(BF16)
    return hi, mid, lo


IN_TM = 1024
IN_TN = 512
HG_COLS = 4 * HG_WIDTH


HG_ROWS = 2048
HG_NCHUNK = HG_ROWS // HG_CHUNK
HG_E_ROWS = (2 + HG_LEVELS) * HG_CHUNK


def _hgrn_constants():
    c = HG_CHUNK
    tri = np.tril(np.ones((c, c), np.float32))
    rows = [tri, 1.0 - tri]
    idx = np.arange(c)
    for lvl in range(HG_LEVELS):
        m = c >> (lvl + 1)
        ref = (idx // (2 * m)) * (2 * m) + m - 1
        upper = (idx % (2 * m) >= m)[:, None]
        diff = tri - tri[ref]
        rows.append(np.where(upper, diff, -diff))
    mat = np.concatenate(rows, axis=0)
    mcat = np.concatenate([mat, mat], axis=1)
    x = idx[:, None] ^ idx[None, :]
    top = np.floor(np.log2(np.maximum(x, 1))).astype(np.int32)
    level = (HG_LEVELS - 1) - top
    level = np.where(idx[:, None] == idx[None, :], HG_LEVELS, level)
    level = np.where(idx[:, None] < idx[None, :], -1, level)
    level = np.concatenate([level, level], axis=1)
    return mcat.astype(np.float32), level.astype(np.int32)


def _stack_heads(t, heads):
    return jnp.concatenate([t[:, h * HG_DK:(h + 1) * HG_DK] for h in heads], axis=0)


def _hgrn_kernel(q_ref, f_ref, i_ref, g_ref, lb_ref, nw_ref, mcat_ref, lvl_ref,
                 o_ref, st_ref):
    @pl.when(pl.program_id(1) == 0)
    def _():
        st_ref[...] = jnp.zeros_like(st_ref)

    lb = lb_ref[...]
    nw = nw_ref[...]
    mcat = mcat_ref[...]
    level = lvl_ref[...]
    scale = HG_DK ** -0.5

    def chunk(c, carry):
        r0 = pl.multiple_of(c * HG_CHUNK, HG_CHUNK)
        rows = pl.ds(r0, HG_CHUNK)
        fp = f_ref[rows, :].astype(F32)
        f = lb + (1.0 - lb) * _sigmoid(fp)
        lf = jnp.log(f) * LOG2E
        kk = 1.0 - f
        hi, lo = _split2(lf)
        e_all = _dot(mcat, jnp.concatenate([hi, lo], axis=0))
        b = e_all[0:HG_CHUNK]
        qs = q_ref[rows, :].astype(F32) * scale
        q_in = (qs * jnp.exp2(b)).astype(BF16)
        k_out = (kk * jnp.exp2(e_all[HG_CHUNK:2 * HG_CHUNK])).astype(BF16)
        dec = jnp.exp2(b[HG_CHUNK - 1:HG_CHUNK, :])
        qb = qs.astype(BF16)
        kb = kk.astype(BF16)
        v = i_ref[rows, :]
        g = g_ref[rows, :].astype(F32)

        a = [jnp.zeros((HG_CHUNK, LANE), F32) for _ in range(HG_HEADS)]
        for lvl in range(HG_LEVELS + 1):
            if lvl < HG_LEVELS:
                x = jnp.exp2(e_all[(2 + lvl) * HG_CHUNK:(3 + lvl) * HG_CHUNK]).astype(BF16)
                ql, kl = qb * x, kb * x
            else:
                ql, kl = qb, kb
            for pair in range(HG_HEADS // 2):
                heads = (2 * pair, 2 * pair + 1)
                d = _dot_nt(_stack_heads(ql, heads), _stack_heads(kl, heads))
                for i, h in enumerate(heads):
                    blk = d[i * HG_CHUNK:(i + 1) * HG_CHUNK, :]
                    a[h] = jnp.where(level == lvl, blk, a[h])

        zeros = jnp.zeros((HG_CHUNK, HG_DK), BF16)
        for h in range(HG_HEADS):
            ls = slice(h * HG_DK, (h + 1) * HG_DK)
            v_pad = jnp.concatenate([v[:, ls], zeros] if h % 2 == 0 else [zeros, v[:, ls]], axis=0)
            st = st_ref[h]
            o = _dot(a[h].astype(BF16), v_pad) + _dot_nt(q_in[:, ls], st.astype(BF16))
            st_ref[h] = st * dec[:, ls] + _dot_tn(v[:, ls], k_out[:, ls])
            var = jnp.mean(o * o, axis=-1, keepdims=True)
            gh = g[:, ls]
            y = o * lax.rsqrt(var + NORM_EPS) * nw * (gh * _sigmoid(gh))
            o_ref[rows, ls] = y.astype(o_ref.dtype)
        return carry

    lax.fori_loop(0, HG_NCHUNK, chunk, 0, unroll=HG_NCHUNK)


def _hgrn(proj3d, lb, norm_w):
    bsz, s, _ = proj3d.shape
    mcat, level = _hgrn_constants()

    def col(j):
        return pl.BlockSpec((None, HG_ROWS, HG_WIDTH), lambda b, i, j=j: (b, i, j))

    return pl.pallas_call(
        _hgrn_kernel,
        out_shape=jax.ShapeDtypeStruct((bsz, s, HG_WIDTH), BF16),
        grid=(bsz, s // HG_ROWS),
        in_specs=[
            col(0), col(1), col(2), col(3),
            pl.BlockSpec((1, HG_WIDTH), lambda b, i: (0, 0)),
            pl.BlockSpec((1, HG_DK), lambda b, i: (0, 0)),
            pl.BlockSpec((HG_E_ROWS, 2 * HG_CHUNK), lambda b, i: (0, 0)),
            pl.BlockSpec((HG_CHUNK, 2 * HG_CHUNK), lambda b, i: (0, 0)),
        ],
        out_specs=pl.BlockSpec((None, HG_ROWS, HG_WIDTH), lambda b, i: (b, i, 0)),
        scratch_shapes=[pltpu.VMEM((HG_HEADS, HG_DK, HG_DK), F32)],
        compiler_params=pltpu.CompilerParams(
            dimension_semantics=("arbitrary", "arbitrary"), vmem_limit_bytes=VMEM_LIMIT),
        name="hgrn2",
    )(proj3d, proj3d, proj3d, proj3d, lb, norm_w,
      jnp.asarray(mcat, BF16), jnp.asarray(level))


AT_BLK = 256
PREP_SUB = IN_TM // AT_BLK
AUG_PIECES = 3
VT_ROWS = 80


def _prep_constants():
    grp = np.zeros((FX_WIDTH // 2, FX_WIDTH // 2), np.float32)
    for h in range(FX_HEADS // 2):
        grp[h * FX_DH:(h + 1) * FX_DH, h * FX_DH:(h + 1) * FX_DH] = 1.0 / FX_DH
    tri = np.tril(np.ones((AT_BLK, AT_BLK), np.float32))
    ones_at = FX_HEADS * AUG_PIECES
    pk = np.zeros((LANE, FX_WIDTH), np.float32)
    cq = np.zeros((1, LANE), np.float32)
    ck = np.zeros((1, FX_WIDTH), np.float32)
    for p in range(AUG_PIECES):
        cq[0, ones_at + p] = 1.0
        for h in range(FX_HEADS):
            base = (h // 2) * LANE + (FX_DH if h % 2 == 0 else 0)
            ck[0, base + FX_HEADS * p + h] = 1.0
            pk[FX_HEADS * p + h, base + ones_at + p] = -1.0
    return grp, tri, pk, cq, ck


def _inproj_prep_kernel(x_ref, nw_ref, w_ref, wff_ref, fb_ref, qnw_ref, knw_ref,
                        grp_ref, tri_ref, pk_ref, cq_ref, ck_ref,
                        proj_ref, qa_ref, ka_ref, vt_ref, r_ref, carry_ref, *, steps_per_seq):
    @pl.when(pl.program_id(0) % steps_per_seq == 0)
    def _():
        carry_ref[...] = jnp.zeros_like(carry_ref)

    x = x_ref[...]
    ms = jnp.mean(x * x, axis=-1, keepdims=True)
    xn = (x * lax.rsqrt(ms + NORM_EPS) * nw_ref[...]).astype(BF16)

    def proj(c):
        return _dot(xn, w_ref[:, c * IN_TN:(c + 1) * IN_TN])

    hg_groups = list(range(HG_COLS // IN_TN))
    q_all, k_all = (proj(len(hg_groups) + c) for c in range(2))
    v_cols = slice(PROJ_MAIN - IN_TN, PROJ_MAIN)
    vz = _dot(xn, jnp.concatenate([w_ref[:, v_cols], wff_ref[...]], axis=1))
    v_all = vz[:, :IN_TN]

    z = vz[:, IN_TN:] + fb_ref[...]
    lf = (jnp.minimum(z, 0.0) - jnp.log(1.0 + jnp.exp(-jnp.abs(z)))) * LOG2E
    grp = grp_ref[...]
    tri = tri_ref[...]
    lane = lax.broadcasted_iota(jnp.int32, (AT_BLK, LANE), 1)
    low_half = lane < FX_DH
    head_lane = lane < FX_HEADS

    def qk_norm(t, w):
        sq = (t * t).astype(BF16)
        half = FX_WIDTH // 2
        ms = jnp.concatenate([_dot(sq[:, :half], grp), _dot(sq[:, half:], grp)], axis=1)
        return t * lax.rsqrt(ms + NORM_EPS) * w

    def pack_pieces(t):
        packed = None
        for p, piece in enumerate(_split3(t)):
            piece = jnp.where(head_lane, piece.astype(F32), 0.0)
            piece = piece if p == 0 else pltpu.roll(piece, FX_HEADS * p, axis=1)
            packed = piece if packed is None else packed + piece
        return packed

    for j in range(PREP_SUB):
        rows = slice(j * AT_BLK, (j + 1) * AT_BLK)
        sums = _dot(tri, jnp.concatenate(_split3(lf[rows]), axis=1))
        c_rel = sums[:, :LANE] + sums[:, LANE:2 * LANE] + sums[:, 2 * LANE:]
        r_ref[j] = carry_ref[...]
        carry_ref[...] = carry_ref[...] + c_rel[AT_BLK - 1:AT_BLK, :]
        packed = pack_pieces(c_rel)
        aug_q_odd = packed + cq_ref[...]
        aug_q_even = pltpu.roll(aug_q_odd, FX_DH, axis=1)
        aug_k = _dot(packed.astype(BF16), pk_ref[...]) + ck_ref[...]
        qn = qk_norm(q_all[rows], qnw_ref[...])
        kn = qk_norm(k_all[rows], knw_ref[...])
        for h in range(FX_HEADS):
            pair = slice((h // 2) * LANE, (h // 2 + 1) * LANE)
            head = slice(h * LANE, (h + 1) * LANE)
            data = low_half if h % 2 == 0 else jnp.logical_not(low_half)
            aug_q = aug_q_even if h % 2 == 0 else aug_q_odd
            qa_ref[rows, head] = jnp.where(data, qn[:, pair], aug_q).astype(BF16)
            ka_ref[rows, head] = jnp.where(data, kn[:, pair], aug_k[:, pair]).astype(BF16)
        v_t = v_all[rows].T.astype(BF16)
        for h in range(FX_HEADS):
            vt_ref[j, h * VT_ROWS:h * VT_ROWS + FX_DH, :] = v_t[h * FX_DH:(h + 1) * FX_DH]
            vt_ref[j, h * VT_ROWS + FX_DH:(h + 1) * VT_ROWS, :] = jnp.ones(
                (VT_ROWS - FX_DH, AT_BLK), BF16)
        for c in hg_groups[j::PREP_SUB]:
            proj_ref[:, c * IN_TN:(c + 1) * IN_TN] = proj(c).astype(BF16)


def _inproj_prep(x2d, seq_len, norm_w, w_all, w_ff, layer, f_bias, qnw, knw):
    t = x2d.shape[0]
    consts = _prep_constants()
    grp, tri, pk = (jnp.asarray(c, BF16) for c in consts[:3])
    cq, ck = (jnp.asarray(c, F32) for c in consts[3:])
    width = FX_HEADS * LANE

    def full(shape):
        return pl.BlockSpec(shape, lambda i: (0,) * len(shape))

    return pl.pallas_call(
        functools.partial(_inproj_prep_kernel, steps_per_seq=seq_len // IN_TM),
        out_shape=(jax.ShapeDtypeStruct((t, HG_COLS), BF16),
                   jax.ShapeDtypeStruct((t, width), BF16),
                   jax.ShapeDtypeStruct((t, width), BF16),
                   jax.ShapeDtypeStruct((t // AT_BLK, FX_HEADS * VT_ROWS, AT_BLK), BF16),
                   jax.ShapeDtypeStruct((t // AT_BLK, 8, LANE), F32)),
        grid=(t // IN_TM,),
        in_specs=[
            pl.BlockSpec((IN_TM, D_MODEL), lambda i: (i, 0)),
            full((1, D_MODEL)),
            pl.BlockSpec((None, D_MODEL, PROJ_MAIN), lambda i: (layer, 0, 0),
                         pipeline_mode=pl.Buffered(1)),
            pl.BlockSpec((None, D_MODEL, LANE), lambda i: (layer, 0, 0)),
            full((1, LANE)), full((1, FX_WIDTH)), full((1, FX_WIDTH)),
            full((FX_WIDTH // 2, FX_WIDTH // 2)), full((AT_BLK, AT_BLK)),
            full((LANE, FX_WIDTH)), full((1, LANE)), full((1, FX_WIDTH)),
        ],
        out_specs=(pl.BlockSpec((IN_TM, HG_COLS), lambda i: (i, 0)),
                   pl.BlockSpec((IN_TM, width), lambda i: (i, 0)),
                   pl.BlockSpec((IN_TM, width), lambda i: (i, 0)),
                   pl.BlockSpec((PREP_SUB, FX_HEADS * VT_ROWS, AT_BLK), lambda i: (i, 0, 0)),
                   pl.BlockSpec((PREP_SUB, 8, LANE), lambda i: (i, 0, 0))),
        scratch_shapes=[pltpu.VMEM((8, LANE), F32)],
        compiler_params=pltpu.CompilerParams(
            dimension_semantics=("arbitrary",), vmem_limit_bytes=VMEM_LIMIT),
        name="inproj_prep",
    )(x2d, norm_w, w_all, w_ff, f_bias, qnw, knw, grp, tri, pk, cq, ck)


AT_HG = 2
AT_SLOTS = 3
AT_TRIP_TICKS = 4 * AT_SLOTS
F32_EXP2_ZERO = 150.0
FIXED_STABILISER_MAX = 32.0
SUBLANES = 8


def _tree_reduce(op, final, x):
    while x.shape[0] > SUBLANES:
        half = x.shape[0] // 2
        x = op(x[:half], x[half:])
    return final(x, axis=0, keepdims=True)


def _attn_kernel(order_ref, r_ref, mask_ref, *refs, nblk, n_r):
    q_refs, k_refs, vt_refs = (refs[i * AT_HG:(i + 1) * AT_HG] for i in range(3))
    o_ref, s_ref, p_ref, acc_ref, m_ref, a_ref, start_ref = refs[3 * AT_HG:]
    b = pl.program_id(0)
    hg = pl.program_id(1)
    bases = [(b * FX_HEADS + order_ref[hg * AT_HG + g]) * nblk for g in range(AT_HG)]
    last = nblk - 1

    thr = r_ref[n_r]

    def find_start(qi, n_items):
        def needed(first):
            hit = None
            for base in bases:
                c = r_ref[base + qi] - r_ref[base + first] >= thr
                hit = c if hit is None else jnp.logical_or(hit, c)
            return hit

        first = lax.while_loop(lambda f: jnp.logical_and(f > 0, needed(f)),
                               lambda f: f - 1, qi)
        start_ref[qi] = first
        return n_items + (qi - first + 1)

    n_items = lax.fori_loop(0, nblk, find_start, 0)

    def next_item(item):
        qi, ki, valid = item
        at_diag = ki == qi
        more = jnp.logical_and(valid != 0,
                               jnp.logical_not(jnp.logical_and(at_diag, qi == last)))
        qn = jnp.where(at_diag, jnp.minimum(qi + 1, last), qi)
        kn = jnp.where(at_diag, start_ref[qn], ki + 1)
        return (jnp.where(more, qn, last), jnp.where(more, kn, last), more.astype(jnp.int32))

    def stage_a(slot, item):
        qi, ki, _ = item
        rows_q = pl.ds(pl.multiple_of(qi * AT_BLK, AT_BLK), AT_BLK)
        rows_k = pl.ds(pl.multiple_of(ki * AT_BLK, AT_BLK), AT_BLK)
        mask = mask_ref[(ki == qi).astype(jnp.int32)]
        for g in range(AT_HG):
            s_ref[slot, g] = _dot_nt(k_refs[g][rows_k, :], q_refs[g][rows_q, :]) + mask

    qk_bound = r_ref[n_r + 1]

    def stage_b(slot, item, fixed):
        qi, ki, valid = item
        live = valid != 0
        first = jnp.logical_and(live, ki == start_ref[qi])
        for g in range(AT_HG):
            delta = jnp.where(live, r_ref[bases[g] + qi] - r_ref[bases[g] + ki], NEG_BIG)
            s = s_ref[slot, g]
            if fixed:
                p_ref[slot, g] = jnp.exp2(s + (delta - qk_bound)).astype(BF16)
            else:
                m = jnp.where(first, NEG_BIG, m_ref[g])
                m_new = jnp.maximum(m, _tree_reduce(jnp.maximum, jnp.max, s) + delta)
                p_ref[slot, g] = jnp.exp2(s - (m_new - delta)).astype(BF16)
                a_ref[g] = jnp.exp2(m - m_new)
                m_ref[g] = m_new

    def stage_c(slot, item, fixed):
        qi, ki, _ = item
        for g in range(AT_HG):
            pv = _dot(vt_refs[g][ki], p_ref[slot, g])
            acc_ref[qi, g] = acc_ref[qi, g] + pv if fixed else a_ref[g] * acc_ref[qi, g] + pv

    def tick(slot, items, fixed):
        item_c, item_b, item_s, item_a = items
        ahead = (slot + 2) % AT_SLOTS
        stage_c(ahead, item_c, fixed)
        stage_b(slot, item_b, fixed)
        stage_a(ahead, item_a)
        return (item_b, item_s, item_a, next_item(item_a))

    zero = jnp.int32(0)
    item0 = (zero, zero, jnp.int32(1))
    item1 = next_item(item0)
    idle = (zero, zero, zero)
    stage_a(0, item0)
    stage_a(1, item1)

    def sweep(fixed):
        for g in range(AT_HG):
            p_ref[AT_SLOTS - 1, g] = jnp.zeros((AT_BLK, AT_BLK), BF16)
            a_ref[g] = jnp.ones((1, AT_BLK), F32)
            m_ref[g] = jnp.zeros((1, AT_BLK), F32)
        acc_ref[...] = jnp.zeros(acc_ref.shape, F32) if fixed else jnp.ones(acc_ref.shape, F32)

        def trip(_, items):
            for t in range(AT_TRIP_TICKS):
                items = tick(t % AT_SLOTS, items, fixed)
            return items

        lax.fori_loop(0, (n_items + AT_TRIP_TICKS) // AT_TRIP_TICKS, trip,
                      (idle, item0, item1, next_item(item1)))

    use_fixed = order_ref[FX_HEADS] != 0

    @pl.when(use_fixed)
    def _():
        sweep(True)

    @pl.when(jnp.logical_not(use_fixed))
    def _():
        sweep(False)

    def normalize(qi, carry):
        for g in range(AT_HG):
            acc = acc_ref[qi, g]
            o_ref[qi, g * FX_DH:(g + 1) * FX_DH, :] = (
                acc[:FX_DH] / acc[FX_DH:FX_DH + 1]).astype(o_ref.dtype)
        return carry

    lax.fori_loop(0, nblk, normalize, 0)


def _attention(order, r_flat, q_aug, k_aug, v_t):
    bsz, s, _ = q_aug.shape
    nblk = s // AT_BLK
    causal = np.where(np.arange(AT_BLK)[:, None] <= np.arange(AT_BLK)[None, :], 0.0, NEG_BIG)
    mask = jnp.asarray(np.stack([np.zeros_like(causal), causal]), F32)

    def head_cols(g):
        return pl.BlockSpec((None, s, LANE), lambda b, h, order: (b, 0, order[h * AT_HG + g]))

    def head_vt(g):
        return pl.BlockSpec((None, nblk, VT_ROWS, AT_BLK),
                            lambda b, h, order: (b, 0, order[h * AT_HG + g], 0))

    grid_spec = pltpu.PrefetchScalarGridSpec(
        num_scalar_prefetch=1,
        grid=(bsz, FX_HEADS // AT_HG),
        in_specs=[
            pl.BlockSpec(memory_space=pltpu.SMEM),
            pl.BlockSpec((2, AT_BLK, AT_BLK), lambda b, h, order: (0, 0, 0)),
            *[head_cols(g) for g in range(AT_HG)],
            *[head_cols(g) for g in range(AT_HG)],
            *[head_vt(g) for g in range(AT_HG)],
        ],
        out_specs=pl.BlockSpec((None, nblk, AT_HG * FX_DH, AT_BLK),
                               lambda b, h, order: (b, 0, h, 0)),
        scratch_shapes=[
            pltpu.VMEM((AT_SLOTS, AT_HG, AT_BLK, AT_BLK), F32),
            pltpu.VMEM((AT_SLOTS, AT_HG, AT_BLK, AT_BLK), BF16),
            pltpu.VMEM((nblk, AT_HG, VT_ROWS, AT_BLK), F32),
            pltpu.VMEM((AT_HG, 1, AT_BLK), F32),
            pltpu.VMEM((AT_HG, 1, AT_BLK), F32),
            pltpu.SMEM((nblk,), jnp.int32),
        ],
    )
    return pl.pallas_call(
        functools.partial(_attn_kernel, nblk=nblk, n_r=r_flat.shape[0] - 2),
        out_shape=jax.ShapeDtypeStruct((bsz, nblk, FX_WIDTH, AT_BLK), BF16),
        grid_spec=grid_spec,
        compiler_params=pltpu.CompilerParams(
            dimension_semantics=("arbitrary", "arbitrary"),
            vmem_limit_bytes=VMEM_LIMIT),
        name="fox_attention",
    )(order, r_flat, mask, *([q_aug] * AT_HG), *([k_aug] * AT_HG), *([v_t] * AT_HG))


FF_TM = 1024
FF_TF = D_FF
FF_CHUNK = 1024


def _ffn_kernel(x_ref, oa_ref, obt_ref, woa_ref, wob_ref, nw_ref, wu_ref, wd_ref,
                o_ref, h_ref):
    @pl.when(pl.program_id(2) == 0)
    def _():
        o_b = jnp.concatenate([_dot_tn(obt_ref[j], wob_ref[...])
                               for j in range(FF_TM // AT_BLK)], axis=0)
        x1 = x_ref[...] + _dot(oa_ref[...], woa_ref[...]) + o_b
        ms = jnp.mean(x1 * x1, axis=-1, keepdims=True)
        h_ref[...] = (x1 * lax.rsqrt(ms + NORM_EPS) * nw_ref[...]).astype(BF16)
        o_ref[...] = x1

    for c in range(FF_TF // FF_CHUNK):
        cols = slice(c * FF_CHUNK, (c + 1) * FF_CHUNK)
        u = jnp.maximum(_dot(h_ref[...], wu_ref[:, cols]), 0.0)
        o_ref[...] += _dot((u * u).astype(BF16), wd_ref[cols, :])


def _outproj_ffn(x3d, o_a, o_bt, w_out, w_out_fox, norm_w, w_up, w_down, layer):
    bsz, s, _ = x3d.shape
    resident = dict(pipeline_mode=pl.Buffered(1)) if FF_TF == D_FF else {}
    return pl.pallas_call(
        _ffn_kernel,
        out_shape=jax.ShapeDtypeStruct(x3d.shape, F32),
        grid=(bsz, s // FF_TM, D_FF // FF_TF),
        in_specs=[
            pl.BlockSpec((None, FF_TM, D_MODEL), lambda b, i, j: (b, i, 0)),
            pl.BlockSpec((None, FF_TM, HG_WIDTH), lambda b, i, j: (b, i, 0)),
            pl.BlockSpec((None, FF_TM // AT_BLK, FX_WIDTH, AT_BLK), lambda b, i, j: (b, i, 0, 0)),
            pl.BlockSpec((None, HG_WIDTH, D_MODEL), lambda b, i, j: (layer, 0, 0), **resident),
            pl.BlockSpec((FX_WIDTH, D_MODEL), lambda b, i, j: (0, 0), **resident),
            pl.BlockSpec((1, D_MODEL), lambda b, i, j: (0, 0)),
            pl.BlockSpec((None, D_MODEL, FF_TF), lambda b, i, j: (layer, 0, j), **resident),
            pl.BlockSpec((None, FF_TF, D_MODEL), lambda b, i, j: (layer, j, 0), **resident),
        ],
        out_specs=pl.BlockSpec((None, FF_TM, D_MODEL), lambda b, i, j: (b, i, 0)),
        scratch_shapes=[pltpu.VMEM((FF_TM, D_MODEL), BF16)],
        compiler_params=pltpu.CompilerParams(
            dimension_semantics=("arbitrary", "arbitrary", "arbitrary"),
            vmem_limit_bytes=VMEM_LIMIT),
        name="outproj_ffn",
    )(x3d, o_a, o_bt, w_out, w_out_fox, norm_w, w_up, w_down)


def _layer_lower_bounds(lower_bounds):
    p = jax.nn.softmax(lower_bounds.astype(F32), axis=0)
    c = jnp.cumsum(p, axis=0)
    return c - c[0:1]


def kernel(x, lower_bounds, norm1_w, w_in, fox_f_bias, q_norm_w, k_norm_w,
           hgrn_norm_w, w_out, norm2_w, w_up, w_down):
    bsz, s, d = x.shape
    depth = w_in.shape[0]
    nblk = s // AT_BLK
    lbs = _layer_lower_bounds(lower_bounds)
    w_in_b = w_in.astype(BF16)
    w_in_ff = jnp.pad(w_in_b[:, :, PROJ_MAIN:], ((0, 0), (0, 0), (0, PROJ_PAD - w_in.shape[-1])))
    w_out_b = w_out.astype(BF16)
    w_up_b = w_up.astype(BF16)
    w_down_b = w_down.astype(BF16)
    f_bias = jnp.pad(fox_f_bias, ((0, 0), (0, LANE - FX_HEADS)))

    for l in range(depth):
        proj, q_aug, k_aug, v_t, r_blk = _inproj_prep(
            x.reshape(bsz * s, d), s, norm1_w[l][None, :], w_in_b, w_in_ff, l, f_bias[l][None, :],
            jnp.tile(q_norm_w[l] * (FX_DH ** -0.5 * LOG2E), FX_HEADS)[None, :],
            jnp.tile(k_norm_w[l], FX_HEADS)[None, :])
        width = FX_HEADS * LANE
        q_aug = q_aug.reshape(bsz, s, width)
        k_aug = k_aug.reshape(bsz, s, width)
        v_t = v_t.reshape(bsz, nblk, FX_HEADS * VT_ROWS, AT_BLK)
        r_blk = r_blk.reshape(bsz, nblk, 8, LANE)
        o_a = _hgrn(proj.reshape(bsz, s, HG_COLS), lbs[l][None, :], hgrn_norm_w[l][None, :])
        qk_bound = (1.02 * FX_DH ** 0.5 * LOG2E * jnp.max(jnp.abs(q_norm_w[l]))
                    * jnp.max(jnp.abs(k_norm_w[l])))
        skip_thr = -(F32_EXP2_ZERO + 2.0 * qk_bound)
        r_flat = jnp.concatenate([
            jnp.transpose(r_blk[:, :, 0, :FX_HEADS], (0, 2, 1)).reshape(-1),
            skip_thr.reshape(1).astype(F32), qk_bound.reshape(1).astype(F32)])
        order = jnp.argsort(fox_f_bias[l]).astype(jnp.int32)
        use_fixed = (qk_bound <= FIXED_STABILISER_MAX).astype(jnp.int32).reshape(1)
        o_bt = _attention(jnp.concatenate([order, use_fixed]), r_flat, q_aug, k_aug, v_t)
        fox_rows = (HG_WIDTH + order[:, None] * FX_DH + jnp.arange(FX_DH)[None, :]).reshape(-1)
        x = _outproj_ffn(x, o_a, o_bt, w_out_b, w_out_b[l][fox_rows], norm2_w[l][None, :],
                         w_up_b, w_down_b, l)
    return x
```

```python
import functools

import numpy as np
import jax
import jax.numpy as jnp
from jax import lax
from jax.experimental import pallas as pl
from jax.experimental.pallas import tpu as pltpu

F32 = jnp.float32
BF16 = jnp.bfloat16

D_MODEL = 1024
D_FF = 4 * D_MODEL
NORM_EPS = 1e-6

HG_WIDTH = 512
HG_HEADS = 4
HG_DK = 128
HG_CHUNK = 64
HG_LEVELS = 6

FX_WIDTH = 512
FX_HEADS = 8
FX_DH = 64

PROJ_MAIN = 4 * HG_WIDTH + 3 * FX_WIDTH
LANE = 128
PROJ_PAD = PROJ_MAIN + LANE

VMEM_LIMIT = 56 * 1024 * 1024

NEG_BIG = -1e30
LOG2E = 1.4426950408889634

NT_DIMS = (((1,), (1,)), ((), ()))
TN_DIMS = (((0,), (0,)), ((), ()))


def _dot(a, b):
    return jnp.dot(a, b, preferred_element_type=F32)


def _dot_nt(a, b):
    return lax.dot_general(a, b, NT_DIMS, preferred_element_type=F32)


def _dot_tn(a, b):
    return lax.dot_general(a, b, TN_DIMS, preferred_element_type=F32)


def _sigmoid(x):
    return 0.5 * jnp.tanh(0.5 * x) + 0.5


def _split2(x):
    hi = x.astype(BF16)
    lo = (x - hi.astype(F32)).astype(BF16)
    return hi, lo


def _split3(x):
    hi = x.astype(BF16)
    r = x - hi.astype(F32)
    mid = r.astype(BF16)
    lo = (r - mid.astype(F32)).astype(BF16)
    return hi, mid, lo


IN_TM = 1024
IN_TN = 512
HG_COLS = 4 * HG_WIDTH


HG_ROWS = 2048
HG_NCHUNK = HG_ROWS // HG_CHUNK
HG_E_ROWS = (2 + HG_LEVELS) * HG_CHUNK


def _hgrn_constants():
    c = HG_CHUNK
    tri = np.tril(np.ones((c, c), np.float32))
    rows = [tri, 1.0 - tri]
    idx = np.arange(c)
    for lvl in range(HG_LEVELS):
        m = c >> (lvl + 1)
        ref = (idx // (2 * m)) * (2 * m) + m - 1
        upper = (idx % (2 * m) >= m)[:, None]
        diff = tri - tri[ref]
        rows.append(np.where(upper, diff, -diff))
    mat = np.concatenate(rows, axis=0)
    mcat = np.concatenate([mat, mat], axis=1)
    x = idx[:, None] ^ idx[None, :]
    top = np.floor(np.log2(np.maximum(x, 1))).astype(np.int32)
    level = (HG_LEVELS - 1) - top
    level = np.where(idx[:, None] == idx[None, :], HG_LEVELS, level)
    level = np.where(idx[:, None] < idx[None, :], -1, level)
    level = np.concatenate([level, level], axis=1)
    return mcat.astype(np.float32), level.astype(np.int32)


def _stack_heads(t, heads):
    return jnp.concatenate([t[:, h * HG_DK:(h + 1) * HG_DK] for h in heads], axis=0)


def _hgrn_kernel(q_ref, f_ref, i_ref, g_ref, lb_ref, nw_ref, mcat_ref, lvl_ref,
                 o_ref, st_ref):
    @pl.when(pl.program_id(1) == 0)
    def _():
        st_ref[...] = jnp.zeros_like(st_ref)

    lb = lb_ref[...]
    nw = nw_ref[...]
    mcat = mcat_ref[...]
    level = lvl_ref[...]
    scale = HG_DK ** -0.5

    def chunk(c, carry):
        r0 = pl.multiple_of(c * HG_CHUNK, HG_CHUNK)
        rows = pl.ds(r0, HG_CHUNK)
        fp = f_ref[rows, :].astype(F32)
        f = lb + (1.0 - lb) * _sigmoid(fp)
        lf = jnp.log(f) * LOG2E
        kk = 1.0 - f
        hi, lo = _split2(lf)
        e_all = _dot(mcat, jnp.concatenate([hi, lo], axis=0))
        b = e_all[0:HG_CHUNK]
        qs = q_ref[rows, :].astype(F32) * scale
        q_in = (qs * jnp.exp2(b)).astype(BF16)
        k_out = (kk * jnp.exp2(e_all[HG_CHUNK:2 * HG_CHUNK])).astype(BF16)
        dec = jnp.exp2(b[HG_CHUNK - 1:HG_CHUNK, :])
        qb = qs.astype(BF16)
        kb = kk.astype(BF16)
        v = i_ref[rows, :]
        g = g_ref[rows, :].astype(F32)

        a = [jnp.zeros((HG_CHUNK, LANE), F32) for _ in range(HG_HEADS)]
        for lvl in range(HG_LEVELS + 1):
            if lvl < HG_LEVELS:
                x = jnp.exp2(e_all[(2 + lvl) * HG_CHUNK:(3 + lvl) * HG_CHUNK]).astype(BF16)
                ql, kl = qb * x, kb * x
            else:
                ql, kl = qb, kb
            for pair in range(HG_HEADS // 2):
                heads = (2 * pair, 2 * pair + 1)
                d = _dot_nt(_stack_heads(ql, heads), _stack_heads(kl, heads))
                for i, h in enumerate(heads):
                    blk = d[i * HG_CHUNK:(i + 1) * HG_CHUNK, :]
                    a[h] = jnp.where(level == lvl, blk, a[h])

        zeros = jnp.zeros((HG_CHUNK, HG_DK), BF16)
        for h in range(HG_HEADS):
            ls = slice(h * HG_DK, (h + 1) * HG_DK)
            v_pad = jnp.concatenate([v[:, ls], zeros] if h % 2 == 0 else [zeros, v[:, ls]], axis=0)
            st = st_ref[h]
            o = _dot(a[h].astype(BF16), v_pad) + _dot_nt(q_in[:, ls], st.astype(BF16))
            st_ref[h] = st * dec[:, ls] + _dot_tn(v[:, ls], k_out[:, ls])
            var = jnp.mean(o * o, axis=-1, keepdims=True)
            gh = g[:, ls]
            y = o * lax.rsqrt(var + NORM_EPS) * nw * (gh * _sigmoid(gh))
            o_ref[rows, ls] = y.astype(o_ref.dtype)
        return carry

    lax.fori_loop(0, HG_NCHUNK, chunk, 0, unroll=HG_NCHUNK)


def _hgrn(proj3d, lb, norm_w):
    bsz, s, _ = proj3d.shape
    mcat, level = _hgrn_constants()

    def col(j):
        return pl.BlockSpec((None, HG_ROWS, HG_WIDTH), lambda b, i, j=j: (b, i, j))

    return pl.pallas_call(
        _hgrn_kernel,
        out_shape=jax.ShapeDtypeStruct((bsz, s, HG_WIDTH), BF16),
        grid=(bsz, s // HG_ROWS),
        in_specs=[
            col(0), col(1), col(2), col(3),
            pl.BlockSpec((1, HG_WIDTH), lambda b, i: (0, 0)),
            pl.BlockSpec((1, HG_DK), lambda b, i: (0, 0)),
            pl.BlockSpec((HG_E_ROWS, 2 * HG_CHUNK), lambda b, i: (0, 0)),
            pl.BlockSpec((HG_CHUNK, 2 * HG_CHUNK), lambda b, i: (0, 0)),
        ],
        out_specs=pl.BlockSpec((None, HG_ROWS, HG_WIDTH), lambda b, i: (b, i, 0)),
        scratch_shapes=[pltpu.VMEM((HG_HEADS, HG_DK, HG_DK), F32)],
        compiler_params=pltpu.CompilerParams(
            dimension_semantics=("arbitrary", "arbitrary"), vmem_limit_bytes=VMEM_LIMIT),
        name="hgrn2",
    )(proj3d, proj3d, proj3d, proj3d, lb, norm_w,
      jnp.asarray(mcat, BF16), jnp.asarray(level))


AT_BLK = 256
PREP_SUB = IN_TM // AT_BLK
AUG_PIECES = 3
VT_ROWS = 80


def _prep_constants():
    grp = np.zeros((FX_WIDTH // 2, FX_WIDTH // 2), np.float32)
    for h in range(FX_HEADS // 2):
        grp[h * FX_DH:(h + 1) * FX_DH, h * FX_DH:(h + 1) * FX_DH] = 1.0 / FX_DH
    tri = np.tril(np.ones((AT_BLK, AT_BLK), np.float32))
    ones_at = FX_HEADS * AUG_PIECES
    pk = np.zeros((LANE, FX_WIDTH), np.float32)
    cq = np.zeros((1, LANE), np.float32)
    ck = np.zeros((1, FX_WIDTH), np.float32)
    for p in range(AUG_PIECES):
        cq[0, ones_at + p] = 1.0
        for h in range(FX_HEADS):
            base = (h // 2) * LANE + (FX_DH if h % 2 == 0 else 0)
            ck[0, base + FX_HEADS * p + h] = 1.0
            pk[FX_HEADS * p + h, base + ones_at + p] = -1.0
    return grp, tri, pk, cq, ck


def _inproj_prep_kernel(x_ref, nw_ref, w_ref, wff_ref, fb_ref, qnw_ref, knw_ref,
                        grp_ref, tri_ref, pk_ref, cq_ref, ck_ref,
                        proj_ref, qa_ref, ka_ref, vt_ref, r_ref, carry_ref, *, steps_per_seq):
    @pl.when(pl.program_id(0) % steps_per_seq == 0)
    def _():
        carry_ref[...] = jnp.zeros_like(carry_ref)

    x = x_ref[...]
    ms = jnp.mean(x * x, axis=-1, keepdims=True)
    xn = (x * lax.rsqrt(ms + NORM_EPS) * nw_ref[...]).astype(BF16)

    def proj(c):
        return _dot(xn, w_ref[:, c * IN_TN:(c + 1) * IN_TN])

    hg_groups = list(range(HG_COLS // IN_TN))
    q_all, k_all = (proj(len(hg_groups) + c) for c in range(2))
    v_cols = slice(PROJ_MAIN - IN_TN, PROJ_MAIN)
    vz = _dot(xn, jnp.concatenate([w_ref[:, v_cols], wff_ref[...]], axis=1))
    v_all = vz[:, :IN_TN]

    z = vz[:, IN_TN:] + fb_ref[...]
    lf = (jnp.minimum(z, 0.0) - jnp.log(1.0 + jnp.exp(-jnp.abs(z)))) * LOG2E
    grp = grp_ref[...]
    tri = tri_ref[...]
    lane = lax.broadcasted_iota(jnp.int32, (AT_BLK, LANE), 1)
    low_half = lane < FX_DH
    head_lane = lane < FX_HEADS

    def qk_norm(t, w):
        sq = (t * t).astype(BF16)
        half = FX_WIDTH // 2
        ms = jnp.concatenate([_dot(sq[:, :half], grp), _dot(sq[:, half:], grp)], axis=1)
        return t * lax.rsqrt(ms + NORM_EPS) * w

    def pack_pieces(t):
        packed = None
        for p, piece in enumerate(_split3(t)):
            piece = jnp.where(head_lane, piece.astype(F32), 0.0)
            piece = piece if p == 0 else pltpu.roll(piece, FX_HEADS * p, axis=1)
            packed = piece if packed is None else packed + piece
        return packed

    for j in range(PREP_SUB):
        rows = slice(j * AT_BLK, (j + 1) * AT_BLK)
        sums = _dot(tri, jnp.concatenate(_split3(lf[rows]), axis=1))
        c_rel = sums[:, :LANE] + sums[:, LANE:2 * LANE] + sums[:, 2 * LANE:]
        r_ref[j] = carry_ref[...]
        carry_ref[...] = carry_ref[...] + c_rel[AT_BLK - 1:AT_BLK, :]
        packed = pack_pieces(c_rel)
        aug_q_odd = packed + cq_ref[...]
        aug_q_even = pltpu.roll(aug_q_odd, FX_DH, axis=1)
        aug_k = _dot(packed.astype(BF16), pk_ref[...]) + ck_ref[...]
        qn = qk_norm(q_all[rows], qnw_ref[...])
        kn = qk_norm(k_all[rows], knw_ref[...])
        for h in range(FX_HEADS):
            pair = slice((h // 2) * LANE, (h // 2 + 1) * LANE)
            head = slice(h * LANE, (h + 1) * LANE)
            data = low_half if h % 2 == 0 else jnp.logical_not(low_half)
            aug_q = aug_q_even if h % 2 == 0 else aug_q_odd
            qa_ref[rows, head] = jnp.where(data, qn[:, pair], aug_q).astype(BF16)
            ka_ref[rows, head] = jnp.where(data, kn[:, pair], aug_k[:, pair]).astype(BF16)
        v_t = v_all[rows].T.astype(BF16)
        for h in range(FX_HEADS):
            vt_ref[j, h * VT_ROWS:h * VT_ROWS + FX_DH, :] = v_t[h * FX_DH:(h + 1) * FX_DH]
            vt_ref[j, h * VT_ROWS + FX_DH:(h + 1) * VT_ROWS, :] = jnp.ones(
                (VT_ROWS - FX_DH, AT_BLK), BF16)
        for c in hg_groups[j::PREP_SUB]:
            proj_ref[:, c * IN_TN:(c + 1) * IN_TN] = proj(c).astype(BF16)


def _inproj_prep(x2d, seq_len, norm_w, w_all, w_ff, layer, f_bias, qnw, knw):
    t = x2d.shape[0]
    consts = _prep_constants()
    grp, tri, pk = (jnp.asarray(c, BF16) for c in consts[:3])
    cq, ck = (jnp.asarray(c, F32) for c in consts[3:])
    width = FX_HEADS * LANE

    def full(shape):
        return pl.BlockSpec(shape, lambda i: (0,) * len(shape))

    return pl.pallas_call(
        functools.partial(_inproj_prep_kernel, steps_per_seq=seq_len // IN_TM),
        out_shape=(jax.ShapeDtypeStruct((t, HG_COLS), BF16),
                   jax.ShapeDtypeStruct((t, width), BF16),
                   jax.ShapeDtypeStruct((t, width), BF16),
                   jax.ShapeDtypeStruct((t // AT_BLK, FX_HEADS * VT_ROWS, AT_BLK), BF16),
                   jax.ShapeDtypeStruct((t // AT_BLK, 8, LANE), F32)),
        grid=(t // IN_TM,),
        in_specs=[
            pl.BlockSpec((IN_TM, D_MODEL), lambda i: (i, 0)),
            full((1, D_MODEL)),
            pl.BlockSpec((None, D_MODEL, PROJ_MAIN), lambda i: (layer, 0, 0),
                         pipeline_mode=pl.Buffered(1)),
            pl.BlockSpec((None, D_MODEL, LANE), lambda i: (layer, 0, 0)),
            full((1, LANE)), full((1, FX_WIDTH)), full((1, FX_WIDTH)),
            full((FX_WIDTH // 2, FX_WIDTH // 2)), full((AT_BLK, AT_BLK)),
            full((LANE, FX_WIDTH)), full((1, LANE)), full((1, FX_WIDTH)),
        ],
        out_specs=(pl.BlockSpec((IN_TM, HG_COLS), lambda i: (i, 0)),
                   pl.BlockSpec((IN_TM, width), lambda i: (i, 0)),
                   pl.BlockSpec((IN_TM, width), lambda i: (i, 0)),
                   pl.BlockSpec((PREP_SUB, FX_HEADS * VT_ROWS, AT_BLK), lambda i: (i, 0, 0)),
                   pl.BlockSpec((PREP_SUB, 8, LANE), lambda i: (i, 0, 0))),
        scratch_shapes=[pltpu.VMEM((8, LANE), F32)],
        compiler_params=pltpu.CompilerParams(
            dimension_semantics=("arbitrary",), vmem_limit_bytes=VMEM_LIMIT),
        name="inproj_prep",
    )(x2d, norm_w, w_all, w_ff, f_bias, qnw, knw, grp, tri, pk, cq, ck)


AT_HG = 2
AT_SLOTS = 3
AT_TRIP_TICKS = 4 * AT_SLOTS
F32_EXP2_ZERO = 150.0
FIXED_STABILISER_MAX = 32.0
SUBLANES = 8


def _tree_reduce(op, final, x):
    while x.shape[0] > SUBLANES:
        half = x.shape[0] // 2
        x = op(x[:half], x[half:])
    return final(x, axis=0, keepdims=True)


def _attn_kernel(order_ref, r_ref, mask_ref, *refs, nblk, n_r):
    q_refs, k_refs, vt_refs = (refs[i * AT_HG:(i + 1) * AT_HG] for i in range(3))
    o_ref, s_ref, p_ref, acc_ref, m_ref, a_ref, start_ref = refs[3 * AT_HG:]
    b = pl.program_id(0)
    hg = pl.program_id(1)
    bases = [(b * FX_HEADS + order_ref[hg * AT_HG + g]) * nblk for g in range(AT_HG)]
    last = nblk - 1

    thr = r_ref[n_r]

    def find_start(qi, n_items):
        def needed(first):
            hit = None
            for base in bases:
                c = r_ref[base + qi] - r_ref[base + first] >= thr
                hit = c if hit is None else jnp.logical_or(hit, c)
            return hit

        first = lax.while_loop(lambda f: jnp.logical_and(f > 0, needed(f)),
                               lambda f: f - 1, qi)
        start_ref[qi] = first
        return n_items + (qi - first + 1)

    n_items = lax.fori_loop(0, nblk, find_start, 0)

    def next_item(item):
        qi, ki, valid = item
        at_diag = ki == qi
        more = jnp.logical_and(valid != 0,
                               jnp.logical_not(jnp.logical_and(at_diag, qi == last)))
        qn = jnp.where(at_diag, jnp.minimum(qi + 1, last), qi)
        kn = jnp.where(at_diag, start_ref[qn], ki + 1)
        return (jnp.where(more, qn, last), jnp.where(more, kn, last), more.astype(jnp.int32))

    def stage_a(slot, item):
        qi, ki, _ = item
        rows_q = pl.ds(pl.multiple_of(qi * AT_BLK, AT_BLK), AT_BLK)
        rows_k = pl.ds(pl.multiple_of(ki * AT_BLK, AT_BLK), AT_BLK)
        mask = mask_ref[(ki == qi).astype(jnp.int32)]
        for g in range(AT_HG):
            s_ref[slot, g] = _dot_nt(k_refs[g][rows_k, :], q_refs[g][rows_q, :]) + mask

    qk_bound = r_ref[n_r + 1]

    def stage_b(slot, item, fixed):
        qi, ki, valid = item
        live = valid != 0
        first = jnp.logical_and(live, ki == start_ref[qi])
        for g in range(AT_HG):
            delta = jnp.where(live, r_ref[bases[g] + qi] - r_ref[bases[g] + ki], NEG_BIG)
            s = s_ref[slot, g]
            if fixed:
                p_ref[slot, g] = jnp.exp2(s + (delta - qk_bound)).astype(BF16)
            else:
                m = jnp.where(first, NEG_BIG, m_ref[g])
                m_new = jnp.maximum(m, _tree_reduce(jnp.maximum, jnp.max, s) + delta)
                p_ref[slot, g] = jnp.exp2(s - (m_new - delta)).astype(BF16)
                a_ref[g] = jnp.exp2(m - m_new)
                m_ref[g] = m_new

    def stage_c(slot, item, fixed):
        qi, ki, _ = item
        for g in range(AT_HG):
            pv = _dot(vt_refs[g][ki], p_ref[slot, g])
            acc_ref[qi, g] = acc_ref[qi, g] + pv if fixed else a_ref[g] * acc_ref[qi, g] + pv

    def tick(slot, items, fixed):
        item_c, item_b, item_s, item_a = items
        ahead = (slot + 2) % AT_SLOTS
        stage_c(ahead, item_c, fixed)
        stage_b(slot, item_b, fixed)
        stage_a(ahead, item_a)
        return (item_b, item_s, item_a, next_item(item_a))

    zero = jnp.int32(0)
    item0 = (zero, zero, jnp.int32(1))
    item1 = next_item(item0)
    idle = (zero, zero, zero)
    stage_a(0, item0)
    stage_a(1, item1)

    def sweep(fixed):
        for g in range(AT_HG):
            p_ref[AT_SLOTS - 1, g] = jnp.zeros((AT_BLK, AT_BLK), BF16)
            a_ref[g] = jnp.ones((1, AT_BLK), F32)
            m_ref[g] = jnp.zeros((1, AT_BLK), F32)
        acc_ref[...] = jnp.zeros(acc_ref.shape, F32) if fixed else jnp.ones(acc_ref.shape, F32)

        def trip(_, items):
            for t in range(AT_TRIP_TICKS):
                items = tick(t % AT_SLOTS, items, fixed)
            return items

        lax.fori_loop(0, (n_items + AT_TRIP_TICKS) // AT_TRIP_TICKS, trip,
                      (idle, item0, item1, next_item(item1)))

    use_fixed = order_ref[FX_HEADS] != 0

    @pl.when(use_fixed)
    def _():
        sweep(True)

    @pl.when(jnp.logical_not(use_fixed))
    def _():
        sweep(False)

    def normalize(qi, carry):
        for g in range(AT_HG):
            acc = acc_ref[qi, g]
            o_ref[qi, g * FX_DH:(g + 1) * FX_DH, :] = (
                acc[:FX_DH] / acc[FX_DH:FX_DH + 1]).astype(o_ref.dtype)
        return carry

    lax.fori_loop(0, nblk, normalize, 0)


def _attention(order, r_flat, q_aug, k_aug, v_t):
    bsz, s, _ = q_aug.shape
    nblk = s // AT_BLK
    causal = np.where(np.arange(AT_BLK)[:, None] <= np.arange(AT_BLK)[None, :], 0.0, NEG_BIG)
    mask = jnp.asarray(np.stack([np.zeros_like(causal), causal]), F32)

    def head_cols(g):
        return pl.BlockSpec((None, s, LANE), lambda b, h, order: (b, 0, order[h * AT_HG + g]))

    def head_vt(g):
        return pl.BlockSpec((None, nblk, VT_ROWS, AT_BLK),
                            lambda b, h, order: (b, 0, order[h * AT_HG + g], 0))

    grid_spec = pltpu.PrefetchScalarGridSpec(
        num_scalar_prefetch=1,
        grid=(bsz, FX_HEADS // AT_HG),
        in_specs=[
            pl.BlockSpec(memory_space=pltpu.SMEM),
            pl.BlockSpec((2, AT_BLK, AT_BLK), lambda b, h, order: (0, 0, 0)),
            *[head_cols(g) for g in range(AT_HG)],
            *[head_cols(g) for g in range(AT_HG)],
            *[head_vt(g) for g in range(AT_HG)],
        ],
        out_specs=pl.BlockSpec((None, nblk, AT_HG * FX_DH, AT_BLK),
                               lambda b, h, order: (b, 0, h, 0)),
        scratch_shapes=[
            pltpu.VMEM((AT_SLOTS, AT_HG, AT_BLK, AT_BLK), F32),
            pltpu.VMEM((AT_SLOTS, AT_HG, AT_BLK, AT_BLK), BF16),
            pltpu.VMEM((nblk, AT_HG, VT_ROWS, AT_BLK), F32),
            pltpu.VMEM((AT_HG, 1, AT_BLK), F32),
            pltpu.VMEM((AT_HG, 1, AT_BLK), F32),
            pltpu.SMEM((nblk,), jnp.int32),
        ],
    )
    return pl.pallas_call(
        functools.partial(_attn_kernel, nblk=nblk, n_r=r_flat.shape[0] - 2),
        out_shape=jax.ShapeDtypeStruct((bsz, nblk, FX_WIDTH, AT_BLK), BF16),
        grid_spec=grid_spec,
        compiler_params=pltpu.CompilerParams(
            dimension_semantics=("arbitrary", "arbitrary"),
            vmem_limit_bytes=VMEM_LIMIT),
        name="fox_attention",
    )(order, r_flat, mask, *([q_aug] * AT_HG), *([k_aug] * AT_HG), *([v_t] * AT_HG))


FF_TM = 1024
FF_TF = D_FF
FF_CHUNK = 1024


def _ffn_kernel(order_ref, x_ref, oa_ref, obt_ref, woa_ref, *refs):
    del order_ref
    wob_refs = refs[:FX_HEADS]
    nw_ref, wu_ref, wd_ref, o_ref, h_ref = refs[FX_HEADS:]

    @pl.when(pl.program_id(2) == 0)
    def _():
        wob = jnp.concatenate([r[...] for r in wob_refs], axis=0)
        o_b = jnp.concatenate([_dot_tn(obt_ref[j], wob)
                               for j in range(FF_TM // AT_BLK)], axis=0)
        x1 = x_ref[...] + _dot(oa_ref[...], woa_ref[...]) + o_b
        ms = jnp.mean(x1 * x1, axis=-1, keepdims=True)
        h_ref[...] = (x1 * lax.rsqrt(ms + NORM_EPS) * nw_ref[...]).astype(BF16)
        o_ref[...] = x1

    for c in range(FF_TF // FF_CHUNK):
        cols = slice(c * FF_CHUNK, (c + 1) * FF_CHUNK)
        u = jnp.maximum(_dot(h_ref[...], wu_ref[:, cols]), 0.0)
        o_ref[...] += _dot((u * u).astype(BF16), wd_ref[cols, :])


def _outproj_ffn(order, x3d, o_a, o_bt, w_out, norm_w, w_up, w_down, layer):
    bsz, s, _ = x3d.shape
    resident = dict(pipeline_mode=pl.Buffered(1)) if FF_TF == D_FF else {}

    def head_rows(i):
        return pl.BlockSpec((None, FX_DH, D_MODEL),
                            lambda b, t, j, order: (layer, HG_WIDTH // FX_DH + order[i], 0), **resident)

    grid_spec = pltpu.PrefetchScalarGridSpec(
        num_scalar_prefetch=1,
        grid=(bsz, s // FF_TM, D_FF // FF_TF),
        in_specs=[
            pl.BlockSpec((None, FF_TM, D_MODEL), lambda b, t, j, order: (b, t, 0)),
            pl.BlockSpec((None, FF_TM, HG_WIDTH), lambda b, t, j, order: (b, t, 0)),
            pl.BlockSpec((None, FF_TM // AT_BLK, FX_WIDTH, AT_BLK),
                         lambda b, t, j, order: (b, t, 0, 0)),
            pl.BlockSpec((None, HG_WIDTH, D_MODEL), lambda b, t, j, order: (layer, 0, 0), **resident),
            *[head_rows(i) for i in range(FX_HEADS)],
            pl.BlockSpec((1, D_MODEL), lambda b, t, j, order: (0, 0)),
            pl.BlockSpec((None, D_MODEL, FF_TF), lambda b, t, j, order: (layer, 0, j), **resident),
            pl.BlockSpec((None, FF_TF, D_MODEL), lambda b, t, j, order: (layer, j, 0), **resident),
        ],
        out_specs=pl.BlockSpec((None, FF_TM, D_MODEL), lambda b, t, j, order: (b, t, 0)),
        scratch_shapes=[pltpu.VMEM((FF_TM, D_MODEL), BF16)],
    )
    return pl.pallas_call(
        _ffn_kernel,
        out_shape=jax.ShapeDtypeStruct(x3d.shape, F32),
        grid_spec=grid_spec,
        compiler_params=pltpu.CompilerParams(
            dimension_semantics=("arbitrary", "arbitrary", "arbitrary"),
            vmem_limit_bytes=VMEM_LIMIT),
        name="outproj_ffn",
    )(order, x3d, o_a, o_bt, w_out, *([w_out] * FX_HEADS), norm_w, w_up, w_down)


def _layer_lower_bounds(lower_bounds):
    p = jax.nn.softmax(lower_bounds.astype(F32), axis=0)
    c = jnp.cumsum(p, axis=0)
    return c - c[0:1]


def kernel(x, lower_bounds, norm1_w, w_in, fox_f_bias, q_norm_w, k_norm_w,
           hgrn_norm_w, w_out, norm2_w, w_up, w_down):
    bsz, s, d = x.shape
    depth = w_in.shape[0]
    nblk = s // AT_BLK
    lbs = _layer_lower_bounds(lower_bounds)
    w_in_b = w_in.astype(BF16)
    w_in_ff = jnp.pad(w_in_b[:, :, PROJ_MAIN:], ((0, 0), (0, 0), (0, PROJ_PAD - w_in.shape[-1])))
    w_out_b = w_out.astype(BF16)
    w_up_b = w_up.astype(BF16)
    w_down_b = w_down.astype(BF16)
    f_bias = jnp.pad(fox_f_bias, ((0, 0), (0, LANE - FX_HEADS)))

    for l in range(depth):
        proj, q_aug, k_aug, v_t, r_blk = _inproj_prep(
            x.reshape(bsz * s, d), s, norm1_w[l][None, :], w_in_b, w_in_ff, l, f_bias[l][None, :],
            jnp.tile(q_norm_w[l] * (FX_DH ** -0.5 * LOG2E), FX_HEADS)[None, :],
            jnp.tile(k_norm_w[l], FX_HEADS)[None, :])
        width = FX_HEADS * LANE
        q_aug = q_aug.reshape(bsz, s, width)
        k_aug = k_aug.reshape(bsz, s, width)
        v_t = v_t.reshape(bsz, nblk, FX_HEADS * VT_ROWS, AT_BLK)
        r_blk = r_blk.reshape(bsz, nblk, 8, LANE)
        o_a = _hgrn(proj.reshape(bsz, s, HG_COLS), lbs[l][None, :], hgrn_norm_w[l][None, :])
        qk_bound = (1.02 * FX_DH ** 0.5 * LOG2E * jnp.max(jnp.abs(q_norm_w[l]))
                    * jnp.max(jnp.abs(k_norm_w[l])))
        skip_thr = -(F32_EXP2_ZERO + 2.0 * qk_bound)
        r_flat = jnp.concatenate([
            jnp.transpose(r_blk[:, :, 0, :FX_HEADS], (0, 2, 1)).reshape(-1),
            skip_thr.reshape(1).astype(F32), qk_bound.reshape(1).astype(F32)])
        order = jnp.argsort(fox_f_bias[l]).astype(jnp.int32)
        use_fixed = (qk_bound <= FIXED_STABILISER_MAX).astype(jnp.int32).reshape(1)
        o_bt = _attention(jnp.concatenate([order, use_fixed]), r_flat, q_aug, k_aug, v_t)
        x = _outproj_ffn(order, x, o_a, o_bt, w_out_b, norm2_w[l][None, :], w_up_b, w_down_b, l)
    return x
```
